```python
import math
import jax, jax.numpy as jnp
from jax import lax
import numpy as np

D_MODEL = 1024
BATCH = 1
SEQ = 16384
DEPTH = 2

HEAD_DIM = 64
SB_HEADS = 4
NSA_HEADS = 8
NSA_GROUPS = 2
NSA_HPG = NSA_HEADS // NSA_GROUPS
NSA_BRANCHES = 3
MLA_HEADS = 4
MLA_NOPE = 64
MLA_ROPE = 32
MLA_V = 64
MLA_QK = MLA_NOPE + MLA_ROPE
MLA_Q_LORA = 256
MLA_KV_LORA = 128
ROPE_THETA = 10000.0
Q_BLOCK = 128
CMP_LEN = 32
CMP_STRIDE = 16
CMP_HIDDEN = 256
SEL_BLOCK = 64
SEL_TOPK = 16
N_LOCAL_BLOCKS = 2
WINDOW = 512
NUM_BUCKETS = 32
MAX_DISTANCE = 1024
N_MIXERS = 3
D_FF = 2816
N_EXPERTS = 8
TOP_K = 2
EXPERT_FF = 3584
N_DENSE_LAYERS = (DEPTH + 1) // 2
N_MOE_LAYERS = DEPTH // 2
EPS = 1e-6
NEG_INF = -1e30
FORCE = 1e30

SB_W = SB_HEADS * HEAD_DIM
NSA_QW = NSA_HEADS * HEAD_DIM
NSA_KVW = NSA_GROUPS * HEAD_DIM
IN_SPLITS = (SB_W, SB_W, SB_W,
             NSA_QW, NSA_KVW, NSA_KVW, NSA_KVW, NSA_KVW, NSA_KVW, NSA_KVW, NSA_HEADS * NSA_BRANCHES,
             MLA_Q_LORA, MLA_KV_LORA, MLA_ROPE,
             N_MIXERS * D_MODEL)
IN_COLS = sum(IN_SPLITS)

kernel_name = 'hybrid_stickbreak_nsa_mla_moe'


def rmsnorm(x, g):
    xf = x.astype(jnp.float32)
    y = xf * lax.rsqrt(jnp.mean(xf * xf, axis=-1, keepdims=True) + EPS)
    return (y * g.astype(jnp.float32)).astype(x.dtype)


def masked_softmax(s, mask):
    p = jax.nn.softmax(jnp.where(mask, s, NEG_INF), axis=-1)
    return jnp.where(mask, p, 0.0)


def t5_bucket(rel):
    n = jnp.maximum(rel, 0)
    max_exact = NUM_BUCKETS // 2
    nf = jnp.maximum(n, 1).astype(jnp.float32)
    large = max_exact + (jnp.log(nf / max_exact) / math.log(MAX_DISTANCE / max_exact)
                         * (NUM_BUCKETS - max_exact)).astype(jnp.int32)
    large = jnp.minimum(large, NUM_BUCKETS - 1)
    return jnp.where(n < max_exact, n, large)


def apply_rope(x, pos):
    half = x.shape[-1] // 2
    inv = ROPE_THETA ** (-jnp.arange(half, dtype=jnp.float32) / half)
    ang = pos.astype(jnp.float32)[:, None] * inv[None, :]
    cos = jnp.cos(ang)[None, :, None, :]
    sin = jnp.sin(ang)[None, :, None, :]
    xf = x.astype(jnp.float32)
    x1, x2 = xf[..., :half], xf[..., half:]
    return jnp.concatenate([x1 * cos - x2 * sin, x2 * cos + x1 * sin], axis=-1).astype(x.dtype)


def stick_breaking_attention(q, k, v):
    B, S, H, D = q.shape
    scale = D ** -0.5
    kpos = jnp.arange(S)

    def block(i):
        start = i * Q_BLOCK
        qb = lax.dynamic_slice_in_dim(q, start, Q_BLOCK, axis=1)
        z = jnp.einsum('bqhd,bkhd->bhqk', qb, k).astype(jnp.float32) * scale
        qpos = start + jnp.arange(Q_BLOCK)
        causal = kpos[None, :] < qpos[:, None]
        log_stay = jnp.where(causal, jax.nn.log_sigmoid(-z), 0.0)
        after = lax.cumsum(log_stay, axis=3, reverse=True) - log_stay
        w = jnp.where(causal, jnp.exp(jax.nn.log_sigmoid(z) + after), 0.0)
        return jnp.einsum('bhqk,bkhd->bqhd', w.astype(v.dtype), v)

    out = lax.map(block, jnp.arange(S // Q_BLOCK))
    return out.transpose(1, 0, 2, 3, 4).reshape(B, S, H * D)


def causal_softmax_attention(q, k, v, scale):
    B, S, H, _ = q.shape
    Dv = v.shape[-1]
    kpos = jnp.arange(S)

    def block(i):
        start = i * Q_BLOCK
        qb = lax.dynamic_slice_in_dim(q, start, Q_BLOCK, axis=1)
        s = jnp.einsum('bqhd,bkhd->bhqk', qb, k).astype(jnp.float32) * scale
        qpos = start + jnp.arange(Q_BLOCK)
        mask = kpos[None, :] <= qpos[:, None]
        p = jax.nn.softmax(jnp.where(mask, s, NEG_INF), axis=-1)
        return jnp.einsum('bhqk,bkhd->bqhd', p.astype(v.dtype), v)

    out = lax.map(block, jnp.arange(S // Q_BLOCK))
    return out.transpose(1, 0, 2, 3, 4).reshape(B, S, H * Dv)


def compress_blocks(x, pos_emb, w1, w2):
    B, S, G, Dh = x.shape
    n_chunk = S // CMP_STRIDE
    r = CMP_LEN // CMP_STRIDE
    n_cmp = n_chunk - r + 1
    chunks = x.reshape(B, n_chunk, CMP_STRIDE, G, Dh)
    blocks = jnp.concatenate([chunks[:, j:j + n_cmp] for j in range(r)], axis=2)
    blocks = blocks + pos_emb[:, None, :]
    flat = blocks.transpose(0, 1, 3, 2, 4).reshape(B, n_cmp, G, CMP_LEN * Dh)
    return jax.nn.silu(flat @ w1) @ w2


def native_sparse_attention(q, k_cmp_in, v_cmp_in, k_sel, v_sel, k_win, v_win, gate_logits,
                            q_norm, k_norms, cmp_pos, cmp_w1, cmp_w2, bias_table):
    B, S, H, Dh = q.shape
    G, R = NSA_GROUPS, NSA_HPG
    scale = Dh ** -0.5
    q = rmsnorm(q, q_norm)
    k_cmp = rmsnorm(compress_blocks(k_cmp_in, cmp_pos[0], cmp_w1[0], cmp_w2[0]), k_norms[0])
    v_cmp = compress_blocks(v_cmp_in, cmp_pos[1], cmp_w1[1], cmp_w2[1])
    k_sel = rmsnorm(k_sel, k_norms[1])
    k_win = rmsnorm(k_win, k_norms[2])
    gates = jax.nn.sigmoid(gate_logits.astype(jnp.float32)).astype(q.dtype).reshape(B, S, G, R, NSA_BRANCHES)

    n_chunk = S // CMP_STRIDE
    r_c = CMP_LEN // CMP_STRIDE
    r_s = SEL_BLOCK // CMP_STRIDE
    n_cmp = n_chunk - r_c + 1
    cmp_end = jnp.arange(n_cmp) * CMP_STRIDE + CMP_LEN - 1
    n_blk = S // SEL_BLOCK
    n_sel = min(SEL_TOPK, n_blk)
    n_keys_sel = n_sel * SEL_BLOCK
    blk_ids = jnp.arange(n_blk)
    ks_blocks = k_sel.reshape(B, n_blk, SEL_BLOCK, G, Dh).transpose(0, 3, 1, 2, 4)
    vs_blocks = v_sel.reshape(B, n_blk, SEL_BLOCK, G, Dh).transpose(0, 3, 1, 2, 4)
    pad = ((0, 0), (WINDOW, 0), (0, 0), (0, 0))
    kw_pad = jnp.pad(k_win, pad)
    vw_pad = jnp.pad(v_win, pad)
    table = bias_table.reshape(NUM_BUCKETS, G, R)
    table_g = table.transpose(1, 0, 2)
    b_ix = jnp.arange(B)[:, None, None, None]
    g_ix = jnp.arange(G)[None, :, None, None]

    def block(i):
        start = i * Q_BLOCK
        qpos = start + jnp.arange(Q_BLOCK)
        qb = lax.dynamic_slice_in_dim(q, start, Q_BLOCK, axis=1).reshape(B, Q_BLOCK, G, R, Dh)
        rel_c = qpos[:, None] - cmp_end[None, :]
        bias_c = table[t5_bucket(rel_c)].transpose(2, 3, 0, 1)
        s_c = jnp.einsum('bqgrd,bngd->bgrqn', qb, k_cmp).astype(jnp.float32) * scale + bias_c
        p_c = masked_softmax(s_c, rel_c >= 0)
        o_c = jnp.einsum('bgrqn,bngd->bqgrd', p_c.astype(v_cmp.dtype), v_cmp)
        imp = jnp.pad(p_c.sum(axis=2), ((0, 0), (0, 0), (0, 0), (r_c - 1, r_c - 1)))
        imp_blk = imp[..., 0:r_s * n_blk:r_s]
        for m in range(1, r_s + r_c - 1):
            imp_blk = imp_blk + imp[..., m:m + r_s * n_blk:r_s]
        qblk = qpos // SEL_BLOCK
        causal_b = blk_ids[None, :] * SEL_BLOCK <= qpos[:, None]
        dist_b = qblk[:, None] - blk_ids[None, :]
        forced = (blk_ids[None, :] == 0) | ((dist_b >= 0) & (dist_b < N_LOCAL_BLOCKS))
        score = jnp.where(causal_b, jnp.where(forced, FORCE, imp_blk), NEG_INF)
        top_val, top_idx = lax.top_k(score, n_sel)
        sel_ok = top_val > 0.5 * NEG_INF
        kg = ks_blocks[b_ix, g_ix, top_idx]
        vg = vs_blocks[b_ix, g_ix, top_idx]
        tok = top_idx[..., None] * SEL_BLOCK + jnp.arange(SEL_BLOCK)
        rel_s = qpos[None, None, :, None, None] - tok
        mask_s = (sel_ok[..., None] & (rel_s >= 0)).reshape(B, G, 1, Q_BLOCK, n_keys_sel)
        bias_s = table_g[g_ix[..., None], t5_bucket(rel_s)]
        s_s = (jnp.einsum('bqgrd,bgqnld->bgrqnl', qb, kg).astype(jnp.float32) * scale
               + bias_s.transpose(0, 1, 5, 2, 3, 4)).reshape(B, G, R, Q_BLOCK, n_keys_sel)
        p_s = masked_softmax(s_s, mask_s)
        o_s = jnp.einsum('bgrqk,bgqkd->bqgrd', p_s.astype(vg.dtype),
                         vg.reshape(B, G, Q_BLOCK, n_keys_sel, Dh))
        kwb = lax.dynamic_slice_in_dim(kw_pad, start, WINDOW + Q_BLOCK, axis=1)
        vwb = lax.dynamic_slice_in_dim(vw_pad, start, WINDOW + Q_BLOCK, axis=1)
        kpos_w = start - WINDOW + jnp.arange(WINDOW + Q_BLOCK)
        rel_w = qpos[:, None] - kpos_w[None, :]
        mask_w = (rel_w >= 0) & (rel_w < WINDOW) & (kpos_w[None, :] >= 0)
        bias_w = table[t5_bucket(rel_w)].transpose(2, 3, 0, 1)
        s_w = jnp.einsum('bqgrd,bkgd->bgrqk', qb, kwb).astype(jnp.float32) * scale + bias_w
        p_w = masked_softmax(s_w, mask_w)
        o_w = jnp.einsum('bgrqk,bkgd->bqgrd', p_w.astype(vwb.dtype), vwb)
        gb = lax.dynamic_slice_in_dim(gates, start, Q_BLOCK, axis=1)
        o = gb[..., 0:1] * o_c + gb[..., 1:2] * o_s + gb[..., 2:3] * o_w
        return o.reshape(B, Q_BLOCK, H * Dh)

    out = lax.map(block, jnp.arange(S // Q_BLOCK))
    return out.transpose(1, 0, 2, 3).reshape(B, S, H * Dh)


def multi_head_latent_attention(c_q, c_kv, k_rope, q_a_norm, kv_a_norm, w_uq, w_ukv, q_norm, k_norm):
    B, S, _ = c_q.shape
    H = MLA_HEADS
    q = (rmsnorm(c_q, q_a_norm) @ w_uq).reshape(B, S, H, MLA_QK)
    kv = (rmsnorm(c_kv, kv_a_norm) @ w_ukv).reshape(B, S, H, MLA_NOPE + MLA_V)
    k = jnp.concatenate([kv[..., :MLA_NOPE],
                         jnp.broadcast_to(k_rope[:, :, None, :], (B, S, H, MLA_ROPE))], axis=-1)
    v = kv[..., MLA_NOPE:]
    q = rmsnorm(q, q_norm)
    k = rmsnorm(k, k_norm)
    pos = jnp.arange(S)
    q = jnp.concatenate([q[..., :MLA_NOPE], apply_rope(q[..., MLA_NOPE:], pos)], axis=-1)
    k = jnp.concatenate([k[..., :MLA_NOPE], apply_rope(k[..., MLA_NOPE:], pos)], axis=-1)
    return causal_softmax_attention(q, k, v, MLA_QK ** -0.5)


def swiglu(h, w_gate, w_up, w_down):
    return (jax.nn.silu(h @ w_gate) * (h @ w_up)) @ w_down


def moe_swiglu(h, w_router, w_gate, w_up, w_down):
    logits = (h @ w_router).astype(jnp.float32)
    top_val, top_idx = lax.top_k(logits, TOP_K)
    weights = jax.nn.softmax(top_val, axis=-1)
    combine = jnp.sum(jax.nn.one_hot(top_idx, N_EXPERTS, dtype=jnp.float32) * weights[..., None],
                      axis=-2).astype(h.dtype)
    y = jnp.zeros_like(h)
    for e in range(N_EXPERTS):
        y = y + combine[..., e:e + 1] * swiglu(h, w_gate[e], w_up[e], w_down[e])
    return y


def setup_inputs(seed: int = 0) -> dict:
    key = jax.random.key(seed)
    keys = iter(jax.random.split(key, 48))

    def w(shape, fan_in):
        return jax.random.normal(next(keys), shape, jnp.float32) * (fan_in ** -0.5)

    def gain(shape):
        return 1.0 + 0.02 * jax.random.normal(next(keys), shape, jnp.float32)

    def small(shape, scale):
        return scale * jax.random.normal(next(keys), shape, jnp.float32)

    L, ND, NM = DEPTH, N_DENSE_LAYERS, N_MOE_LAYERS
    return {
        'x': small((BATCH, SEQ, D_MODEL), 1.0),
        'rel_bias_table': small((NUM_BUCKETS, NSA_HEADS), 0.5),
        'mix_norm': gain((L, D_MODEL)),
        'w_in': w((L, D_MODEL, IN_COLS), D_MODEL),
        'b_gate': small((L, N_MIXERS * D_MODEL), 0.01),
        'sb_w_o': w((L, SB_W, D_MODEL), SB_W),
        'nsa_q_norm': gain((L, HEAD_DIM)),
        'nsa_k_norm': gain((L, NSA_BRANCHES, HEAD_DIM)),
        'nsa_cmp_pos': small((L, 2, CMP_LEN, HEAD_DIM), 0.1),
        'nsa_cmp_w1': w((L, 2, CMP_LEN * HEAD_DIM, CMP_HIDDEN), CMP_LEN * HEAD_DIM),
        'nsa_cmp_w2': w((L, 2, CMP_HIDDEN, HEAD_DIM), CMP_HIDDEN),
        'nsa_w_o': w((L, NSA_QW, D_MODEL), NSA_QW),
        'mla_q_a_norm': gain((L, MLA_Q_LORA)),
        'mla_kv_a_norm': gain((L, MLA_KV_LORA)),
        'mla_w_uq': w((L, MLA_Q_LORA, MLA_HEADS * MLA_QK), MLA_Q_LORA),
        'mla_w_ukv': w((L, MLA_KV_LORA, MLA_HEADS * (MLA_NOPE + MLA_V)), MLA_KV_LORA),
        'mla_q_norm': gain((L, MLA_QK)),
        'mla_k_norm': gain((L, MLA_QK)),
        'mla_w_o': w((L, MLA_HEADS * MLA_V, D_MODEL), MLA_HEADS * MLA_V),
        'w_out': w((L, D_MODEL, D_MODEL), D_MODEL),
        'ffn_norm': gain((L, D_MODEL)),
        'dense_w_gate': w((ND, D_MODEL, D_FF), D_MODEL),
        'dense_w_up': w((ND, D_MODEL, D_FF), D_MODEL),
        'dense_w_down': w((ND, D_FF, D_MODEL), D_FF),
        'moe_router': w((NM, D_MODEL, N_EXPERTS), D_MODEL),
        'moe_w_gate': w((NM, N_EXPERTS, D_MODEL, EXPERT_FF), D_MODEL),
        'moe_w_up': w((NM, N_EXPERTS, D_MODEL, EXPERT_FF), D_MODEL),
        'moe_w_down': w((NM, N_EXPERTS, EXPERT_FF, D_MODEL), EXPERT_FF),
    }


def reference(x, rel_bias_table, mix_norm, w_in, b_gate, sb_w_o, nsa_q_norm, nsa_k_norm,
              nsa_cmp_pos, nsa_cmp_w1, nsa_cmp_w2, nsa_w_o, mla_q_a_norm, mla_kv_a_norm,
              mla_w_uq, mla_w_ukv, mla_q_norm, mla_k_norm, mla_w_o, w_out, ffn_norm,
              dense_w_gate, dense_w_up, dense_w_down, moe_router, moe_w_gate, moe_w_up, moe_w_down):
    B, S, D = x.shape
    split_points = [int(p) for p in np.cumsum(IN_SPLITS)[:-1]]
    for layer in range(DEPTH):
        h = rmsnorm(x, mix_norm[layer])
        proj = h @ w_in[layer]
        (sb_q, sb_k, sb_v, n_q, n_kc, n_vc, n_ks, n_vs, n_kw, n_vw, n_g,
         m_cq, m_ckv, m_kr, gate_logits) = jnp.split(proj, split_points, axis=-1)
        y_sb = stick_breaking_attention(sb_q.reshape(B, S, SB_HEADS, HEAD_DIM),
                                        sb_k.reshape(B, S, SB_HEADS, HEAD_DIM),
                                        sb_v.reshape(B, S, SB_HEADS, HEAD_DIM)) @ sb_w_o[layer]
        kv_shape = (B, S, NSA_GROUPS, HEAD_DIM)
        y_nsa = native_sparse_attention(
            n_q.reshape(B, S, NSA_HEADS, HEAD_DIM),
            n_kc.reshape(kv_shape), n_vc.reshape(kv_shape),
            n_ks.reshape(kv_shape), n_vs.reshape(kv_shape),
            n_kw.reshape(kv_shape), n_vw.reshape(kv_shape), n_g,
            nsa_q_norm[layer], nsa_k_norm[layer], nsa_cmp_pos[layer],
            nsa_cmp_w1[layer], nsa_cmp_w2[layer], rel_bias_table) @ nsa_w_o[layer]
        y_mla = multi_head_latent_attention(
            m_cq, m_ckv, m_kr, mla_q_a_norm[layer], mla_kv_a_norm[layer],
            mla_w_uq[layer], mla_w_ukv[layer], mla_q_norm[layer], mla_k_norm[layer]) @ mla_w_o[layer]
        g = jax.nn.sigmoid((gate_logits + b_gate[layer]).astype(jnp.float32)).astype(x.dtype)
        g = g.reshape(B, S, N_MIXERS, D)
        merged = g[:, :, 0] * y_sb + g[:, :, 1] * y_nsa + g[:, :, 2] * y_mla
        x = x + merged @ w_out[layer]
        h = rmsnorm(x, ffn_norm[layer])
        if layer % 2 == 0:
            j = layer // 2
            x = x + swiglu(h, dense_w_gate[j], dense_w_up[j], dense_w_down[j])
        else:
            j = layer // 2
            x = x + moe_swiglu(h, moe_router[j], moe_w_gate[j], moe_w_up[j], moe_w_down[j])
    return x
```

```python
import functools
import math

import jax
import jax.numpy as jnp
from jax import lax
from jax.experimental import pallas as pl
from jax.experimental.pallas import tpu as pltpu

F32 = jnp.float32
BF16 = jnp.bfloat16

D_MODEL = 1024
HEAD_DIM = 64
SB_HEADS = 4
NSA_HEADS = 8
NSA_GROUPS = 2
NSA_HPG = NSA_HEADS // NSA_GROUPS
NSA_BRANCHES = 3
MLA_HEADS = 4
MLA_NOPE = 64
MLA_ROPE = 32
MLA_V = 64
MLA_QK = MLA_NOPE + MLA_ROPE
MLA_Q_LORA = 256
MLA_KV_LORA = 128
ROPE_THETA = 10000.0
CMP_LEN = 32
CMP_STRIDE = 16
CMP_HIDDEN = 256
SEL_BLOCK = 64
SEL_TOPK = 16
N_LOCAL_BLOCKS = 2
WINDOW = 512
NUM_BUCKETS = 32
MAX_DISTANCE = 1024
N_MIXERS = 3
N_EXPERTS = 8
EPS = 1e-6
NEG_INF = -1e30
FORCE = 1e30

SB_W = SB_HEADS * HEAD_DIM
NSA_QW = NSA_HEADS * HEAD_DIM
NSA_KVW = NSA_GROUPS * HEAD_DIM
GATE_W = N_MIXERS * D_MODEL
LANES = 128
MLA_PAD = LANES

_C_NQ = 3 * SB_W
_C_NKV = _C_NQ + NSA_QW
_C_NG = _C_NKV + 6 * NSA_KVW
_C_MCQ = _C_NG + NSA_HEADS * NSA_BRANCHES
_C_MCKV = _C_MCQ + MLA_Q_LORA
_C_MKR = _C_MCKV + MLA_KV_LORA
_C_GATE = _C_MKR + MLA_ROPE
_A_SB = 0
_A_NQ = 3 * SB_W
_A_NKV = _A_NQ + NSA_QW
_A_MCQ = _A_NKV + 6 * NSA_KVW
_A_MCKV = _A_MCQ + MLA_Q_LORA
_A_SMALL = _A_MCKV + MLA_KV_LORA
_A_COLS = _A_SMALL + LANES
_KR_LANE = MLA_NOPE

TM_PROJ = 256
TQ_SB = 128
TQ_MLA = 256
TQ_NSA = 128
TK_SEL = 256
TM_MERGE = 256
TM_FFN = 512
TM_MOE = 512
TF_MOE = 512
VMEM_LIMIT = 56 * 1024 * 1024
SB_EXIT = -104.0

_NT = (((1,), (1,)), ((), ()))


def _dot(a, b):
    return jnp.dot(a, b, preferred_element_type=F32)


def _dot_nt(a, b):
    return lax.dot_general(a, b, _NT, preferred_element_type=F32)


def _cparams(*sem):
    return pltpu.CompilerParams(dimension_semantics=sem, vmem_limit_bytes=VMEM_LIMIT)


def _const_spec(shape):
    nd = len(shape)
    return pl.BlockSpec(shape, lambda *_: (0,) * nd)


def _smem_spec():
    return pl.BlockSpec(memory_space=pltpu.SMEM)


def _split3(x):
    h1 = x.astype(BF16)
    r1 = x - h1.astype(F32)
    h2 = r1.astype(BF16)
    h3 = (r1 - h2.astype(F32)).astype(BF16)
    return h1, h2, h3


def _bias_chain(rel, thr_ref, tab_ref, head):
    out = jnp.full(rel.shape, tab_ref[0, head], F32)
    for b in range(1, NUM_BUCKETS):
        out = jnp.where(rel >= thr_ref[b], tab_ref[b, head], out)
    return out


def _rope(y, cos, sin_lo, sin_hi):
    half = MLA_ROPE // 2
    return y * cos + pltpu.roll(y, half, 1) * sin_hi + pltpu.roll(y, MLA_PAD - half, 1) * sin_lo


def _proj_kernel(x_ref, gain_ref, wa_ref, wg_ref, bg_ref, wuq_ref, wukv_ref, nqg_ref, nkg_ref,
                 qag_ref, kvag_ref, mqg_ref, mkg_ref, cos_ref, slo_ref, shi_ref,
                 sbq_ref, sbk_ref, sbv_ref, nq_ref, nkc_ref, nvc_ref, nks_ref, nvs_ref, nkw_ref, nvw_ref,
                 gsm_ref, mq_ref, mk_ref, mv_ref, g_ref):
    x = x_ref[...]
    ms = jnp.mean(x * x, axis=-1, keepdims=True)
    h = (x * lax.rsqrt(ms + EPS) * gain_ref[...]).astype(BF16)

    def rms(p, gain, width):
        m = jnp.sum(p * p, axis=-1, keepdims=True) * (1.0 / width)
        return p * lax.rsqrt(m + EPS) * gain

    p = _dot(h, wa_ref[:, _A_SB:_A_SB + 3 * SB_W])
    for j in range(SB_HEADS):
        sbq_ref[j] = (p[:, j * HEAD_DIM:(j + 1) * HEAD_DIM] * HEAD_DIM ** -0.5).astype(BF16)
        sbk_ref[j] = p[:, SB_W + j * HEAD_DIM:SB_W + (j + 1) * HEAD_DIM].astype(BF16)
        sbv_ref[j] = p[:, 2 * SB_W + j * HEAD_DIM:2 * SB_W + (j + 1) * HEAD_DIM].astype(BF16)

    p = _dot(h, wa_ref[:, _A_NQ:_A_NQ + NSA_QW])
    for j in range(NSA_HEADS):
        ph = rms(p[:, j * HEAD_DIM:(j + 1) * HEAD_DIM], nqg_ref[...], HEAD_DIM)
        nq_ref[j] = (ph * HEAD_DIM ** -0.5).astype(BF16)

    p = _dot(h, wa_ref[:, _A_NKV:_A_NKV + 6 * NSA_KVW])
    for j in range(NSA_GROUPS):
        def seg(s):
            return p[:, s * NSA_KVW + j * HEAD_DIM:s * NSA_KVW + (j + 1) * HEAD_DIM]
        nkc_ref[j] = seg(0)
        nvc_ref[j] = seg(1)
        nks_ref[j] = rms(seg(2), nkg_ref[1:2, :], HEAD_DIM).astype(BF16)
        nvs_ref[j] = seg(3).astype(BF16)
        nkw_ref[j] = rms(seg(4), nkg_ref[2:3, :], HEAD_DIM).astype(BF16)
        nvw_ref[j] = seg(5).astype(BF16)

    p = _dot(h, wa_ref[:, _A_MCQ:_A_COLS])
    cq = rms(p[:, :MLA_Q_LORA], qag_ref[...], MLA_Q_LORA).astype(BF16)
    ckv = rms(p[:, MLA_Q_LORA:MLA_Q_LORA + MLA_KV_LORA], kvag_ref[...], MLA_KV_LORA).astype(BF16)
    small = p[:, MLA_Q_LORA + MLA_KV_LORA:]
    gsm_ref[...] = jax.nn.sigmoid(small)
    lane = lax.broadcasted_iota(jnp.int32, small.shape, 1)
    krope = jnp.where((lane >= _KR_LANE) & (lane < _KR_LANE + MLA_ROPE), small, 0.0)
    cos, slo, shi = cos_ref[...], slo_ref[...], shi_ref[...]
    qu = _dot(cq, wuq_ref[...])
    kvu = _dot(ckv, wukv_ref[...])
    for j in range(MLA_HEADS):
        qh = rms(qu[:, j * MLA_PAD:(j + 1) * MLA_PAD], mqg_ref[...], MLA_QK)
        mq_ref[j] = (_rope(qh, cos, slo, shi) * MLA_QK ** -0.5).astype(BF16)
        kh = rms(kvu[:, j * MLA_PAD:(j + 1) * MLA_PAD] + krope, mkg_ref[...], MLA_QK)
        mk_ref[j] = _rope(kh, cos, slo, shi).astype(BF16)
        v0 = MLA_HEADS * MLA_PAD + j * MLA_V
        mv_ref[j] = kvu[:, v0:v0 + MLA_V].astype(BF16)

    g_ref[...] = jax.nn.sigmoid(_dot(h, wg_ref[...]) + bg_ref[...])


def _input_projection(x, gain, wa, wg, bg, wuq, wukv, nqg, nkg, qag, kvag, mqg, mkg, cos, slo, shi):
    s = x.shape[0]
    tm = TM_PROJ
    row = lambda w: pl.BlockSpec((tm, w), lambda i: (i, 0))
    heads = lambda n, w: pl.BlockSpec((n, tm, w), lambda i: (0, i, 0))
    hs = lambda n, w, dt: jax.ShapeDtypeStruct((n, s, w), dt)
    in_specs = [row(D_MODEL)] + [_const_spec(a.shape) for a in
                                 (gain, wa, wg, bg, wuq, wukv, nqg, nkg, qag, kvag, mqg, mkg)]
    in_specs += [row(MLA_PAD)] * 3
    out_shape = (hs(SB_HEADS, HEAD_DIM, BF16),) * 3 + (hs(NSA_HEADS, HEAD_DIM, BF16),)
    out_shape += (hs(NSA_GROUPS, HEAD_DIM, F32),) * 2 + (hs(NSA_GROUPS, HEAD_DIM, BF16),) * 4
    out_shape += (jax.ShapeDtypeStruct((s, LANES), F32),)
    out_shape += (hs(MLA_HEADS, MLA_PAD, BF16),) * 2 + (hs(MLA_HEADS, MLA_V, BF16),)
    out_shape += (jax.ShapeDtypeStruct((s, GATE_W), F32),)
    out_specs = (heads(SB_HEADS, HEAD_DIM),) * 3 + (heads(NSA_HEADS, HEAD_DIM),)
    out_specs += (heads(NSA_GROUPS, HEAD_DIM),) * 6 + (row(LANES),)
    out_specs += (heads(MLA_HEADS, MLA_PAD),) * 2 + (heads(MLA_HEADS, MLA_V),) + (row(GATE_W),)
    return pl.pallas_call(
        _proj_kernel, grid=(s // tm,), in_specs=in_specs, out_specs=out_specs, out_shape=out_shape,
        compiler_params=_cparams("parallel"), name="input_projection",
    )(x, gain, wa, wg, bg, wuq, wukv, nqg, nkg, qag, kvag, mqg, mkg, cos, slo, shi)


def _sb_kernel(q_ref, k_ref, v_ref, o_ref):
    tq = TQ_SB
    i = pl.program_id(1)
    q = q_ref[0]
    row = lax.broadcasted_iota(jnp.int32, (tq, tq), 0)
    col = lax.broadcasted_iota(jnp.int32, (tq, tq), 1)
    later = (row > col).astype(BF16)
    causal = col < row

    def tile(kt, carry, acc, diagonal):
        off = pl.multiple_of(kt * tq, tq)
        k = k_ref[0, pl.ds(off, tq), :]
        v = v_ref[0, pl.ds(off, tq), :]
        z = _dot_nt(q, k)
        log_stay = -(jnp.maximum(z, 0.0) + jnp.log1p(jnp.exp(-jnp.abs(z))))
        if diagonal:
            log_stay = jnp.where(causal, log_stay, 0.0)
        hi = log_stay.astype(BF16)
        lo = (log_stay - hi.astype(F32)).astype(BF16)
        suffix = _dot(hi, later) + _dot(lo, later)
        logw = z + log_stay + (carry + suffix)
        if diagonal:
            logw = jnp.where(causal, logw, NEG_INF)
        w = jnp.exp(logw)
        acc = acc + _dot(w.astype(BF16), v)
        carry = carry + suffix[:, 0:1] + log_stay[:, 0:1]
        return carry, acc

    carry, acc = tile(i, jnp.zeros((tq, 1), F32), jnp.zeros((tq, HEAD_DIM), F32), True)

    def cond(st):
        return jnp.logical_and(st[0] >= 0, jnp.max(st[1]) > SB_EXIT)

    def body(st):
        c, a = tile(st[0], st[1], st[2], False)
        return st[0] - 1, c, a

    _, _, acc = lax.while_loop(cond, body, (i - 1, carry, acc))
    o_ref[0] = acc.astype(o_ref.dtype)


def _stick_breaking(q, k, v):
    nh, s, _ = q.shape
    tq = TQ_SB
    return pl.pallas_call(
        _sb_kernel, grid=(nh, s // tq),
        in_specs=[pl.BlockSpec((1, tq, HEAD_DIM), lambda h, i: (h, i, 0)),
                  pl.BlockSpec((1, s, HEAD_DIM), lambda h, i: (h, 0, 0)),
                  pl.BlockSpec((1, s, HEAD_DIM), lambda h, i: (h, 0, 0))],
        out_specs=pl.BlockSpec((1, tq, HEAD_DIM), lambda h, i: (h, i, 0)),
        out_shape=jax.ShapeDtypeStruct((nh, s, HEAD_DIM), BF16),
        compiler_params=_cparams("parallel", "parallel"), name="stick_breaking",
    )(q, k, v)


def _mla_kernel(q_ref, k_ref, v_ref, o_ref):
    tq = TQ_MLA
    i = pl.program_id(1)
    q = q_ref[0]

    def tile(kt, st, diagonal):
        m, l, acc = st
        off = pl.multiple_of(kt * tq, tq)
        k = k_ref[0, pl.ds(off, tq), :]
        v = v_ref[0, pl.ds(off, tq), :]
        s = _dot_nt(q, k)
        if diagonal:
            row = lax.broadcasted_iota(jnp.int32, (tq, tq), 0)
            col = lax.broadcasted_iota(jnp.int32, (tq, tq), 1)
            s = jnp.where(col <= row, s, NEG_INF)
        m_new = jnp.maximum(m, jnp.max(s, axis=-1, keepdims=True))
        p = jnp.exp(s - m_new)
        alpha = jnp.exp(m - m_new)
        l = alpha * l + jnp.sum(p, axis=-1, keepdims=True)
        acc = alpha * acc + _dot(p.astype(BF16), v)
        return m_new, l, acc

    st = (jnp.full((tq, 1), NEG_INF, F32), jnp.zeros((tq, 1), F32), jnp.zeros((tq, MLA_V), F32))
    st = lax.fori_loop(0, i, lambda kt, s_: tile(kt, s_, False), st)
    _, l, acc = tile(i, st, True)
    o_ref[0] = (acc / l).astype(o_ref.dtype)


def _mla_attention(q, k, v):
    nh, s, _ = q.shape
    tq = TQ_MLA
    return pl.pallas_call(
        _mla_kernel, grid=(nh, s // tq),
        in_specs=[pl.BlockSpec((1, tq, MLA_PAD), lambda h, i: (h, i, 0)),
                  pl.BlockSpec((1, s, MLA_PAD), lambda h, i: (h, 0, 0)),
                  pl.BlockSpec((1, s, MLA_V), lambda h, i: (h, 0, 0))],
        out_specs=pl.BlockSpec((1, tq, MLA_V), lambda h, i: (h, i, 0)),
        out_shape=jax.ShapeDtypeStruct((nh, s, MLA_V), BF16),
        compiler_params=_cparams("parallel", "parallel"), name="mla_attention",
    )(q, k, v)


def _compress_kernel(x_ref, pos_ref, w1_ref, w2_ref, kg_ref, o_ref):
    kv = pl.program_id(0)
    half = CMP_STRIDE * HEAD_DIM
    a = x_ref[0, 0].astype(BF16)
    w1 = w1_ref[0]
    first = _dot(a, w1[:half])
    second = _dot(a, w1[half:])
    n = second.shape[0]
    second = pltpu.roll(second, n - 1, 0)
    posb = jnp.broadcast_to(pos_ref[0], (8, 2 * half)).astype(BF16)
    hid = first + second + _dot(posb, w1)[0:1]
    hid = hid * jax.nn.sigmoid(hid)
    out = _dot(hid.astype(BF16), w2_ref[0])
    ms = jnp.mean(out * out, axis=-1, keepdims=True)
    normed = out * lax.rsqrt(ms + EPS) * kg_ref[...]
    o_ref[0, 0] = jnp.where(kv == 0, normed, out)


def _compress(x2, pos, w1, w2, kgain):
    _, ng, nc, cw = x2.shape
    return pl.pallas_call(
        _compress_kernel, grid=(2, ng),
        in_specs=[pl.BlockSpec((1, 1, nc, cw), lambda a, g: (a, g, 0, 0)),
                  pl.BlockSpec((1, 1, 2 * cw), lambda a, g: (a, 0, 0)),
                  pl.BlockSpec((1, 2 * cw, CMP_HIDDEN), lambda a, g: (a, 0, 0)),
                  pl.BlockSpec((1, CMP_HIDDEN, HEAD_DIM), lambda a, g: (a, 0, 0)),
                  _const_spec(kgain.shape)],
        out_specs=pl.BlockSpec((1, 1, nc, HEAD_DIM), lambda a, g: (a, g, 0, 0)),
        out_shape=jax.ShapeDtypeStruct((2, ng, nc, HEAD_DIM), F32),
        compiler_params=_cparams("parallel", "parallel"), name="nsa_compress",
    )(x2, pos, w1, w2, kgain)


def _nsa_cmp_kernel(thr_ref, tab_ref, q_ref, kc_ref, vc_ref, g_ref, oc_ref, sel_ref, band_ref, pool_ref):
    tq = TQ_NSA
    r_n = NSA_HPG
    g = pl.program_id(0)
    i = pl.program_id(1)
    nc = kc_ref.shape[2]
    nb = sel_ref.shape[2]
    cpt = tq // CMP_STRIDE
    m0 = nc - cpt

    @pl.when(i == 0)
    def _init():
        a = lax.broadcasted_iota(jnp.int32, (tq, nc), 0)
        m = lax.broadcasted_iota(jnp.int32, (tq, nc), 1)
        rel = a + CMP_STRIDE * (m0 - m) - (CMP_LEN - 1)
        for r in range(r_n):
            band_ref[r] = _bias_chain(rel, thr_ref, tab_ref, g * r_n + r)
        n = lax.broadcasted_iota(jnp.int32, (nc, nb), 0)
        b = lax.broadcasted_iota(jnp.int32, (nc, nb), 1)
        rs = SEL_BLOCK // CMP_STRIDE
        rc = CMP_LEN // CMP_STRIDE
        member = (n >= rs * b - (rc - 1)) & (n <= rs * b + rs - 1) & (n < nc - (rc - 1))
        pool_ref[...] = jnp.where(member, 1.0, 0.0).astype(BF16)

    q = q_ref[...].reshape(r_n * tq, HEAD_DIM)
    kc = kc_ref[0, 0].astype(BF16)
    vc = vc_ref[0, 0].astype(BF16)
    s = _dot_nt(q, kc).reshape(r_n, tq, nc)
    shift = (cpt * i + cpt) % nc
    qpos = i * tq + lax.broadcasted_iota(jnp.int32, (tq, nc), 0)
    cmp_end = lax.broadcasted_iota(jnp.int32, (tq, nc), 1) * CMP_STRIDE + (CMP_LEN - 1)
    valid = cmp_end <= qpos
    imp = jnp.zeros((tq, nc), F32)
    gates = g_ref[0]
    for r in range(r_n):
        sr = s[r] + pltpu.roll(band_ref[r], shift, 1)
        sr = jnp.where(valid, sr, NEG_INF)
        mx = jnp.max(sr, axis=-1, keepdims=True)
        e = jnp.where(valid, jnp.exp(sr - mx), 0.0)
        den = jnp.sum(e, axis=-1, keepdims=True)
        p = e / jnp.where(den > 0.0, den, 1.0)
        imp = imp + p
        c = NSA_BRANCHES * r
        oc_ref[r] = _dot(p.astype(BF16), vc) * gates[:, c:c + 1]

    pool = pool_ref[...]
    h1, h2, h3 = _split3(imp)
    imp_blk = _dot(h1, pool) + _dot(h2, pool) + _dot(h3, pool)
    blk = lax.broadcasted_iota(jnp.int32, (tq, nb), 1)
    qp = i * tq + lax.broadcasted_iota(jnp.int32, (tq, nb), 0)
    causal_b = blk * SEL_BLOCK <= qp
    dist = qp // SEL_BLOCK - blk
    forced = (blk == 0) | ((dist >= 0) & (dist < N_LOCAL_BLOCKS))
    score = jnp.where(causal_b, jnp.where(forced, FORCE, imp_blk), NEG_INF)
    sel = jnp.zeros((tq, nb), F32)
    for _ in range(min(SEL_TOPK, nb)):
        mx = jnp.max(score, axis=-1, keepdims=True)
        first = jnp.min(jnp.where(score == mx, blk, nb), axis=-1, keepdims=True)
        pick = blk == first
        sel = jnp.where(pick & (mx > 0.5 * NEG_INF), 1.0, sel)
        score = jnp.where(pick, -jnp.inf, score)
    sel_ref[0] = sel


def _nsa_compressed(thr, tab, nq, cmp, gates):
    nh, s, _ = nq.shape
    ng = NSA_GROUPS
    tq = TQ_NSA
    nc = cmp.shape[2]
    nb = s // SEL_BLOCK
    return pl.pallas_call(
        _nsa_cmp_kernel, grid=(ng, s // tq),
        in_specs=[_smem_spec(), _smem_spec(),
                  pl.BlockSpec((NSA_HPG, tq, HEAD_DIM), lambda g, i: (g, i, 0)),
                  pl.BlockSpec((1, 1, nc, HEAD_DIM), lambda g, i: (0, g, 0, 0)),
                  pl.BlockSpec((1, 1, nc, HEAD_DIM), lambda g, i: (1, g, 0, 0)),
                  pl.BlockSpec((1, tq, NSA_HPG * NSA_BRANCHES), lambda g, i: (g, i, 0))],
        out_specs=(pl.BlockSpec((NSA_HPG, tq, HEAD_DIM), lambda g, i: (g, i, 0)),
                   pl.BlockSpec((1, tq, nb), lambda g, i: (g, i, 0))),
        out_shape=(jax.ShapeDtypeStruct((nh, s, HEAD_DIM), F32),
                   jax.ShapeDtypeStruct((ng, s, nb), F32)),
        scratch_shapes=[pltpu.VMEM((NSA_HPG, tq, nc), F32), pltpu.VMEM((nc, nb), BF16)],
        compiler_params=_cparams("arbitrary", "arbitrary"), name="nsa_compressed",
    )(thr, tab, nq, cmp, cmp, gates)


_N_NEAR = (MAX_DISTANCE + TK_SEL - 1) // TQ_NSA + 1
_WIN_KEYS = WINDOW + TQ_NSA


def _nsa_sel_kernel(thr_ref, tab_ref, q_ref, ks_ref, vs_ref, kw_ref, vw_ref, sel_ref, g_ref, oc_ref,
                    o_ref, near_ref, win_ref):
    tq, tk, r_n = TQ_NSA, TK_SEL, NSA_HPG
    g = pl.program_id(0)
    i = pl.program_id(1)
    nb = sel_ref.shape[2]

    @pl.when(i == 0)
    def _init():
        a = lax.broadcasted_iota(jnp.int32, (tq, tk), 0)
        c = lax.broadcasted_iota(jnp.int32, (tq, tk), 1)
        for dd in range(_N_NEAR):
            for r in range(r_n):
                near_ref[dd, r] = _bias_chain(dd * tq + a - c, thr_ref, tab_ref, g * r_n + r)
        a = lax.broadcasted_iota(jnp.int32, (tq, _WIN_KEYS), 0)
        c = lax.broadcasted_iota(jnp.int32, (tq, _WIN_KEYS), 1)
        for r in range(r_n):
            win_ref[r] = _bias_chain(a - c + WINDOW, thr_ref, tab_ref, g * r_n + r)

    q = q_ref[...].reshape(r_n * tq, HEAD_DIM)
    selb = sel_ref[0].astype(BF16)
    qpos = i * tq + lax.broadcasted_iota(jnp.int32, (tq, tk), 0)
    col = lax.broadcasted_iota(jnp.int32, (tq, tk), 1)
    eb = lax.broadcasted_iota(jnp.int32, (nb, tk), 0)
    ec = lax.broadcasted_iota(jnp.int32, (nb, tk), 1)
    far_bias = [tab_ref[NUM_BUCKETS - 1, g * r_n + r] for r in range(r_n)]

    def tile(j, st, near):
        m, l, acc = st
        off = pl.multiple_of(j * tk, tk)
        k = ks_ref[0, pl.ds(off, tk), :]
        v = vs_ref[0, pl.ds(off, tk), :]
        s = _dot_nt(q, k).reshape(r_n, tq, tk)
        expand = jnp.where(eb == (off + ec) // SEL_BLOCK, 1.0, 0.0).astype(BF16)
        ok = (_dot(selb, expand) > 0.5) & (off + col <= qpos)
        if near:
            bias = near_ref[i - (tk // tq) * j]
            s = jnp.stack([jnp.where(ok, s[r] + bias[r], NEG_INF) for r in range(r_n)])
        else:
            s = jnp.stack([jnp.where(ok, s[r] + far_bias[r], NEG_INF) for r in range(r_n)])
        m_new = jnp.maximum(m, jnp.max(s, axis=-1, keepdims=True))
        p = jnp.exp(s - m_new)
        alpha = jnp.exp(m - m_new)
        l = alpha * l + jnp.sum(p, axis=-1, keepdims=True)
        pv = _dot(p.reshape(r_n * tq, tk).astype(BF16), v).reshape(r_n, tq, HEAD_DIM)
        return m_new, l, alpha * acc + pv

    st = (jnp.full((r_n, tq, 1), NEG_INF, F32), jnp.zeros((r_n, tq, 1), F32),
          jnp.zeros((r_n, tq, HEAD_DIM), F32))
    ratio = tk // tq
    n_tiles = i // ratio + 1
    n_far = jnp.maximum(0, (i - _N_NEAR) // ratio + 1)
    st = lax.fori_loop(0, n_far, lambda j, s_: tile(j, s_, False), st)
    st = lax.fori_loop(n_far, n_tiles, lambda j, s_: tile(j, s_, True), st)
    _, l, acc = st
    o_sel = acc / l

    woff = pl.multiple_of(i * tq, tq)
    kw = kw_ref[0, pl.ds(woff, _WIN_KEYS), :]
    vw = vw_ref[0, pl.ds(woff, _WIN_KEYS), :]
    sw = _dot_nt(q, kw).reshape(r_n, tq, _WIN_KEYS)
    a = lax.broadcasted_iota(jnp.int32, (tq, _WIN_KEYS), 0)
    c = lax.broadcasted_iota(jnp.int32, (tq, _WIN_KEYS), 1)
    rel = a - c + WINDOW
    okw = (rel >= 0) & (rel < WINDOW) & (i * tq - WINDOW + c >= 0)
    sw = jnp.stack([jnp.where(okw, sw[r] + win_ref[r], NEG_INF) for r in range(r_n)])
    pw = jnp.exp(sw - jnp.max(sw, axis=-1, keepdims=True))
    lw = jnp.sum(pw, axis=-1, keepdims=True)
    o_win = _dot(pw.reshape(r_n * tq, _WIN_KEYS).astype(BF16), vw).reshape(r_n, tq, HEAD_DIM) / lw

    gates = g_ref[0]
    for r in range(r_n):
        c0 = NSA_BRANCHES * r
        o = oc_ref[r] + gates[:, c0 + 1:c0 + 2] * o_sel[r] + gates[:, c0 + 2:c0 + 3] * o_win[r]
        o_ref[r] = o.astype(o_ref.dtype)


def _nsa_selected(thr, tab, nq, ks, vs, kw_pad, vw_pad, sel, gates, oc):
    nh, s, _ = nq.shape
    ng, tq = NSA_GROUPS, TQ_NSA
    nb = s // SEL_BLOCK
    qspec = pl.BlockSpec((NSA_HPG, tq, HEAD_DIM), lambda g, i: (g, i, 0))
    full = lambda rows: pl.BlockSpec((1, rows, HEAD_DIM), lambda g, i: (g, 0, 0))
    return pl.pallas_call(
        _nsa_sel_kernel, grid=(ng, s // tq),
        in_specs=[_smem_spec(), _smem_spec(), qspec, full(s), full(s), full(s + WINDOW), full(s + WINDOW),
                  pl.BlockSpec((1, tq, nb), lambda g, i: (g, i, 0)),
                  pl.BlockSpec((1, tq, NSA_HPG * NSA_BRANCHES), lambda g, i: (g, i, 0)), qspec],
        out_specs=qspec,
        out_shape=jax.ShapeDtypeStruct((nh, s, HEAD_DIM), BF16),
        scratch_shapes=[pltpu.VMEM((_N_NEAR, NSA_HPG, tq, TK_SEL), F32),
                        pltpu.VMEM((NSA_HPG, tq, _WIN_KEYS), F32)],
        compiler_params=_cparams("arbitrary", "arbitrary"), name="nsa_selected",
    )(thr, tab, nq, ks, vs, kw_pad, vw_pad, sel, gates, oc)


def _merge_body(osb_ref, onsa_ref, omla_ref, g_ref, x_ref, wsb_ref, wnsa_ref, wmla_ref, wout_ref, fg_ref):
    def heads_proj(o_ref, w_ref):
        y = _dot(o_ref[0], w_ref[0])
        for j in range(1, o_ref.shape[0]):
            y = y + _dot(o_ref[j], w_ref[j])
        return y

    merged = g_ref[:, 0:D_MODEL] * heads_proj(osb_ref, wsb_ref)
    merged = merged + g_ref[:, D_MODEL:2 * D_MODEL] * heads_proj(onsa_ref, wnsa_ref)
    merged = merged + g_ref[:, 2 * D_MODEL:3 * D_MODEL] * heads_proj(omla_ref, wmla_ref)
    x1 = x_ref[...] + _dot(merged.astype(BF16), wout_ref[...])
    ms = jnp.mean(x1 * x1, axis=-1, keepdims=True)
    return x1, x1 * lax.rsqrt(ms + EPS) * fg_ref[...]


def _merge_kernel(osb_ref, onsa_ref, omla_ref, g_ref, x_ref, wsb_ref, wnsa_ref, wmla_ref, wout_ref, fg_ref,
                  x1_ref, h_ref):
    x1, h = _merge_body(osb_ref, onsa_ref, omla_ref, g_ref, x_ref, wsb_ref, wnsa_ref, wmla_ref, wout_ref,
                        fg_ref)
    x1_ref[...] = x1
    h_ref[...] = h.astype(BF16)


def _merge_router_kernel(osb_ref, onsa_ref, omla_ref, g_ref, x_ref, wsb_ref, wnsa_ref, wmla_ref, wout_ref,
                         fg_ref, wr_ref, x1_ref, h_ref, comb_ref, sel_ref):
    x1, h = _merge_body(osb_ref, onsa_ref, omla_ref, g_ref, x_ref, wsb_ref, wnsa_ref, wmla_ref, wout_ref,
                        fg_ref)
    x1_ref[...] = x1
    h_ref[...] = h.astype(BF16)
    wr = wr_ref[...]
    hp = _split3(h)
    wp = _split3(wr)
    logits = jnp.zeros((h.shape[0], wr.shape[1]), F32)
    for a_i, b_i in ((2, 0), (0, 2), (1, 1), (1, 0), (0, 1), (0, 0)):
        logits = logits + _dot(hp[a_i], wp[b_i])
    lane = lax.broadcasted_iota(jnp.int32, logits.shape, 1)
    nl = logits.shape[1]
    logits = jnp.where(lane < N_EXPERTS, logits, -jnp.inf)
    v1 = jnp.max(logits, axis=-1, keepdims=True)
    i1 = jnp.min(jnp.where(logits == v1, lane, nl), axis=-1, keepdims=True)
    rest = jnp.where(lane == i1, -jnp.inf, logits)
    v2 = jnp.max(rest, axis=-1, keepdims=True)
    i2 = jnp.min(jnp.where(rest == v2, lane, nl), axis=-1, keepdims=True)
    e2 = jnp.exp(v2 - v1)
    w1 = 1.0 / (1.0 + e2)
    w2 = e2 / (1.0 + e2)
    comb_ref[...] = jnp.where(lane == i1, w1, jnp.where(lane == i2, w2, 0.0))
    sel_ref[...] = jnp.where((lane == i1) | (lane == i2), 1.0, 0.0)


def _merge(osb, onsa, omla, g, x, wsb, wnsa, wmla, wout, fgain, wrouter=None):
    s = x.shape[0]
    tm = TM_MERGE
    heads = lambda a: pl.BlockSpec((a.shape[0], tm, a.shape[2]), lambda i: (0, i, 0))
    row = lambda w: pl.BlockSpec((tm, w), lambda i: (i, 0))
    in_specs = [heads(osb), heads(onsa), heads(omla), row(GATE_W), row(D_MODEL)]
    in_specs += [_const_spec(a.shape) for a in (wsb, wnsa, wmla, wout, fgain)]
    out_shape = [jax.ShapeDtypeStruct((s, D_MODEL), F32), jax.ShapeDtypeStruct((s, D_MODEL), BF16)]
    out_specs = [row(D_MODEL), row(D_MODEL)]
    args = [osb, onsa, omla, g, x, wsb, wnsa, wmla, wout, fgain]
    if wrouter is None:
        kern = _merge_kernel
    else:
        kern = _merge_router_kernel
        in_specs.append(_const_spec(wrouter.shape))
        args.append(wrouter)
        out_shape += [jax.ShapeDtypeStruct((s, LANES), F32)] * 2
        out_specs += [row(LANES)] * 2
    return pl.pallas_call(
        kern, grid=(s // tm,), in_specs=in_specs, out_specs=tuple(out_specs), out_shape=tuple(out_shape),
        compiler_params=_cparams("parallel"), name="merge",
    )(*args)


def _ffn_kernel(x_ref, h_ref, wg_ref, wu_ref, wd_ref, o_ref, acc_ref):
    f = pl.program_id(1)

    @pl.when(f == 0)
    def _():
        acc_ref[...] = x_ref[...]

    h = h_ref[...]
    gate = _dot(h, wg_ref[...])
    up = _dot(h, wu_ref[...])
    act = (gate * jax.nn.sigmoid(gate) * up).astype(BF16)
    acc_ref[...] += _dot(act, wd_ref[...])

    @pl.when(f == pl.num_programs(1) - 1)
    def _():
        o_ref[...] = acc_ref[...]


def _dense_ffn(x1, h, wg, wu, wd, tf):
    s = x1.shape[0]
    tm = TM_FFN
    nf = wg.shape[1] // tf
    return pl.pallas_call(
        _ffn_kernel, grid=(s // tm, nf),
        in_specs=[pl.BlockSpec((tm, D_MODEL), lambda i, f: (i, 0)),
                  pl.BlockSpec((tm, D_MODEL), lambda i, f: (i, 0)),
                  pl.BlockSpec((D_MODEL, tf), lambda i, f: (0, f)),
                  pl.BlockSpec((D_MODEL, tf), lambda i, f: (0, f)),
                  pl.BlockSpec((tf, D_MODEL), lambda i, f: (f, 0))],
        out_specs=pl.BlockSpec((tm, D_MODEL), lambda i, f: (i, 0)),
        out_shape=jax.ShapeDtypeStruct((s, D_MODEL), F32),
        scratch_shapes=[pltpu.VMEM((tm, D_MODEL), F32)],
        compiler_params=_cparams("parallel", "arbitrary"), name="dense_ffn",
    )(x1, h, wg, wu, wd)


def _permute_kernel(tile_ref, chunk_ref, flag_ref, exp_ref, dest_ref, cw_ref, h_ref, xg_ref, rw_ref, acc_ref,
                    wacc_ref):
    k = pl.program_id(0)
    tm = TM_MOE
    flags = flag_ref[k]

    @pl.when((flags & 1) != 0)
    def _():
        acc_ref[...] = jnp.zeros_like(acc_ref)
        wacc_ref[...] = jnp.zeros_like(wacc_ref)

    @pl.when((flags & 4) != 0)
    def _():
        row = tile_ref[k] * tm + lax.broadcasted_iota(jnp.int32, (tm, tm), 0)
        hit = dest_ref[0] == row
        acc_ref[...] += _dot(jnp.where(hit, 1.0, 0.0).astype(BF16), h_ref[...])
        wacc_ref[...] += jnp.sum(jnp.where(hit, cw_ref[0], 0.0), axis=-1, keepdims=True)

    @pl.when((flags & 2) != 0)
    def _():
        xg_ref[...] = acc_ref[...].astype(BF16)
        rw_ref[...] = wacc_ref[...]


def _moe_ffn_kernel(te_ref, tv_ref, xg_ref, rw_ref, wg_ref, wu_ref, wd_ref, y_ref, acc_ref):
    t = pl.program_id(0)
    f = pl.program_id(1)

    @pl.when(tv_ref[t] != 0)
    def _():
        @pl.when(f == 0)
        def _():
            acc_ref[...] = jnp.zeros_like(acc_ref)

        x = xg_ref[...]
        gate = _dot(x, wg_ref[0])
        up = _dot(x, wu_ref[0])
        act = (gate * jax.nn.sigmoid(gate) * up * rw_ref[...]).astype(BF16)
        acc_ref[...] += _dot(act, wd_ref[0])

        @pl.when(f == pl.num_programs(1) - 1)
        def _():
            y_ref[...] = acc_ref[...].astype(BF16)

    @pl.when(tv_ref[t] == 0)
    def _():
        y_ref[...] = jnp.zeros_like(y_ref)


def _unpermute_kernel(tile_ref, chunk_ref, flag_ref, grow_ref, x_ref, y_ref, o_ref):
    k = pl.program_id(0)
    tm = TM_MOE
    flags = flag_ref[k]

    @pl.when((flags & 1) != 0)
    def _():
        o_ref[...] = x_ref[...]

    @pl.when((flags & 4) != 0)
    def _():
        row = chunk_ref[k] * tm + lax.broadcasted_iota(jnp.int32, (tm, tm), 1)
        hit = (grow_ref[:, 0:1] == row) | (grow_ref[:, 1:2] == row)
        o_ref[...] += _dot(jnp.where(hit, 1.0, 0.0).astype(BF16), y_ref[...])


def _work_list(first, last, n_items):
    n_tiles = first.shape[0]
    cnt_real = jnp.maximum(last - first + 1, 0)
    cnt = jnp.maximum(cnt_real, 1)
    ends = jnp.cumsum(cnt)
    starts = ends - cnt
    total = ends[-1]
    k = jnp.arange(n_items, dtype=jnp.int32)
    kk = jnp.minimum(k, total - 1)
    tile = jnp.minimum(jnp.searchsorted(ends, kk, side="right"), n_tiles - 1).astype(jnp.int32)
    pos = kk - starts[tile]
    chunk = jnp.where(cnt_real[tile] > 0, first[tile] + pos, 0).astype(jnp.int32)
    live = k < total
    flags = (jnp.where(live & (pos == 0), 1, 0) | jnp.where(live & (pos == cnt[tile] - 1), 2, 0)
             | jnp.where(live & (pos < cnt_real[tile]), 4, 0))
    return tile, chunk, flags.astype(jnp.int32)


def _moe_ffn(x1, h, comb, selm, wg, wu, wd):
    s = x1.shape[0]
    tm, tf = TM_MOE, TF_MOE
    ne = N_EXPERTS
    n_chunks = s // tm
    n_tiles = 2 * s // tm + ne
    rows = n_tiles * tm
    sel_t = selm[:, :ne].T.astype(jnp.int32)
    cum = jnp.cumsum(sel_t, axis=1)
    counts = cum[:, -1]
    tiles_e = (counts + tm - 1) // tm
    tile_end = jnp.cumsum(tiles_e)
    tile_start = tile_end - tiles_e
    used = tile_end[-1]
    dest = jnp.where(sel_t > 0, tile_start[:, None] * tm + cum - 1, -1).astype(jnp.int32)
    t_idx = jnp.arange(n_tiles, dtype=jnp.int32)
    te_raw = jnp.minimum(jnp.searchsorted(tile_end, t_idx, side="right"), ne - 1).astype(jnp.int32)
    tvalid = (t_idx < used).astype(jnp.int32)
    last_e = te_raw[jnp.maximum(used - 1, 0)]
    te = jnp.where(tvalid > 0, te_raw, last_e).astype(jnp.int32)
    r0 = (t_idx - tile_start[te]) * tm
    r1 = jnp.minimum(r0 + tm, counts[te])
    cum_t = cum[te]
    tok_lo = jax.vmap(lambda c, r: jnp.searchsorted(c, r, side="left"))(cum_t, r0 + 1)
    tok_hi = jax.vmap(lambda c, r: jnp.searchsorted(c, r, side="left"))(cum_t, r1)
    c_lo = jnp.where(tvalid > 0, tok_lo // tm, 0).astype(jnp.int32)
    c_hi = jnp.where(tvalid > 0, jnp.minimum(tok_hi, s - 1) // tm, -1).astype(jnp.int32)
    p_tile, p_chunk, p_flag = _work_list(c_lo, c_hi, n_tiles + ne * n_chunks)
    p_exp = te[p_tile]

    cw_t = comb[:, :ne].T.reshape(ne, 1, s)
    dest3 = dest.reshape(ne, 1, s)
    xg, roww = pl.pallas_call(
        _permute_kernel,
        grid_spec=pltpu.PrefetchScalarGridSpec(
            num_scalar_prefetch=4, grid=(p_tile.shape[0],),
            in_specs=[pl.BlockSpec((1, 1, tm), lambda k, pt, pc, pf, pe: (pe[k], 0, pc[k])),
                      pl.BlockSpec((1, 1, tm), lambda k, pt, pc, pf, pe: (pe[k], 0, pc[k])),
                      pl.BlockSpec((tm, D_MODEL), lambda k, pt, pc, pf, pe: (pc[k], 0))],
            out_specs=(pl.BlockSpec((tm, D_MODEL), lambda k, pt, pc, pf, pe: (pt[k], 0)),
                       pl.BlockSpec((tm, 1), lambda k, pt, pc, pf, pe: (pt[k], 0))),
            scratch_shapes=[pltpu.VMEM((tm, D_MODEL), F32), pltpu.VMEM((tm, 1), F32)]),
        out_shape=(jax.ShapeDtypeStruct((rows, D_MODEL), BF16), jax.ShapeDtypeStruct((rows, 1), F32)),
        compiler_params=_cparams("arbitrary"), name="moe_permute",
    )(p_tile, p_chunk, p_flag, p_exp, dest3, cw_t, h)

    nf = wg.shape[2] // tf
    last_f = nf - 1
    fsel = lambda t, f, tv: f * tv[t] + last_f * (1 - tv[t])
    yg = pl.pallas_call(
        _moe_ffn_kernel,
        grid_spec=pltpu.PrefetchScalarGridSpec(
            num_scalar_prefetch=2, grid=(n_tiles, nf),
            in_specs=[pl.BlockSpec((tm, D_MODEL), lambda t, f, te_, tv: (t, 0)),
                      pl.BlockSpec((tm, 1), lambda t, f, te_, tv: (t, 0)),
                      pl.BlockSpec((1, D_MODEL, tf), lambda t, f, te_, tv: (te_[t], 0, fsel(t, f, tv))),
                      pl.BlockSpec((1, D_MODEL, tf), lambda t, f, te_, tv: (te_[t], 0, fsel(t, f, tv))),
                      pl.BlockSpec((1, tf, D_MODEL), lambda t, f, te_, tv: (te_[t], fsel(t, f, tv), 0))],
            out_specs=pl.BlockSpec((tm, D_MODEL), lambda t, f, te_, tv: (t, 0)),
            scratch_shapes=[pltpu.VMEM((tm, D_MODEL), F32)]),
        out_shape=jax.ShapeDtypeStruct((rows, D_MODEL), BF16),
        compiler_params=_cparams("arbitrary", "arbitrary"), name="moe_ffn",
    )(te, tvalid, xg, roww, wg, wu, wd)

    e_top = jnp.argsort(-selm[:, :ne], axis=1, stable=True)[:, :2].astype(jnp.int32)
    grow = jnp.take_along_axis(dest.T, e_top, axis=1).astype(jnp.int32)
    dest_tiles = dest.reshape(ne, n_chunks, tm)
    big = jnp.int32(2 ** 30)
    lo_e = jnp.min(jnp.where(dest_tiles >= 0, dest_tiles, big), axis=2) // tm
    hi_e = jnp.max(dest_tiles, axis=2) // tm
    has = jnp.max(dest_tiles, axis=2) >= 0
    first = jnp.where(has, lo_e, 0).T.reshape(-1).astype(jnp.int32)
    last = jnp.where(has, hi_e, -1).T.reshape(-1).astype(jnp.int32)
    u_sub, u_chunk, u_flag = _work_list(first, last, n_chunks * 2 * ne)
    u_tile = (u_sub // ne).astype(jnp.int32)
    prev_tile = jnp.concatenate([jnp.full((1,), -1, jnp.int32), u_tile[:-1]])
    u_flag = (u_flag & 4) | jnp.where((u_flag != 0) & (u_tile != prev_tile), 1, 0)
    out = pl.pallas_call(
        _unpermute_kernel,
        grid_spec=pltpu.PrefetchScalarGridSpec(
            num_scalar_prefetch=3, grid=(u_tile.shape[0],),
            in_specs=[pl.BlockSpec((tm, 2), lambda k, ut, uc, uf: (ut[k], 0)),
                      pl.BlockSpec((tm, D_MODEL), lambda k, ut, uc, uf: (ut[k], 0)),
                      pl.BlockSpec((tm, D_MODEL), lambda k, ut, uc, uf: (uc[k], 0))],
            out_specs=pl.BlockSpec((tm, D_MODEL), lambda k, ut, uc, uf: (ut[k], 0))),
        out_shape=jax.ShapeDtypeStruct((s, D_MODEL), F32),
        compiler_params=_cparams("arbitrary"), name="moe_unpermute",
    )(u_tile, u_chunk, u_flag, grow, x1, yg)
    return out


def _t5_bucket(rel):
    n = jnp.maximum(rel, 0)
    max_exact = NUM_BUCKETS // 2
    nf = jnp.maximum(n, 1).astype(jnp.float32)
    large = max_exact + (jnp.log(nf / max_exact) / math.log(MAX_DISTANCE / max_exact)
                         * (NUM_BUCKETS - max_exact)).astype(jnp.int32)
    large = jnp.minimum(large, NUM_BUCKETS - 1)
    return jnp.where(n < max_exact, n, large)


def _bucket_thresholds():
    buckets = _t5_bucket(jnp.arange(MAX_DISTANCE + 1, dtype=jnp.int32))
    b = jnp.arange(NUM_BUCKETS, dtype=jnp.int32)
    return jnp.sum(buckets[None, :] < b[:, None], axis=1).astype(jnp.int32)


def _rope_tables(s):
    half = MLA_ROPE // 2
    inv = ROPE_THETA ** (-jnp.arange(half, dtype=jnp.float32) / half)
    ang = jnp.arange(s).astype(jnp.float32)[:, None] * inv[None, :]
    cos, sin = jnp.cos(ang), jnp.sin(ang)
    z = lambda w: jnp.zeros((s, w), F32)
    pad = MLA_PAD - MLA_QK
    cos_t = jnp.concatenate([jnp.ones((s, MLA_NOPE), F32), cos, cos, z(pad)], axis=1)
    sin_lo = jnp.concatenate([z(MLA_NOPE), -sin, z(half), z(pad)], axis=1)
    sin_hi = jnp.concatenate([z(MLA_NOPE), z(half), sin, z(pad)], axis=1)
    return cos_t, sin_lo, sin_hi


def _pad_lanes(a, width):
    return jnp.pad(a, [(0, 0)] * (a.ndim - 1) + [(0, width - a.shape[-1])])


def kernel(x, rel_bias_table, mix_norm, w_in, b_gate, sb_w_o, nsa_q_norm, nsa_k_norm, nsa_cmp_pos, nsa_cmp_w1, nsa_cmp_w2, nsa_w_o, mla_q_a_norm, mla_kv_a_norm, mla_w_uq, mla_w_ukv, mla_q_norm, mla_k_norm, mla_w_o, w_out, ffn_norm, dense_w_gate, dense_w_up, dense_w_down, moe_router, moe_w_gate, moe_w_up, moe_w_down):
    b, s, d = x.shape
    assert b == 1 and d == D_MODEL and s % TM_MOE == 0 and s // SEL_BLOCK >= SEL_TOPK
    depth = w_in.shape[0]
    xs = x.reshape(s, d)
    thr = _bucket_thresholds()
    cos_t, sin_lo, sin_hi = _rope_tables(s)
    n_chunk = s // CMP_STRIDE

    for layer in range(depth):
        w = w_in[layer]
        small = jnp.concatenate([
            w[:, _C_NG:_C_MCQ], jnp.zeros((d, _KR_LANE - NSA_HEADS * NSA_BRANCHES), F32),
            w[:, _C_MKR:_C_GATE], jnp.zeros((d, LANES - _KR_LANE - MLA_ROPE), F32)], axis=1)
        wa = jnp.concatenate([w[:, :_C_NG], w[:, _C_MCQ:_C_MKR], small], axis=1).astype(BF16)
        wg = w[:, _C_GATE:].astype(BF16)
        wuq = _pad_lanes(mla_w_uq[layer].reshape(MLA_Q_LORA, MLA_HEADS, MLA_QK), MLA_PAD)
        wuq = wuq.reshape(MLA_Q_LORA, MLA_HEADS * MLA_PAD).astype(BF16)
        wukv = mla_w_ukv[layer].reshape(MLA_KV_LORA, MLA_HEADS, MLA_NOPE + MLA_V)
        wukv = jnp.concatenate([
            _pad_lanes(wukv[:, :, :MLA_NOPE], MLA_PAD).reshape(MLA_KV_LORA, MLA_HEADS * MLA_PAD),
            wukv[:, :, MLA_NOPE:].reshape(MLA_KV_LORA, MLA_HEADS * MLA_V)], axis=1).astype(BF16)
        (sbq, sbk, sbv, nq, nkc, nvc, nks, nvs, nkw, nvw, gsm, mq, mk, mv, g) = _input_projection(
            xs, mix_norm[layer][None], wa, wg, b_gate[layer][None], wuq, wukv,
            nsa_q_norm[layer][None], nsa_k_norm[layer], mla_q_a_norm[layer][None],
            mla_kv_a_norm[layer][None], _pad_lanes(mla_q_norm[layer][None], MLA_PAD),
            _pad_lanes(mla_k_norm[layer][None], MLA_PAD), cos_t, sin_lo, sin_hi)

        o_sb = _stick_breaking(sbq, sbk, sbv)
        o_mla = _mla_attention(mq, mk, mv)

        x2 = jnp.stack([nkc, nvc]).reshape(2, NSA_GROUPS, n_chunk, CMP_STRIDE * HEAD_DIM)
        cmp = _compress(x2, nsa_cmp_pos[layer].reshape(2, 1, CMP_LEN * HEAD_DIM),
                        nsa_cmp_w1[layer].astype(BF16), nsa_cmp_w2[layer].astype(BF16),
                        nsa_k_norm[layer][0:1])
        gates = gsm[:, :NSA_HEADS * NSA_BRANCHES].reshape(s, NSA_GROUPS, NSA_HPG * NSA_BRANCHES)
        gates = gates.transpose(1, 0, 2)
        oc, sel = _nsa_compressed(thr, rel_bias_table, nq, cmp, gates)
        front = ((0, 0), (WINDOW, 0), (0, 0))
        o_nsa = _nsa_selected(thr, rel_bias_table, nq, nks, nvs, jnp.pad(nkw, front), jnp.pad(nvw, front),
                              sel, gates, oc)

        wsb = sb_w_o[layer].reshape(SB_HEADS, HEAD_DIM, d).astype(BF16)
        wnsa = nsa_w_o[layer].reshape(NSA_HEADS, HEAD_DIM, d).astype(BF16)
        wmla = mla_w_o[layer].reshape(MLA_HEADS, MLA_V, d).astype(BF16)
        wout = w_out[layer].astype(BF16)
        j = layer // 2
        if layer % 2 == 0:
            x1, h = _merge(o_sb, o_nsa, o_mla, g, xs, wsb, wnsa, wmla, wout, ffn_norm[layer][None])
            d_ff = dense_w_gate.shape[2]
            xs = _dense_ffn(x1, h, dense_w_gate[j].astype(BF16), dense_w_up[j].astype(BF16),
                            dense_w_down[j].astype(BF16), d_ff // 2)
        else:
            x1, h, comb, selm = _merge(o_sb, o_nsa, o_mla, g, xs, wsb, wnsa, wmla, wout,
                                       ffn_norm[layer][None], _pad_lanes(moe_router[j], LANES))
            xs = _moe_ffn(x1, h, comb, selm, moe_w_gate[j].astype(BF16), moe_w_up[j].astype(BF16),
                          moe_w_down[j].astype(BF16))
    return xs.reshape(b, s, d)
```

```python
import functools
import math

import jax
import jax.numpy as jnp
from jax import lax
from jax.experimental import pallas as pl
from jax.experimental.pallas import tpu as pltpu

F32 = jnp.float32
BF16 = jnp.bfloat16

D_MODEL = 1024
HEAD_DIM = 64
SB_HEADS = 4
NSA_HEADS = 8
NSA_GROUPS = 2
NSA_HPG = NSA_HEADS // NSA_GROUPS
NSA_BRANCHES = 3
MLA_HEADS = 4
MLA_NOPE = 64
MLA_ROPE = 32
MLA_V = 64
MLA_QK = MLA_NOPE + MLA_ROPE
MLA_Q_LORA = 256
MLA_KV_LORA = 128
ROPE_THETA = 10000.0
CMP_LEN = 32
CMP_STRIDE = 16
CMP_HIDDEN = 256
SEL_BLOCK = 64
SEL_TOPK = 16
N_LOCAL_BLOCKS = 2
WINDOW = 512
NUM_BUCKETS = 32
MAX_DISTANCE = 1024
N_MIXERS = 3
N_EXPERTS = 8
EPS = 1e-6
NEG_INF = -1e30
FORCE = 1e30
LOG2E = math.log2(math.e)

SB_W = SB_HEADS * HEAD_DIM
NSA_QW = NSA_HEADS * HEAD_DIM
NSA_KVW = NSA_GROUPS * HEAD_DIM
GATE_W = N_MIXERS * D_MODEL
LANES = 128
MLA_PAD = LANES

_C_NQ = 3 * SB_W
_C_NKV = _C_NQ + NSA_QW
_C_NG = _C_NKV + 6 * NSA_KVW
_C_MCQ = _C_NG + NSA_HEADS * NSA_BRANCHES
_C_MCKV = _C_MCQ + MLA_Q_LORA
_C_MKR = _C_MCKV + MLA_KV_LORA
_C_GATE = _C_MKR + MLA_ROPE
_A_SB = 0
_A_NQ = 3 * SB_W
_A_NKV = _A_NQ + NSA_QW
_A_MCQ = _A_NKV + 6 * NSA_KVW
_A_MCKV = _A_MCQ + MLA_Q_LORA
_A_SMALL = _A_MCKV + MLA_KV_LORA
_A_COLS = _A_SMALL + LANES
_KR_LANE = MLA_NOPE

TM_PROJ = 256
TQ_SB = 128
TQ_MLA = 256
TK_MLA = 1024
TQ_CMP = 128
TQ_SEL = 256
TK_SEL = 512
TM_MERGE = 256
TM_FFN = 512
TM_MOE = 512
TF_MOE = 512
VMEM_LIMIT = 56 * 1024 * 1024
SB_EXIT = -104.0

_NT = (((1,), (1,)), ((), ()))


def _dot(a, b):
    return jnp.dot(a, b, preferred_element_type=F32)


def _dot_nt(a, b):
    return lax.dot_general(a, b, _NT, preferred_element_type=F32)


def _cparams(*sem):
    return pltpu.CompilerParams(dimension_semantics=sem, vmem_limit_bytes=VMEM_LIMIT)


def _const_spec(shape):
    nd = len(shape)
    return pl.BlockSpec(shape, lambda *_: (0,) * nd)


def _smem_spec():
    return pl.BlockSpec(memory_space=pltpu.SMEM)


def _split3(x):
    h1 = x.astype(BF16)
    r1 = x - h1.astype(F32)
    h2 = r1.astype(BF16)
    h3 = (r1 - h2.astype(F32)).astype(BF16)
    return h1, h2, h3


def _bias_chain(rel, thr_ref, tab_ref, head):
    out = jnp.full(rel.shape, tab_ref[0, head], F32)
    for b in range(1, NUM_BUCKETS):
        out = jnp.where(rel >= thr_ref[b], tab_ref[b, head], out)
    return out


def _rope(y, cos, sin_lo, sin_hi):
    half = MLA_ROPE // 2
    return y * cos + pltpu.roll(y, half, 1) * sin_hi + pltpu.roll(y, MLA_PAD - half, 1) * sin_lo


def _proj_kernel(x_ref, gain_ref, wa_ref, wg_ref, bg_ref, wuq_ref, wukv_ref, nqg_ref, nkg_ref,
                 qag_ref, kvag_ref, mqg_ref, mkg_ref, cos_ref, slo_ref, shi_ref,
                 sbq_ref, sbk_ref, sbv_ref, nq_ref, nkc_ref, nvc_ref, nks_ref, nvs_ref, nkw_ref, nvw_ref,
                 gsm_ref, mq_ref, mk_ref, mv_ref, g_ref):
    x = x_ref[...]
    ms = jnp.mean(x * x, axis=-1, keepdims=True)
    h = (x * lax.rsqrt(ms + EPS) * gain_ref[...]).astype(BF16)

    def rms(p, gain, width):
        m = jnp.sum(p * p, axis=-1, keepdims=True) * (1.0 / width)
        return p * lax.rsqrt(m + EPS) * gain

    p = _dot(h, wa_ref[:, _A_SB:_A_SB + 3 * SB_W])
    for j in range(SB_HEADS):
        sbq_ref[j] = (p[:, j * HEAD_DIM:(j + 1) * HEAD_DIM] * HEAD_DIM ** -0.5).astype(BF16)
        sbk_ref[j] = p[:, SB_W + j * HEAD_DIM:SB_W + (j + 1) * HEAD_DIM].astype(BF16)
        sbv_ref[j] = p[:, 2 * SB_W + j * HEAD_DIM:2 * SB_W + (j + 1) * HEAD_DIM].astype(BF16)

    p = _dot(h, wa_ref[:, _A_NQ:_A_NQ + NSA_QW])
    qn = [rms(p[:, j * HEAD_DIM:(j + 1) * HEAD_DIM], nqg_ref[...], HEAD_DIM) * (HEAD_DIM ** -0.5 * LOG2E)
          for j in range(NSA_HEADS)]
    qn_t = jnp.concatenate(qn, axis=1).T.astype(BF16)
    for j in range(NSA_HEADS):
        nq_ref[j] = qn_t[j * HEAD_DIM:(j + 1) * HEAD_DIM]

    p = _dot(h, wa_ref[:, _A_NKV:_A_NKV + 6 * NSA_KVW])
    vs_t = p[:, 3 * NSA_KVW:4 * NSA_KVW].T.astype(BF16)
    tag_shape = (x.shape[0], LANES - HEAD_DIM)
    block_id = (pl.program_id(0) * x.shape[0] + lax.broadcasted_iota(jnp.int32, tag_shape, 0)) // SEL_BLOCK
    block_tag = jnp.where(lax.broadcasted_iota(jnp.int32, tag_shape, 1) == block_id % (TK_SEL // SEL_BLOCK),
                          1.0, 0.0)
    vw_t = p[:, 5 * NSA_KVW:6 * NSA_KVW].T.astype(BF16)
    for j in range(NSA_GROUPS):
        def seg(s):
            return p[:, s * NSA_KVW + j * HEAD_DIM:s * NSA_KVW + (j + 1) * HEAD_DIM]
        nkc_ref[j] = seg(0)
        nvc_ref[j] = seg(1)
        nks_ref[j] = jnp.concatenate([rms(seg(2), nkg_ref[1:2, :], HEAD_DIM), block_tag], axis=1).astype(BF16)
        nvs_ref[j] = vs_t[j * HEAD_DIM:(j + 1) * HEAD_DIM]
        nkw_ref[j] = rms(seg(4), nkg_ref[2:3, :], HEAD_DIM).astype(BF16)
        nvw_ref[j] = vw_t[j * HEAD_DIM:(j + 1) * HEAD_DIM]

    p = _dot(h, wa_ref[:, _A_MCQ:_A_COLS])
    cq = rms(p[:, :MLA_Q_LORA], qag_ref[...], MLA_Q_LORA).astype(BF16)
    ckv = rms(p[:, MLA_Q_LORA:MLA_Q_LORA + MLA_KV_LORA], kvag_ref[...], MLA_KV_LORA).astype(BF16)
    small = p[:, MLA_Q_LORA + MLA_KV_LORA:]
    gsm_ref[...] = jax.nn.sigmoid(small).T
    lane = lax.broadcasted_iota(jnp.int32, small.shape, 1)
    krope = jnp.where((lane >= _KR_LANE) & (lane < _KR_LANE + MLA_ROPE), small, 0.0)
    cos, slo, shi = cos_ref[...], slo_ref[...], shi_ref[...]
    qu = _dot(cq, wuq_ref[...])
    kvu = _dot(ckv, wukv_ref[...])
    for j in range(MLA_HEADS):
        qh = rms(qu[:, j * MLA_PAD:(j + 1) * MLA_PAD], mqg_ref[...], MLA_QK)
        mq_ref[j] = (_rope(qh, cos, slo, shi) * MLA_QK ** -0.5).T.astype(BF16)
        kh = rms(kvu[:, j * MLA_PAD:(j + 1) * MLA_PAD] + krope, mkg_ref[...], MLA_QK)
        mk_ref[j] = _rope(kh, cos, slo, shi).astype(BF16)
    v_t = kvu[:, MLA_HEADS * MLA_PAD:].T.astype(BF16)
    for j in range(MLA_HEADS):
        mv_ref[j] = v_t[j * MLA_V:(j + 1) * MLA_V]

    g_ref[...] = jax.nn.sigmoid(_dot(h, wg_ref[...]) + bg_ref[...])


def _input_projection(x, gain, wa, wg, bg, wuq, wukv, nqg, nkg, qag, kvag, mqg, mkg, cos, slo, shi):
    s = x.shape[0]
    tm = TM_PROJ
    row = lambda w: pl.BlockSpec((tm, w), lambda i: (i, 0))
    heads = lambda n, w: pl.BlockSpec((n, tm, w), lambda i: (0, i, 0))
    hs = lambda n, w, dt: jax.ShapeDtypeStruct((n, s, w), dt)
    in_specs = [row(D_MODEL)] + [_const_spec(a.shape) for a in
                                 (gain, wa, wg, bg, wuq, wukv, nqg, nkg, qag, kvag, mqg, mkg)]
    in_specs += [row(MLA_PAD)] * 3
    hst = lambda n, w, dt: jax.ShapeDtypeStruct((n, w, s), dt)
    heads_t = lambda n, w: pl.BlockSpec((n, w, tm), lambda i: (0, 0, i))
    ng, hd = NSA_GROUPS, HEAD_DIM
    out_shape = (hs(SB_HEADS, hd, BF16),) * 3 + (hst(NSA_HEADS, hd, BF16),)
    out_shape += (hs(ng, hd, F32),) * 2 + (hs(ng, LANES, BF16), hst(ng, hd, BF16), hs(ng, hd, BF16), hst(ng, hd, BF16))
    out_shape += (jax.ShapeDtypeStruct((LANES, s), F32),)
    out_shape += (hst(MLA_HEADS, MLA_PAD, BF16), hs(MLA_HEADS, MLA_PAD, BF16), hst(MLA_HEADS, MLA_V, BF16))
    out_shape += (jax.ShapeDtypeStruct((s, GATE_W), F32),)
    out_specs = (heads(SB_HEADS, hd),) * 3 + (heads_t(NSA_HEADS, hd),)
    out_specs += (heads(ng, hd),) * 2 + (heads(ng, LANES), heads_t(ng, hd), heads(ng, hd), heads_t(ng, hd))
    out_specs += (pl.BlockSpec((LANES, tm), lambda i: (0, i)),)
    out_specs += (heads_t(MLA_HEADS, MLA_PAD), heads(MLA_HEADS, MLA_PAD), heads_t(MLA_HEADS, MLA_V), row(GATE_W))
    return pl.pallas_call(
        _proj_kernel, grid=(s // tm,), in_specs=in_specs, out_specs=out_specs, out_shape=out_shape,
        compiler_params=_cparams("parallel"), name="input_projection",
    )(x, gain, wa, wg, bg, wuq, wukv, nqg, nkg, qag, kvag, mqg, mkg, cos, slo, shi)


def _sb_kernel(q_ref, k_ref, v_ref, o_ref):
    tq = TQ_SB
    i = pl.program_id(1)
    q = q_ref[0]
    row = lax.broadcasted_iota(jnp.int32, (tq, tq), 0)
    col = lax.broadcasted_iota(jnp.int32, (tq, tq), 1)
    later = (row > col).astype(BF16)
    causal = col < row

    def tile(kt, carry, acc, diagonal):
        off = pl.multiple_of(kt * tq, tq)
        k = k_ref[0, pl.ds(off, tq), :]
        v = v_ref[0, pl.ds(off, tq), :]
        z = _dot_nt(q, k)
        log_stay = -(jnp.maximum(z, 0.0) + jnp.log1p(jnp.exp(-jnp.abs(z))))
        if diagonal:
            log_stay = jnp.where(causal, log_stay, 0.0)
        hi = log_stay.astype(BF16)
        lo = (log_stay - hi.astype(F32)).astype(BF16)
        suffix = _dot(hi, later) + _dot(lo, later)
        logw = z + log_stay + (carry + suffix)
        if diagonal:
            logw = jnp.where(causal, logw, NEG_INF)
        w = jnp.exp(logw)
        acc = acc + _dot(w.astype(BF16), v)
        carry = carry + suffix[:, 0:1] + log_stay[:, 0:1]
        return carry, acc

    carry, acc = tile(i, jnp.zeros((tq, 1), F32), jnp.zeros((tq, HEAD_DIM), F32), True)

    def cond(st):
        return jnp.logical_and(st[0] >= 0, jnp.max(st[1]) > SB_EXIT)

    def body(st):
        c, a = tile(st[0], st[1], st[2], False)
        return st[0] - 1, c, a

    _, _, acc = lax.while_loop(cond, body, (i - 1, carry, acc))
    o_ref[0] = acc.astype(o_ref.dtype)


def _stick_breaking(q, k, v):
    nh, s, _ = q.shape
    tq = TQ_SB
    return pl.pallas_call(
        _sb_kernel, grid=(nh, s // tq),
        in_specs=[pl.BlockSpec((1, tq, HEAD_DIM), lambda h, i: (h, i, 0)),
                  pl.BlockSpec((1, s, HEAD_DIM), lambda h, i: (h, 0, 0)),
                  pl.BlockSpec((1, s, HEAD_DIM), lambda h, i: (h, 0, 0))],
        out_specs=pl.BlockSpec((1, tq, HEAD_DIM), lambda h, i: (h, i, 0)),
        out_shape=jax.ShapeDtypeStruct((nh, s, HEAD_DIM), BF16),
        compiler_params=_cparams("parallel", "parallel"), name="stick_breaking",
    )(q, k, v)


ONES_ROWS = 16


def _with_ones_rows(v_t):
    return jnp.concatenate([v_t, jnp.ones((ONES_ROWS, v_t.shape[1]), v_t.dtype)], axis=0)


def _pipelined_blocks(n_last, prepare, consume, bufs, st):
    buf_a, buf_b = bufs

    def pair(p, carry):
        mx_a, st_ = carry
        kb = 2 * p
        mx_b = prepare(kb + 1, buf_b, False)
        st_ = consume(kb, buf_a, mx_a, st_)
        mx_a = prepare(kb + 2, buf_a, False)
        return mx_a, consume(kb + 1, buf_b, mx_b, st_)

    trips = jnp.maximum(n_last - 1, 0) // 2
    carry = lax.fori_loop(0, trips, pair, (prepare(0, buf_a, True), st))
    kb0 = 2 * trips

    def tail0(c):
        return consume(kb0, buf_a, c[0], c[1])

    def tail1(c):
        mx_b = prepare(kb0 + 1, buf_b, True)
        st_ = consume(kb0, buf_a, c[0], c[1])
        return consume(kb0 + 1, buf_b, mx_b, st_)

    def tail2(c):
        mx_b = prepare(kb0 + 1, buf_b, False)
        st_ = consume(kb0, buf_a, c[0], c[1])
        mx_a = prepare(kb0 + 2, buf_a, True)
        st_ = consume(kb0 + 1, buf_b, mx_b, st_)
        return consume(kb0 + 2, buf_a, mx_a, st_)

    return lax.switch(n_last - kb0, (tail0, tail1, tail2), carry)


def _mla_kernel(qt_ref, k_ref, vt_ref, o_ref, sa_ref, sb_ref):
    tq, tk = TQ_MLA, TK_MLA
    i = pl.program_id(1)
    q_t = qt_ref[0]

    def prepare(kb, buf, maybe_diagonal):
        off = pl.multiple_of(kb * tk, tk)
        s = _dot(k_ref[0, pl.ds(off, tk), :], q_t)
        if maybe_diagonal:
            key = off + lax.broadcasted_iota(jnp.int32, (tk, tq), 0)
            qry = i * tq + lax.broadcasted_iota(jnp.int32, (tk, tq), 1)
            s = jnp.where(key <= qry, s, NEG_INF)
        buf[...] = s
        return jnp.max(s, axis=0, keepdims=True)

    def consume(kb, buf, mx, st):
        m, acc = st
        off = pl.multiple_of(kb * tk, tk)
        m_new = jnp.maximum(m, mx)
        p = jnp.exp(buf[...] - m_new).astype(BF16)
        return m_new, jnp.exp(m - m_new) * acc + _dot(_with_ones_rows(vt_ref[0, :, pl.ds(off, tk)]), p)

    st = (jnp.full((1, tq), NEG_INF, F32), jnp.zeros((MLA_V + ONES_ROWS, tq), F32))
    _, acc = _pipelined_blocks((i * tq) // tk, prepare, consume, (sa_ref, sb_ref), st)
    o_ref[0] = acc[:MLA_V] / acc[MLA_V:MLA_V + 1]


def _mla_attention(q_t, k, v_t):
    nh, s, _ = k.shape
    tq, tk = TQ_MLA, TK_MLA
    return pl.pallas_call(
        _mla_kernel, grid=(nh, s // tq),
        in_specs=[pl.BlockSpec((1, MLA_PAD, tq), lambda h, i: (h, 0, i)),
                  pl.BlockSpec((1, s, MLA_PAD), lambda h, i: (h, 0, 0)),
                  pl.BlockSpec((1, MLA_V, s), lambda h, i: (h, 0, 0))],
        out_specs=pl.BlockSpec((1, MLA_V, tq), lambda h, i: (h, 0, i)),
        out_shape=jax.ShapeDtypeStruct((nh, MLA_V, s), F32),
        scratch_shapes=[pltpu.VMEM((tk, tq), F32), pltpu.VMEM((tk, tq), F32)],
        compiler_params=_cparams("parallel", "parallel"), name="mla_attention",
    )(q_t, k, v_t)


def _compress_kernel(x_ref, pos_ref, w1_ref, w2_ref, kg_ref, o_ref, ot_ref):
    kv = pl.program_id(0)
    half = CMP_STRIDE * HEAD_DIM
    a = x_ref[0, 0].astype(BF16)
    w1 = w1_ref[0]
    first = _dot(a, w1[:half])
    second = _dot(a, w1[half:])
    n = second.shape[0]
    second = pltpu.roll(second, n - 1, 0)
    posb = jnp.broadcast_to(pos_ref[0], (8, 2 * half)).astype(BF16)
    hid = first + second + _dot(posb, w1)[0:1]
    hid = hid * jax.nn.sigmoid(hid)
    out = _dot(hid.astype(BF16), w2_ref[0])
    ms = jnp.mean(out * out, axis=-1, keepdims=True)
    normed = out * lax.rsqrt(ms + EPS) * kg_ref[...]
    res = jnp.where(kv == 0, normed, out)
    o_ref[0, 0] = res
    ot_ref[0, 0] = res.T


def _compress(x2, pos, w1, w2, kgain):
    _, ng, nc, cw = x2.shape
    return pl.pallas_call(
        _compress_kernel, grid=(2, ng),
        in_specs=[pl.BlockSpec((1, 1, nc, cw), lambda a, g: (a, g, 0, 0)),
                  pl.BlockSpec((1, 1, 2 * cw), lambda a, g: (a, 0, 0)),
                  pl.BlockSpec((1, 2 * cw, CMP_HIDDEN), lambda a, g: (a, 0, 0)),
                  pl.BlockSpec((1, CMP_HIDDEN, HEAD_DIM), lambda a, g: (a, 0, 0)),
                  _const_spec(kgain.shape)],
        out_specs=(pl.BlockSpec((1, 1, nc, HEAD_DIM), lambda a, g: (a, g, 0, 0)),
                   pl.BlockSpec((1, 1, HEAD_DIM, nc), lambda a, g: (a, g, 0, 0))),
        out_shape=(jax.ShapeDtypeStruct((2, ng, nc, HEAD_DIM), F32),
                   jax.ShapeDtypeStruct((2, ng, HEAD_DIM, nc), F32)),
        compiler_params=_cparams("parallel", "parallel"), name="nsa_compress",
    )(x2, pos, w1, w2, kgain)


def _nsa_cmp_kernel(thr_ref, tab_ref, qt_ref, kc_ref, vct_ref, gt_ref, oc_ref, selt_ref, band_ref, pool_ref):
    tq = TQ_CMP
    r_n = NSA_HPG
    g = pl.program_id(0)
    i = pl.program_id(1)
    nc = kc_ref.shape[2]
    nb = selt_ref.shape[1]
    cpt = tq // CMP_STRIDE

    @pl.when(i == 0)
    def _init():
        m = lax.broadcasted_iota(jnp.int32, (2 * nc, tq), 0)
        a = lax.broadcasted_iota(jnp.int32, (2 * nc, tq), 1)
        rel = a - CMP_STRIDE * (m - nc) - (CMP_LEN - 1)
        for r in range(r_n):
            band_ref[r] = _bias_chain(rel, thr_ref, tab_ref, g * r_n + r) * LOG2E
        b = lax.broadcasted_iota(jnp.int32, (nb, nc), 0)
        n = lax.broadcasted_iota(jnp.int32, (nb, nc), 1)
        rs = SEL_BLOCK // CMP_STRIDE
        rc = CMP_LEN // CMP_STRIDE
        member = (n >= rs * b - (rc - 1)) & (n <= rs * b + rs - 1) & (n < nc - (rc - 1))
        pool_ref[...] = jnp.where(member, 1.0, 0.0).astype(BF16)

    q_t = jnp.concatenate([qt_ref[r] for r in range(r_n)], axis=1)
    s = _dot(kc_ref[0, 0].astype(BF16), q_t)
    boff = pl.multiple_of(nc - cpt * i, cpt)
    qpos = i * tq + lax.broadcasted_iota(jnp.int32, (nc, tq), 1)
    cmp_end = lax.broadcasted_iota(jnp.int32, (nc, tq), 0) * CMP_STRIDE + (CMP_LEN - 1)
    valid = cmp_end <= qpos
    imp = jnp.zeros((nc, tq), F32)
    probs = []
    for r in range(r_n):
        sr = s[:, r * tq:(r + 1) * tq] + band_ref[r, pl.ds(boff, nc), :]
        sr = jnp.where(valid, sr, NEG_INF)
        mx = jnp.max(sr, axis=0, keepdims=True)
        e = jnp.where(valid, jnp.exp2(sr - mx), 0.0)
        den = jnp.sum(e, axis=0, keepdims=True)
        p = e / jnp.where(den > 0.0, den, 1.0)
        imp = imp + p
        probs.append(p.astype(BF16))
    oc = _dot(vct_ref[0, 0].astype(BF16), jnp.concatenate(probs, axis=1))
    gates = gt_ref[0]
    for r in range(r_n):
        c = NSA_BRANCHES * r
        oc_ref[r] = oc[:, r * tq:(r + 1) * tq] * gates[c:c + 1, :]

    pool = pool_ref[...]
    h1, h2, h3 = _split3(imp)
    imp_blk = _dot(pool, h1) + _dot(pool, h2) + _dot(pool, h3)
    blk = lax.broadcasted_iota(jnp.int32, (nb, tq), 0)
    qp = i * tq + lax.broadcasted_iota(jnp.int32, (nb, tq), 1)
    causal_b = blk * SEL_BLOCK <= qp
    dist = qp // SEL_BLOCK - blk
    forced = (blk == 0) | ((dist >= 0) & (dist < N_LOCAL_BLOCKS))
    score = jnp.where(causal_b, jnp.where(forced, FORCE, imp_blk), NEG_INF)
    sel = jnp.zeros((nb, tq), F32)
    for _ in range(min(SEL_TOPK, nb)):
        mx = jnp.max(score, axis=0, keepdims=True)
        first = jnp.min(jnp.where(score == mx, blk, nb), axis=0, keepdims=True)
        pick = blk == first
        sel = jnp.where(pick & (mx > 0.5 * NEG_INF), 1.0, sel)
        score = jnp.where(pick, -jnp.inf, score)
    selt_ref[0] = sel


def _nsa_compressed(thr, tab, nq_t, cmp, cmp_t, gates_t):
    nh, _, s = nq_t.shape
    ng = NSA_GROUPS
    tq = TQ_CMP
    nc = cmp.shape[2]
    nb = s // SEL_BLOCK
    qspec = pl.BlockSpec((NSA_HPG, HEAD_DIM, tq), lambda g, i: (g, 0, i))
    return pl.pallas_call(
        _nsa_cmp_kernel, grid=(ng, s // tq),
        in_specs=[_smem_spec(), _smem_spec(), qspec,
                  pl.BlockSpec((1, 1, nc, HEAD_DIM), lambda g, i: (0, g, 0, 0)),
                  pl.BlockSpec((1, 1, HEAD_DIM, nc), lambda g, i: (1, g, 0, 0)),
                  pl.BlockSpec((1, NSA_HPG * NSA_BRANCHES, tq), lambda g, i: (g, 0, i))],
        out_specs=(qspec, pl.BlockSpec((1, nb, tq), lambda g, i: (g, 0, i))),
        out_shape=(jax.ShapeDtypeStruct((nh, HEAD_DIM, s), F32),
                   jax.ShapeDtypeStruct((ng, nb, s), F32)),
        scratch_shapes=[pltpu.VMEM((NSA_HPG, 2 * nc, tq), F32), pltpu.VMEM((nb, nc), BF16)],
        compiler_params=_cparams("arbitrary", "arbitrary"), name="nsa_compressed",
    )(thr, tab, nq_t, cmp, cmp_t, gates_t)


_NEAR_SPAN = ((MAX_DISTANCE + TK_SEL - 2) // TQ_SEL) * TQ_SEL
assert _NEAR_SPAN >= MAX_DISTANCE and TK_SEL % TQ_SEL == 0
_STRIP = 2 * TK_SEL + _NEAR_SPAN
_WIN_KEYS = WINDOW + TQ_SEL
_BLOCKS_PER_TILE = TK_SEL // SEL_BLOCK


def _nsa_sel_kernel(thr_ref, tab_ref, qt_ref, ks_ref, vst_ref, kw_ref, vwt_ref, selt_ref, gt_ref, oc_ref,
                    o_ref, strip_ref, sa_ref, sb_ref):
    tq, tk, r_n = TQ_SEL, TK_SEL, NSA_HPG
    g = pl.program_id(0)
    i = pl.program_id(1)
    top = tk + _NEAR_SPAN
    lanes = r_n * tq

    @pl.when(i == 0)
    def _init():
        u = lax.broadcasted_iota(jnp.int32, (_STRIP, tq), 0)
        a = lax.broadcasted_iota(jnp.int32, (_STRIP, tq), 1)
        for r in range(r_n):
            strip_ref[:, r * tq:(r + 1) * tq] = _bias_chain(a - u + top, thr_ref, tab_ref, g * r_n + r) * LOG2E

    q_t = jnp.concatenate([qt_ref[r] for r in range(r_n)], axis=1)
    pad_rows = jnp.zeros((LANES - HEAD_DIM - _BLOCKS_PER_TILE, lanes), F32)
    qpos = i * tq + (lax.broadcasted_iota(jnp.int32, (1, lanes), 1) & (tq - 1))

    def prepare(kb, buf, maybe_diagonal):
        off = pl.multiple_of(kb * tk, tk)
        boff = pl.multiple_of(kb * _BLOCKS_PER_TILE, _BLOCKS_PER_TILE)
        picked = selt_ref[0, pl.ds(boff, _BLOCKS_PER_TILE), :]
        penalty = jnp.where(picked > 0.5, 0.0, NEG_INF)
        penalty = jnp.concatenate([penalty] * r_n, axis=1)
        q_aug = jnp.concatenate([q_t, jnp.concatenate([penalty, pad_rows], axis=0).astype(BF16)], axis=0)
        soff = pl.multiple_of(jnp.maximum(top - (i * tq - off), 0), tq)
        s = _dot(ks_ref[0, pl.ds(off, tk), :], q_aug) + strip_ref[pl.ds(soff, tk), :]
        if maybe_diagonal:
            key = off + lax.broadcasted_iota(jnp.int32, (tk, lanes), 0)
            s = jnp.where(key <= qpos, s, NEG_INF)
        buf[...] = s
        return jnp.max(s, axis=0, keepdims=True)

    def consume(kb, buf, mx, st):
        m, acc = st
        off = pl.multiple_of(kb * tk, tk)
        m_new = jnp.maximum(m, mx)
        p = jnp.exp2(buf[...] - m_new).astype(BF16)
        return m_new, jnp.exp2(m - m_new) * acc + _dot(_with_ones_rows(vst_ref[0, :, pl.ds(off, tk)]), p)

    st = (jnp.full((1, lanes), NEG_INF, F32), jnp.zeros((HEAD_DIM + ONES_ROWS, lanes), F32))
    _, acc = _pipelined_blocks((i * tq) // tk, prepare, consume, (sa_ref, sb_ref), st)
    o_sel = acc[:HEAD_DIM] / acc[HEAD_DIM:HEAD_DIM + 1]

    woff = pl.multiple_of(i * tq, tq)
    w0 = top - WINDOW
    sw = _dot(kw_ref[0, pl.ds(woff, _WIN_KEYS), :], q_t) + strip_ref[w0:w0 + _WIN_KEYS, :]
    kpos = i * tq - WINDOW + lax.broadcasted_iota(jnp.int32, (_WIN_KEYS, lanes), 0)
    rel = qpos - kpos
    sw = jnp.where((rel >= 0) & (rel < WINDOW) & (kpos >= 0), sw, NEG_INF)
    pw = jnp.exp2(sw - jnp.max(sw, axis=0, keepdims=True)).astype(BF16)
    o_win = _dot(_with_ones_rows(vwt_ref[0, :, pl.ds(woff, _WIN_KEYS)]), pw)
    o_win = o_win[:HEAD_DIM] / o_win[HEAD_DIM:HEAD_DIM + 1]

    gates = gt_ref[0]
    for r in range(r_n):
        c0 = NSA_BRANCHES * r
        sl = slice(r * tq, (r + 1) * tq)
        o_ref[r] = oc_ref[r] + gates[c0 + 1:c0 + 2, :] * o_sel[:, sl] + gates[c0 + 2:c0 + 3, :] * o_win[:, sl]


def _nsa_selected(thr, tab, nq_t, ks, vs_t, kw_pad, vw_t_pad, sel_t, gates_t, oc):
    nh, _, s = nq_t.shape
    ng, tq = NSA_GROUPS, TQ_SEL
    nb = s // SEL_BLOCK
    qspec = pl.BlockSpec((NSA_HPG, HEAD_DIM, tq), lambda g, i: (g, 0, i))
    rows = lambda n: pl.BlockSpec((1, n, HEAD_DIM), lambda g, i: (g, 0, 0), pipeline_mode=pl.Buffered(1))
    cols = lambda n: pl.BlockSpec((1, HEAD_DIM, n), lambda g, i: (g, 0, 0), pipeline_mode=pl.Buffered(1))
    return pl.pallas_call(
        _nsa_sel_kernel, grid=(ng, s // tq),
        in_specs=[_smem_spec(), _smem_spec(), qspec,
                  pl.BlockSpec((1, s, LANES), lambda g, i: (g, 0, 0), pipeline_mode=pl.Buffered(1)),
                  cols(s), rows(s + WINDOW), cols(s + WINDOW),
                  pl.BlockSpec((1, nb, tq), lambda g, i: (g, 0, i)),
                  pl.BlockSpec((1, NSA_HPG * NSA_BRANCHES, tq), lambda g, i: (g, 0, i)), qspec],
        out_specs=qspec,
        out_shape=jax.ShapeDtypeStruct((nh, HEAD_DIM, s), F32),
        scratch_shapes=[pltpu.VMEM((_STRIP, NSA_HPG * tq), F32),
                        pltpu.VMEM((TK_SEL, NSA_HPG * tq), F32), pltpu.VMEM((TK_SEL, NSA_HPG * tq), F32)],
        compiler_params=_cparams("arbitrary", "arbitrary"), name="nsa_selected",
    )(thr, tab, nq_t, ks, vs_t, kw_pad, vw_t_pad, sel_t, gates_t, oc)


def _merge_body(osb_ref, onsa_ref, omla_ref, g_ref, x_ref, wsb_ref, wnsa_ref, wmla_ref, wout_ref, fg_ref):
    def heads_proj(o_ref, w_ref):
        nh, hd = w_ref.shape[0], w_ref.shape[1]
        if o_ref.shape[1] == hd:
            o = o_ref[...].reshape(nh * hd, o_ref.shape[2]).T.astype(BF16)
            return _dot(o, w_ref[...].reshape(nh * hd, w_ref.shape[2]))
        y = _dot(o_ref[0], w_ref[0])
        for j in range(1, nh):
            y = y + _dot(o_ref[j], w_ref[j])
        return y

    merged = g_ref[:, 0:D_MODEL] * heads_proj(osb_ref, wsb_ref)
    merged = merged + g_ref[:, D_MODEL:2 * D_MODEL] * heads_proj(onsa_ref, wnsa_ref)
    merged = merged + g_ref[:, 2 * D_MODEL:3 * D_MODEL] * heads_proj(omla_ref, wmla_ref)
    x1 = x_ref[...] + _dot(merged.astype(BF16), wout_ref[...])
    ms = jnp.mean(x1 * x1, axis=-1, keepdims=True)
    return x1, x1 * lax.rsqrt(ms + EPS) * fg_ref[...]


def _merge_kernel(osb_ref, onsa_ref, omla_ref, g_ref, x_ref, wsb_ref, wnsa_ref, wmla_ref, wout_ref, fg_ref,
                  x1_ref, h_ref):
    x1, h = _merge_body(osb_ref, onsa_ref, omla_ref, g_ref, x_ref, wsb_ref, wnsa_ref, wmla_ref, wout_ref,
                        fg_ref)
    x1_ref[...] = x1
    h_ref[...] = h.astype(BF16)


def _merge_router_kernel(osb_ref, onsa_ref, omla_ref, g_ref, x_ref, wsb_ref, wnsa_ref, wmla_ref, wout_ref,
                         fg_ref, wr_ref, x1_ref, h_ref, comb_ref, sel_ref):
    x1, h = _merge_body(osb_ref, onsa_ref, omla_ref, g_ref, x_ref, wsb_ref, wnsa_ref, wmla_ref, wout_ref,
                        fg_ref)
    x1_ref[...] = x1
    h_ref[...] = h.astype(BF16)
    wr = wr_ref[...]
    hp = _split3(h)
    wp = _split3(wr)
    logits = jnp.zeros((h.shape[0], wr.shape[1]), F32)
    for a_i, b_i in ((2, 0), (0, 2), (1, 1), (1, 0), (0, 1), (0, 0)):
        logits = logits + _dot(hp[a_i], wp[b_i])
    lane = lax.broadcasted_iota(jnp.int32, logits.shape, 1)
    nl = logits.shape[1]
    logits = jnp.where(lane < N_EXPERTS, logits, -jnp.inf)
    v1 = jnp.max(logits, axis=-1, keepdims=True)
    i1 = jnp.min(jnp.where(logits == v1, lane, nl), axis=-1, keepdims=True)
    rest = jnp.where(lane == i1, -jnp.inf, logits)
    v2 = jnp.max(rest, axis=-1, keepdims=True)
    i2 = jnp.min(jnp.where(rest == v2, lane, nl), axis=-1, keepdims=True)
    e2 = jnp.exp(v2 - v1)
    w1 = 1.0 / (1.0 + e2)
    w2 = e2 / (1.0 + e2)
    comb_ref[...] = jnp.where(lane == i1, w1, jnp.where(lane == i2, w2, 0.0))
    sel_ref[...] = jnp.where((lane == i1) | (lane == i2), 1.0, 0.0)


def _merge(osb, onsa, omla, g, x, wsb, wnsa, wmla, wout, fgain, wrouter=None):
    s = x.shape[0]
    tm = TM_MERGE
    def heads(a):
        if a.shape[1] == s:
            return pl.BlockSpec((a.shape[0], tm, a.shape[2]), lambda i: (0, i, 0))
        return pl.BlockSpec((a.shape[0], a.shape[1], tm), lambda i: (0, 0, i))

    row = lambda w: pl.BlockSpec((tm, w), lambda i: (i, 0))
    in_specs = [heads(osb), heads(onsa), heads(omla), row(GATE_W), row(D_MODEL)]
    in_specs += [_const_spec(a.shape) for a in (wsb, wnsa, wmla, wout, fgain)]
    out_shape = [jax.ShapeDtypeStruct((s, D_MODEL), F32), jax.ShapeDtypeStruct((s, D_MODEL), BF16)]
    out_specs = [row(D_MODEL), row(D_MODEL)]
    args = [osb, onsa, omla, g, x, wsb, wnsa, wmla, wout, fgain]
    if wrouter is None:
        kern = _merge_kernel
    else:
        kern = _merge_router_kernel
        in_specs.append(_const_spec(wrouter.shape))
        args.append(wrouter)
        out_shape += [jax.ShapeDtypeStruct((s, LANES), F32)] * 2
        out_specs += [row(LANES)] * 2
    return pl.pallas_call(
        kern, grid=(s // tm,), in_specs=in_specs, out_specs=tuple(out_specs), out_shape=tuple(out_shape),
        compiler_params=_cparams("parallel"), name="merge",
    )(*args)


def _ffn_kernel(x_ref, h_ref, wg_ref, wu_ref, wd_ref, o_ref, acc_ref):
    f = pl.program_id(1)

    @pl.when(f == 0)
    def _():
        acc_ref[...] = x_ref[...]

    h = h_ref[...]
    gate = _dot(h, wg_ref[...])
    up = _dot(h, wu_ref[...])
    act = (gate * jax.nn.sigmoid(gate) * up).astype(BF16)
    acc_ref[...] += _dot(act, wd_ref[...])

    @pl.when(f == pl.num_programs(1) - 1)
    def _():
        o_ref[...] = acc_ref[...]


def _dense_ffn(x1, h, wg, wu, wd, tf):
    s = x1.shape[0]
    tm = TM_FFN
    nf = wg.shape[1] // tf
    return pl.pallas_call(
        _ffn_kernel, grid=(s // tm, nf),
        in_specs=[pl.BlockSpec((tm, D_MODEL), lambda i, f: (i, 0)),
                  pl.BlockSpec((tm, D_MODEL), lambda i, f: (i, 0)),
                  pl.BlockSpec((D_MODEL, tf), lambda i, f: (0, f)),
                  pl.BlockSpec((D_MODEL, tf), lambda i, f: (0, f)),
                  pl.BlockSpec((tf, D_MODEL), lambda i, f: (f, 0))],
        out_specs=pl.BlockSpec((tm, D_MODEL), lambda i, f: (i, 0)),
        out_shape=jax.ShapeDtypeStruct((s, D_MODEL), F32),
        scratch_shapes=[pltpu.VMEM((tm, D_MODEL), F32)],
        compiler_params=_cparams("parallel", "arbitrary"), name="dense_ffn",
    )(x1, h, wg, wu, wd)


def _permute_kernel(tile_ref, chunk_ref, flag_ref, exp_ref, dest_ref, cw_ref, h_ref, xg_ref, rw_ref, acc_ref,
                    wacc_ref):
    k = pl.program_id(0)
    tm = TM_MOE
    flags = flag_ref[k]

    @pl.when((flags & 1) != 0)
    def _():
        acc_ref[...] = jnp.zeros_like(acc_ref)
        wacc_ref[...] = jnp.zeros_like(wacc_ref)

    @pl.when((flags & 4) != 0)
    def _():
        row = tile_ref[k] * tm + lax.broadcasted_iota(jnp.int32, (tm, tm), 0)
        hit = dest_ref[0] == row
        acc_ref[...] += _dot(jnp.where(hit, 1.0, 0.0).astype(BF16), h_ref[...])
        wacc_ref[...] += jnp.sum(jnp.where(hit, cw_ref[0], 0.0), axis=-1, keepdims=True)

    @pl.when((flags & 2) != 0)
    def _():
        xg_ref[...] = acc_ref[...].astype(BF16)
        rw_ref[...] = wacc_ref[...]


def _moe_ffn_kernel(te_ref, tv_ref, xg_ref, rw_ref, wg_ref, wu_ref, wd_ref, y_ref, acc_ref):
    t = pl.program_id(0)
    f = pl.program_id(1)

    @pl.when(tv_ref[t] != 0)
    def _():
        @pl.when(f == 0)
        def _():
            acc_ref[...] = jnp.zeros_like(acc_ref)

        x = xg_ref[...]
        gate = _dot(x, wg_ref[0])
        up = _dot(x, wu_ref[0])
        act = (gate * jax.nn.sigmoid(gate) * up * rw_ref[...]).astype(BF16)
        acc_ref[...] += _dot(act, wd_ref[0])

        @pl.when(f == pl.num_programs(1) - 1)
        def _():
            y_ref[...] = acc_ref[...].astype(BF16)

    @pl.when(tv_ref[t] == 0)
    def _():
        y_ref[...] = jnp.zeros_like(y_ref)


def _unpermute_kernel(tile_ref, chunk_ref, flag_ref, grow_ref, x_ref, y_ref, o_ref):
    k = pl.program_id(0)
    tm = TM_MOE
    flags = flag_ref[k]

    @pl.when((flags & 1) != 0)
    def _():
        o_ref[...] = x_ref[...]

    @pl.when((flags & 4) != 0)
    def _():
        row = chunk_ref[k] * tm + lax.broadcasted_iota(jnp.int32, (tm, tm), 1)
        hit = (grow_ref[:, 0:1] == row) | (grow_ref[:, 1:2] == row)
        o_ref[...] += _dot(jnp.where(hit, 1.0, 0.0).astype(BF16), y_ref[...])


def _work_list(first, last, n_items):
    n_tiles = first.shape[0]
    cnt_real = jnp.maximum(last - first + 1, 0)
    cnt = jnp.maximum(cnt_real, 1)
    ends = jnp.cumsum(cnt)
    starts = ends - cnt
    total = ends[-1]
    k = jnp.arange(n_items, dtype=jnp.int32)
    kk = jnp.minimum(k, total - 1)
    tile = jnp.minimum(jnp.searchsorted(ends, kk, side="right"), n_tiles - 1).astype(jnp.int32)
    pos = kk - starts[tile]
    chunk = jnp.where(cnt_real[tile] > 0, first[tile] + pos, 0).astype(jnp.int32)
    live = k < total
    flags = (jnp.where(live & (pos == 0), 1, 0) | jnp.where(live & (pos == cnt[tile] - 1), 2, 0)
             | jnp.where(live & (pos < cnt_real[tile]), 4, 0))
    return tile, chunk, flags.astype(jnp.int32)


def _moe_ffn(x1, h, comb, selm, wg, wu, wd):
    s = x1.shape[0]
    tm, tf = TM_MOE, TF_MOE
    ne = N_EXPERTS
    n_chunks = s // tm
    n_tiles = 2 * s // tm + ne
    rows = n_tiles * tm
    sel_t = selm[:, :ne].T.astype(jnp.int32)
    cum = jnp.cumsum(sel_t, axis=1)
    counts = cum[:, -1]
    tiles_e = (counts + tm - 1) // tm
    tile_end = jnp.cumsum(tiles_e)
    tile_start = tile_end - tiles_e
    used = tile_end[-1]
    dest = jnp.where(sel_t > 0, tile_start[:, None] * tm + cum - 1, -1).astype(jnp.int32)
    t_idx = jnp.arange(n_tiles, dtype=jnp.int32)
    te_raw = jnp.minimum(jnp.searchsorted(tile_end, t_idx, side="right"), ne - 1).astype(jnp.int32)
    tvalid = (t_idx < used).astype(jnp.int32)
    last_e = te_raw[jnp.maximum(used - 1, 0)]
    te = jnp.where(tvalid > 0, te_raw, last_e).astype(jnp.int32)
    r0 = (t_idx - tile_start[te]) * tm
    r1 = jnp.minimum(r0 + tm, counts[te])
    cum_t = cum[te]
    tok_lo = jax.vmap(lambda c, r: jnp.searchsorted(c, r, side="left"))(cum_t, r0 + 1)
    tok_hi = jax.vmap(lambda c, r: jnp.searchsorted(c, r, side="left"))(cum_t, r1)
    c_lo = jnp.where(tvalid > 0, tok_lo // tm, 0).astype(jnp.int32)
    c_hi = jnp.where(tvalid > 0, jnp.minimum(tok_hi, s - 1) // tm, -1).astype(jnp.int32)
    p_tile, p_chunk, p_flag = _work_list(c_lo, c_hi, n_tiles + ne * n_chunks)
    p_exp = te[p_tile]

    cw_t = comb[:, :ne].T.reshape(ne, 1, s)
    dest3 = dest.reshape(ne, 1, s)
    xg, roww = pl.pallas_call(
        _permute_kernel,
        grid_spec=pltpu.PrefetchScalarGridSpec(
            num_scalar_prefetch=4, grid=(p_tile.shape[0],),
            in_specs=[pl.BlockSpec((1, 1, tm), lambda k, pt, pc, pf, pe: (pe[k], 0, pc[k])),
                      pl.BlockSpec((1, 1, tm), lambda k, pt, pc, pf, pe: (pe[k], 0, pc[k])),
                      pl.BlockSpec((tm, D_MODEL), lambda k, pt, pc, pf, pe: (pc[k], 0))],
            out_specs=(pl.BlockSpec((tm, D_MODEL), lambda k, pt, pc, pf, pe: (pt[k], 0)),
                       pl.BlockSpec((tm, 1), lambda k, pt, pc, pf, pe: (pt[k], 0))),
            scratch_shapes=[pltpu.VMEM((tm, D_MODEL), F32), pltpu.VMEM((tm, 1), F32)]),
        out_shape=(jax.ShapeDtypeStruct((rows, D_MODEL), BF16), jax.ShapeDtypeStruct((rows, 1), F32)),
        compiler_params=_cparams("arbitrary"), name="moe_permute",
    )(p_tile, p_chunk, p_flag, p_exp, dest3, cw_t, h)

    nf = wg.shape[2] // tf
    last_f = nf - 1
    fsel = lambda t, f, tv: f * tv[t] + last_f * (1 - tv[t])
    yg = pl.pallas_call(
        _moe_ffn_kernel,
        grid_spec=pltpu.PrefetchScalarGridSpec(
            num_scalar_prefetch=2, grid=(n_tiles, nf),
            in_specs=[pl.BlockSpec((tm, D_MODEL), lambda t, f, te_, tv: (t, 0)),
                      pl.BlockSpec((tm, 1), lambda t, f, te_, tv: (t, 0)),
                      pl.BlockSpec((1, D_MODEL, tf), lambda t, f, te_, tv: (te_[t], 0, fsel(t, f, tv))),
                      pl.BlockSpec((1, D_MODEL, tf), lambda t, f, te_, tv: (te_[t], 0, fsel(t, f, tv))),
                      pl.BlockSpec((1, tf, D_MODEL), lambda t, f, te_, tv: (te_[t], fsel(t, f, tv), 0))],
            out_specs=pl.BlockSpec((tm, D_MODEL), lambda t, f, te_, tv: (t, 0)),
            scratch_shapes=[pltpu.VMEM((tm, D_MODEL), F32)]),
        out_shape=jax.ShapeDtypeStruct((rows, D_MODEL), BF16),
        compiler_params=_cparams("arbitrary", "arbitrary"), name="moe_ffn",
    )(te, tvalid, xg, roww, wg, wu, wd)

    e_top = jnp.argsort(-selm[:, :ne], axis=1, stable=True)[:, :2].astype(jnp.int32)
    grow = jnp.take_along_axis(dest.T, e_top, axis=1).astype(jnp.int32)
    dest_tiles = dest.reshape(ne, n_chunks, tm)
    big = jnp.int32(2 ** 30)
    lo_e = jnp.min(jnp.where(dest_tiles >= 0, dest_tiles, big), axis=2) // tm
    hi_e = jnp.max(dest_tiles, axis=2) // tm
    has = jnp.max(dest_tiles, axis=2) >= 0
    first = jnp.where(has, lo_e, 0).T.reshape(-1).astype(jnp.int32)
    last = jnp.where(has, hi_e, -1).T.reshape(-1).astype(jnp.int32)
    u_sub, u_chunk, u_flag = _work_list(first, last, n_chunks * 2 * ne)
    u_tile = (u_sub // ne).astype(jnp.int32)
    prev_tile = jnp.concatenate([jnp.full((1,), -1, jnp.int32), u_tile[:-1]])
    u_flag = (u_flag & 4) | jnp.where((u_flag != 0) & (u_tile != prev_tile), 1, 0)
    out = pl.pallas_call(
        _unpermute_kernel,
        grid_spec=pltpu.PrefetchScalarGridSpec(
            num_scalar_prefetch=3, grid=(u_tile.shape[0],),
            in_specs=[pl.BlockSpec((tm, 2), lambda k, ut, uc, uf: (ut[k], 0)),
                      pl.BlockSpec((tm, D_MODEL), lambda k, ut, uc, uf: (ut[k], 0)),
                      pl.BlockSpec((tm, D_MODEL), lambda k, ut, uc, uf: (uc[k], 0))],
            out_specs=pl.BlockSpec((tm, D_MODEL), lambda k, ut, uc, uf: (ut[k], 0))),
        out_shape=jax.ShapeDtypeStruct((s, D_MODEL), F32),
        compiler_params=_cparams("arbitrary"), name="moe_unpermute",
    )(u_tile, u_chunk, u_flag, grow, x1, yg)
    return out


def _t5_bucket(rel):
    n = jnp.maximum(rel, 0)
    max_exact = NUM_BUCKETS // 2
    nf = jnp.maximum(n, 1).astype(jnp.float32)
    large = max_exact + (jnp.log(nf / max_exact) / math.log(MAX_DISTANCE / max_exact)
                         * (NUM_BUCKETS - max_exact)).astype(jnp.int32)
    large = jnp.minimum(large, NUM_BUCKETS - 1)
    return jnp.where(n < max_exact, n, large)


def _bucket_thresholds():
    buckets = _t5_bucket(jnp.arange(MAX_DISTANCE + 1, dtype=jnp.int32))
    b = jnp.arange(NUM_BUCKETS, dtype=jnp.int32)
    return jnp.sum(buckets[None, :] < b[:, None], axis=1).astype(jnp.int32)


def _rope_tables(s):
    half = MLA_ROPE // 2
    inv = ROPE_THETA ** (-jnp.arange(half, dtype=jnp.float32) / half)
    ang = jnp.arange(s).astype(jnp.float32)[:, None] * inv[None, :]
    cos, sin = jnp.cos(ang), jnp.sin(ang)
    z = lambda w: jnp.zeros((s, w), F32)
    pad = MLA_PAD - MLA_QK
    cos_t = jnp.concatenate([jnp.ones((s, MLA_NOPE), F32), cos, cos, z(pad)], axis=1)
    sin_lo = jnp.concatenate([z(MLA_NOPE), -sin, z(half), z(pad)], axis=1)
    sin_hi = jnp.concatenate([z(MLA_NOPE), z(half), sin, z(pad)], axis=1)
    return cos_t, sin_lo, sin_hi


def _pad_lanes(a, width):
    return jnp.pad(a, [(0, 0)] * (a.ndim - 1) + [(0, width - a.shape[-1])])


def kernel(x, rel_bias_table, mix_norm, w_in, b_gate, sb_w_o, nsa_q_norm, nsa_k_norm, nsa_cmp_pos, nsa_cmp_w1, nsa_cmp_w2, nsa_w_o, mla_q_a_norm, mla_kv_a_norm, mla_w_uq, mla_w_ukv, mla_q_norm, mla_k_norm, mla_w_o, w_out, ffn_norm, dense_w_gate, dense_w_up, dense_w_down, moe_router, moe_w_gate, moe_w_up, moe_w_down):
    b, s, d = x.shape
    assert b == 1 and d == D_MODEL and s % TM_MOE == 0 and s // SEL_BLOCK >= SEL_TOPK
    depth = w_in.shape[0]
    xs = x.reshape(s, d)
    thr = _bucket_thresholds()
    cos_t, sin_lo, sin_hi = _rope_tables(s)
    n_chunk = s // CMP_STRIDE

    for layer in range(depth):
        w = w_in[layer]
        small = jnp.concatenate([
            w[:, _C_NG:_C_MCQ], jnp.zeros((d, _KR_LANE - NSA_HEADS * NSA_BRANCHES), F32),
            w[:, _C_MKR:_C_GATE], jnp.zeros((d, LANES - _KR_LANE - MLA_ROPE), F32)], axis=1)
        wa = jnp.concatenate([w[:, :_C_NG], w[:, _C_MCQ:_C_MKR], small], axis=1).astype(BF16)
        wg = w[:, _C_GATE:].astype(BF16)
        wuq = _pad_lanes(mla_w_uq[layer].reshape(MLA_Q_LORA, MLA_HEADS, MLA_QK), MLA_PAD)
        wuq = wuq.reshape(MLA_Q_LORA, MLA_HEADS * MLA_PAD).astype(BF16)
        wukv = mla_w_ukv[layer].reshape(MLA_KV_LORA, MLA_HEADS, MLA_NOPE + MLA_V)
        wukv = jnp.concatenate([
            _pad_lanes(wukv[:, :, :MLA_NOPE], MLA_PAD).reshape(MLA_KV_LORA, MLA_HEADS * MLA_PAD),
            wukv[:, :, MLA_NOPE:].reshape(MLA_KV_LORA, MLA_HEADS * MLA_V)], axis=1).astype(BF16)
        (sbq, sbk, sbv, nq, nkc, nvc, nks, nvs, nkw, nvw, gsm, mq, mk, mv, g) = _input_projection(
            xs, mix_norm[layer][None], wa, wg, b_gate[layer][None], wuq, wukv,
            nsa_q_norm[layer][None], nsa_k_norm[layer], mla_q_a_norm[layer][None],
            mla_kv_a_norm[layer][None], _pad_lanes(mla_q_norm[layer][None], MLA_PAD),
            _pad_lanes(mla_k_norm[layer][None], MLA_PAD), cos_t, sin_lo, sin_hi)

        o_sb = _stick_breaking(sbq, sbk, sbv)
        o_mla = _mla_attention(mq, mk, mv)

        x2 = jnp.stack([nkc, nvc]).reshape(2, NSA_GROUPS, n_chunk, CMP_STRIDE * HEAD_DIM)
        cmp, cmp_t = _compress(x2, nsa_cmp_pos[layer].reshape(2, 1, CMP_LEN * HEAD_DIM),
                               nsa_cmp_w1[layer].astype(BF16), nsa_cmp_w2[layer].astype(BF16),
                               nsa_k_norm[layer][0:1])
        gates = gsm[:NSA_HEADS * NSA_BRANCHES].reshape(NSA_GROUPS, NSA_HPG * NSA_BRANCHES, s)
        oc, sel = _nsa_compressed(thr, rel_bias_table, nq, cmp, cmp_t, gates)
        o_nsa = _nsa_selected(thr, rel_bias_table, nq, nks, nvs,
                              jnp.pad(nkw, ((0, 0), (WINDOW, 0), (0, 0))),
                              jnp.pad(nvw, ((0, 0), (0, 0), (WINDOW, 0))), sel, gates, oc)

        wsb = sb_w_o[layer].reshape(SB_HEADS, HEAD_DIM, d).astype(BF16)
        wnsa = nsa_w_o[layer].reshape(NSA_HEADS, HEAD_DIM, d).astype(BF16)
        wmla = mla_w_o[layer].reshape(MLA_HEADS, MLA_V, d).astype(BF16)
        wout = w_out[layer].astype(BF16)
        j = layer // 2
        if layer % 2 == 0:
            x1, h = _merge(o_sb, o_nsa, o_mla, g, xs, wsb, wnsa, wmla, wout, ffn_norm[layer][None])
            d_ff = dense_w_gate.shape[2]
            xs = _dense_ffn(x1, h, dense_w_gate[j].astype(BF16), dense_w_up[j].astype(BF16),
                            dense_w_down[j].astype(BF16), d_ff // 2)
        else:
            x1, h, comb, selm = _merge(o_sb, o_nsa, o_mla, g, xs, wsb, wnsa, wmla, wout,
                                       ffn_norm[layer][None], _pad_lanes(moe_router[j], LANES))
            xs = _moe_ffn(x1, h, comb, selm, moe_w_gate[j].astype(BF16), moe_w_up[j].astype(BF16),
                          moe_w_down[j].astype(BF16))
    return xs.reshape(b, s, d)
```

```python
import functools
import math

import jax
import jax.numpy as jnp
from jax import lax
from jax.experimental import pallas as pl
from jax.experimental.pallas import tpu as pltpu

F32 = jnp.float32
BF16 = jnp.bfloat16

D_MODEL = 1024
HEAD_DIM = 64
SB_HEADS = 4
NSA_HEADS = 8
NSA_GROUPS = 2
NSA_HPG = NSA_HEADS // NSA_GROUPS
NSA_BRANCHES = 3
MLA_HEADS = 4
MLA_NOPE = 64
MLA_ROPE = 32
MLA_V = 64
MLA_QK = MLA_NOPE + MLA_ROPE
MLA_Q_LORA = 256
MLA_KV_LORA = 128
ROPE_THETA = 10000.0
CMP_LEN = 32
CMP_STRIDE = 16
CMP_HIDDEN = 256
SEL_BLOCK = 64
SEL_TOPK = 16
N_LOCAL_BLOCKS = 2
WINDOW = 512
NUM_BUCKETS = 32
MAX_DISTANCE = 1024
N_MIXERS = 3
N_EXPERTS = 8
EPS = 1e-6
NEG_INF = -1e30
FORCE = 1e30
LOG2E = math.log2(math.e)

SB_W = SB_HEADS * HEAD_DIM
NSA_QW = NSA_HEADS * HEAD_DIM
NSA_KVW = NSA_GROUPS * HEAD_DIM
GATE_W = N_MIXERS * D_MODEL
LANES = 128
MLA_PAD = LANES

_C_NQ = 3 * SB_W
_C_NKV = _C_NQ + NSA_QW
_C_NG = _C_NKV + 6 * NSA_KVW
_C_MCQ = _C_NG + NSA_HEADS * NSA_BRANCHES
_C_MCKV = _C_MCQ + MLA_Q_LORA
_C_MKR = _C_MCKV + MLA_KV_LORA
_C_GATE = _C_MKR + MLA_ROPE
_A_SB = 0
_A_NQ = 3 * SB_W
_A_NKV = _A_NQ + NSA_QW
_A_MCQ = _A_NKV + 6 * NSA_KVW
_A_MCKV = _A_MCQ + MLA_Q_LORA
_A_SMALL = _A_MCKV + MLA_KV_LORA
_A_COLS = _A_SMALL + LANES
_KR_LANE = MLA_NOPE

TM_PROJ = 256
TQ_SB = 256
TQ_MLA = 256
TK_MLA = 1024
TQ_CMP = 128
CMP_PARTS = 4
TQ_SEL = 256
TK_SEL = 512
TM_MERGE = 256
TM_FFN = 512
TM_MOE = 512
TF_MOE = 512
VMEM_LIMIT = 56 * 1024 * 1024
SB_EXIT = -104.0

_NT = (((1,), (1,)), ((), ()))


def _dot(a, b):
    return jnp.dot(a, b, preferred_element_type=F32)


def _dot_nt(a, b):
    return lax.dot_general(a, b, _NT, preferred_element_type=F32)


def _cparams(*sem):
    return pltpu.CompilerParams(dimension_semantics=sem, vmem_limit_bytes=VMEM_LIMIT)


def _const_spec(shape):
    nd = len(shape)
    return pl.BlockSpec(shape, lambda *_: (0,) * nd)


def _smem_spec():
    return pl.BlockSpec(memory_space=pltpu.SMEM)


def _split3(x):
    h1 = x.astype(BF16)
    r1 = x - h1.astype(F32)
    h2 = r1.astype(BF16)
    h3 = (r1 - h2.astype(F32)).astype(BF16)
    return h1, h2, h3


def _bias_chain(rel, thr_ref, tab_ref, head):
    out = jnp.full(rel.shape, tab_ref[0, head], F32)
    for b in range(1, NUM_BUCKETS):
        out = jnp.where(rel >= thr_ref[b], tab_ref[b, head], out)
    return out


def _rope(y, cos, sin_lo, sin_hi):
    half = MLA_ROPE // 2
    return y * cos + pltpu.roll(y, half, 1) * sin_hi + pltpu.roll(y, MLA_PAD - half, 1) * sin_lo


def _proj_kernel(x_ref, gain_ref, wa_ref, wg_ref, bg_ref, wuq_ref, wukv_ref, nqg_ref, nkg_ref,
                 qag_ref, kvag_ref, mqg_ref, mkg_ref, cos_ref, slo_ref, shi_ref,
                 sbq_ref, sbk_ref, sbv_ref, nq_ref, nkc_ref, nvc_ref, nks_ref, nvs_ref, nkw_ref, nvw_ref,
                 gsm_ref, mq_ref, mk_ref, mv_ref, g_ref):
    x = x_ref[...]
    ms = jnp.mean(x * x, axis=-1, keepdims=True)
    h = (x * lax.rsqrt(ms + EPS) * gain_ref[...]).astype(BF16)

    def rms(p, gain, width):
        m = jnp.sum(p * p, axis=-1, keepdims=True) * (1.0 / width)
        return p * lax.rsqrt(m + EPS) * gain

    p = _dot(h, wa_ref[:, _A_SB:_A_SB + 3 * SB_W])
    sbq_t = (p[:, :SB_W] * HEAD_DIM ** -0.5).T.astype(BF16)
    sbv_t = p[:, 2 * SB_W:3 * SB_W].T.astype(BF16)
    for j in range(SB_HEADS):
        sbq_ref[j] = sbq_t[j * HEAD_DIM:(j + 1) * HEAD_DIM]
        sbv_ref[j] = sbv_t[j * HEAD_DIM:(j + 1) * HEAD_DIM]
    for j in range(SB_HEADS // 2):
        sbk_ref[j] = p[:, SB_W + j * LANES:SB_W + (j + 1) * LANES].astype(BF16)

    p = _dot(h, wa_ref[:, _A_NQ:_A_NQ + NSA_QW])
    qn = [rms(p[:, j * HEAD_DIM:(j + 1) * HEAD_DIM], nqg_ref[...], HEAD_DIM) * (HEAD_DIM ** -0.5 * LOG2E)
          for j in range(NSA_HEADS)]
    qn_t = jnp.concatenate(qn, axis=1).T.astype(BF16)
    for j in range(NSA_HEADS):
        nq_ref[j] = qn_t[j * HEAD_DIM:(j + 1) * HEAD_DIM]

    p = _dot(h, wa_ref[:, _A_NKV:_A_NKV + 6 * NSA_KVW])
    vs_t = p[:, 3 * NSA_KVW:4 * NSA_KVW].T.astype(BF16)
    tag_shape = (x.shape[0], LANES - HEAD_DIM)
    block_id = (pl.program_id(0) * x.shape[0] + lax.broadcasted_iota(jnp.int32, tag_shape, 0)) // SEL_BLOCK
    block_tag = jnp.where(lax.broadcasted_iota(jnp.int32, tag_shape, 1) == block_id % (TK_SEL // SEL_BLOCK),
                          1.0, 0.0)
    vw_t = p[:, 5 * NSA_KVW:6 * NSA_KVW].T.astype(BF16)
    for j in range(NSA_GROUPS):
        def seg(s):
            return p[:, s * NSA_KVW + j * HEAD_DIM:s * NSA_KVW + (j + 1) * HEAD_DIM]
        nkc_ref[j] = seg(0)
        nvc_ref[j] = seg(1)
        nks_ref[j] = jnp.concatenate([rms(seg(2), nkg_ref[1:2, :], HEAD_DIM), block_tag], axis=1).astype(BF16)
        nvs_ref[j] = vs_t[j * HEAD_DIM:(j + 1) * HEAD_DIM]
        nkw_ref[j] = rms(seg(4), nkg_ref[2:3, :], HEAD_DIM).astype(BF16)
        nvw_ref[j] = vw_t[j * HEAD_DIM:(j + 1) * HEAD_DIM]

    p = _dot(h, wa_ref[:, _A_MCQ:_A_COLS])
    cq = rms(p[:, :MLA_Q_LORA], qag_ref[...], MLA_Q_LORA).astype(BF16)
    ckv = rms(p[:, MLA_Q_LORA:MLA_Q_LORA + MLA_KV_LORA], kvag_ref[...], MLA_KV_LORA).astype(BF16)
    small = p[:, MLA_Q_LORA + MLA_KV_LORA:]
    gsm_ref[...] = jax.nn.sigmoid(small).T
    lane = lax.broadcasted_iota(jnp.int32, small.shape, 1)
    krope = jnp.where((lane >= _KR_LANE) & (lane < _KR_LANE + MLA_ROPE), small, 0.0)
    cos, slo, shi = cos_ref[...], slo_ref[...], shi_ref[...]
    qu = _dot(cq, wuq_ref[...])
    kvu = _dot(ckv, wukv_ref[...])
    for j in range(MLA_HEADS):
        qh = rms(qu[:, j * MLA_PAD:(j + 1) * MLA_PAD], mqg_ref[...], MLA_QK)
        mq_ref[j] = (_rope(qh, cos, slo, shi) * MLA_QK ** -0.5).T.astype(BF16)
        kh = rms(kvu[:, j * MLA_PAD:(j + 1) * MLA_PAD] + krope, mkg_ref[...], MLA_QK)
        mk_ref[j] = _rope(kh, cos, slo, shi).astype(BF16)
    v_t = kvu[:, MLA_HEADS * MLA_PAD:].T.astype(BF16)
    for j in range(MLA_HEADS):
        mv_ref[j] = v_t[j * MLA_V:(j + 1) * MLA_V]

    g_ref[...] = jax.nn.sigmoid(_dot(h, wg_ref[...]) + bg_ref[...])


def _input_projection(x, gain, wa, wg, bg, wuq, wukv, nqg, nkg, qag, kvag, mqg, mkg, cos, slo, shi):
    s = x.shape[0]
    tm = TM_PROJ
    row = lambda w: pl.BlockSpec((tm, w), lambda i: (i, 0))
    heads = lambda n, w: pl.BlockSpec((n, tm, w), lambda i: (0, i, 0))
    hs = lambda n, w, dt: jax.ShapeDtypeStruct((n, s, w), dt)
    in_specs = [row(D_MODEL)] + [_const_spec(a.shape) for a in
                                 (gain, wa, wg, bg, wuq, wukv, nqg, nkg, qag, kvag, mqg, mkg)]
    in_specs += [row(MLA_PAD)] * 3
    hst = lambda n, w, dt: jax.ShapeDtypeStruct((n, w, s), dt)
    heads_t = lambda n, w: pl.BlockSpec((n, w, tm), lambda i: (0, 0, i))
    ng, hd = NSA_GROUPS, HEAD_DIM
    out_shape = (hst(SB_HEADS, hd, BF16), hs(SB_HEADS // 2, LANES, BF16), hst(SB_HEADS, hd, BF16))
    out_shape += (hst(NSA_HEADS, hd, BF16),)
    out_shape += (hs(ng, hd, F32),) * 2 + (hs(ng, LANES, BF16), hst(ng, hd, BF16), hs(ng, hd, BF16), hst(ng, hd, BF16))
    out_shape += (jax.ShapeDtypeStruct((LANES, s), F32),)
    out_shape += (hst(MLA_HEADS, MLA_PAD, BF16), hs(MLA_HEADS, MLA_PAD, BF16), hst(MLA_HEADS, MLA_V, BF16))
    out_shape += (jax.ShapeDtypeStruct((s, GATE_W), F32),)
    out_specs = (heads_t(SB_HEADS, hd), heads(SB_HEADS // 2, LANES), heads_t(SB_HEADS, hd))
    out_specs += (heads_t(NSA_HEADS, hd),)
    out_specs += (heads(ng, hd),) * 2 + (heads(ng, LANES), heads_t(ng, hd), heads(ng, hd), heads_t(ng, hd))
    out_specs += (pl.BlockSpec((LANES, tm), lambda i: (0, i)),)
    out_specs += (heads_t(MLA_HEADS, MLA_PAD), heads(MLA_HEADS, MLA_PAD), heads_t(MLA_HEADS, MLA_V), row(GATE_W))
    return pl.pallas_call(
        _proj_kernel, grid=(s // tm,), in_specs=in_specs, out_specs=out_specs, out_shape=out_shape,
        compiler_params=_cparams("parallel"), name="input_projection",
    )(x, gain, wa, wg, bg, wuq, wukv, nqg, nkg, qag, kvag, mqg, mkg, cos, slo, shi)


def _sb_kernel(qt_ref, k2_ref, vt_ref, o_ref):
    tq = TQ_SB
    nh = qt_ref.shape[0]
    i = pl.program_id(0)
    key = lax.broadcasted_iota(jnp.int32, (tq, tq), 0)
    qry = lax.broadcasted_iota(jnp.int32, (tq, tq), 1)
    later = (qry > key).astype(BF16)
    causal = key < qry
    zeros = jnp.zeros((HEAD_DIM, tq), BF16)
    q_aug = [jnp.concatenate([qt_ref[h], zeros] if h % 2 == 0 else [zeros, qt_ref[h]], axis=0) for h in range(nh)]

    def tile(kt, h, carry, acc, diagonal):
        off = pl.multiple_of(kt * tq, tq)
        z = _dot(k2_ref[h // 2, pl.ds(off, tq), :], q_aug[h])
        log_stay = -(jnp.maximum(z, 0.0) + jnp.log(1.0 + jnp.exp(-jnp.abs(z))))
        if diagonal:
            log_stay = jnp.where(causal, log_stay, 0.0)
        hi = log_stay.astype(BF16)
        lo = (log_stay - hi.astype(F32)).astype(BF16)
        suffix = _dot(later, hi) + _dot(later, lo)
        logw = z + log_stay + (carry + suffix)
        if diagonal:
            logw = jnp.where(causal, logw, NEG_INF)
        w = jnp.exp(logw).astype(BF16)
        acc = acc + _dot(vt_ref[h, :, pl.ds(off, tq)], w)
        carry = carry + suffix[0:1, :] + log_stay[0:1, :]
        return carry, acc

    def all_heads(kt, st, diagonal):
        out = [tile(kt, h, st[2 * h], st[2 * h + 1], diagonal) for h in range(nh)]
        return tuple(x for pair in out for x in pair)

    st = all_heads(i, (jnp.zeros((1, tq), F32), jnp.zeros((HEAD_DIM, tq), F32)) * nh, True)

    def cond(c):
        top = c[1]
        for h in range(1, nh):
            top = jnp.maximum(top, c[1 + 2 * h])
        return jnp.logical_and(c[0] >= 0, jnp.max(top) > SB_EXIT)

    def body(c):
        return (c[0] - 1,) + all_heads(c[0], c[1:], False)

    c = lax.while_loop(cond, body, (i - 1,) + st)
    for h in range(nh):
        o_ref[h] = c[2 + 2 * h]


def _stick_breaking(q_t, k2, v_t):
    nh, _, s = q_t.shape
    tq = TQ_SB
    whole = lambda a: pl.BlockSpec(a.shape, lambda i: (0, 0, 0), pipeline_mode=pl.Buffered(1))
    return pl.pallas_call(
        _sb_kernel, grid=(s // tq,),
        in_specs=[pl.BlockSpec((nh, HEAD_DIM, tq), lambda i: (0, 0, i)), whole(k2), whole(v_t)],
        out_specs=pl.BlockSpec((nh, HEAD_DIM, tq), lambda i: (0, 0, i)),
        out_shape=jax.ShapeDtypeStruct((nh, HEAD_DIM, s), F32),
        compiler_params=_cparams("parallel"), name="stick_breaking",
    )(q_t, k2, v_t)


ONES_ROWS = 16


def _with_ones_rows(v_t):
    return jnp.concatenate([v_t, jnp.ones((ONES_ROWS, v_t.shape[1]), v_t.dtype)], axis=0)


def _pipelined_blocks(n_last, prepare, consume, bufs, st):
    buf_a, buf_b = bufs

    def pair(p, carry):
        mx_a, st_ = carry
        kb = 2 * p
        mx_b = prepare(kb + 1, buf_b, False)
        st_ = consume(kb, buf_a, mx_a, st_)
        mx_a = prepare(kb + 2, buf_a, False)
        return mx_a, consume(kb + 1, buf_b, mx_b, st_)

    trips = jnp.maximum(n_last - 1, 0) // 2
    carry = lax.fori_loop(0, trips, pair, (prepare(0, buf_a, True), st))
    kb0 = 2 * trips

    def tail0(c):
        return consume(kb0, buf_a, c[0], c[1])

    def tail1(c):
        mx_b = prepare(kb0 + 1, buf_b, True)
        st_ = consume(kb0, buf_a, c[0], c[1])
        return consume(kb0 + 1, buf_b, mx_b, st_)

    def tail2(c):
        mx_b = prepare(kb0 + 1, buf_b, False)
        st_ = consume(kb0, buf_a, c[0], c[1])
        mx_a = prepare(kb0 + 2, buf_a, True)
        st_ = consume(kb0 + 1, buf_b, mx_b, st_)
        return consume(kb0 + 2, buf_a, mx_a, st_)

    return lax.switch(n_last - kb0, (tail0, tail1, tail2), carry)


def _mla_kernel(qt_ref, k_ref, vt_ref, o_ref, sa_ref, sb_ref):
    tq, tk = TQ_MLA, TK_MLA
    i = pl.program_id(1)
    q_t = qt_ref[0]

    def prepare(kb, buf, maybe_diagonal):
        off = pl.multiple_of(kb * tk, tk)
        s = _dot(k_ref[0, pl.ds(off, tk), :], q_t)
        if maybe_diagonal:
            key = off + lax.broadcasted_iota(jnp.int32, (tk, tq), 0)
            qry = i * tq + lax.broadcasted_iota(jnp.int32, (tk, tq), 1)
            s = jnp.where(key <= qry, s, NEG_INF)
        buf[...] = s
        return jnp.max(s, axis=0, keepdims=True)

    def consume(kb, buf, mx, st):
        m, acc = st
        off = pl.multiple_of(kb * tk, tk)
        m_new = jnp.maximum(m, mx)
        p = jnp.exp(buf[...] - m_new).astype(BF16)
        return m_new, jnp.exp(m - m_new) * acc + _dot(_with_ones_rows(vt_ref[0, :, pl.ds(off, tk)]), p)

    st = (jnp.full((1, tq), NEG_INF, F32), jnp.zeros((MLA_V + ONES_ROWS, tq), F32))
    _, acc = _pipelined_blocks((i * tq) // tk, prepare, consume, (sa_ref, sb_ref), st)
    o_ref[0] = acc[:MLA_V] / acc[MLA_V:MLA_V + 1]


def _mla_attention(q_t, k, v_t):
    nh, s, _ = k.shape
    tq, tk = TQ_MLA, TK_MLA
    return pl.pallas_call(
        _mla_kernel, grid=(nh, s // tq),
        in_specs=[pl.BlockSpec((1, MLA_PAD, tq), lambda h, i: (h, 0, i)),
                  pl.BlockSpec((1, s, MLA_PAD), lambda h, i: (h, 0, 0)),
                  pl.BlockSpec((1, MLA_V, s), lambda h, i: (h, 0, 0))],
        out_specs=pl.BlockSpec((1, MLA_V, tq), lambda h, i: (h, 0, i)),
        out_shape=jax.ShapeDtypeStruct((nh, MLA_V, s), F32),
        scratch_shapes=[pltpu.VMEM((tk, tq), F32), pltpu.VMEM((tk, tq), F32)],
        compiler_params=_cparams("parallel", "parallel"), name="mla_attention",
    )(q_t, k, v_t)


def _compress_kernel(x_ref, pos_ref, w1_ref, w2_ref, kg_ref, o_ref, ot_ref):
    kv = pl.program_id(0)
    half = CMP_STRIDE * HEAD_DIM
    a = x_ref[0, 0].astype(BF16)
    w1 = w1_ref[0]
    first = _dot(a, w1[:half])
    second = _dot(a, w1[half:])
    n = second.shape[0]
    second = pltpu.roll(second, n - 1, 0)
    posb = jnp.broadcast_to(pos_ref[0], (8, 2 * half)).astype(BF16)
    hid = first + second + _dot(posb, w1)[0:1]
    hid = hid * jax.nn.sigmoid(hid)
    out = _dot(hid.astype(BF16), w2_ref[0])
    ms = jnp.mean(out * out, axis=-1, keepdims=True)
    normed = out * lax.rsqrt(ms + EPS) * kg_ref[...]
    res = jnp.where(kv == 0, normed, out)
    o_ref[0, 0] = res
    ot_ref[0, 0] = res.T


def _compress(x2, pos, w1, w2, kgain):
    _, ng, nc, cw = x2.shape
    return pl.pallas_call(
        _compress_kernel, grid=(2, ng),
        in_specs=[pl.BlockSpec((1, 1, nc, cw), lambda a, g: (a, g, 0, 0)),
                  pl.BlockSpec((1, 1, 2 * cw), lambda a, g: (a, 0, 0)),
                  pl.BlockSpec((1, 2 * cw, CMP_HIDDEN), lambda a, g: (a, 0, 0)),
                  pl.BlockSpec((1, CMP_HIDDEN, HEAD_DIM), lambda a, g: (a, 0, 0)),
                  _const_spec(kgain.shape)],
        out_specs=(pl.BlockSpec((1, 1, nc, HEAD_DIM), lambda a, g: (a, g, 0, 0)),
                   pl.BlockSpec((1, 1, HEAD_DIM, nc), lambda a, g: (a, g, 0, 0))),
        out_shape=(jax.ShapeDtypeStruct((2, ng, nc, HEAD_DIM), F32),
                   jax.ShapeDtypeStruct((2, ng, HEAD_DIM, nc), F32)),
        compiler_params=_cparams("parallel", "parallel"), name="nsa_compress",
    )(x2, pos, w1, w2, kgain)


def _nsa_cmp_kernel(thr_ref, tab_ref, qt_ref, kc_ref, vct_ref, gt_ref, oc_ref, selt_ref, band_ref, pool_ref):
    tq = TQ_CMP
    r_n = NSA_HPG
    g = pl.program_id(0)
    i = pl.program_id(1)
    nc = kc_ref.shape[2]
    nb = selt_ref.shape[1]
    cpt = tq // CMP_STRIDE

    @pl.when(i == 0)
    def _init():
        m = lax.broadcasted_iota(jnp.int32, (2 * nc, tq), 0)
        a = lax.broadcasted_iota(jnp.int32, (2 * nc, tq), 1)
        rel = a - CMP_STRIDE * (m - nc) - (CMP_LEN - 1)
        for r in range(r_n):
            band_ref[r] = _bias_chain(rel, thr_ref, tab_ref, g * r_n + r) * LOG2E
        b = lax.broadcasted_iota(jnp.int32, (nb, nc), 0)
        n = lax.broadcasted_iota(jnp.int32, (nb, nc), 1)
        rs = SEL_BLOCK // CMP_STRIDE
        rc = CMP_LEN // CMP_STRIDE
        member = (n >= rs * b - (rc - 1)) & (n <= rs * b + rs - 1) & (n < nc - (rc - 1))
        pool_ref[...] = jnp.where(member, 1.0, 0.0).astype(BF16)

    q_t = jnp.concatenate([qt_ref[r] for r in range(r_n)], axis=1)
    boff = pl.multiple_of(nc - cpt * i, cpt)
    gates = gt_ref[0]

    def tile_body(nc_e, nb_e):
        s = _dot(kc_ref[0, 0, :nc_e, :].astype(BF16), q_t)
        qpos = i * tq + lax.broadcasted_iota(jnp.int32, (nc_e, tq), 1)
        cmp_end = lax.broadcasted_iota(jnp.int32, (nc_e, tq), 0) * CMP_STRIDE + (CMP_LEN - 1)
        valid = cmp_end <= qpos
        imp = jnp.zeros((nc_e, tq), F32)
        probs = []
        for r in range(r_n):
            sr = s[:, r * tq:(r + 1) * tq] + band_ref[r, pl.ds(boff, nc_e), :]
            sr = jnp.where(valid, sr, NEG_INF)
            mx = jnp.max(sr, axis=0, keepdims=True)
            e = jnp.where(valid, jnp.exp2(sr - mx), 0.0)
            den = jnp.sum(e, axis=0, keepdims=True)
            p = e / jnp.where(den > 0.0, den, 1.0)
            imp = imp + p
            probs.append(p.astype(BF16))
        oc = _dot(vct_ref[0, 0, :, :nc_e].astype(BF16), jnp.concatenate(probs, axis=1))
        for r in range(r_n):
            c = NSA_BRANCHES * r
            oc_ref[r] = oc[:, r * tq:(r + 1) * tq] * gates[c:c + 1, :]

        pool = pool_ref[:nb_e, :nc_e]
        h1, h2, h3 = _split3(imp)
        imp_blk = _dot(pool, h1) + _dot(pool, h2) + _dot(pool, h3)
        blk = lax.broadcasted_iota(jnp.int32, (nb_e, tq), 0)
        qp = i * tq + lax.broadcasted_iota(jnp.int32, (nb_e, tq), 1)
        causal_b = blk * SEL_BLOCK <= qp
        dist = qp // SEL_BLOCK - blk
        forced = (blk == 0) | ((dist >= 0) & (dist < N_LOCAL_BLOCKS))
        score = jnp.where(causal_b, jnp.where(forced, FORCE, imp_blk), NEG_INF)
        sel = jnp.zeros((nb_e, tq), F32)
        for _ in range(min(SEL_TOPK, nb_e)):
            mx = jnp.max(score, axis=0, keepdims=True)
            first = jnp.min(jnp.where(score == mx, blk, nb_e), axis=0, keepdims=True)
            pick = blk == first
            sel = jnp.where(pick & (mx > 0.5 * NEG_INF), 1.0, sel)
            score = jnp.where(pick, -jnp.inf, score)
        selt_ref[0, :nb_e, :] = sel
        if nb_e < nb:
            selt_ref[0, nb_e:, :] = jnp.zeros((nb - nb_e, tq), F32)

    tiles_per_part = pl.num_programs(1) // CMP_PARTS
    part = i // tiles_per_part
    for j in range(CMP_PARTS):
        pl.when(part == j)(functools.partial(tile_body, (j + 1) * nc // CMP_PARTS, (j + 1) * nb // CMP_PARTS))


def _nsa_compressed(thr, tab, nq_t, cmp, cmp_t, gates_t):
    nh, _, s = nq_t.shape
    ng = NSA_GROUPS
    tq = TQ_CMP
    nc = cmp.shape[2]
    nb = s // SEL_BLOCK
    qspec = pl.BlockSpec((NSA_HPG, HEAD_DIM, tq), lambda g, i: (g, 0, i))
    return pl.pallas_call(
        _nsa_cmp_kernel, grid=(ng, s // tq),
        in_specs=[_smem_spec(), _smem_spec(), qspec,
                  pl.BlockSpec((1, 1, nc, HEAD_DIM), lambda g, i: (0, g, 0, 0)),
                  pl.BlockSpec((1, 1, HEAD_DIM, nc), lambda g, i: (1, g, 0, 0)),
                  pl.BlockSpec((1, NSA_HPG * NSA_BRANCHES, tq), lambda g, i: (g, 0, i))],
        out_specs=(qspec, pl.BlockSpec((1, nb, tq), lambda g, i: (g, 0, i))),
        out_shape=(jax.ShapeDtypeStruct((nh, HEAD_DIM, s), F32),
                   jax.ShapeDtypeStruct((ng, nb, s), F32)),
        scratch_shapes=[pltpu.VMEM((NSA_HPG, 2 * nc, tq), F32), pltpu.VMEM((nb, nc), BF16)],
        compiler_params=_cparams("arbitrary", "arbitrary"), name="nsa_compressed",
    )(thr, tab, nq_t, cmp, cmp_t, gates_t)


_NEAR_SPAN = ((MAX_DISTANCE + TK_SEL - 2) // TQ_SEL) * TQ_SEL
assert _NEAR_SPAN >= MAX_DISTANCE and TK_SEL % TQ_SEL == 0
_STRIP = 2 * TK_SEL + _NEAR_SPAN
_WIN_KEYS = WINDOW + TQ_SEL
_BLOCKS_PER_TILE = TK_SEL // SEL_BLOCK


def _nsa_sel_kernel(thr_ref, tab_ref, qt_ref, ks_ref, vst_ref, kw_ref, vwt_ref, selt_ref, gt_ref, oc_ref,
                    o_ref, strip_ref, sa_ref, sb_ref):
    tq, tk, r_n = TQ_SEL, TK_SEL, NSA_HPG
    g = pl.program_id(0)
    i = pl.program_id(1)
    top = tk + _NEAR_SPAN
    lanes = r_n * tq

    @pl.when(i == 0)
    def _init():
        u = lax.broadcasted_iota(jnp.int32, (_STRIP, tq), 0)
        a = lax.broadcasted_iota(jnp.int32, (_STRIP, tq), 1)
        for r in range(r_n):
            strip_ref[:, r * tq:(r + 1) * tq] = _bias_chain(a - u + top, thr_ref, tab_ref, g * r_n + r) * LOG2E

    q_t = jnp.concatenate([qt_ref[r] for r in range(r_n)], axis=1)
    pad_rows = jnp.zeros((LANES - HEAD_DIM - _BLOCKS_PER_TILE, lanes), F32)
    qpos = i * tq + (lax.broadcasted_iota(jnp.int32, (1, lanes), 1) & (tq - 1))

    def prepare(kb, buf, maybe_diagonal):
        off = pl.multiple_of(kb * tk, tk)
        boff = pl.multiple_of(kb * _BLOCKS_PER_TILE, _BLOCKS_PER_TILE)
        picked = selt_ref[0, pl.ds(boff, _BLOCKS_PER_TILE), :]
        penalty = jnp.where(picked > 0.5, 0.0, NEG_INF)
        penalty = jnp.concatenate([penalty] * r_n, axis=1)
        q_aug = jnp.concatenate([q_t, jnp.concatenate([penalty, pad_rows], axis=0).astype(BF16)], axis=0)
        soff = pl.multiple_of(jnp.maximum(top - (i * tq - off), 0), tq)
        s = _dot(ks_ref[0, pl.ds(off, tk), :], q_aug) + strip_ref[pl.ds(soff, tk), :]
        if maybe_diagonal:
            key = off + lax.broadcasted_iota(jnp.int32, (tk, lanes), 0)
            s = jnp.where(key <= qpos, s, NEG_INF)
        buf[...] = s
        return jnp.max(s, axis=0, keepdims=True)

    def consume(kb, buf, mx, st):
        m, acc = st
        off = pl.multiple_of(kb * tk, tk)
        m_new = jnp.maximum(m, mx)
        p = jnp.exp2(buf[...] - m_new).astype(BF16)
        return m_new, jnp.exp2(m - m_new) * acc + _dot(_with_ones_rows(vst_ref[0, :, pl.ds(off, tk)]), p)

    st = (jnp.full((1, lanes), NEG_INF, F32), jnp.zeros((HEAD_DIM + ONES_ROWS, lanes), F32))
    _, acc = _pipelined_blocks((i * tq) // tk, prepare, consume, (sa_ref, sb_ref), st)
    o_sel = acc[:HEAD_DIM] / acc[HEAD_DIM:HEAD_DIM + 1]

    woff = pl.multiple_of(i * tq, tq)
    w0 = top - WINDOW
    sw = _dot(kw_ref[0, pl.ds(woff, _WIN_KEYS), :], q_t) + strip_ref[w0:w0 + _WIN_KEYS, :]
    kpos = i * tq - WINDOW + lax.broadcasted_iota(jnp.int32, (_WIN_KEYS, lanes), 0)
    rel = qpos - kpos
    sw = jnp.where((rel >= 0) & (rel < WINDOW) & (kpos >= 0), sw, NEG_INF)
    pw = jnp.exp2(sw - jnp.max(sw, axis=0, keepdims=True)).astype(BF16)
    o_win = _dot(_with_ones_rows(vwt_ref[0, :, pl.ds(woff, _WIN_KEYS)]), pw)
    o_win = o_win[:HEAD_DIM] / o_win[HEAD_DIM:HEAD_DIM + 1]

    gates = gt_ref[0]
    for r in range(r_n):
        c0 = NSA_BRANCHES * r
        sl = slice(r * tq, (r + 1) * tq)
        o_ref[r] = oc_ref[r] + gates[c0 + 1:c0 + 2, :] * o_sel[:, sl] + gates[c0 + 2:c0 + 3, :] * o_win[:, sl]


def _nsa_selected(thr, tab, nq_t, ks, vs_t, kw_pad, vw_t_pad, sel_t, gates_t, oc):
    nh, _, s = nq_t.shape
    ng, tq = NSA_GROUPS, TQ_SEL
    nb = s // SEL_BLOCK
    qspec = pl.BlockSpec((NSA_HPG, HEAD_DIM, tq), lambda g, i: (g, 0, i))
    rows = lambda n: pl.BlockSpec((1, n, HEAD_DIM), lambda g, i: (g, 0, 0), pipeline_mode=pl.Buffered(1))
    cols = lambda n: pl.BlockSpec((1, HEAD_DIM, n), lambda g, i: (g, 0, 0), pipeline_mode=pl.Buffered(1))
    return pl.pallas_call(
        _nsa_sel_kernel, grid=(ng, s // tq),
        in_specs=[_smem_spec(), _smem_spec(), qspec,
                  pl.BlockSpec((1, s, LANES), lambda g, i: (g, 0, 0), pipeline_mode=pl.Buffered(1)),
                  cols(s), rows(s + WINDOW), cols(s + WINDOW),
                  pl.BlockSpec((1, nb, tq), lambda g, i: (g, 0, i)),
                  pl.BlockSpec((1, NSA_HPG * NSA_BRANCHES, tq), lambda g, i: (g, 0, i)), qspec],
        out_specs=qspec,
        out_shape=jax.ShapeDtypeStruct((nh, HEAD_DIM, s), F32),
        scratch_shapes=[pltpu.VMEM((_STRIP, NSA_HPG * tq), F32),
                        pltpu.VMEM((TK_SEL, NSA_HPG * tq), F32), pltpu.VMEM((TK_SEL, NSA_HPG * tq), F32)],
        compiler_params=_cparams("arbitrary", "arbitrary"), name="nsa_selected",
    )(thr, tab, nq_t, ks, vs_t, kw_pad, vw_t_pad, sel_t, gates_t, oc)


def _merge_body(osb_ref, onsa_ref, omla_ref, g_ref, x_ref, wsb_ref, wnsa_ref, wmla_ref, wout_ref, fg_ref):
    def heads_proj(o_ref, w_ref):
        nh, hd = w_ref.shape[0], w_ref.shape[1]
        if o_ref.shape[1] == hd:
            o = o_ref[...].reshape(nh * hd, o_ref.shape[2]).T.astype(BF16)
            return _dot(o, w_ref[...].reshape(nh * hd, w_ref.shape[2]))
        y = _dot(o_ref[0], w_ref[0])
        for j in range(1, nh):
            y = y + _dot(o_ref[j], w_ref[j])
        return y

    merged = g_ref[:, 0:D_MODEL] * heads_proj(osb_ref, wsb_ref)
    merged = merged + g_ref[:, D_MODEL:2 * D_MODEL] * heads_proj(onsa_ref, wnsa_ref)
    merged = merged + g_ref[:, 2 * D_MODEL:3 * D_MODEL] * heads_proj(omla_ref, wmla_ref)
    x1 = x_ref[...] + _dot(merged.astype(BF16), wout_ref[...])
    ms = jnp.mean(x1 * x1, axis=-1, keepdims=True)
    return x1, x1 * lax.rsqrt(ms + EPS) * fg_ref[...]


def _merge_kernel(osb_ref, onsa_ref, omla_ref, g_ref, x_ref, wsb_ref, wnsa_ref, wmla_ref, wout_ref, fg_ref,
                  x1_ref, h_ref):
    x1, h = _merge_body(osb_ref, onsa_ref, omla_ref, g_ref, x_ref, wsb_ref, wnsa_ref, wmla_ref, wout_ref,
                        fg_ref)
    x1_ref[...] = x1
    h_ref[...] = h.astype(BF16)


def _merge_router_kernel(osb_ref, onsa_ref, omla_ref, g_ref, x_ref, wsb_ref, wnsa_ref, wmla_ref, wout_ref,
                         fg_ref, wr_ref, x1_ref, h_ref, comb_ref, sel_ref):
    x1, h = _merge_body(osb_ref, onsa_ref, omla_ref, g_ref, x_ref, wsb_ref, wnsa_ref, wmla_ref, wout_ref,
                        fg_ref)
    x1_ref[...] = x1
    h_ref[...] = h.astype(BF16)
    wr = wr_ref[...]
    hp = _split3(h)
    wp = _split3(wr)
    logits = jnp.zeros((h.shape[0], wr.shape[1]), F32)
    for a_i, b_i in ((2, 0), (0, 2), (1, 1), (1, 0), (0, 1), (0, 0)):
        logits = logits + _dot(hp[a_i], wp[b_i])
    lane = lax.broadcasted_iota(jnp.int32, logits.shape, 1)
    nl = logits.shape[1]
    logits = jnp.where(lane < N_EXPERTS, logits, -jnp.inf)
    v1 = jnp.max(logits, axis=-1, keepdims=True)
    i1 = jnp.min(jnp.where(logits == v1, lane, nl), axis=-1, keepdims=True)
    rest = jnp.where(lane == i1, -jnp.inf, logits)
    v2 = jnp.max(rest, axis=-1, keepdims=True)
    i2 = jnp.min(jnp.where(rest == v2, lane, nl), axis=-1, keepdims=True)
    e2 = jnp.exp(v2 - v1)
    w1 = 1.0 / (1.0 + e2)
    w2 = e2 / (1.0 + e2)
    comb_ref[...] = jnp.where(lane == i1, w1, jnp.where(lane == i2, w2, 0.0))
    sel_ref[...] = jnp.where((lane == i1) | (lane == i2), 1.0, 0.0)


def _merge(osb, onsa, omla, g, x, wsb, wnsa, wmla, wout, fgain, wrouter=None):
    s = x.shape[0]
    tm = TM_MERGE
    def heads(a):
        if a.shape[1] == s:
            return pl.BlockSpec((a.shape[0], tm, a.shape[2]), lambda i: (0, i, 0))
        return pl.BlockSpec((a.shape[0], a.shape[1], tm), lambda i: (0, 0, i))

    row = lambda w: pl.BlockSpec((tm, w), lambda i: (i, 0))
    in_specs = [heads(osb), heads(onsa), heads(omla), row(GATE_W), row(D_MODEL)]
    in_specs += [_const_spec(a.shape) for a in (wsb, wnsa, wmla, wout, fgain)]
    out_shape = [jax.ShapeDtypeStruct((s, D_MODEL), F32), jax.ShapeDtypeStruct((s, D_MODEL), BF16)]
    out_specs = [row(D_MODEL), row(D_MODEL)]
    args = [osb, onsa, omla, g, x, wsb, wnsa, wmla, wout, fgain]
    if wrouter is None:
        kern = _merge_kernel
    else:
        kern = _merge_router_kernel
        in_specs.append(_const_spec(wrouter.shape))
        args.append(wrouter)
        out_shape += [jax.ShapeDtypeStruct((s, LANES), F32)] * 2
        out_specs += [row(LANES)] * 2
    return pl.pallas_call(
        kern, grid=(s // tm,), in_specs=in_specs, out_specs=tuple(out_specs), out_shape=tuple(out_shape),
        compiler_params=_cparams("parallel"), name="merge",
    )(*args)


def _ffn_kernel(x_ref, h_ref, wg_ref, wu_ref, wd_ref, o_ref, acc_ref):
    f = pl.program_id(1)

    @pl.when(f == 0)
    def _():
        acc_ref[...] = x_ref[...]

    h = h_ref[...]
    gate = _dot(h, wg_ref[...])
    up = _dot(h, wu_ref[...])
    act = (gate * jax.nn.sigmoid(gate) * up).astype(BF16)
    acc_ref[...] += _dot(act, wd_ref[...])

    @pl.when(f == pl.num_programs(1) - 1)
    def _():
        o_ref[...] = acc_ref[...]


def _dense_ffn(x1, h, wg, wu, wd, tf):
    s = x1.shape[0]
    tm = TM_FFN
    nf = wg.shape[1] // tf
    return pl.pallas_call(
        _ffn_kernel, grid=(s // tm, nf),
        in_specs=[pl.BlockSpec((tm, D_MODEL), lambda i, f: (i, 0)),
                  pl.BlockSpec((tm, D_MODEL), lambda i, f: (i, 0)),
                  pl.BlockSpec((D_MODEL, tf), lambda i, f: (0, f)),
                  pl.BlockSpec((D_MODEL, tf), lambda i, f: (0, f)),
                  pl.BlockSpec((tf, D_MODEL), lambda i, f: (f, 0))],
        out_specs=pl.BlockSpec((tm, D_MODEL), lambda i, f: (i, 0)),
        out_shape=jax.ShapeDtypeStruct((s, D_MODEL), F32),
        scratch_shapes=[pltpu.VMEM((tm, D_MODEL), F32)],
        compiler_params=_cparams("parallel", "arbitrary"), name="dense_ffn",
    )(x1, h, wg, wu, wd)


def _permute_kernel(tile_ref, chunk_ref, flag_ref, exp_ref, dest_ref, cw_ref, h_ref, xg_ref, rw_ref, acc_ref,
                    wacc_ref):
    k = pl.program_id(0)
    tm = TM_MOE
    flags = flag_ref[k]

    @pl.when((flags & 1) != 0)
    def _():
        acc_ref[...] = jnp.zeros_like(acc_ref)
        wacc_ref[...] = jnp.zeros_like(wacc_ref)

    @pl.when((flags & 4) != 0)
    def _():
        row = tile_ref[k] * tm + lax.broadcasted_iota(jnp.int32, (tm, tm), 0)
        hit = dest_ref[0] == row
        acc_ref[...] += _dot(jnp.where(hit, 1.0, 0.0).astype(BF16), h_ref[...])
        wacc_ref[...] += jnp.sum(jnp.where(hit, cw_ref[0], 0.0), axis=-1, keepdims=True)

    @pl.when((flags & 2) != 0)
    def _():
        xg_ref[...] = acc_ref[...].astype(BF16)
        rw_ref[...] = wacc_ref[...]


def _moe_ffn_kernel(te_ref, tv_ref, xg_ref, rw_ref, wg_ref, wu_ref, wd_ref, y_ref, acc_ref):
    t = pl.program_id(0)
    f = pl.program_id(1)

    @pl.when(tv_ref[t] != 0)
    def _():
        @pl.when(f == 0)
        def _():
            acc_ref[...] = jnp.zeros_like(acc_ref)

        x = xg_ref[...]
        gate = _dot(x, wg_ref[0])
        up = _dot(x, wu_ref[0])
        act = (gate * jax.nn.sigmoid(gate) * up * rw_ref[...]).astype(BF16)
        acc_ref[...] += _dot(act, wd_ref[0])

        @pl.when(f == pl.num_programs(1) - 1)
        def _():
            y_ref[...] = acc_ref[...].astype(BF16)

    @pl.when(tv_ref[t] == 0)
    def _():
        y_ref[...] = jnp.zeros_like(y_ref)


def _unpermute_kernel(tile_ref, chunk_ref, flag_ref, grow_ref, x_ref, y_ref, o_ref):
    k = pl.program_id(0)
    tm = TM_MOE
    flags = flag_ref[k]

    @pl.when((flags & 1) != 0)
    def _():
        o_ref[...] = x_ref[...]

    @pl.when((flags & 4) != 0)
    def _():
        row = chunk_ref[k] * tm + lax.broadcasted_iota(jnp.int32, (tm, tm), 1)
        hit = (grow_ref[:, 0:1] == row) | (grow_ref[:, 1:2] == row)
        o_ref[...] += _dot(jnp.where(hit, 1.0, 0.0).astype(BF16), y_ref[...])


def _work_list(first, last, n_items):
    n_tiles = first.shape[0]
    cnt_real = jnp.maximum(last - first + 1, 0)
    cnt = jnp.maximum(cnt_real, 1)
    ends = jnp.cumsum(cnt)
    starts = ends - cnt
    total = ends[-1]
    k = jnp.arange(n_items, dtype=jnp.int32)
    kk = jnp.minimum(k, total - 1)
    tile = jnp.minimum(jnp.sum(kk[:, None] >= ends[None, :], axis=1), n_tiles - 1).astype(jnp.int32)
    pos = kk - starts[tile]
    chunk = jnp.where(cnt_real[tile] > 0, first[tile] + pos, 0).astype(jnp.int32)
    live = k < total
    flags = (jnp.where(live & (pos == 0), 1, 0) | jnp.where(live & (pos == cnt[tile] - 1), 2, 0)
             | jnp.where(live & (pos < cnt_real[tile]), 4, 0))
    return tile, chunk, flags.astype(jnp.int32)


def _moe_ffn(x1, h, comb, selm, wg, wu, wd):
    s = x1.shape[0]
    tm, tf = TM_MOE, TF_MOE
    ne = N_EXPERTS
    n_chunks = s // tm
    n_tiles = 2 * s // tm + ne
    rows = n_tiles * tm
    sel_t = selm[:, :ne].T.astype(jnp.int32)
    cum = jnp.cumsum(sel_t, axis=1)
    counts = cum[:, -1]
    tiles_e = (counts + tm - 1) // tm
    tile_end = jnp.cumsum(tiles_e)
    tile_start = tile_end - tiles_e
    used = tile_end[-1]
    dest = jnp.where(sel_t > 0, tile_start[:, None] * tm + cum - 1, -1).astype(jnp.int32)
    t_idx = jnp.arange(n_tiles, dtype=jnp.int32)
    te_raw = jnp.minimum(jnp.sum(t_idx[:, None] >= tile_end[None, :], axis=1), ne - 1).astype(jnp.int32)
    tvalid = (t_idx < used).astype(jnp.int32)
    last_e = te_raw[jnp.maximum(used - 1, 0)]
    te = jnp.where(tvalid > 0, te_raw, last_e).astype(jnp.int32)
    dest_tiles = dest.reshape(ne, n_chunks, tm)
    big = jnp.int32(2 ** 30)
    d_max = jnp.max(dest_tiles, axis=2)
    d_min = jnp.min(jnp.where(dest_tiles >= 0, dest_tiles, big), axis=2)
    overlap = (d_max[te] >= t_idx[:, None] * tm) & (d_min[te] < (t_idx[:, None] + 1) * tm) & (tvalid[:, None] > 0)
    c_idx = jnp.arange(n_chunks, dtype=jnp.int32)[None, :]
    c_lo = jnp.min(jnp.where(overlap, c_idx, n_chunks), axis=1)
    c_hi = jnp.max(jnp.where(overlap, c_idx, -1), axis=1)
    c_lo = jnp.where(c_hi >= 0, c_lo, 0).astype(jnp.int32)
    p_tile, p_chunk, p_flag = _work_list(c_lo, c_hi.astype(jnp.int32), n_tiles + ne * n_chunks)
    p_exp = te[p_tile]

    cw_t = comb[:, :ne].T.reshape(ne, 1, s)
    dest3 = dest.reshape(ne, 1, s)
    xg, roww = pl.pallas_call(
        _permute_kernel,
        grid_spec=pltpu.PrefetchScalarGridSpec(
            num_scalar_prefetch=4, grid=(p_tile.shape[0],),
            in_specs=[pl.BlockSpec((1, 1, tm), lambda k, pt, pc, pf, pe: (pe[k], 0, pc[k])),
                      pl.BlockSpec((1, 1, tm), lambda k, pt, pc, pf, pe: (pe[k], 0, pc[k])),
                      pl.BlockSpec((tm, D_MODEL), lambda k, pt, pc, pf, pe: (pc[k], 0))],
            out_specs=(pl.BlockSpec((tm, D_MODEL), lambda k, pt, pc, pf, pe: (pt[k], 0)),
                       pl.BlockSpec((tm, 1), lambda k, pt, pc, pf, pe: (pt[k], 0))),
            scratch_shapes=[pltpu.VMEM((tm, D_MODEL), F32), pltpu.VMEM((tm, 1), F32)]),
        out_shape=(jax.ShapeDtypeStruct((rows, D_MODEL), BF16), jax.ShapeDtypeStruct((rows, 1), F32)),
        compiler_params=_cparams("arbitrary"), name="moe_permute",
    )(p_tile, p_chunk, p_flag, p_exp, dest3, cw_t, h)

    nf = wg.shape[2] // tf
    last_f = nf - 1
    fsel = lambda t, f, tv: f * tv[t] + last_f * (1 - tv[t])
    yg = pl.pallas_call(
        _moe_ffn_kernel,
        grid_spec=pltpu.PrefetchScalarGridSpec(
            num_scalar_prefetch=2, grid=(n_tiles, nf),
            in_specs=[pl.BlockSpec((tm, D_MODEL), lambda t, f, te_, tv: (t, 0)),
                      pl.BlockSpec((tm, 1), lambda t, f, te_, tv: (t, 0)),
                      pl.BlockSpec((1, D_MODEL, tf), lambda t, f, te_, tv: (te_[t], 0, fsel(t, f, tv))),
                      pl.BlockSpec((1, D_MODEL, tf), lambda t, f, te_, tv: (te_[t], 0, fsel(t, f, tv))),
                      pl.BlockSpec((1, tf, D_MODEL), lambda t, f, te_, tv: (te_[t], fsel(t, f, tv), 0))],
            out_specs=pl.BlockSpec((tm, D_MODEL), lambda t, f, te_, tv: (t, 0)),
            scratch_shapes=[pltpu.VMEM((tm, D_MODEL), F32)]),
        out_shape=jax.ShapeDtypeStruct((rows, D_MODEL), BF16),
        compiler_params=_cparams("arbitrary", "arbitrary"), name="moe_ffn",
    )(te, tvalid, xg, roww, wg, wu, wd)

    grow = jnp.stack([jnp.min(jnp.where(dest >= 0, dest, big), axis=0), jnp.max(dest, axis=0)], axis=1)
    has = d_max >= 0
    first = jnp.where(has, d_min // tm, 0).T.reshape(-1).astype(jnp.int32)
    last = jnp.where(has, d_max // tm, -1).T.reshape(-1).astype(jnp.int32)
    u_sub, u_chunk, u_flag = _work_list(first, last, n_chunks * 2 * ne)
    u_tile = (u_sub // ne).astype(jnp.int32)
    prev_tile = jnp.concatenate([jnp.full((1,), -1, jnp.int32), u_tile[:-1]])
    u_flag = (u_flag & 4) | jnp.where((u_flag != 0) & (u_tile != prev_tile), 1, 0)
    out = pl.pallas_call(
        _unpermute_kernel,
        grid_spec=pltpu.PrefetchScalarGridSpec(
            num_scalar_prefetch=3, grid=(u_tile.shape[0],),
            in_specs=[pl.BlockSpec((tm, 2), lambda k, ut, uc, uf: (ut[k], 0)),
                      pl.BlockSpec((tm, D_MODEL), lambda k, ut, uc, uf: (ut[k], 0)),
                      pl.BlockSpec((tm, D_MODEL), lambda k, ut, uc, uf: (uc[k], 0))],
            out_specs=pl.BlockSpec((tm, D_MODEL), lambda k, ut, uc, uf: (ut[k], 0))),
        out_shape=jax.ShapeDtypeStruct((s, D_MODEL), F32),
        compiler_params=_cparams("arbitrary"), name="moe_unpermute",
    )(u_tile, u_chunk, u_flag, grow, x1, yg)
    return out


def _t5_bucket(rel):
    n = jnp.maximum(rel, 0)
    max_exact = NUM_BUCKETS // 2
    nf = jnp.maximum(n, 1).astype(jnp.float32)
    large = max_exact + (jnp.log(nf / max_exact) / math.log(MAX_DISTANCE / max_exact)
                         * (NUM_BUCKETS - max_exact)).astype(jnp.int32)
    large = jnp.minimum(large, NUM_BUCKETS - 1)
    return jnp.where(n < max_exact, n, large)


def _bucket_thresholds():
    buckets = _t5_bucket(jnp.arange(MAX_DISTANCE + 1, dtype=jnp.int32))
    b = jnp.arange(NUM_BUCKETS, dtype=jnp.int32)
    return jnp.sum(buckets[None, :] < b[:, None], axis=1).astype(jnp.int32)


def _rope_tables(s):
    half = MLA_ROPE // 2
    inv = ROPE_THETA ** (-jnp.arange(half, dtype=jnp.float32) / half)
    ang = jnp.arange(s).astype(jnp.float32)[:, None] * inv[None, :]
    cos, sin = jnp.cos(ang), jnp.sin(ang)
    z = lambda w: jnp.zeros((s, w), F32)
    pad = MLA_PAD - MLA_QK
    cos_t = jnp.concatenate([jnp.ones((s, MLA_NOPE), F32), cos, cos, z(pad)], axis=1)
    sin_lo = jnp.concatenate([z(MLA_NOPE), -sin, z(half), z(pad)], axis=1)
    sin_hi = jnp.concatenate([z(MLA_NOPE), z(half), sin, z(pad)], axis=1)
    return cos_t, sin_lo, sin_hi


def _pad_lanes(a, width):
    return jnp.pad(a, [(0, 0)] * (a.ndim - 1) + [(0, width - a.shape[-1])])


def kernel(x, rel_bias_table, mix_norm, w_in, b_gate, sb_w_o, nsa_q_norm, nsa_k_norm, nsa_cmp_pos, nsa_cmp_w1, nsa_cmp_w2, nsa_w_o, mla_q_a_norm, mla_kv_a_norm, mla_w_uq, mla_w_ukv, mla_q_norm, mla_k_norm, mla_w_o, w_out, ffn_norm, dense_w_gate, dense_w_up, dense_w_down, moe_router, moe_w_gate, moe_w_up, moe_w_down):
    b, s, d = x.shape
    assert b == 1 and d == D_MODEL and s % TM_MOE == 0 and s // SEL_BLOCK >= SEL_TOPK
    depth = w_in.shape[0]
    xs = x.reshape(s, d)
    thr = _bucket_thresholds()
    cos_t, sin_lo, sin_hi = _rope_tables(s)
    n_chunk = s // CMP_STRIDE

    for layer in range(depth):
        w = w_in[layer]
        small = jnp.concatenate([
            w[:, _C_NG:_C_MCQ], jnp.zeros((d, _KR_LANE - NSA_HEADS * NSA_BRANCHES), F32),
            w[:, _C_MKR:_C_GATE], jnp.zeros((d, LANES - _KR_LANE - MLA_ROPE), F32)], axis=1)
        wa = jnp.concatenate([w[:, :_C_NG], w[:, _C_MCQ:_C_MKR], small], axis=1).astype(BF16)
        wg = w[:, _C_GATE:].astype(BF16)
        wuq = _pad_lanes(mla_w_uq[layer].reshape(MLA_Q_LORA, MLA_HEADS, MLA_QK), MLA_PAD)
        wuq = wuq.reshape(MLA_Q_LORA, MLA_HEADS * MLA_PAD).astype(BF16)
        wukv = mla_w_ukv[layer].reshape(MLA_KV_LORA, MLA_HEADS, MLA_NOPE + MLA_V)
        wukv = jnp.concatenate([
            _pad_lanes(wukv[:, :, :MLA_NOPE], MLA_PAD).reshape(MLA_KV_LORA, MLA_HEADS * MLA_PAD),
            wukv[:, :, MLA_NOPE:].reshape(MLA_KV_LORA, MLA_HEADS * MLA_V)], axis=1).astype(BF16)
        (sbq, sbk, sbv, nq, nkc, nvc, nks, nvs, nkw, nvw, gsm, mq, mk, mv, g) = _input_projection(
            xs, mix_norm[layer][None], wa, wg, b_gate[layer][None], wuq, wukv,
            nsa_q_norm[layer][None], nsa_k_norm[layer], mla_q_a_norm[layer][None],
            mla_kv_a_norm[layer][None], _pad_lanes(mla_q_norm[layer][None], MLA_PAD),
            _pad_lanes(mla_k_norm[layer][None], MLA_PAD), cos_t, sin_lo, sin_hi)

        o_sb = _stick_breaking(sbq, sbk, sbv)
        o_mla = _mla_attention(mq, mk, mv)

        x2 = jnp.stack([nkc, nvc]).reshape(2, NSA_GROUPS, n_chunk, CMP_STRIDE * HEAD_DIM)
        cmp, cmp_t = _compress(x2, nsa_cmp_pos[layer].reshape(2, 1, CMP_LEN * HEAD_DIM),
                               nsa_cmp_w1[layer].astype(BF16), nsa_cmp_w2[layer].astype(BF16),
                               nsa_k_norm[layer][0:1])
        gates = gsm[:NSA_HEADS * NSA_BRANCHES].reshape(NSA_GROUPS, NSA_HPG * NSA_BRANCHES, s)
        oc, sel = _nsa_compressed(thr, rel_bias_table, nq, cmp, cmp_t, gates)
        o_nsa = _nsa_selected(thr, rel_bias_table, nq, nks, nvs,
                              jnp.pad(nkw, ((0, 0), (WINDOW, 0), (0, 0))),
                              jnp.pad(nvw, ((0, 0), (0, 0), (WINDOW, 0))), sel, gates, oc)

        wsb = sb_w_o[layer].reshape(SB_HEADS, HEAD_DIM, d).astype(BF16)
        wnsa = nsa_w_o[layer].reshape(NSA_HEADS, HEAD_DIM, d).astype(BF16)
        wmla = mla_w_o[layer].reshape(MLA_HEADS, MLA_V, d).astype(BF16)
        wout = w_out[layer].astype(BF16)
        j = layer // 2
        if layer % 2 == 0:
            x1, h = _merge(o_sb, o_nsa, o_mla, g, xs, wsb, wnsa, wmla, wout, ffn_norm[layer][None])
            d_ff = dense_w_gate.shape[2]
            xs = _dense_ffn(x1, h, dense_w_gate[j].astype(BF16), dense_w_up[j].astype(BF16),
                            dense_w_down[j].astype(BF16), d_ff // 2)
        else:
            x1, h, comb, selm = _merge(o_sb, o_nsa, o_mla, g, xs, wsb, wnsa, wmla, wout,
                                       ffn_norm[layer][None], _pad_lanes(moe_router[j], LANES))
            xs = _moe_ffn(x1, h, comb, selm, moe_w_gate[j].astype(BF16), moe_w_up[j].astype(BF16),
                          moe_w_down[j].astype(BF16))
    return xs.reshape(b, s, d)
```

```python
import functools
import math

import jax
import jax.numpy as jnp
from jax import lax
from jax.experimental import pallas as pl
from jax.experimental.pallas import tpu as pltpu

F32 = jnp.float32
BF16 = jnp.bfloat16

D_MODEL = 1024
HEAD_DIM = 64
SB_HEADS = 4
NSA_HEADS = 8
NSA_GROUPS = 2
NSA_HPG = NSA_HEADS // NSA_GROUPS
NSA_BRANCHES = 3
MLA_HEADS = 4
MLA_NOPE = 64
MLA_ROPE = 32
MLA_V = 64
MLA_QK = MLA_NOPE + MLA_ROPE
MLA_Q_LORA = 256
MLA_KV_LORA = 128
ROPE_THETA = 10000.0
CMP_LEN = 32
CMP_STRIDE = 16
CMP_HIDDEN = 256
SEL_BLOCK = 64
SEL_TOPK = 16
N_LOCAL_BLOCKS = 2
WINDOW = 512
NUM_BUCKETS = 32
MAX_DISTANCE = 1024
N_MIXERS = 3
N_EXPERTS = 8
EPS = 1e-6
NEG_INF = -1e30
FORCE = 1e30
LOG2E = math.log2(math.e)

SB_W = SB_HEADS * HEAD_DIM
NSA_QW = NSA_HEADS * HEAD_DIM
NSA_KVW = NSA_GROUPS * HEAD_DIM
GATE_W = N_MIXERS * D_MODEL
LANES = 128
MLA_PAD = LANES

_C_NQ = 3 * SB_W
_C_NKV = _C_NQ + NSA_QW
_C_NG = _C_NKV + 6 * NSA_KVW
_C_MCQ = _C_NG + NSA_HEADS * NSA_BRANCHES
_C_MCKV = _C_MCQ + MLA_Q_LORA
_C_MKR = _C_MCKV + MLA_KV_LORA
_C_GATE = _C_MKR + MLA_ROPE
_A_SB = 0
_A_NQ = 3 * SB_W
_A_NKV = _A_NQ + NSA_QW
_A_MCQ = _A_NKV + 6 * NSA_KVW
_A_MCKV = _A_MCQ + MLA_Q_LORA
_A_SMALL = _A_MCKV + MLA_KV_LORA
_A_COLS = _A_SMALL + LANES
_KR_LANE = MLA_NOPE

TM_PROJ = 256
TQ_SB = 256
TQ_MLA = 256
TK_MLA = 1024
CHUNK_MLA = 256
CHUNK_SEL = 512
TQ_CMP = 128
CMP_PARTS = 4
TQ_SEL = 256
TK_SEL = 512
TM_MERGE = 256
TM_FFN = 512
TM_MOE = 512
TF_MOE = 512
VMEM_LIMIT = 56 * 1024 * 1024
SB_EXIT = -104.0

_NT = (((1,), (1,)), ((), ()))


def _dot(a, b):
    return jnp.dot(a, b, preferred_element_type=F32)


def _dot_nt(a, b):
    return lax.dot_general(a, b, _NT, preferred_element_type=F32)


def _cparams(*sem):
    return pltpu.CompilerParams(dimension_semantics=sem, vmem_limit_bytes=VMEM_LIMIT)


def _const_spec(shape):
    nd = len(shape)
    return pl.BlockSpec(shape, lambda *_: (0,) * nd)


def _smem_spec():
    return pl.BlockSpec(memory_space=pltpu.SMEM)


def _split3(x):
    h1 = x.astype(BF16)
    r1 = x - h1.astype(F32)
    h2 = r1.astype(BF16)
    h3 = (r1 - h2.astype(F32)).astype(BF16)
    return h1, h2, h3


def _bias_chain(rel, thr_ref, tab_ref, head):
    out = jnp.full(rel.shape, tab_ref[0, head], F32)
    for b in range(1, NUM_BUCKETS):
        out = jnp.where(rel >= thr_ref[b], tab_ref[b, head], out)
    return out


def _rope(y, cos, sin_lo, sin_hi):
    half = MLA_ROPE // 2
    return y * cos + pltpu.roll(y, half, 1) * sin_hi + pltpu.roll(y, MLA_PAD - half, 1) * sin_lo


def _proj_kernel(x_ref, gain_ref, wa_ref, wg_ref, bg_ref, wuq_ref, wukv_ref, nqg_ref, nkg_ref,
                 qag_ref, kvag_ref, mqg_ref, mkg_ref, cos_ref, slo_ref, shi_ref,
                 sbq_ref, sbk_ref, sbv_ref, nq_ref, nkc_ref, nvc_ref, nks_ref, nvs_ref, nkw_ref, nvw_ref,
                 gsm_ref, mq_ref, mk_ref, mv_ref, g_ref):
    x = x_ref[...]
    ms = jnp.mean(x * x, axis=-1, keepdims=True)
    h = (x * lax.rsqrt(ms + EPS) * gain_ref[...]).astype(BF16)

    def rms(p, gain, width):
        m = jnp.sum(p * p, axis=-1, keepdims=True) * (1.0 / width)
        return p * lax.rsqrt(m + EPS) * gain

    p = _dot(h, wa_ref[:, _A_SB:_A_SB + 3 * SB_W])
    sbq_t = (p[:, :SB_W] * HEAD_DIM ** -0.5).T.astype(BF16)
    sbv_t = p[:, 2 * SB_W:3 * SB_W].T.astype(BF16)
    for j in range(SB_HEADS):
        sbq_ref[j] = sbq_t[j * HEAD_DIM:(j + 1) * HEAD_DIM]
        sbv_ref[j] = sbv_t[j * HEAD_DIM:(j + 1) * HEAD_DIM]
    for j in range(SB_HEADS // 2):
        sbk_ref[j] = p[:, SB_W + j * LANES:SB_W + (j + 1) * LANES].astype(BF16)

    p = _dot(h, wa_ref[:, _A_NQ:_A_NQ + NSA_QW])
    qn = [rms(p[:, j * HEAD_DIM:(j + 1) * HEAD_DIM], nqg_ref[...], HEAD_DIM) * (HEAD_DIM ** -0.5 * LOG2E)
          for j in range(NSA_HEADS)]
    qn_t = jnp.concatenate(qn, axis=1).T.astype(BF16)
    for j in range(NSA_HEADS):
        nq_ref[j] = qn_t[j * HEAD_DIM:(j + 1) * HEAD_DIM]

    p = _dot(h, wa_ref[:, _A_NKV:_A_NKV + 6 * NSA_KVW])
    vs_t = p[:, 3 * NSA_KVW:4 * NSA_KVW].T.astype(BF16)
    tag_shape = (x.shape[0], LANES - HEAD_DIM)
    block_id = (pl.program_id(0) * x.shape[0] + lax.broadcasted_iota(jnp.int32, tag_shape, 0)) // SEL_BLOCK
    block_tag = jnp.where(lax.broadcasted_iota(jnp.int32, tag_shape, 1) == block_id % (TK_SEL // SEL_BLOCK),
                          1.0, 0.0)
    vw_t = p[:, 5 * NSA_KVW:6 * NSA_KVW].T.astype(BF16)
    for j in range(NSA_GROUPS):
        def seg(s):
            return p[:, s * NSA_KVW + j * HEAD_DIM:s * NSA_KVW + (j + 1) * HEAD_DIM]
        nkc_ref[j] = seg(0)
        nvc_ref[j] = seg(1)
        nks_ref[j] = jnp.concatenate([rms(seg(2), nkg_ref[1:2, :], HEAD_DIM), block_tag], axis=1).astype(BF16)
        nvs_ref[j] = vs_t[j * HEAD_DIM:(j + 1) * HEAD_DIM]
        nkw_ref[j] = rms(seg(4), nkg_ref[2:3, :], HEAD_DIM).astype(BF16)
        nvw_ref[j] = vw_t[j * HEAD_DIM:(j + 1) * HEAD_DIM]

    p = _dot(h, wa_ref[:, _A_MCQ:_A_COLS])
    cq = rms(p[:, :MLA_Q_LORA], qag_ref[...], MLA_Q_LORA).astype(BF16)
    ckv = rms(p[:, MLA_Q_LORA:MLA_Q_LORA + MLA_KV_LORA], kvag_ref[...], MLA_KV_LORA).astype(BF16)
    small = p[:, MLA_Q_LORA + MLA_KV_LORA:]
    gsm_ref[...] = jax.nn.sigmoid(small).T
    lane = lax.broadcasted_iota(jnp.int32, small.shape, 1)
    krope = jnp.where((lane >= _KR_LANE) & (lane < _KR_LANE + MLA_ROPE), small, 0.0)
    cos, slo, shi = cos_ref[...], slo_ref[...], shi_ref[...]
    qu = _dot(cq, wuq_ref[...])
    kvu = _dot(ckv, wukv_ref[...])
    for j in range(MLA_HEADS):
        qh = rms(qu[:, j * MLA_PAD:(j + 1) * MLA_PAD], mqg_ref[...], MLA_QK)
        mq_ref[j] = (_rope(qh, cos, slo, shi) * MLA_QK ** -0.5).T.astype(BF16)
        kh = rms(kvu[:, j * MLA_PAD:(j + 1) * MLA_PAD] + krope, mkg_ref[...], MLA_QK)
        mk_ref[j] = _rope(kh, cos, slo, shi).astype(BF16)
    v_t = kvu[:, MLA_HEADS * MLA_PAD:].T.astype(BF16)
    for j in range(MLA_HEADS):
        mv_ref[j] = v_t[j * MLA_V:(j + 1) * MLA_V]

    g_ref[...] = jax.nn.sigmoid(_dot(h, wg_ref[...]) + bg_ref[...])


def _input_projection(x, gain, wa, wg, bg, wuq, wukv, nqg, nkg, qag, kvag, mqg, mkg, cos, slo, shi):
    s = x.shape[0]
    tm = TM_PROJ
    row = lambda w: pl.BlockSpec((tm, w), lambda i: (i, 0))
    heads = lambda n, w: pl.BlockSpec((n, tm, w), lambda i: (0, i, 0))
    hs = lambda n, w, dt: jax.ShapeDtypeStruct((n, s, w), dt)
    in_specs = [row(D_MODEL)] + [_const_spec(a.shape) for a in
                                 (gain, wa, wg, bg, wuq, wukv, nqg, nkg, qag, kvag, mqg, mkg)]
    in_specs += [row(MLA_PAD)] * 3
    hst = lambda n, w, dt: jax.ShapeDtypeStruct((n, w, s), dt)
    heads_t = lambda n, w: pl.BlockSpec((n, w, tm), lambda i: (0, 0, i))
    ng, hd = NSA_GROUPS, HEAD_DIM
    out_shape = (hst(SB_HEADS, hd, BF16), hs(SB_HEADS // 2, LANES, BF16), hst(SB_HEADS, hd, BF16))
    out_shape += (hst(NSA_HEADS, hd, BF16),)
    out_shape += (hs(ng, hd, F32),) * 2 + (hs(ng, LANES, BF16), hst(ng, hd, BF16), hs(ng, hd, BF16), hst(ng, hd, BF16))
    out_shape += (jax.ShapeDtypeStruct((LANES, s), F32),)
    out_shape += (hst(MLA_HEADS, MLA_PAD, BF16), hs(MLA_HEADS, MLA_PAD, BF16), hst(MLA_HEADS, MLA_V, BF16))
    out_shape += (jax.ShapeDtypeStruct((s, GATE_W), F32),)
    out_specs = (heads_t(SB_HEADS, hd), heads(SB_HEADS // 2, LANES), heads_t(SB_HEADS, hd))
    out_specs += (heads_t(NSA_HEADS, hd),)
    out_specs += (heads(ng, hd),) * 2 + (heads(ng, LANES), heads_t(ng, hd), heads(ng, hd), heads_t(ng, hd))
    out_specs += (pl.BlockSpec((LANES, tm), lambda i: (0, i)),)
    out_specs += (heads_t(MLA_HEADS, MLA_PAD), heads(MLA_HEADS, MLA_PAD), heads_t(MLA_HEADS, MLA_V), row(GATE_W))
    return pl.pallas_call(
        _proj_kernel, grid=(s // tm,), in_specs=in_specs, out_specs=out_specs, out_shape=out_shape,
        compiler_params=_cparams("parallel"), name="input_projection",
    )(x, gain, wa, wg, bg, wuq, wukv, nqg, nkg, qag, kvag, mqg, mkg, cos, slo, shi)


def _sb_kernel(qt_ref, k2_ref, vt_ref, o_ref):
    tq = TQ_SB
    nh = qt_ref.shape[0]
    i = pl.program_id(0)
    key = lax.broadcasted_iota(jnp.int32, (tq, tq), 0)
    qry = lax.broadcasted_iota(jnp.int32, (tq, tq), 1)
    later = (qry > key).astype(BF16)
    causal = key < qry
    zeros = jnp.zeros((HEAD_DIM, tq), BF16)
    q_aug = [jnp.concatenate([qt_ref[h], zeros] if h % 2 == 0 else [zeros, qt_ref[h]], axis=0) for h in range(nh)]

    def tile(kt, h, carry, acc, diagonal):
        off = pl.multiple_of(kt * tq, tq)
        z = _dot(k2_ref[h // 2, pl.ds(off, tq), :], q_aug[h])
        log_stay = -(jnp.maximum(z, 0.0) + jnp.log(1.0 + jnp.exp(-jnp.abs(z))))
        if diagonal:
            log_stay = jnp.where(causal, log_stay, 0.0)
        hi = log_stay.astype(BF16)
        lo = (log_stay - hi.astype(F32)).astype(BF16)
        suffix = _dot(later, hi) + _dot(later, lo)
        logw = z + log_stay + (carry + suffix)
        if diagonal:
            logw = jnp.where(causal, logw, NEG_INF)
        w = jnp.exp(logw).astype(BF16)
        acc = acc + _dot(vt_ref[h, :, pl.ds(off, tq)], w)
        carry = carry + suffix[0:1, :] + log_stay[0:1, :]
        return carry, acc

    def all_heads(kt, st, diagonal):
        out = [tile(kt, h, st[2 * h], st[2 * h + 1], diagonal) for h in range(nh)]
        return tuple(x for pair in out for x in pair)

    st = all_heads(i, (jnp.zeros((1, tq), F32), jnp.zeros((HEAD_DIM, tq), F32)) * nh, True)

    def cond(c):
        top = c[1]
        for h in range(1, nh):
            top = jnp.maximum(top, c[1 + 2 * h])
        return jnp.logical_and(c[0] >= 0, jnp.max(top) > SB_EXIT)

    def body(c):
        return (c[0] - 1,) + all_heads(c[0], c[1:], False)

    c = lax.while_loop(cond, body, (i - 1,) + st)
    for h in range(nh):
        o_ref[h] = c[2 + 2 * h]


def _stick_breaking(q_t, k2, v_t):
    nh, _, s = q_t.shape
    tq = TQ_SB
    whole = lambda a: pl.BlockSpec(a.shape, lambda i: (0, 0, 0), pipeline_mode=pl.Buffered(1))
    return pl.pallas_call(
        _sb_kernel, grid=(s // tq,),
        in_specs=[pl.BlockSpec((nh, HEAD_DIM, tq), lambda i: (0, 0, i)), whole(k2), whole(v_t)],
        out_specs=pl.BlockSpec((nh, HEAD_DIM, tq), lambda i: (0, 0, i)),
        out_shape=jax.ShapeDtypeStruct((nh, HEAD_DIM, s), F32),
        compiler_params=_cparams("parallel"), name="stick_breaking",
    )(q_t, k2, v_t)


ONES_ROWS = 16


def _with_ones_rows(v_t):
    return jnp.concatenate([v_t, jnp.ones((ONES_ROWS, v_t.shape[1]), v_t.dtype)], axis=0)


def _pipelined_blocks(n_last, score_chunk, value_chunk, n_chunks, chunk, exp_fn, bufs, st):
    rows = lambda c: pl.ds(c * chunk, chunk)

    def prepare(kb, slot, maybe_diagonal):
        mx = None
        for c in range(n_chunks):
            s = score_chunk(kb, c, maybe_diagonal)
            bufs[slot][rows(c), :] = s
            smax = jnp.max(s, axis=0, keepdims=True)
            mx = smax if mx is None else jnp.maximum(mx, smax)
        return mx

    def consume(kb, slot, mx, st_):
        m, acc = st_
        m_new = jnp.maximum(m, mx)
        part = None
        for c in range(n_chunks):
            p = exp_fn(bufs[slot][rows(c), :] - m_new).astype(BF16)
            d = _dot(value_chunk(kb, c), p)
            part = d if part is None else part + d
        return m_new, exp_fn(m - m_new) * acc + part

    def run(kb0, carry, n_blocks, last_is_diagonal):
        mx = [carry[0], carry[1]]
        st_ = carry[2]
        n_prep = n_blocks if last_is_diagonal else n_blocks + 2
        for j in range(n_blocks):
            if j + 2 < n_prep:
                mx.append(prepare(kb0 + j + 2, (j + 2) % 3, last_is_diagonal and j + 2 == n_blocks - 1))
            st_ = consume(kb0 + j, j % 3, mx[j], st_)
        return (mx[n_blocks], mx[n_blocks + 1], st_) if not last_is_diagonal else st_

    trips = jnp.maximum(n_last - 2, 0) // 3
    carry = (prepare(0, 0, True), prepare(1, 1, True), st)
    carry = lax.fori_loop(0, trips, lambda t, c: run(3 * t, c, 3, False), carry)
    kb0 = 3 * trips
    tails = [functools.partial(run, kb0, n_blocks=r + 1, last_is_diagonal=True) for r in range(5)]
    return lax.switch(n_last - kb0, tails, carry)


def _mla_kernel(qt_ref, k_ref, vt_ref, o_ref, *bufs):
    tq, tk = TQ_MLA, TK_MLA
    i = pl.program_id(1)
    q_t = qt_ref[0]

    ck = CHUNK_MLA

    def score_chunk(kb, c, maybe_diagonal):
        off = pl.multiple_of(kb * tk + c * ck, ck)
        s = _dot(k_ref[0, pl.ds(off, ck), :], q_t)
        if maybe_diagonal:
            key = off + lax.broadcasted_iota(jnp.int32, (ck, tq), 0)
            qry = i * tq + lax.broadcasted_iota(jnp.int32, (ck, tq), 1)
            s = jnp.where(key <= qry, s, NEG_INF)
        return s

    def value_chunk(kb, c):
        off = pl.multiple_of(kb * tk + c * ck, ck)
        return _with_ones_rows(vt_ref[0, :, pl.ds(off, ck)])

    st = (jnp.full((1, tq), NEG_INF, F32), jnp.zeros((MLA_V + ONES_ROWS, tq), F32))
    _, acc = _pipelined_blocks((i * tq) // tk, score_chunk, value_chunk, tk // ck, ck, jnp.exp,
                               bufs, st)
    o_ref[0] = acc[:MLA_V] / acc[MLA_V:MLA_V + 1]


def _mla_attention(q_t, k, v_t):
    nh, s, _ = k.shape
    tq, tk = TQ_MLA, TK_MLA
    return pl.pallas_call(
        _mla_kernel, grid=(nh, s // tq),
        in_specs=[pl.BlockSpec((1, MLA_PAD, tq), lambda h, i: (h, 0, i)),
                  pl.BlockSpec((1, s, MLA_PAD), lambda h, i: (h, 0, 0)),
                  pl.BlockSpec((1, MLA_V, s), lambda h, i: (h, 0, 0))],
        out_specs=pl.BlockSpec((1, MLA_V, tq), lambda h, i: (h, 0, i)),
        out_shape=jax.ShapeDtypeStruct((nh, MLA_V, s), F32),
        scratch_shapes=[pltpu.VMEM((tk, tq), F32)] * 3,
        compiler_params=_cparams("parallel", "parallel"), name="mla_attention",
    )(q_t, k, v_t)


def _compress_kernel(x_ref, pos_ref, w1_ref, w2_ref, kg_ref, o_ref, ot_ref):
    kv = pl.program_id(0)
    half = CMP_STRIDE * HEAD_DIM
    a = x_ref[0, 0].astype(BF16)
    w1 = w1_ref[0]
    first = _dot(a, w1[:half])
    second = _dot(a, w1[half:])
    n = second.shape[0]
    second = pltpu.roll(second, n - 1, 0)
    posb = jnp.broadcast_to(pos_ref[0], (8, 2 * half)).astype(BF16)
    hid = first + second + _dot(posb, w1)[0:1]
    hid = hid * jax.nn.sigmoid(hid)
    out = _dot(hid.astype(BF16), w2_ref[0])
    ms = jnp.mean(out * out, axis=-1, keepdims=True)
    normed = out * lax.rsqrt(ms + EPS) * kg_ref[...]
    res = jnp.where(kv == 0, normed, out)
    o_ref[0, 0] = res
    ot_ref[0, 0] = res.T


def _compress(x2, pos, w1, w2, kgain):
    _, ng, nc, cw = x2.shape
    return pl.pallas_call(
        _compress_kernel, grid=(2, ng),
        in_specs=[pl.BlockSpec((1, 1, nc, cw), lambda a, g: (a, g, 0, 0)),
                  pl.BlockSpec((1, 1, 2 * cw), lambda a, g: (a, 0, 0)),
                  pl.BlockSpec((1, 2 * cw, CMP_HIDDEN), lambda a, g: (a, 0, 0)),
                  pl.BlockSpec((1, CMP_HIDDEN, HEAD_DIM), lambda a, g: (a, 0, 0)),
                  _const_spec(kgain.shape)],
        out_specs=(pl.BlockSpec((1, 1, nc, HEAD_DIM), lambda a, g: (a, g, 0, 0)),
                   pl.BlockSpec((1, 1, HEAD_DIM, nc), lambda a, g: (a, g, 0, 0))),
        out_shape=(jax.ShapeDtypeStruct((2, ng, nc, HEAD_DIM), F32),
                   jax.ShapeDtypeStruct((2, ng, HEAD_DIM, nc), F32)),
        compiler_params=_cparams("parallel", "parallel"), name="nsa_compress",
    )(x2, pos, w1, w2, kgain)


def _nsa_cmp_kernel(thr_ref, tab_ref, qt_ref, kc_ref, vct_ref, gt_ref, oc_ref, selt_ref, band_ref, pool_ref):
    tq = TQ_CMP
    r_n = NSA_HPG
    g = pl.program_id(0)
    i = pl.program_id(1)
    nc = kc_ref.shape[2]
    nb = selt_ref.shape[1]
    cpt = tq // CMP_STRIDE

    @pl.when(i == 0)
    def _init():
        m = lax.broadcasted_iota(jnp.int32, (2 * nc, tq), 0)
        a = lax.broadcasted_iota(jnp.int32, (2 * nc, tq), 1)
        rel = a - CMP_STRIDE * (m - nc) - (CMP_LEN - 1)
        for r in range(r_n):
            band_ref[r] = _bias_chain(rel, thr_ref, tab_ref, g * r_n + r) * LOG2E
        b = lax.broadcasted_iota(jnp.int32, (nb, nc), 0)
        n = lax.broadcasted_iota(jnp.int32, (nb, nc), 1)
        rs = SEL_BLOCK // CMP_STRIDE
        rc = CMP_LEN // CMP_STRIDE
        member = (n >= rs * b - (rc - 1)) & (n <= rs * b + rs - 1) & (n < nc - (rc - 1))
        pool_ref[...] = jnp.where(member, 1.0, 0.0).astype(BF16)

    q_t = jnp.concatenate([qt_ref[r] for r in range(r_n)], axis=1)
    boff = pl.multiple_of(nc - cpt * i, cpt)
    gates = gt_ref[0]

    def tile_body(nc_e, nb_e):
        s = _dot(kc_ref[0, 0, :nc_e, :].astype(BF16), q_t)
        qpos = i * tq + lax.broadcasted_iota(jnp.int32, (nc_e, tq), 1)
        cmp_end = lax.broadcasted_iota(jnp.int32, (nc_e, tq), 0) * CMP_STRIDE + (CMP_LEN - 1)
        valid = cmp_end <= qpos
        imp = jnp.zeros((nc_e, tq), F32)
        probs = []
        for r in range(r_n):
            sr = s[:, r * tq:(r + 1) * tq] + band_ref[r, pl.ds(boff, nc_e), :]
            sr = jnp.where(valid, sr, NEG_INF)
            mx = jnp.max(sr, axis=0, keepdims=True)
            e = jnp.where(valid, jnp.exp2(sr - mx), 0.0)
            den = jnp.sum(e, axis=0, keepdims=True)
            p = e / jnp.where(den > 0.0, den, 1.0)
            imp = imp + p
            probs.append(p.astype(BF16))
        oc = _dot(vct_ref[0, 0, :, :nc_e].astype(BF16), jnp.concatenate(probs, axis=1))
        for r in range(r_n):
            c = NSA_BRANCHES * r
            oc_ref[r] = oc[:, r * tq:(r + 1) * tq] * gates[c:c + 1, :]

        pool = pool_ref[:nb_e, :nc_e]
        h1, h2, h3 = _split3(imp)
        imp_blk = _dot(pool, h1) + _dot(pool, h2) + _dot(pool, h3)
        blk = lax.broadcasted_iota(jnp.int32, (nb_e, tq), 0)
        qp = i * tq + lax.broadcasted_iota(jnp.int32, (nb_e, tq), 1)
        causal_b = blk * SEL_BLOCK <= qp
        dist = qp // SEL_BLOCK - blk
        forced = (blk == 0) | ((dist >= 0) & (dist < N_LOCAL_BLOCKS))
        score = jnp.where(causal_b, jnp.where(forced, FORCE, imp_blk), NEG_INF)
        sel = jnp.zeros((nb_e, tq), F32)
        for _ in range(min(SEL_TOPK, nb_e)):
            mx = jnp.max(score, axis=0, keepdims=True)
            first = jnp.min(jnp.where(score == mx, blk, nb_e), axis=0, keepdims=True)
            pick = blk == first
            sel = jnp.where(pick & (mx > 0.5 * NEG_INF), 1.0, sel)
            score = jnp.where(pick, -jnp.inf, score)
        selt_ref[0, :nb_e, :] = sel
        if nb_e < nb:
            selt_ref[0, nb_e:, :] = jnp.zeros((nb - nb_e, tq), F32)

    tiles_per_part = pl.num_programs(1) // CMP_PARTS
    part = i // tiles_per_part
    for j in range(CMP_PARTS):
        pl.when(part == j)(functools.partial(tile_body, (j + 1) * nc // CMP_PARTS, (j + 1) * nb // CMP_PARTS))


def _nsa_compressed(thr, tab, nq_t, cmp, cmp_t, gates_t):
    nh, _, s = nq_t.shape
    ng = NSA_GROUPS
    tq = TQ_CMP
    nc = cmp.shape[2]
    nb = s // SEL_BLOCK
    qspec = pl.BlockSpec((NSA_HPG, HEAD_DIM, tq), lambda g, i: (g, 0, i))
    return pl.pallas_call(
        _nsa_cmp_kernel, grid=(ng, s // tq),
        in_specs=[_smem_spec(), _smem_spec(), qspec,
                  pl.BlockSpec((1, 1, nc, HEAD_DIM), lambda g, i: (0, g, 0, 0)),
                  pl.BlockSpec((1, 1, HEAD_DIM, nc), lambda g, i: (1, g, 0, 0)),
                  pl.BlockSpec((1, NSA_HPG * NSA_BRANCHES, tq), lambda g, i: (g, 0, i))],
        out_specs=(qspec, pl.BlockSpec((1, nb, tq), lambda g, i: (g, 0, i))),
        out_shape=(jax.ShapeDtypeStruct((nh, HEAD_DIM, s), F32),
                   jax.ShapeDtypeStruct((ng, nb, s), F32)),
        scratch_shapes=[pltpu.VMEM((NSA_HPG, 2 * nc, tq), F32), pltpu.VMEM((nb, nc), BF16)],
        compiler_params=_cparams("arbitrary", "arbitrary"), name="nsa_compressed",
    )(thr, tab, nq_t, cmp, cmp_t, gates_t)


_NEAR_SPAN = ((MAX_DISTANCE + TK_SEL - 2) // TQ_SEL) * TQ_SEL
assert _NEAR_SPAN >= MAX_DISTANCE and TK_SEL % TQ_SEL == 0
_STRIP = 2 * TK_SEL + _NEAR_SPAN
_WIN_KEYS = WINDOW + TQ_SEL
_BLOCKS_PER_TILE = TK_SEL // SEL_BLOCK


def _nsa_sel_kernel(thr_ref, tab_ref, qt_ref, ks_ref, vst_ref, kw_ref, vwt_ref, selt_ref, gt_ref, oc_ref,
                    o_ref, strip_ref, *bufs):
    tq, tk, r_n = TQ_SEL, TK_SEL, NSA_HPG
    g = pl.program_id(0)
    i = pl.program_id(1)
    top = tk + _NEAR_SPAN
    lanes = r_n * tq

    @pl.when(i == 0)
    def _init():
        u = lax.broadcasted_iota(jnp.int32, (_STRIP, tq), 0)
        a = lax.broadcasted_iota(jnp.int32, (_STRIP, tq), 1)
        for r in range(r_n):
            strip_ref[:, r * tq:(r + 1) * tq] = _bias_chain(a - u + top, thr_ref, tab_ref, g * r_n + r) * LOG2E

    q_t = jnp.concatenate([qt_ref[r] for r in range(r_n)], axis=1)
    pad_rows = jnp.zeros((LANES - HEAD_DIM - _BLOCKS_PER_TILE, lanes), F32)
    qpos = i * tq + (lax.broadcasted_iota(jnp.int32, (1, lanes), 1) & (tq - 1))

    ck = CHUNK_SEL

    def score_chunk(kb, c, maybe_diagonal):
        off = pl.multiple_of(kb * tk + c * ck, ck)
        boff = pl.multiple_of(kb * _BLOCKS_PER_TILE, _BLOCKS_PER_TILE)
        picked = selt_ref[0, pl.ds(boff, _BLOCKS_PER_TILE), :]
        penalty = jnp.where(picked > 0.5, 0.0, NEG_INF)
        penalty = jnp.concatenate([penalty] * r_n, axis=1)
        q_aug = jnp.concatenate([q_t, jnp.concatenate([penalty, pad_rows], axis=0).astype(BF16)], axis=0)
        soff = pl.multiple_of(jnp.clip(top - (i * tq - kb * tk), 0, top) + c * ck, ck)
        s = _dot(ks_ref[0, pl.ds(off, ck), :], q_aug) + strip_ref[pl.ds(soff, ck), :]
        if maybe_diagonal:
            key = off + lax.broadcasted_iota(jnp.int32, (ck, lanes), 0)
            s = jnp.where(key <= qpos, s, NEG_INF)
        return s

    def value_chunk(kb, c):
        off = pl.multiple_of(kb * tk + c * ck, ck)
        return _with_ones_rows(vst_ref[0, :, pl.ds(off, ck)])

    st = (jnp.full((1, lanes), NEG_INF, F32), jnp.zeros((HEAD_DIM + ONES_ROWS, lanes), F32))
    _, acc = _pipelined_blocks((i * tq) // tk, score_chunk, value_chunk, tk // ck, ck, jnp.exp2,
                               bufs, st)
    o_sel = acc[:HEAD_DIM] / acc[HEAD_DIM:HEAD_DIM + 1]

    woff = pl.multiple_of(i * tq, tq)
    w0 = top - WINDOW
    sw = _dot(kw_ref[0, pl.ds(woff, _WIN_KEYS), :], q_t) + strip_ref[w0:w0 + _WIN_KEYS, :]
    kpos = i * tq - WINDOW + lax.broadcasted_iota(jnp.int32, (_WIN_KEYS, lanes), 0)
    rel = qpos - kpos
    sw = jnp.where((rel >= 0) & (rel < WINDOW) & (kpos >= 0), sw, NEG_INF)
    pw = jnp.exp2(sw - jnp.max(sw, axis=0, keepdims=True)).astype(BF16)
    o_win = _dot(_with_ones_rows(vwt_ref[0, :, pl.ds(woff, _WIN_KEYS)]), pw)
    o_win = o_win[:HEAD_DIM] / o_win[HEAD_DIM:HEAD_DIM + 1]

    gates = gt_ref[0]
    for r in range(r_n):
        c0 = NSA_BRANCHES * r
        sl = slice(r * tq, (r + 1) * tq)
        o_ref[r] = oc_ref[r] + gates[c0 + 1:c0 + 2, :] * o_sel[:, sl] + gates[c0 + 2:c0 + 3, :] * o_win[:, sl]


def _nsa_selected(thr, tab, nq_t, ks, vs_t, kw_pad, vw_t_pad, sel_t, gates_t, oc):
    nh, _, s = nq_t.shape
    ng, tq = NSA_GROUPS, TQ_SEL
    nb = s // SEL_BLOCK
    qspec = pl.BlockSpec((NSA_HPG, HEAD_DIM, tq), lambda g, i: (g, 0, i))
    rows = lambda n: pl.BlockSpec((1, n, HEAD_DIM), lambda g, i: (g, 0, 0), pipeline_mode=pl.Buffered(1))
    cols = lambda n: pl.BlockSpec((1, HEAD_DIM, n), lambda g, i: (g, 0, 0), pipeline_mode=pl.Buffered(1))
    return pl.pallas_call(
        _nsa_sel_kernel, grid=(ng, s // tq),
        in_specs=[_smem_spec(), _smem_spec(), qspec,
                  pl.BlockSpec((1, s, LANES), lambda g, i: (g, 0, 0), pipeline_mode=pl.Buffered(1)),
                  cols(s), rows(s + WINDOW), cols(s + WINDOW),
                  pl.BlockSpec((1, nb, tq), lambda g, i: (g, 0, i)),
                  pl.BlockSpec((1, NSA_HPG * NSA_BRANCHES, tq), lambda g, i: (g, 0, i)), qspec],
        out_specs=qspec,
        out_shape=jax.ShapeDtypeStruct((nh, HEAD_DIM, s), F32),
        scratch_shapes=[pltpu.VMEM((_STRIP, NSA_HPG * tq), F32)] + [pltpu.VMEM((TK_SEL, NSA_HPG * tq), F32)] * 3,
        compiler_params=_cparams("arbitrary", "arbitrary"), name="nsa_selected",
    )(thr, tab, nq_t, ks, vs_t, kw_pad, vw_t_pad, sel_t, gates_t, oc)


def _merge_body(osb_ref, onsa_ref, omla_ref, g_ref, x_ref, wsb_ref, wnsa_ref, wmla_ref, wout_ref, fg_ref):
    def heads_proj(o_ref, w_ref):
        nh, hd = w_ref.shape[0], w_ref.shape[1]
        if o_ref.shape[1] == hd:
            o = o_ref[...].reshape(nh * hd, o_ref.shape[2]).T.astype(BF16)
            return _dot(o, w_ref[...].reshape(nh * hd, w_ref.shape[2]))
        y = _dot(o_ref[0], w_ref[0])
        for j in range(1, nh):
            y = y + _dot(o_ref[j], w_ref[j])
        return y

    merged = g_ref[:, 0:D_MODEL] * heads_proj(osb_ref, wsb_ref)
    merged = merged + g_ref[:, D_MODEL:2 * D_MODEL] * heads_proj(onsa_ref, wnsa_ref)
    merged = merged + g_ref[:, 2 * D_MODEL:3 * D_MODEL] * heads_proj(omla_ref, wmla_ref)
    x1 = x_ref[...] + _dot(merged.astype(BF16), wout_ref[...])
    ms = jnp.mean(x1 * x1, axis=-1, keepdims=True)
    return x1, x1 * lax.rsqrt(ms + EPS) * fg_ref[...]


def _merge_kernel(osb_ref, onsa_ref, omla_ref, g_ref, x_ref, wsb_ref, wnsa_ref, wmla_ref, wout_ref, fg_ref,
                  x1_ref, h_ref):
    x1, h = _merge_body(osb_ref, onsa_ref, omla_ref, g_ref, x_ref, wsb_ref, wnsa_ref, wmla_ref, wout_ref,
                        fg_ref)
    x1_ref[...] = x1
    h_ref[...] = h.astype(BF16)


def _merge_router_kernel(osb_ref, onsa_ref, omla_ref, g_ref, x_ref, wsb_ref, wnsa_ref, wmla_ref, wout_ref,
                         fg_ref, wr_ref, x1_ref, h_ref, comb_ref, sel_ref):
    x1, h = _merge_body(osb_ref, onsa_ref, omla_ref, g_ref, x_ref, wsb_ref, wnsa_ref, wmla_ref, wout_ref,
                        fg_ref)
    x1_ref[...] = x1
    h_ref[...] = h.astype(BF16)
    wr = wr_ref[...]
    hp = _split3(h)
    wp = _split3(wr)
    logits = jnp.zeros((h.shape[0], wr.shape[1]), F32)
    for a_i, b_i in ((2, 0), (0, 2), (1, 1), (1, 0), (0, 1), (0, 0)):
        logits = logits + _dot(hp[a_i], wp[b_i])
    lane = lax.broadcasted_iota(jnp.int32, logits.shape, 1)
    nl = logits.shape[1]
    logits = jnp.where(lane < N_EXPERTS, logits, -jnp.inf)
    v1 = jnp.max(logits, axis=-1, keepdims=True)
    i1 = jnp.min(jnp.where(logits == v1, lane, nl), axis=-1, keepdims=True)
    rest = jnp.where(lane == i1, -jnp.inf, logits)
    v2 = jnp.max(rest, axis=-1, keepdims=True)
    i2 = jnp.min(jnp.where(rest == v2, lane, nl), axis=-1, keepdims=True)
    e2 = jnp.exp(v2 - v1)
    w1 = 1.0 / (1.0 + e2)
    w2 = e2 / (1.0 + e2)
    comb_ref[...] = jnp.where(lane == i1, w1, jnp.where(lane == i2, w2, 0.0))
    sel_ref[...] = jnp.where((lane == i1) | (lane == i2), 1.0, 0.0)


def _merge(osb, onsa, omla, g, x, wsb, wnsa, wmla, wout, fgain, wrouter=None):
    s = x.shape[0]
    tm = TM_MERGE
    def heads(a):
        if a.shape[1] == s:
            return pl.BlockSpec((a.shape[0], tm, a.shape[2]), lambda i: (0, i, 0))
        return pl.BlockSpec((a.shape[0], a.shape[1], tm), lambda i: (0, 0, i))

    row = lambda w: pl.BlockSpec((tm, w), lambda i: (i, 0))
    in_specs = [heads(osb), heads(onsa), heads(omla), row(GATE_W), row(D_MODEL)]
    in_specs += [_const_spec(a.shape) for a in (wsb, wnsa, wmla, wout, fgain)]
    out_shape = [jax.ShapeDtypeStruct((s, D_MODEL), F32), jax.ShapeDtypeStruct((s, D_MODEL), BF16)]
    out_specs = [row(D_MODEL), row(D_MODEL)]
    args = [osb, onsa, omla, g, x, wsb, wnsa, wmla, wout, fgain]
    if wrouter is None:
        kern = _merge_kernel
    else:
        kern = _merge_router_kernel
        in_specs.append(_const_spec(wrouter.shape))
        args.append(wrouter)
        out_shape += [jax.ShapeDtypeStruct((s, LANES), F32)] * 2
        out_specs += [row(LANES)] * 2
    return pl.pallas_call(
        kern, grid=(s // tm,), in_specs=in_specs, out_specs=tuple(out_specs), out_shape=tuple(out_shape),
        compiler_params=_cparams("parallel"), name="merge",
    )(*args)


def _ffn_kernel(x_ref, h_ref, wg_ref, wu_ref, wd_ref, o_ref, acc_ref):
    f = pl.program_id(1)

    @pl.when(f == 0)
    def _():
        acc_ref[...] = x_ref[...]

    h = h_ref[...]
    gate = _dot(h, wg_ref[...])
    up = _dot(h, wu_ref[...])
    act = (gate * jax.nn.sigmoid(gate) * up).astype(BF16)
    acc_ref[...] += _dot(act, wd_ref[...])

    @pl.when(f == pl.num_programs(1) - 1)
    def _():
        o_ref[...] = acc_ref[...]


def _dense_ffn(x1, h, wg, wu, wd, tf):
    s = x1.shape[0]
    tm = TM_FFN
    nf = wg.shape[1] // tf
    return pl.pallas_call(
        _ffn_kernel, grid=(s // tm, nf),
        in_specs=[pl.BlockSpec((tm, D_MODEL), lambda i, f: (i, 0)),
                  pl.BlockSpec((tm, D_MODEL), lambda i, f: (i, 0)),
                  pl.BlockSpec((D_MODEL, tf), lambda i, f: (0, f)),
                  pl.BlockSpec((D_MODEL, tf), lambda i, f: (0, f)),
                  pl.BlockSpec((tf, D_MODEL), lambda i, f: (f, 0))],
        out_specs=pl.BlockSpec((tm, D_MODEL), lambda i, f: (i, 0)),
        out_shape=jax.ShapeDtypeStruct((s, D_MODEL), F32),
        scratch_shapes=[pltpu.VMEM((tm, D_MODEL), F32)],
        compiler_params=_cparams("parallel", "arbitrary"), name="dense_ffn",
    )(x1, h, wg, wu, wd)


def _permute_kernel(tile_ref, chunk_ref, flag_ref, exp_ref, dest_ref, cw_ref, h_ref, xg_ref, rw_ref, acc_ref,
                    wacc_ref):
    k = pl.program_id(0)
    tm = TM_MOE
    flags = flag_ref[k]

    @pl.when((flags & 1) != 0)
    def _():
        acc_ref[...] = jnp.zeros_like(acc_ref)
        wacc_ref[...] = jnp.zeros_like(wacc_ref)

    @pl.when((flags & 4) != 0)
    def _():
        row = tile_ref[k] * tm + lax.broadcasted_iota(jnp.int32, (tm, tm), 0)
        hit = dest_ref[0] == row
        acc_ref[...] += _dot(jnp.where(hit, 1.0, 0.0).astype(BF16), h_ref[...])
        wacc_ref[...] += jnp.sum(jnp.where(hit, cw_ref[0], 0.0), axis=-1, keepdims=True)

    @pl.when((flags & 2) != 0)
    def _():
        xg_ref[...] = acc_ref[...].astype(BF16)
        rw_ref[...] = wacc_ref[...]


def _moe_ffn_kernel(te_ref, tv_ref, xg_ref, rw_ref, wg_ref, wu_ref, wd_ref, y_ref, acc_ref):
    t = pl.program_id(0)
    f = pl.program_id(1)

    @pl.when(tv_ref[t] != 0)
    def _():
        @pl.when(f == 0)
        def _():
            acc_ref[...] = jnp.zeros_like(acc_ref)

        x = xg_ref[...]
        gate = _dot(x, wg_ref[0])
        up = _dot(x, wu_ref[0])
        act = (gate * jax.nn.sigmoid(gate) * up * rw_ref[...]).astype(BF16)
        acc_ref[...] += _dot(act, wd_ref[0])

        @pl.when(f == pl.num_programs(1) - 1)
        def _():
            y_ref[...] = acc_ref[...].astype(BF16)

    @pl.when(tv_ref[t] == 0)
    def _():
        y_ref[...] = jnp.zeros_like(y_ref)


def _unpermute_kernel(tile_ref, chunk_ref, flag_ref, grow_ref, x_ref, y_ref, o_ref):
    k = pl.program_id(0)
    tm = TM_MOE
    flags = flag_ref[k]

    @pl.when((flags & 1) != 0)
    def _():
        o_ref[...] = x_ref[...]

    @pl.when((flags & 4) != 0)
    def _():
        row = chunk_ref[k] * tm + lax.broadcasted_iota(jnp.int32, (tm, tm), 1)
        hit = (grow_ref[:, 0:1] == row) | (grow_ref[:, 1:2] == row)
        o_ref[...] += _dot(jnp.where(hit, 1.0, 0.0).astype(BF16), y_ref[...])


def _work_list(first, last, n_items):
    n_tiles = first.shape[0]
    cnt_real = jnp.maximum(last - first + 1, 0)
    cnt = jnp.maximum(cnt_real, 1)
    ends = jnp.cumsum(cnt)
    starts = ends - cnt
    total = ends[-1]
    k = jnp.arange(n_items, dtype=jnp.int32)
    kk = jnp.minimum(k, total - 1)
    tile = jnp.minimum(jnp.sum(kk[:, None] >= ends[None, :], axis=1), n_tiles - 1).astype(jnp.int32)
    pos = kk - starts[tile]
    chunk = jnp.where(cnt_real[tile] > 0, first[tile] + pos, 0).astype(jnp.int32)
    live = k < total
    flags = (jnp.where(live & (pos == 0), 1, 0) | jnp.where(live & (pos == cnt[tile] - 1), 2, 0)
             | jnp.where(live & (pos < cnt_real[tile]), 4, 0))
    return tile, chunk, flags.astype(jnp.int32)


def _moe_ffn(x1, h, comb, selm, wg, wu, wd):
    s = x1.shape[0]
    tm, tf = TM_MOE, TF_MOE
    ne = N_EXPERTS
    n_chunks = s // tm
    n_tiles = 2 * s // tm + ne
    rows = n_tiles * tm
    sel_t = selm[:, :ne].T.astype(jnp.int32)
    cum = jnp.cumsum(sel_t, axis=1)
    counts = cum[:, -1]
    tiles_e = (counts + tm - 1) // tm
    tile_end = jnp.cumsum(tiles_e)
    tile_start = tile_end - tiles_e
    used = tile_end[-1]
    dest = jnp.where(sel_t > 0, tile_start[:, None] * tm + cum - 1, -1).astype(jnp.int32)
    t_idx = jnp.arange(n_tiles, dtype=jnp.int32)
    te_raw = jnp.minimum(jnp.sum(t_idx[:, None] >= tile_end[None, :], axis=1), ne - 1).astype(jnp.int32)
    tvalid = (t_idx < used).astype(jnp.int32)
    last_e = te_raw[jnp.maximum(used - 1, 0)]
    te = jnp.where(tvalid > 0, te_raw, last_e).astype(jnp.int32)
    dest_tiles = dest.reshape(ne, n_chunks, tm)
    big = jnp.int32(2 ** 30)
    d_max = jnp.max(dest_tiles, axis=2)
    d_min = jnp.min(jnp.where(dest_tiles >= 0, dest_tiles, big), axis=2)
    overlap = (d_max[te] >= t_idx[:, None] * tm) & (d_min[te] < (t_idx[:, None] + 1) * tm) & (tvalid[:, None] > 0)
    c_idx = jnp.arange(n_chunks, dtype=jnp.int32)[None, :]
    c_lo = jnp.min(jnp.where(overlap, c_idx, n_chunks), axis=1)
    c_hi = jnp.max(jnp.where(overlap, c_idx, -1), axis=1)
    c_lo = jnp.where(c_hi >= 0, c_lo, 0).astype(jnp.int32)
    p_tile, p_chunk, p_flag = _work_list(c_lo, c_hi.astype(jnp.int32), n_tiles + ne * n_chunks)
    p_exp = te[p_tile]

    cw_t = comb[:, :ne].T.reshape(ne, 1, s)
    dest3 = dest.reshape(ne, 1, s)
    xg, roww = pl.pallas_call(
        _permute_kernel,
        grid_spec=pltpu.PrefetchScalarGridSpec(
            num_scalar_prefetch=4, grid=(p_tile.shape[0],),
            in_specs=[pl.BlockSpec((1, 1, tm), lambda k, pt, pc, pf, pe: (pe[k], 0, pc[k])),
                      pl.BlockSpec((1, 1, tm), lambda k, pt, pc, pf, pe: (pe[k], 0, pc[k])),
                      pl.BlockSpec((tm, D_MODEL), lambda k, pt, pc, pf, pe: (pc[k], 0))],
            out_specs=(pl.BlockSpec((tm, D_MODEL), lambda k, pt, pc, pf, pe: (pt[k], 0)),
                       pl.BlockSpec((tm, 1), lambda k, pt, pc, pf, pe: (pt[k], 0))),
            scratch_shapes=[pltpu.VMEM((tm, D_MODEL), F32), pltpu.VMEM((tm, 1), F32)]),
        out_shape=(jax.ShapeDtypeStruct((rows, D_MODEL), BF16), jax.ShapeDtypeStruct((rows, 1), F32)),
        compiler_params=_cparams("arbitrary"), name="moe_permute",
    )(p_tile, p_chunk, p_flag, p_exp, dest3, cw_t, h)

    nf = wg.shape[2] // tf
    last_f = nf - 1
    fsel = lambda t, f, tv: f * tv[t] + last_f * (1 - tv[t])
    yg = pl.pallas_call(
        _moe_ffn_kernel,
        grid_spec=pltpu.PrefetchScalarGridSpec(
            num_scalar_prefetch=2, grid=(n_tiles, nf),
            in_specs=[pl.BlockSpec((tm, D_MODEL), lambda t, f, te_, tv: (t, 0)),
                      pl.BlockSpec((tm, 1), lambda t, f, te_, tv: (t, 0)),
                      pl.BlockSpec((1, D_MODEL, tf), lambda t, f, te_, tv: (te_[t], 0, fsel(t, f, tv))),
                      pl.BlockSpec((1, D_MODEL, tf), lambda t, f, te_, tv: (te_[t], 0, fsel(t, f, tv))),
                      pl.BlockSpec((1, tf, D_MODEL), lambda t, f, te_, tv: (te_[t], fsel(t, f, tv), 0))],
            out_specs=pl.BlockSpec((tm, D_MODEL), lambda t, f, te_, tv: (t, 0)),
            scratch_shapes=[pltpu.VMEM((tm, D_MODEL), F32)]),
        out_shape=jax.ShapeDtypeStruct((rows, D_MODEL), BF16),
        compiler_params=_cparams("arbitrary", "arbitrary"), name="moe_ffn",
    )(te, tvalid, xg, roww, wg, wu, wd)

    grow = jnp.stack([jnp.min(jnp.where(dest >= 0, dest, big), axis=0), jnp.max(dest, axis=0)], axis=1)
    has = d_max >= 0
    first = jnp.where(has, d_min // tm, 0).T.reshape(-1).astype(jnp.int32)
    last = jnp.where(has, d_max // tm, -1).T.reshape(-1).astype(jnp.int32)
    u_sub, u_chunk, u_flag = _work_list(first, last, n_chunks * 2 * ne)
    u_tile = (u_sub // ne).astype(jnp.int32)
    prev_tile = jnp.concatenate([jnp.full((1,), -1, jnp.int32), u_tile[:-1]])
    u_flag = (u_flag & 4) | jnp.where((u_flag != 0) & (u_tile != prev_tile), 1, 0)
    out = pl.pallas_call(
        _unpermute_kernel,
        grid_spec=pltpu.PrefetchScalarGridSpec(
            num_scalar_prefetch=3, grid=(u_tile.shape[0],),
            in_specs=[pl.BlockSpec((tm, 2), lambda k, ut, uc, uf: (ut[k], 0)),
                      pl.BlockSpec((tm, D_MODEL), lambda k, ut, uc, uf: (ut[k], 0)),
                      pl.BlockSpec((tm, D_MODEL), lambda k, ut, uc, uf: (uc[k], 0))],
            out_specs=pl.BlockSpec((tm, D_MODEL), lambda k, ut, uc, uf: (ut[k], 0))),
        out_shape=jax.ShapeDtypeStruct((s, D_MODEL), F32),
        compiler_params=_cparams("arbitrary"), name="moe_unpermute",
    )(u_tile, u_chunk, u_flag, grow, x1, yg)
    return out


def _t5_bucket(rel):
    n = jnp.maximum(rel, 0)
    max_exact = NUM_BUCKETS // 2
    nf = jnp.maximum(n, 1).astype(jnp.float32)
    large = max_exact + (jnp.log(nf / max_exact) / math.log(MAX_DISTANCE / max_exact)
                         * (NUM_BUCKETS - max_exact)).astype(jnp.int32)
    large = jnp.minimum(large, NUM_BUCKETS - 1)
    return jnp.where(n < max_exact, n, large)


def _bucket_thresholds():
    buckets = _t5_bucket(jnp.arange(MAX_DISTANCE + 1, dtype=jnp.int32))
    b = jnp.arange(NUM_BUCKETS, dtype=jnp.int32)
    return jnp.sum(buckets[None, :] < b[:, None], axis=1).astype(jnp.int32)


def _rope_tables(s):
    half = MLA_ROPE // 2
    inv = ROPE_THETA ** (-jnp.arange(half, dtype=jnp.float32) / half)
    ang = jnp.arange(s).astype(jnp.float32)[:, None] * inv[None, :]
    cos, sin = jnp.cos(ang), jnp.sin(ang)
    z = lambda w: jnp.zeros((s, w), F32)
    pad = MLA_PAD - MLA_QK
    cos_t = jnp.concatenate([jnp.ones((s, MLA_NOPE), F32), cos, cos, z(pad)], axis=1)
    sin_lo = jnp.concatenate([z(MLA_NOPE), -sin, z(half), z(pad)], axis=1)
    sin_hi = jnp.concatenate([z(MLA_NOPE), z(half), sin, z(pad)], axis=1)
    return cos_t, sin_lo, sin_hi


def _pad_lanes(a, width):
    return jnp.pad(a, [(0, 0)] * (a.ndim - 1) + [(0, width - a.shape[-1])])


def kernel(x, rel_bias_table, mix_norm, w_in, b_gate, sb_w_o, nsa_q_norm, nsa_k_norm, nsa_cmp_pos, nsa_cmp_w1, nsa_cmp_w2, nsa_w_o, mla_q_a_norm, mla_kv_a_norm, mla_w_uq, mla_w_ukv, mla_q_norm, mla_k_norm, mla_w_o, w_out, ffn_norm, dense_w_gate, dense_w_up, dense_w_down, moe_router, moe_w_gate, moe_w_up, moe_w_down):
    b, s, d = x.shape
    assert b == 1 and d == D_MODEL and s % TM_MOE == 0 and s // SEL_BLOCK >= SEL_TOPK
    depth = w_in.shape[0]
    xs = x.reshape(s, d)
    thr = _bucket_thresholds()
    cos_t, sin_lo, sin_hi = _rope_tables(s)
    n_chunk = s // CMP_STRIDE

    for layer in range(depth):
        w = w_in[layer]
        small = jnp.concatenate([
            w[:, _C_NG:_C_MCQ], jnp.zeros((d, _KR_LANE - NSA_HEADS * NSA_BRANCHES), F32),
            w[:, _C_MKR:_C_GATE], jnp.zeros((d, LANES - _KR_LANE - MLA_ROPE), F32)], axis=1)
        wa = jnp.concatenate([w[:, :_C_NG], w[:, _C_MCQ:_C_MKR], small], axis=1).astype(BF16)
        wg = w[:, _C_GATE:].astype(BF16)
        wuq = _pad_lanes(mla_w_uq[layer].reshape(MLA_Q_LORA, MLA_HEADS, MLA_QK), MLA_PAD)
        wuq = wuq.reshape(MLA_Q_LORA, MLA_HEADS * MLA_PAD).astype(BF16)
        wukv = mla_w_ukv[layer].reshape(MLA_KV_LORA, MLA_HEADS, MLA_NOPE + MLA_V)
        wukv = jnp.concatenate([
            _pad_lanes(wukv[:, :, :MLA_NOPE], MLA_PAD).reshape(MLA_KV_LORA, MLA_HEADS * MLA_PAD),
            wukv[:, :, MLA_NOPE:].reshape(MLA_KV_LORA, MLA_HEADS * MLA_V)], axis=1).astype(BF16)
        (sbq, sbk, sbv, nq, nkc, nvc, nks, nvs, nkw, nvw, gsm, mq, mk, mv, g) = _input_projection(
            xs, mix_norm[layer][None], wa, wg, b_gate[layer][None], wuq, wukv,
            nsa_q_norm[layer][None], nsa_k_norm[layer], mla_q_a_norm[layer][None],
            mla_kv_a_norm[layer][None], _pad_lanes(mla_q_norm[layer][None], MLA_PAD),
            _pad_lanes(mla_k_norm[layer][None], MLA_PAD), cos_t, sin_lo, sin_hi)

        o_sb = _stick_breaking(sbq, sbk, sbv)
        o_mla = _mla_attention(mq, mk, mv)

        x2 = jnp.stack([nkc, nvc]).reshape(2, NSA_GROUPS, n_chunk, CMP_STRIDE * HEAD_DIM)
        cmp, cmp_t = _compress(x2, nsa_cmp_pos[layer].reshape(2, 1, CMP_LEN * HEAD_DIM),
                               nsa_cmp_w1[layer].astype(BF16), nsa_cmp_w2[layer].astype(BF16),
                               nsa_k_norm[layer][0:1])
        gates = gsm[:NSA_HEADS * NSA_BRANCHES].reshape(NSA_GROUPS, NSA_HPG * NSA_BRANCHES, s)
        oc, sel = _nsa_compressed(thr, rel_bias_table, nq, cmp, cmp_t, gates)
        o_nsa = _nsa_selected(thr, rel_bias_table, nq, nks, nvs,
                              jnp.pad(nkw, ((0, 0), (WINDOW, 0), (0, 0))),
                              jnp.pad(nvw, ((0, 0), (0, 0), (WINDOW, 0))), sel, gates, oc)

        wsb = sb_w_o[layer].reshape(SB_HEADS, HEAD_DIM, d).astype(BF16)
        wnsa = nsa_w_o[layer].reshape(NSA_HEADS, HEAD_DIM, d).astype(BF16)
        wmla = mla_w_o[layer].reshape(MLA_HEADS, MLA_V, d).astype(BF16)
        wout = w_out[layer].astype(BF16)
        j = layer // 2
        if layer % 2 == 0:
            x1, h = _merge(o_sb, o_nsa, o_mla, g, xs, wsb, wnsa, wmla, wout, ffn_norm[layer][None])
            d_ff = dense_w_gate.shape[2]
            xs = _dense_ffn(x1, h, dense_w_gate[j].astype(BF16), dense_w_up[j].astype(BF16),
                            dense_w_down[j].astype(BF16), d_ff // 2)
        else:
            x1, h, comb, selm = _merge(o_sb, o_nsa, o_mla, g, xs, wsb, wnsa, wmla, wout,
                                       ffn_norm[layer][None], _pad_lanes(moe_router[j], LANES))
            xs = _moe_ffn(x1, h, comb, selm, moe_w_gate[j].astype(BF16), moe_w_up[j].astype(BF16),
                          moe_w_down[j].astype(BF16))
    return xs.reshape(b, s, d)
```

```python
import functools
import math

import jax
import jax.numpy as jnp
from jax import lax
from jax.experimental import pallas as pl
from jax.experimental.pallas import tpu as pltpu

F32 = jnp.float32
BF16 = jnp.bfloat16

D_MODEL = 1024
HEAD_DIM = 64
SB_HEADS = 4
NSA_HEADS = 8
NSA_GROUPS = 2
NSA_HPG = NSA_HEADS // NSA_GROUPS
NSA_BRANCHES = 3
MLA_HEADS = 4
MLA_NOPE = 64
MLA_ROPE = 32
MLA_V = 64
MLA_QK = MLA_NOPE + MLA_ROPE
MLA_Q_LORA = 256
MLA_KV_LORA = 128
ROPE_THETA = 10000.0
CMP_LEN = 32
CMP_STRIDE = 16
CMP_HIDDEN = 256
SEL_BLOCK = 64
SEL_TOPK = 16
N_LOCAL_BLOCKS = 2
WINDOW = 512
NUM_BUCKETS = 32
MAX_DISTANCE = 1024
N_MIXERS = 3
N_EXPERTS = 8
EPS = 1e-6
NEG_INF = -1e30
FORCE = 1e30
LOG2E = math.log2(math.e)

SB_W = SB_HEADS * HEAD_DIM
NSA_QW = NSA_HEADS * HEAD_DIM
NSA_KVW = NSA_GROUPS * HEAD_DIM
GATE_W = N_MIXERS * D_MODEL
LANES = 128
MLA_PAD = LANES

_C_NQ = 3 * SB_W
_C_NKV = _C_NQ + NSA_QW
_C_NG = _C_NKV + 6 * NSA_KVW
_C_MCQ = _C_NG + NSA_HEADS * NSA_BRANCHES
_C_MCKV = _C_MCQ + MLA_Q_LORA
_C_MKR = _C_MCKV + MLA_KV_LORA
_C_GATE = _C_MKR + MLA_ROPE
_A_SB = 0
_A_NQ = 3 * SB_W
_A_NKV = _A_NQ + NSA_QW
_A_MCQ = _A_NKV + 6 * NSA_KVW
_A_MCKV = _A_MCQ + MLA_Q_LORA
_A_SMALL = _A_MCKV + MLA_KV_LORA
_A_COLS = _A_SMALL + LANES
_KR_LANE = MLA_NOPE

TM_PROJ = 256
TQ_SB = 512
TK_SB = 256
TQ_MLA = 256
TK_MLA = 1024
CHUNK_MLA = 256
CHUNK_SEL = 512
TQ_CMP = 128
CMP_PARTS = 4
TQ_SEL = 256
TK_SEL = 512
TM_MERGE = 512
TM_MERGE_ROUTER = 256
TM_FFN = 512
TM_MOE = 512
TF_MOE = 512
VMEM_LIMIT = 56 * 1024 * 1024
SB_EXIT = -104.0

_NT = (((1,), (1,)), ((), ()))


def _dot(a, b):
    return jnp.dot(a, b, preferred_element_type=F32)


def _dot_nt(a, b):
    return lax.dot_general(a, b, _NT, preferred_element_type=F32)


def _cparams(*sem):
    return pltpu.CompilerParams(dimension_semantics=sem, vmem_limit_bytes=VMEM_LIMIT)


def _const_spec(shape):
    nd = len(shape)
    return pl.BlockSpec(shape, lambda *_: (0,) * nd, pipeline_mode=pl.Buffered(1))


def _smem_spec():
    return pl.BlockSpec(memory_space=pltpu.SMEM)


def _split3(x):
    h1 = x.astype(BF16)
    r1 = x - h1.astype(F32)
    h2 = r1.astype(BF16)
    h3 = (r1 - h2.astype(F32)).astype(BF16)
    return h1, h2, h3


def _bias_chain(rel, thr_ref, tab_ref, head):
    out = jnp.full(rel.shape, tab_ref[0, head], F32)
    for b in range(1, NUM_BUCKETS):
        out = jnp.where(rel >= thr_ref[b], tab_ref[b, head], out)
    return out


def _rope(y, cos, sin_lo, sin_hi):
    half = MLA_ROPE // 2
    return y * cos + pltpu.roll(y, half, 1) * sin_hi + pltpu.roll(y, MLA_PAD - half, 1) * sin_lo


def _proj_kernel(x_ref, gain_ref, wa_ref, wg_ref, bg_ref, wuq_ref, wukv_ref, nqg_ref, nkg_ref,
                 qag_ref, kvag_ref, mqg_ref, mkg_ref, cos_ref, slo_ref, shi_ref,
                 sbq_ref, sbk_ref, sbv_ref, nq_ref, nkvc_ref, nks_ref, nvs_ref, nkw_ref, nvw_ref,
                 gsm_ref, mq_ref, mk_ref, mv_ref, g_ref):
    x = x_ref[...]
    ms = jnp.mean(x * x, axis=-1, keepdims=True)
    h = (x * lax.rsqrt(ms + EPS) * gain_ref[...]).astype(BF16)

    def rms(p, gain, width):
        m = jnp.sum(p * p, axis=-1, keepdims=True) * (1.0 / width)
        return p * lax.rsqrt(m + EPS) * gain

    p = _dot(h, wa_ref[:, _A_SB:_A_SB + 3 * SB_W])
    sbq_t = (p[:, :SB_W] * HEAD_DIM ** -0.5).T.astype(BF16)
    sbv_t = p[:, 2 * SB_W:3 * SB_W].T.astype(BF16)
    for j in range(SB_HEADS):
        sbq_ref[j] = sbq_t[j * HEAD_DIM:(j + 1) * HEAD_DIM]
        sbv_ref[j] = sbv_t[j * HEAD_DIM:(j + 1) * HEAD_DIM]
    for j in range(SB_HEADS // 2):
        sbk_ref[j] = p[:, SB_W + j * LANES:SB_W + (j + 1) * LANES].astype(BF16)

    p = _dot(h, wa_ref[:, _A_NQ:_A_NQ + NSA_QW])
    qn = [rms(p[:, j * HEAD_DIM:(j + 1) * HEAD_DIM], nqg_ref[...], HEAD_DIM) * (HEAD_DIM ** -0.5 * LOG2E)
          for j in range(NSA_HEADS)]
    qn_t = jnp.concatenate(qn, axis=1).T.astype(BF16)
    for j in range(NSA_HEADS):
        nq_ref[j] = qn_t[j * HEAD_DIM:(j + 1) * HEAD_DIM]

    p = _dot(h, wa_ref[:, _A_NKV:_A_NKV + 6 * NSA_KVW])
    vs_t = p[:, 3 * NSA_KVW:4 * NSA_KVW].T.astype(BF16)
    tag_shape = (x.shape[0], LANES - HEAD_DIM)
    block_id = (pl.program_id(0) * x.shape[0] + lax.broadcasted_iota(jnp.int32, tag_shape, 0)) // SEL_BLOCK
    block_tag = jnp.where(lax.broadcasted_iota(jnp.int32, tag_shape, 1) == block_id % (TK_SEL // SEL_BLOCK),
                          1.0, 0.0)
    vw_t = p[:, 5 * NSA_KVW:6 * NSA_KVW].T.astype(BF16)
    for j in range(NSA_GROUPS):
        def seg(s):
            return p[:, s * NSA_KVW + j * HEAD_DIM:s * NSA_KVW + (j + 1) * HEAD_DIM]
        nkvc_ref[0, j] = seg(0)
        nkvc_ref[1, j] = seg(1)
        nks_ref[j] = jnp.concatenate([rms(seg(2), nkg_ref[1:2, :], HEAD_DIM), block_tag], axis=1).astype(BF16)
        nvs_ref[j] = vs_t[j * HEAD_DIM:(j + 1) * HEAD_DIM]
        nkw_ref[j] = rms(seg(4), nkg_ref[2:3, :], HEAD_DIM).astype(BF16)
        nvw_ref[j] = vw_t[j * HEAD_DIM:(j + 1) * HEAD_DIM]

    p = _dot(h, wa_ref[:, _A_MCQ:_A_COLS])
    cq = rms(p[:, :MLA_Q_LORA], qag_ref[...], MLA_Q_LORA).astype(BF16)
    ckv = rms(p[:, MLA_Q_LORA:MLA_Q_LORA + MLA_KV_LORA], kvag_ref[...], MLA_KV_LORA).astype(BF16)
    small = p[:, MLA_Q_LORA + MLA_KV_LORA:]
    gsm_ref[...] = jax.nn.sigmoid(small).T
    lane = lax.broadcasted_iota(jnp.int32, small.shape, 1)
    krope = jnp.where((lane >= _KR_LANE) & (lane < _KR_LANE + MLA_ROPE), small, 0.0)
    cos, slo, shi = cos_ref[...], slo_ref[...], shi_ref[...]
    qu = _dot(cq, wuq_ref[...])
    kvu = _dot(ckv, wukv_ref[...])
    for j in range(MLA_HEADS):
        qh = rms(qu[:, j * MLA_PAD:(j + 1) * MLA_PAD], mqg_ref[...], MLA_QK)
        mq_ref[j] = (_rope(qh, cos, slo, shi) * MLA_QK ** -0.5).T.astype(BF16)
        kh = rms(kvu[:, j * MLA_PAD:(j + 1) * MLA_PAD] + krope, mkg_ref[...], MLA_QK)
        mk_ref[j] = _rope(kh, cos, slo, shi).astype(BF16)
    v_t = kvu[:, MLA_HEADS * MLA_PAD:].T.astype(BF16)
    for j in range(MLA_HEADS):
        mv_ref[j] = v_t[j * MLA_V:(j + 1) * MLA_V]

    g_ref[...] = jax.nn.sigmoid(_dot(h, wg_ref[...]) + bg_ref[...])


def _input_projection(x, gain, wa, wg, bg, wuq, wukv, nqg, nkg, qag, kvag, mqg, mkg, cos, slo, shi):
    s = x.shape[0]
    tm = TM_PROJ
    row = lambda w: pl.BlockSpec((tm, w), lambda i: (i, 0))
    heads = lambda n, w: pl.BlockSpec((n, tm, w), lambda i: (0, i, 0))
    hs = lambda n, w, dt: jax.ShapeDtypeStruct((n, s, w), dt)
    in_specs = [row(D_MODEL)] + [_const_spec(a.shape) for a in
                                 (gain, wa, wg, bg, wuq, wukv, nqg, nkg, qag, kvag, mqg, mkg)]
    in_specs += [row(MLA_PAD)] * 3
    hst = lambda n, w, dt: jax.ShapeDtypeStruct((n, w, s), dt)
    heads_t = lambda n, w: pl.BlockSpec((n, w, tm), lambda i: (0, 0, i))
    ng, hd = NSA_GROUPS, HEAD_DIM
    out_shape = (hst(SB_HEADS, hd, BF16), hs(SB_HEADS // 2, LANES, BF16), hst(SB_HEADS, hd, BF16))
    out_shape += (hst(NSA_HEADS, hd, BF16),)
    out_shape += (jax.ShapeDtypeStruct((2, ng, s, hd), F32),) + (hs(ng, LANES, BF16), hst(ng, hd, BF16), hs(ng, hd, BF16), hst(ng, hd, BF16))
    out_shape += (jax.ShapeDtypeStruct((LANES, s), F32),)
    out_shape += (hst(MLA_HEADS, MLA_PAD, BF16), hs(MLA_HEADS, MLA_PAD, BF16), hst(MLA_HEADS, MLA_V, BF16))
    out_shape += (jax.ShapeDtypeStruct((s, GATE_W), F32),)
    out_specs = (heads_t(SB_HEADS, hd), heads(SB_HEADS // 2, LANES), heads_t(SB_HEADS, hd))
    out_specs += (heads_t(NSA_HEADS, hd),)
    out_specs += (pl.BlockSpec((2, ng, tm, hd), lambda i: (0, 0, i, 0)),) + (heads(ng, LANES), heads_t(ng, hd), heads(ng, hd), heads_t(ng, hd))
    out_specs += (pl.BlockSpec((LANES, tm), lambda i: (0, i)),)
    out_specs += (heads_t(MLA_HEADS, MLA_PAD), heads(MLA_HEADS, MLA_PAD), heads_t(MLA_HEADS, MLA_V), row(GATE_W))
    return pl.pallas_call(
        _proj_kernel, grid=(s // tm,), in_specs=in_specs, out_specs=out_specs, out_shape=out_shape,
        compiler_params=_cparams("parallel"), name="input_projection",
    )(x, gain, wa, wg, bg, wuq, wukv, nqg, nkg, qag, kvag, mqg, mkg, cos, slo, shi)


def _sb_kernel(qt_ref, k2_ref, vt_ref, o_ref):
    tq, tk = TQ_SB, TK_SB
    nh = qt_ref.shape[0]
    i = pl.program_id(0)
    later = (lax.broadcasted_iota(jnp.int32, (tk, tk), 1)
             > lax.broadcasted_iota(jnp.int32, (tk, tk), 0)).astype(BF16)
    key = lax.broadcasted_iota(jnp.int32, (tk, tq), 0)
    qry = i * tq + lax.broadcasted_iota(jnp.int32, (tk, tq), 1)
    zeros = jnp.zeros((HEAD_DIM, tq), BF16)
    q_aug = [jnp.concatenate([qt_ref[h], zeros] if h % 2 == 0 else [zeros, qt_ref[h]], axis=0) for h in range(nh)]

    def tile(kt, h, carry, acc, diagonal):
        off = pl.multiple_of(kt * tk, tk)
        z = _dot(k2_ref[h // 2, pl.ds(off, tk), :], q_aug[h])
        log_stay = -(jnp.maximum(z, 0.0) + jnp.log(1.0 + jnp.exp(-jnp.abs(z))))
        if diagonal:
            causal = off + key < qry
            log_stay = jnp.where(causal, log_stay, 0.0)
        hi = log_stay.astype(BF16)
        lo = (log_stay - hi.astype(F32)).astype(BF16)
        suffix = _dot(later, hi) + _dot(later, lo)
        logw = z + log_stay + (carry + suffix)
        if diagonal:
            logw = jnp.where(causal, logw, NEG_INF)
        w = jnp.exp(logw).astype(BF16)
        acc = acc + _dot(vt_ref[h, :, pl.ds(off, tk)], w)
        carry = carry + suffix[0:1, :] + log_stay[0:1, :]
        return carry, acc

    def all_heads(kt, st, diagonal):
        out = [tile(kt, h, st[2 * h], st[2 * h + 1], diagonal) for h in range(nh)]
        return tuple(x for pair in out for x in pair)

    ratio = tq // tk
    st = (jnp.zeros((1, tq), F32), jnp.zeros((HEAD_DIM, tq), F32)) * nh
    for d in range(ratio):
        st = all_heads(ratio * i + ratio - 1 - d, st, True)

    def cond(c):
        top = c[1]
        for h in range(1, nh):
            top = jnp.maximum(top, c[1 + 2 * h])
        return jnp.logical_and(c[0] >= 0, jnp.max(top) > SB_EXIT)

    def body(c):
        return (c[0] - 1,) + all_heads(c[0], c[1:], False)

    c = lax.while_loop(cond, body, (ratio * i - 1,) + st)
    for h in range(nh):
        o_ref[h] = c[2 + 2 * h]


def _stick_breaking(q_t, k2, v_t):
    nh, _, s = q_t.shape
    tq = TQ_SB
    whole = lambda a: pl.BlockSpec(a.shape, lambda i: (0, 0, 0), pipeline_mode=pl.Buffered(1))
    return pl.pallas_call(
        _sb_kernel, grid=(s // tq,),
        in_specs=[pl.BlockSpec((nh, HEAD_DIM, tq), lambda i: (0, 0, i)), whole(k2), whole(v_t)],
        out_specs=pl.BlockSpec((nh, HEAD_DIM, tq), lambda i: (0, 0, i)),
        out_shape=jax.ShapeDtypeStruct((nh, HEAD_DIM, s), F32),
        compiler_params=_cparams("parallel"), name="stick_breaking",
    )(q_t, k2, v_t)


ONES_ROWS = 16


def _with_ones_rows(v_t):
    return jnp.concatenate([v_t, jnp.ones((ONES_ROWS, v_t.shape[1]), v_t.dtype)], axis=0)


def _pipelined_blocks(n_last, score_chunk, value_chunk, n_chunks, chunk, exp_fn, bufs, st):
    rows = lambda c: pl.ds(c * chunk, chunk)
    bufs, stat_ref, acc_ref = bufs[:3], bufs[3], bufs[4]
    m_row = 3
    stat_ref[m_row:m_row + 1, :] = st[0]
    acc_ref[...] = st[1]

    def prepare(kb, slot, maybe_diagonal):
        mx = None
        for c in range(n_chunks):
            s = score_chunk(kb, c, maybe_diagonal)
            bufs[slot][rows(c), :] = s
            smax = jnp.max(s, axis=0, keepdims=True)
            mx = smax if mx is None else jnp.maximum(mx, smax)
        stat_ref[slot:slot + 1, :] = mx

    def consume(kb, slot):
        m = stat_ref[m_row:m_row + 1, :]
        m_new = jnp.maximum(m, stat_ref[slot:slot + 1, :])
        stat_ref[m_row:m_row + 1, :] = m_new
        part = None
        for c in range(n_chunks):
            p = exp_fn(bufs[slot][rows(c), :] - m_new).astype(BF16)
            d = _dot(value_chunk(kb, c), p)
            part = d if part is None else part + d
        acc_ref[...] = exp_fn(m - m_new) * acc_ref[...] + part

    def run(kb0, n_blocks, last_is_diagonal):
        n_prep = n_blocks if last_is_diagonal else n_blocks + 2
        for j in range(n_blocks):
            if j + 2 < n_prep:
                prepare(kb0 + j + 2, (j + 2) % 3, last_is_diagonal and j + 2 == n_blocks - 1)
            consume(kb0 + j, j % 3)

    trips = jnp.maximum(n_last - 2, 0) // 3
    prepare(0, 0, True)
    prepare(1, 1, True)

    def trip(t, carry):
        run(3 * t, 3, False)
        return carry

    lax.fori_loop(0, trips, trip, 0)
    kb0 = 3 * trips
    for r in range(5):
        pl.when(n_last - kb0 == r)(functools.partial(run, kb0, r + 1, True))
    return stat_ref[m_row:m_row + 1, :], acc_ref[...]


def _mla_kernel(qt_ref, k_ref, vt_ref, o_ref, *bufs):
    tq, tk = TQ_MLA, TK_MLA
    i = pl.program_id(1)
    q_t = qt_ref[0]

    ck = CHUNK_MLA

    def score_chunk(kb, c, maybe_diagonal):
        off = pl.multiple_of(kb * tk + c * ck, ck)
        s = _dot(k_ref[0, pl.ds(off, ck), :], q_t)
        if maybe_diagonal:
            key = off + lax.broadcasted_iota(jnp.int32, (ck, tq), 0)
            qry = i * tq + lax.broadcasted_iota(jnp.int32, (ck, tq), 1)
            s = jnp.where(key <= qry, s, NEG_INF)
        return s

    def value_chunk(kb, c):
        off = pl.multiple_of(kb * tk + c * ck, ck)
        return _with_ones_rows(vt_ref[0, :, pl.ds(off, ck)])

    st = (jnp.full((1, tq), NEG_INF, F32), jnp.zeros((MLA_V + ONES_ROWS, tq), F32))
    _, acc = _pipelined_blocks((i * tq) // tk, score_chunk, value_chunk, tk // ck, ck, jnp.exp,
                               bufs, st)
    o_ref[0] = acc[:MLA_V] / acc[MLA_V:MLA_V + 1]


def _mla_attention(q_t, k, v_t):
    nh, s, _ = k.shape
    tq, tk = TQ_MLA, TK_MLA
    return pl.pallas_call(
        _mla_kernel, grid=(nh, s // tq),
        in_specs=[pl.BlockSpec((1, MLA_PAD, tq), lambda h, i: (h, 0, i)),
                  pl.BlockSpec((1, s, MLA_PAD), lambda h, i: (h, 0, 0)),
                  pl.BlockSpec((1, MLA_V, s), lambda h, i: (h, 0, 0))],
        out_specs=pl.BlockSpec((1, MLA_V, tq), lambda h, i: (h, 0, i)),
        out_shape=jax.ShapeDtypeStruct((nh, MLA_V, s), F32),
        scratch_shapes=[pltpu.VMEM((tk, tq), F32)] * 3 + [pltpu.VMEM((8, tq), F32),
                                                          pltpu.VMEM((MLA_V + ONES_ROWS, tq), F32)],
        compiler_params=_cparams("parallel", "parallel"), name="mla_attention",
    )(q_t, k, v_t)


def _compress_kernel(x_ref, pos_ref, w1_ref, w2_ref, kg_ref, o_ref, ot_ref):
    kv = pl.program_id(0)
    half = CMP_STRIDE * HEAD_DIM
    a = x_ref[0, 0].astype(BF16)
    w1 = w1_ref[0]
    first = _dot(a, w1[:half])
    second = _dot(a, w1[half:])
    n = second.shape[0]
    second = pltpu.roll(second, n - 1, 0)
    posb = jnp.broadcast_to(pos_ref[0], (8, 2 * half)).astype(BF16)
    hid = first + second + _dot(posb, w1)[0:1]
    hid = hid * jax.nn.sigmoid(hid)
    out = _dot(hid.astype(BF16), w2_ref[0])
    ms = jnp.mean(out * out, axis=-1, keepdims=True)
    normed = out * lax.rsqrt(ms + EPS) * kg_ref[...]
    res = jnp.where(kv == 0, normed, out)
    o_ref[0, 0] = res
    ot_ref[0, 0] = res.T


def _compress(x2, pos, w1, w2, kgain):
    _, ng, nc, cw = x2.shape
    return pl.pallas_call(
        _compress_kernel, grid=(2, ng),
        in_specs=[pl.BlockSpec((1, 1, nc, cw), lambda a, g: (a, g, 0, 0)),
                  pl.BlockSpec((1, 1, 2 * cw), lambda a, g: (a, 0, 0)),
                  pl.BlockSpec((1, 2 * cw, CMP_HIDDEN), lambda a, g: (a, 0, 0)),
                  pl.BlockSpec((1, CMP_HIDDEN, HEAD_DIM), lambda a, g: (a, 0, 0)),
                  _const_spec(kgain.shape)],
        out_specs=(pl.BlockSpec((1, 1, nc, HEAD_DIM), lambda a, g: (a, g, 0, 0)),
                   pl.BlockSpec((1, 1, HEAD_DIM, nc), lambda a, g: (a, g, 0, 0))),
        out_shape=(jax.ShapeDtypeStruct((2, ng, nc, HEAD_DIM), F32),
                   jax.ShapeDtypeStruct((2, ng, HEAD_DIM, nc), F32)),
        compiler_params=_cparams("parallel", "parallel"), name="nsa_compress",
    )(x2, pos, w1, w2, kgain)


def _nsa_cmp_kernel(thr_ref, tab_ref, qt_ref, kc_ref, vct_ref, gt_ref, oc_ref, selt_ref, band_ref, pool_ref):
    tq = TQ_CMP
    r_n = NSA_HPG
    g = pl.program_id(0)
    i = pl.program_id(1)
    nc = kc_ref.shape[2]
    nb = selt_ref.shape[1]
    cpt = tq // CMP_STRIDE

    @pl.when(i == 0)
    def _init():
        m = lax.broadcasted_iota(jnp.int32, (2 * nc, tq), 0)
        a = lax.broadcasted_iota(jnp.int32, (2 * nc, tq), 1)
        rel = a - CMP_STRIDE * (m - nc) - (CMP_LEN - 1)
        for r in range(r_n):
            band_ref[r] = _bias_chain(rel, thr_ref, tab_ref, g * r_n + r) * LOG2E
        b = lax.broadcasted_iota(jnp.int32, (nb, nc), 0)
        n = lax.broadcasted_iota(jnp.int32, (nb, nc), 1)
        rs = SEL_BLOCK // CMP_STRIDE
        rc = CMP_LEN // CMP_STRIDE
        member = (n >= rs * b - (rc - 1)) & (n <= rs * b + rs - 1) & (n < nc - (rc - 1))
        pool_ref[...] = jnp.where(member, 1.0, 0.0).astype(BF16)

    q_t = jnp.concatenate([qt_ref[r] for r in range(r_n)], axis=1)
    boff = pl.multiple_of(nc - cpt * i, cpt)
    gates = gt_ref[0]

    def tile_body(nc_e, nb_e):
        s = _dot(kc_ref[0, 0, :nc_e, :].astype(BF16), q_t)
        qpos = i * tq + lax.broadcasted_iota(jnp.int32, (nc_e, tq), 1)
        cmp_end = lax.broadcasted_iota(jnp.int32, (nc_e, tq), 0) * CMP_STRIDE + (CMP_LEN - 1)
        valid = cmp_end <= qpos
        imp = jnp.zeros((nc_e, tq), F32)
        probs = []
        for r in range(r_n):
            sr = s[:, r * tq:(r + 1) * tq] + band_ref[r, pl.ds(boff, nc_e), :]
            sr = jnp.where(valid, sr, NEG_INF)
            mx = jnp.max(sr, axis=0, keepdims=True)
            e = jnp.where(valid, jnp.exp2(sr - mx), 0.0)
            den = jnp.sum(e, axis=0, keepdims=True)
            p = e / jnp.where(den > 0.0, den, 1.0)
            imp = imp + p
            probs.append(p.astype(BF16))
        oc = _dot(vct_ref[0, 0, :, :nc_e].astype(BF16), jnp.concatenate(probs, axis=1))
        for r in range(r_n):
            c = NSA_BRANCHES * r
            oc_ref[r] = oc[:, r * tq:(r + 1) * tq] * gates[c:c + 1, :]

        pool = pool_ref[:nb_e, :nc_e]
        h1, h2, h3 = _split3(imp)
        imp_blk = _dot(pool, h1) + _dot(pool, h2) + _dot(pool, h3)
        blk = lax.broadcasted_iota(jnp.int32, (nb_e, tq), 0)
        qp = i * tq + lax.broadcasted_iota(jnp.int32, (nb_e, tq), 1)
        causal_b = blk * SEL_BLOCK <= qp
        dist = qp // SEL_BLOCK - blk
        forced = (blk == 0) | ((dist >= 0) & (dist < N_LOCAL_BLOCKS))
        score = jnp.where(causal_b, jnp.where(forced, FORCE, imp_blk), NEG_INF)
        sel = jnp.zeros((nb_e, tq), F32)
        for _ in range(min(SEL_TOPK, nb_e)):
            mx = jnp.max(score, axis=0, keepdims=True)
            first = jnp.min(jnp.where(score == mx, blk, nb_e), axis=0, keepdims=True)
            pick = blk == first
            sel = jnp.where(pick & (mx > 0.5 * NEG_INF), 1.0, sel)
            score = jnp.where(pick, -jnp.inf, score)
        selt_ref[0, :nb_e, :] = sel
        if nb_e < nb:
            selt_ref[0, nb_e:, :] = jnp.zeros((nb - nb_e, tq), F32)

    tiles_per_part = pl.num_programs(1) // CMP_PARTS
    part = i // tiles_per_part
    for j in range(CMP_PARTS):
        pl.when(part == j)(functools.partial(tile_body, (j + 1) * nc // CMP_PARTS, (j + 1) * nb // CMP_PARTS))


def _nsa_compressed(thr, tab, nq_t, cmp, cmp_t, gates_t):
    nh, _, s = nq_t.shape
    ng = NSA_GROUPS
    tq = TQ_CMP
    nc = cmp.shape[2]
    nb = s // SEL_BLOCK
    qspec = pl.BlockSpec((NSA_HPG, HEAD_DIM, tq), lambda g, i: (g, 0, i))
    return pl.pallas_call(
        _nsa_cmp_kernel, grid=(ng, s // tq),
        in_specs=[_smem_spec(), _smem_spec(), qspec,
                  pl.BlockSpec((1, 1, nc, HEAD_DIM), lambda g, i: (0, g, 0, 0)),
                  pl.BlockSpec((1, 1, HEAD_DIM, nc), lambda g, i: (1, g, 0, 0)),
                  pl.BlockSpec((1, NSA_HPG * NSA_BRANCHES, tq), lambda g, i: (g, 0, i))],
        out_specs=(qspec, pl.BlockSpec((1, nb, tq), lambda g, i: (g, 0, i))),
        out_shape=(jax.ShapeDtypeStruct((nh, HEAD_DIM, s), F32),
                   jax.ShapeDtypeStruct((ng, nb, s), F32)),
        scratch_shapes=[pltpu.VMEM((NSA_HPG, 2 * nc, tq), F32), pltpu.VMEM((nb, nc), BF16)],
        compiler_params=_cparams("arbitrary", "arbitrary"), name="nsa_compressed",
    )(thr, tab, nq_t, cmp, cmp_t, gates_t)


_NEAR_SPAN = ((MAX_DISTANCE + TK_SEL - 2) // TQ_SEL) * TQ_SEL
assert _NEAR_SPAN >= MAX_DISTANCE and TK_SEL % TQ_SEL == 0
_STRIP = 2 * TK_SEL + _NEAR_SPAN
_WIN_KEYS = WINDOW + TQ_SEL
_BLOCKS_PER_TILE = TK_SEL // SEL_BLOCK


def _nsa_sel_kernel(thr_ref, tab_ref, qt_ref, ks_ref, vst_ref, kw_ref, vwt_ref, selt_ref, gt_ref, oc_ref,
                    o_ref, strip_ref, *bufs):
    tq, tk, r_n = TQ_SEL, TK_SEL, NSA_HPG
    g = pl.program_id(0)
    i = pl.program_id(1)
    top = tk + _NEAR_SPAN
    lanes = r_n * tq

    @pl.when(i == 0)
    def _init():
        u = lax.broadcasted_iota(jnp.int32, (_STRIP, tq), 0)
        a = lax.broadcasted_iota(jnp.int32, (_STRIP, tq), 1)
        for r in range(r_n):
            strip_ref[:, r * tq:(r + 1) * tq] = _bias_chain(a - u + top, thr_ref, tab_ref, g * r_n + r) * LOG2E

    q_t = jnp.concatenate([qt_ref[r] for r in range(r_n)], axis=1)
    pad_rows = jnp.zeros((LANES - HEAD_DIM - _BLOCKS_PER_TILE, lanes), F32)
    qpos = i * tq + (lax.broadcasted_iota(jnp.int32, (1, lanes), 1) & (tq - 1))

    ck = CHUNK_SEL

    def score_chunk(kb, c, maybe_diagonal):
        off = pl.multiple_of(kb * tk + c * ck, ck)
        boff = pl.multiple_of(kb * _BLOCKS_PER_TILE, _BLOCKS_PER_TILE)
        picked = selt_ref[0, pl.ds(boff, _BLOCKS_PER_TILE), :]
        penalty = jnp.where(picked > 0.5, 0.0, NEG_INF)
        penalty = jnp.concatenate([penalty] * r_n, axis=1)
        q_aug = jnp.concatenate([q_t, jnp.concatenate([penalty, pad_rows], axis=0).astype(BF16)], axis=0)
        soff = pl.multiple_of(jnp.clip(top - (i * tq - kb * tk), 0, top) + c * ck, ck)
        s = _dot(ks_ref[0, pl.ds(off, ck), :], q_aug) + strip_ref[pl.ds(soff, ck), :]
        if maybe_diagonal:
            key = off + lax.broadcasted_iota(jnp.int32, (ck, lanes), 0)
            s = jnp.where(key <= qpos, s, NEG_INF)
        return s

    def value_chunk(kb, c):
        off = pl.multiple_of(kb * tk + c * ck, ck)
        return _with_ones_rows(vst_ref[0, :, pl.ds(off, ck)])

    st = (jnp.full((1, lanes), NEG_INF, F32), jnp.zeros((HEAD_DIM + ONES_ROWS, lanes), F32))
    _, acc = _pipelined_blocks((i * tq) // tk, score_chunk, value_chunk, tk // ck, ck, jnp.exp2,
                               bufs, st)
    o_sel = acc[:HEAD_DIM] / acc[HEAD_DIM:HEAD_DIM + 1]

    woff = pl.multiple_of(i * tq, tq)
    w0 = top - WINDOW
    sw = _dot(kw_ref[0, pl.ds(woff, _WIN_KEYS), :], q_t) + strip_ref[w0:w0 + _WIN_KEYS, :]
    kpos = i * tq - WINDOW + lax.broadcasted_iota(jnp.int32, (_WIN_KEYS, lanes), 0)
    rel = qpos - kpos
    sw = jnp.where((rel >= 0) & (rel < WINDOW) & (kpos >= 0), sw, NEG_INF)
    pw = jnp.exp2(sw - jnp.max(sw, axis=0, keepdims=True)).astype(BF16)
    o_win = _dot(_with_ones_rows(vwt_ref[0, :, pl.ds(woff, _WIN_KEYS)]), pw)
    o_win = o_win[:HEAD_DIM] / o_win[HEAD_DIM:HEAD_DIM + 1]

    gates = gt_ref[0]
    for r in range(r_n):
        c0 = NSA_BRANCHES * r
        sl = slice(r * tq, (r + 1) * tq)
        o_ref[r] = oc_ref[r] + gates[c0 + 1:c0 + 2, :] * o_sel[:, sl] + gates[c0 + 2:c0 + 3, :] * o_win[:, sl]


def _nsa_selected(thr, tab, nq_t, ks, vs_t, kw_pad, vw_t_pad, sel_t, gates_t, oc):
    nh, _, s = nq_t.shape
    ng, tq = NSA_GROUPS, TQ_SEL
    nb = s // SEL_BLOCK
    qspec = pl.BlockSpec((NSA_HPG, HEAD_DIM, tq), lambda g, i: (g, 0, i))
    rows = lambda n: pl.BlockSpec((1, n, HEAD_DIM), lambda g, i: (g, 0, 0), pipeline_mode=pl.Buffered(1))
    cols = lambda n: pl.BlockSpec((1, HEAD_DIM, n), lambda g, i: (g, 0, 0), pipeline_mode=pl.Buffered(1))
    return pl.pallas_call(
        _nsa_sel_kernel, grid=(ng, s // tq),
        in_specs=[_smem_spec(), _smem_spec(), qspec,
                  pl.BlockSpec((1, s, LANES), lambda g, i: (g, 0, 0), pipeline_mode=pl.Buffered(1)),
                  cols(s), rows(s + WINDOW), cols(s + WINDOW),
                  pl.BlockSpec((1, nb, tq), lambda g, i: (g, 0, i)),
                  pl.BlockSpec((1, NSA_HPG * NSA_BRANCHES, tq), lambda g, i: (g, 0, i)), qspec],
        out_specs=qspec,
        out_shape=jax.ShapeDtypeStruct((nh, HEAD_DIM, s), F32),
        scratch_shapes=[pltpu.VMEM((_STRIP, NSA_HPG * tq), F32)] + [pltpu.VMEM((TK_SEL, NSA_HPG * tq), F32)] * 3
        + [pltpu.VMEM((8, NSA_HPG * tq), F32), pltpu.VMEM((HEAD_DIM + ONES_ROWS, NSA_HPG * tq), F32)],
        compiler_params=_cparams("arbitrary", "arbitrary"), name="nsa_selected",
    )(thr, tab, nq_t, ks, vs_t, kw_pad, vw_t_pad, sel_t, gates_t, oc)


def _merge_body(osb_ref, onsa_ref, omla_ref, g_ref, x_ref, wsb_ref, wnsa_ref, wmla_ref, wout_ref, fg_ref):
    def heads_proj(o_ref, w_ref):
        nh, hd = w_ref.shape[0], w_ref.shape[1]
        if o_ref.shape[1] == hd:
            o = o_ref[...].reshape(nh * hd, o_ref.shape[2]).T.astype(BF16)
            return _dot(o, w_ref[...].reshape(nh * hd, w_ref.shape[2]))
        y = _dot(o_ref[0], w_ref[0])
        for j in range(1, nh):
            y = y + _dot(o_ref[j], w_ref[j])
        return y

    merged = g_ref[:, 0:D_MODEL] * heads_proj(osb_ref, wsb_ref)
    merged = merged + g_ref[:, D_MODEL:2 * D_MODEL] * heads_proj(onsa_ref, wnsa_ref)
    merged = merged + g_ref[:, 2 * D_MODEL:3 * D_MODEL] * heads_proj(omla_ref, wmla_ref)
    x1 = x_ref[...] + _dot(merged.astype(BF16), wout_ref[...])
    ms = jnp.mean(x1 * x1, axis=-1, keepdims=True)
    return x1, x1 * lax.rsqrt(ms + EPS) * fg_ref[...]


def _merge_kernel(osb_ref, onsa_ref, omla_ref, g_ref, x_ref, wsb_ref, wnsa_ref, wmla_ref, wout_ref, fg_ref,
                  x1_ref, h_ref):
    x1, h = _merge_body(osb_ref, onsa_ref, omla_ref, g_ref, x_ref, wsb_ref, wnsa_ref, wmla_ref, wout_ref,
                        fg_ref)
    x1_ref[...] = x1
    h_ref[...] = h.astype(BF16)


def _merge_router_kernel(osb_ref, onsa_ref, omla_ref, g_ref, x_ref, wsb_ref, wnsa_ref, wmla_ref, wout_ref,
                         fg_ref, wr_ref, x1_ref, h_ref, comb_ref, sel_ref):
    x1, h = _merge_body(osb_ref, onsa_ref, omla_ref, g_ref, x_ref, wsb_ref, wnsa_ref, wmla_ref, wout_ref,
                        fg_ref)
    x1_ref[...] = x1
    h_ref[...] = h.astype(BF16)
    wr = wr_ref[...]
    hp = _split3(h)
    wp = _split3(wr)
    logits = jnp.zeros((h.shape[0], wr.shape[1]), F32)
    for a_i, b_i in ((2, 0), (0, 2), (1, 1), (1, 0), (0, 1), (0, 0)):
        logits = logits + _dot(hp[a_i], wp[b_i])
    lane = lax.broadcasted_iota(jnp.int32, logits.shape, 1)
    nl = logits.shape[1]
    logits = jnp.where(lane < N_EXPERTS, logits, -jnp.inf)
    v1 = jnp.max(logits, axis=-1, keepdims=True)
    i1 = jnp.min(jnp.where(logits == v1, lane, nl), axis=-1, keepdims=True)
    rest = jnp.where(lane == i1, -jnp.inf, logits)
    v2 = jnp.max(rest, axis=-1, keepdims=True)
    i2 = jnp.min(jnp.where(rest == v2, lane, nl), axis=-1, keepdims=True)
    e2 = jnp.exp(v2 - v1)
    w1 = 1.0 / (1.0 + e2)
    w2 = e2 / (1.0 + e2)
    comb_ref[...] = jnp.where(lane == i1, w1, jnp.where(lane == i2, w2, 0.0))
    sel_ref[...] = jnp.where((lane == i1) | (lane == i2), 1.0, 0.0)


def _merge(osb, onsa, omla, g, x, wsb, wnsa, wmla, wout, fgain, wrouter=None):
    s = x.shape[0]
    tm = TM_MERGE if wrouter is None else TM_MERGE_ROUTER

    def heads(a):
        if a.shape[1] == s:
            return pl.BlockSpec((a.shape[0], tm, a.shape[2]), lambda i: (0, i, 0))
        return pl.BlockSpec((a.shape[0], a.shape[1], tm), lambda i: (0, 0, i))

    row = lambda w: pl.BlockSpec((tm, w), lambda i: (i, 0))
    in_specs = [heads(osb), heads(onsa), heads(omla), row(GATE_W), row(D_MODEL)]
    in_specs += [_const_spec(a.shape) for a in (wsb, wnsa, wmla, wout, fgain)]
    out_shape = [jax.ShapeDtypeStruct((s, D_MODEL), F32), jax.ShapeDtypeStruct((s, D_MODEL), BF16)]
    out_specs = [row(D_MODEL), row(D_MODEL)]
    args = [osb, onsa, omla, g, x, wsb, wnsa, wmla, wout, fgain]
    if wrouter is None:
        kern = _merge_kernel
    else:
        kern = _merge_router_kernel
        in_specs.append(_const_spec(wrouter.shape))
        args.append(wrouter)
        out_shape += [jax.ShapeDtypeStruct((s, LANES), F32)] * 2
        out_specs += [row(LANES)] * 2
    return pl.pallas_call(
        kern, grid=(s // tm,), in_specs=in_specs, out_specs=tuple(out_specs), out_shape=tuple(out_shape),
        compiler_params=_cparams("parallel"), name="merge",
    )(*args)


def _ffn_kernel(x_ref, h_ref, wg_ref, wu_ref, wd_ref, o_ref, acc_ref):
    f = pl.program_id(1)

    @pl.when(f == 0)
    def _():
        acc_ref[...] = x_ref[...]

    h = h_ref[...]
    gate = _dot(h, wg_ref[...])
    up = _dot(h, wu_ref[...])
    act = (gate * jax.nn.sigmoid(gate) * up).astype(BF16)
    acc_ref[...] += _dot(act, wd_ref[...])

    @pl.when(f == pl.num_programs(1) - 1)
    def _():
        o_ref[...] = acc_ref[...]


def _dense_ffn(x1, h, wg, wu, wd, tf):
    s = x1.shape[0]
    tm = TM_FFN
    nf = wg.shape[1] // tf
    return pl.pallas_call(
        _ffn_kernel, grid=(s // tm, nf),
        in_specs=[pl.BlockSpec((tm, D_MODEL), lambda i, f: (i, 0)),
                  pl.BlockSpec((tm, D_MODEL), lambda i, f: (i, 0)),
                  pl.BlockSpec((D_MODEL, tf), lambda i, f: (0, f)),
                  pl.BlockSpec((D_MODEL, tf), lambda i, f: (0, f)),
                  pl.BlockSpec((tf, D_MODEL), lambda i, f: (f, 0))],
        out_specs=pl.BlockSpec((tm, D_MODEL), lambda i, f: (i, 0)),
        out_shape=jax.ShapeDtypeStruct((s, D_MODEL), F32),
        scratch_shapes=[pltpu.VMEM((tm, D_MODEL), F32)],
        compiler_params=_cparams("parallel", "arbitrary"), name="dense_ffn",
    )(x1, h, wg, wu, wd)


def _permute_kernel(tile_ref, chunk_ref, flag_ref, exp_ref, dest_ref, cw_ref, h_ref, xg_ref, rw_ref, acc_ref,
                    wacc_ref):
    k = pl.program_id(0)
    tm = TM_MOE
    flags = flag_ref[k]

    @pl.when((flags & 1) != 0)
    def _():
        acc_ref[...] = jnp.zeros_like(acc_ref)
        wacc_ref[...] = jnp.zeros_like(wacc_ref)

    @pl.when((flags & 4) != 0)
    def _():
        row = tile_ref[k] * tm + lax.broadcasted_iota(jnp.int32, (tm, tm), 0)
        hit = dest_ref[0] == row
        acc_ref[...] += _dot(jnp.where(hit, 1.0, 0.0).astype(BF16), h_ref[...])
        wacc_ref[...] += jnp.sum(jnp.where(hit, cw_ref[0], 0.0), axis=-1, keepdims=True)

    @pl.when((flags & 2) != 0)
    def _():
        xg_ref[...] = acc_ref[...].astype(BF16)
        rw_ref[...] = wacc_ref[...]


def _moe_ffn_kernel(te_ref, tv_ref, xg_ref, rw_ref, wg_ref, wu_ref, wd_ref, y_ref, acc_ref):
    t = pl.program_id(0)
    f = pl.program_id(1)

    @pl.when(tv_ref[t] != 0)
    def _():
        @pl.when(f == 0)
        def _():
            acc_ref[...] = jnp.zeros_like(acc_ref)

        x = xg_ref[...]
        gate = _dot(x, wg_ref[0])
        up = _dot(x, wu_ref[0])
        act = (gate * jax.nn.sigmoid(gate) * up * rw_ref[...]).astype(BF16)
        acc_ref[...] += _dot(act, wd_ref[0])

        @pl.when(f == pl.num_programs(1) - 1)
        def _():
            y_ref[...] = acc_ref[...].astype(BF16)

    @pl.when(tv_ref[t] == 0)
    def _():
        y_ref[...] = jnp.zeros_like(y_ref)


def _unpermute_kernel(tile_ref, chunk_ref, flag_ref, grow_ref, x_ref, y_ref, o_ref):
    k = pl.program_id(0)
    tm = TM_MOE
    flags = flag_ref[k]

    @pl.when((flags & 1) != 0)
    def _():
        o_ref[...] = x_ref[...]

    @pl.when((flags & 4) != 0)
    def _():
        row = chunk_ref[k] * tm + lax.broadcasted_iota(jnp.int32, (tm, tm), 1)
        hit = (grow_ref[:, 0:1] == row) | (grow_ref[:, 1:2] == row)
        o_ref[...] += _dot(jnp.where(hit, 1.0, 0.0).astype(BF16), y_ref[...])


def _work_list(first, last, n_items):
    n_tiles = first.shape[0]
    cnt_real = jnp.maximum(last - first + 1, 0)
    cnt = jnp.maximum(cnt_real, 1)
    ends = jnp.cumsum(cnt)
    starts = ends - cnt
    total = ends[-1]
    k = jnp.arange(n_items, dtype=jnp.int32)
    kk = jnp.minimum(k, total - 1)
    tile = jnp.minimum(jnp.sum(kk[:, None] >= ends[None, :], axis=1), n_tiles - 1).astype(jnp.int32)
    pos = kk - starts[tile]
    chunk = jnp.where(cnt_real[tile] > 0, first[tile] + pos, 0).astype(jnp.int32)
    live = k < total
    flags = (jnp.where(live & (pos == 0), 1, 0) | jnp.where(live & (pos == cnt[tile] - 1), 2, 0)
             | jnp.where(live & (pos < cnt_real[tile]), 4, 0))
    return tile, chunk, flags.astype(jnp.int32)


def _moe_ffn(x1, h, comb, selm, wg, wu, wd):
    s = x1.shape[0]
    tm, tf = TM_MOE, TF_MOE
    ne = N_EXPERTS
    n_chunks = s // tm
    n_tiles = 2 * s // tm + ne
    rows = n_tiles * tm
    sel_t = selm[:, :ne].T.astype(jnp.int32)
    cum = jnp.cumsum(sel_t, axis=1)
    counts = cum[:, -1]
    tiles_e = (counts + tm - 1) // tm
    tile_end = jnp.cumsum(tiles_e)
    tile_start = tile_end - tiles_e
    used = tile_end[-1]
    dest = jnp.where(sel_t > 0, tile_start[:, None] * tm + cum - 1, -1).astype(jnp.int32)
    t_idx = jnp.arange(n_tiles, dtype=jnp.int32)
    te_raw = jnp.minimum(jnp.sum(t_idx[:, None] >= tile_end[None, :], axis=1), ne - 1).astype(jnp.int32)
    tvalid = (t_idx < used).astype(jnp.int32)
    last_e = te_raw[jnp.maximum(used - 1, 0)]
    te = jnp.where(tvalid > 0, te_raw, last_e).astype(jnp.int32)
    dest_tiles = dest.reshape(ne, n_chunks, tm)
    big = jnp.int32(2 ** 30)
    d_max = jnp.max(dest_tiles, axis=2)
    d_min = jnp.min(jnp.where(dest_tiles >= 0, dest_tiles, big), axis=2)
    overlap = (d_max[te] >= t_idx[:, None] * tm) & (d_min[te] < (t_idx[:, None] + 1) * tm) & (tvalid[:, None] > 0)
    c_idx = jnp.arange(n_chunks, dtype=jnp.int32)[None, :]
    c_lo = jnp.min(jnp.where(overlap, c_idx, n_chunks), axis=1)
    c_hi = jnp.max(jnp.where(overlap, c_idx, -1), axis=1)
    c_lo = jnp.where(c_hi >= 0, c_lo, 0).astype(jnp.int32)
    p_tile, p_chunk, p_flag = _work_list(c_lo, c_hi.astype(jnp.int32), n_tiles + ne * n_chunks)
    p_exp = te[p_tile]

    cw_t = comb[:, :ne].T.reshape(ne, 1, s)
    dest3 = dest.reshape(ne, 1, s)
    xg, roww = pl.pallas_call(
        _permute_kernel,
        grid_spec=pltpu.PrefetchScalarGridSpec(
            num_scalar_prefetch=4, grid=(p_tile.shape[0],),
            in_specs=[pl.BlockSpec((1, 1, tm), lambda k, pt, pc, pf, pe: (pe[k], 0, pc[k])),
                      pl.BlockSpec((1, 1, tm), lambda k, pt, pc, pf, pe: (pe[k], 0, pc[k])),
                      pl.BlockSpec((tm, D_MODEL), lambda k, pt, pc, pf, pe: (pc[k], 0))],
            out_specs=(pl.BlockSpec((tm, D_MODEL), lambda k, pt, pc, pf, pe: (pt[k], 0)),
                       pl.BlockSpec((tm, 1), lambda k, pt, pc, pf, pe: (pt[k], 0))),
            scratch_shapes=[pltpu.VMEM((tm, D_MODEL), F32), pltpu.VMEM((tm, 1), F32)]),
        out_shape=(jax.ShapeDtypeStruct((rows, D_MODEL), BF16), jax.ShapeDtypeStruct((rows, 1), F32)),
        compiler_params=_cparams("arbitrary"), name="moe_permute",
    )(p_tile, p_chunk, p_flag, p_exp, dest3, cw_t, h)

    nf = wg.shape[2] // tf
    last_f = nf - 1
    fsel = lambda t, f, tv: f * tv[t] + last_f * (1 - tv[t])
    yg = pl.pallas_call(
        _moe_ffn_kernel,
        grid_spec=pltpu.PrefetchScalarGridSpec(
            num_scalar_prefetch=2, grid=(n_tiles, nf),
            in_specs=[pl.BlockSpec((tm, D_MODEL), lambda t, f, te_, tv: (t, 0)),
                      pl.BlockSpec((tm, 1), lambda t, f, te_, tv: (t, 0)),
                      pl.BlockSpec((1, D_MODEL, tf), lambda t, f, te_, tv: (te_[t], 0, fsel(t, f, tv))),
                      pl.BlockSpec((1, D_MODEL, tf), lambda t, f, te_, tv: (te_[t], 0, fsel(t, f, tv))),
                      pl.BlockSpec((1, tf, D_MODEL), lambda t, f, te_, tv: (te_[t], fsel(t, f, tv), 0))],
            out_specs=pl.BlockSpec((tm, D_MODEL), lambda t, f, te_, tv: (t, 0)),
            scratch_shapes=[pltpu.VMEM((tm, D_MODEL), F32)]),
        out_shape=jax.ShapeDtypeStruct((rows, D_MODEL), BF16),
        compiler_params=_cparams("arbitrary", "arbitrary"), name="moe_ffn",
    )(te, tvalid, xg, roww, wg, wu, wd)

    grow = jnp.stack([jnp.min(jnp.where(dest >= 0, dest, big), axis=0), jnp.max(dest, axis=0)], axis=1)
    has = d_max >= 0
    first = jnp.where(has, d_min // tm, 0).T.reshape(-1).astype(jnp.int32)
    last = jnp.where(has, d_max // tm, -1).T.reshape(-1).astype(jnp.int32)
    u_sub, u_chunk, u_flag = _work_list(first, last, n_chunks * 2 * ne)
    u_tile = (u_sub // ne).astype(jnp.int32)
    prev_tile = jnp.concatenate([jnp.full((1,), -1, jnp.int32), u_tile[:-1]])
    u_flag = (u_flag & 4) | jnp.where((u_flag != 0) & (u_tile != prev_tile), 1, 0)
    out = pl.pallas_call(
        _unpermute_kernel,
        grid_spec=pltpu.PrefetchScalarGridSpec(
            num_scalar_prefetch=3, grid=(u_tile.shape[0],),
            in_specs=[pl.BlockSpec((tm, 2), lambda k, ut, uc, uf: (ut[k], 0)),
                      pl.BlockSpec((tm, D_MODEL), lambda k, ut, uc, uf: (ut[k], 0)),
                      pl.BlockSpec((tm, D_MODEL), lambda k, ut, uc, uf: (uc[k], 0))],
            out_specs=pl.BlockSpec((tm, D_MODEL), lambda k, ut, uc, uf: (ut[k], 0))),
        out_shape=jax.ShapeDtypeStruct((s, D_MODEL), F32),
        compiler_params=_cparams("arbitrary"), name="moe_unpermute",
    )(u_tile, u_chunk, u_flag, grow, x1, yg)
    return out


def _t5_bucket(rel):
    n = jnp.maximum(rel, 0)
    max_exact = NUM_BUCKETS // 2
    nf = jnp.maximum(n, 1).astype(jnp.float32)
    large = max_exact + (jnp.log(nf / max_exact) / math.log(MAX_DISTANCE / max_exact)
                         * (NUM_BUCKETS - max_exact)).astype(jnp.int32)
    large = jnp.minimum(large, NUM_BUCKETS - 1)
    return jnp.where(n < max_exact, n, large)


def _bucket_thresholds():
    buckets = _t5_bucket(jnp.arange(MAX_DISTANCE + 1, dtype=jnp.int32))
    b = jnp.arange(NUM_BUCKETS, dtype=jnp.int32)
    return jnp.sum(buckets[None, :] < b[:, None], axis=1).astype(jnp.int32)


def _rope_tables(s):
    half = MLA_ROPE // 2
    inv = ROPE_THETA ** (-jnp.arange(half, dtype=jnp.float32) / half)
    ang = jnp.arange(s).astype(jnp.float32)[:, None] * inv[None, :]
    cos, sin = jnp.cos(ang), jnp.sin(ang)
    z = lambda w: jnp.zeros((s, w), F32)
    pad = MLA_PAD - MLA_QK
    cos_t = jnp.concatenate([jnp.ones((s, MLA_NOPE), F32), cos, cos, z(pad)], axis=1)
    sin_lo = jnp.concatenate([z(MLA_NOPE), -sin, z(half), z(pad)], axis=1)
    sin_hi = jnp.concatenate([z(MLA_NOPE), z(half), sin, z(pad)], axis=1)
    return cos_t, sin_lo, sin_hi


def _pad_lanes(a, width):
    return jnp.pad(a, [(0, 0)] * (a.ndim - 1) + [(0, width - a.shape[-1])])


def kernel(x, rel_bias_table, mix_norm, w_in, b_gate, sb_w_o, nsa_q_norm, nsa_k_norm, nsa_cmp_pos, nsa_cmp_w1, nsa_cmp_w2, nsa_w_o, mla_q_a_norm, mla_kv_a_norm, mla_w_uq, mla_w_ukv, mla_q_norm, mla_k_norm, mla_w_o, w_out, ffn_norm, dense_w_gate, dense_w_up, dense_w_down, moe_router, moe_w_gate, moe_w_up, moe_w_down):
    b, s, d = x.shape
    assert b == 1 and d == D_MODEL and s % TM_MOE == 0 and s // SEL_BLOCK >= SEL_TOPK
    depth = w_in.shape[0]
    xs = x.reshape(s, d)
    thr = _bucket_thresholds()
    cos_t, sin_lo, sin_hi = _rope_tables(s)
    n_chunk = s // CMP_STRIDE

    for layer in range(depth):
        w = w_in[layer]
        small = jnp.concatenate([
            w[:, _C_NG:_C_MCQ], jnp.zeros((d, _KR_LANE - NSA_HEADS * NSA_BRANCHES), F32),
            w[:, _C_MKR:_C_GATE], jnp.zeros((d, LANES - _KR_LANE - MLA_ROPE), F32)], axis=1)
        wa = jnp.concatenate([w[:, :_C_NG], w[:, _C_MCQ:_C_MKR], small], axis=1).astype(BF16)
        wg = w[:, _C_GATE:].astype(BF16)
        wuq = _pad_lanes(mla_w_uq[layer].reshape(MLA_Q_LORA, MLA_HEADS, MLA_QK), MLA_PAD)
        wuq = wuq.reshape(MLA_Q_LORA, MLA_HEADS * MLA_PAD).astype(BF16)
        wukv = mla_w_ukv[layer].reshape(MLA_KV_LORA, MLA_HEADS, MLA_NOPE + MLA_V)
        wukv = jnp.concatenate([
            _pad_lanes(wukv[:, :, :MLA_NOPE], MLA_PAD).reshape(MLA_KV_LORA, MLA_HEADS * MLA_PAD),
            wukv[:, :, MLA_NOPE:].reshape(MLA_KV_LORA, MLA_HEADS * MLA_V)], axis=1).astype(BF16)
        (sbq, sbk, sbv, nq, nkvc, nks, nvs, nkw, nvw, gsm, mq, mk, mv, g) = _input_projection(
            xs, mix_norm[layer][None], wa, wg, b_gate[layer][None], wuq, wukv,
            nsa_q_norm[layer][None], nsa_k_norm[layer], mla_q_a_norm[layer][None],
            mla_kv_a_norm[layer][None], _pad_lanes(mla_q_norm[layer][None], MLA_PAD),
            _pad_lanes(mla_k_norm[layer][None], MLA_PAD), cos_t, sin_lo, sin_hi)

        o_sb = _stick_breaking(sbq, sbk, sbv)
        o_mla = _mla_attention(mq, mk, mv)

        x2 = nkvc.reshape(2, NSA_GROUPS, n_chunk, CMP_STRIDE * HEAD_DIM)
        cmp, cmp_t = _compress(x2, nsa_cmp_pos[layer].reshape(2, 1, CMP_LEN * HEAD_DIM),
                               nsa_cmp_w1[layer].astype(BF16), nsa_cmp_w2[layer].astype(BF16),
                               nsa_k_norm[layer][0:1])
        gates = gsm[:NSA_HEADS * NSA_BRANCHES].reshape(NSA_GROUPS, NSA_HPG * NSA_BRANCHES, s)
        oc, sel = _nsa_compressed(thr, rel_bias_table, nq, cmp, cmp_t, gates)
        o_nsa = _nsa_selected(thr, rel_bias_table, nq, nks, nvs,
                              jnp.pad(nkw, ((0, 0), (WINDOW, 0), (0, 0))),
                              jnp.pad(nvw, ((0, 0), (0, 0), (WINDOW, 0))), sel, gates, oc)

        wsb = sb_w_o[layer].reshape(SB_HEADS, HEAD_DIM, d).astype(BF16)
        wnsa = nsa_w_o[layer].reshape(NSA_HEADS, HEAD_DIM, d).astype(BF16)
        wmla = mla_w_o[layer].reshape(MLA_HEADS, MLA_V, d).astype(BF16)
        wout = w_out[layer].astype(BF16)
        j = layer // 2
        if layer % 2 == 0:
            x1, h = _merge(o_sb, o_nsa, o_mla, g, xs, wsb, wnsa, wmla, wout, ffn_norm[layer][None])
            d_ff = dense_w_gate.shape[2]
            xs = _dense_ffn(x1, h, dense_w_gate[j].astype(BF16), dense_w_up[j].astype(BF16),
                            dense_w_down[j].astype(BF16), d_ff // 2)
        else:
            x1, h, comb, selm = _merge(o_sb, o_nsa, o_mla, g, xs, wsb, wnsa, wmla, wout,
                                       ffn_norm[layer][None], _pad_lanes(moe_router[j], LANES))
            xs = _moe_ffn(x1, h, comb, selm, moe_w_gate[j].astype(BF16), moe_w_up[j].astype(BF16),
                          moe_w_down[j].astype(BF16))
    return xs.reshape(b, s, d)
```

```python
import functools
import math

import jax
import jax.numpy as jnp
from jax import lax
from jax.experimental import pallas as pl
from jax.experimental.pallas import tpu as pltpu

F32 = jnp.float32
BF16 = jnp.bfloat16

D_MODEL = 1024
HEAD_DIM = 64
SB_HEADS = 4
NSA_HEADS = 8
NSA_GROUPS = 2
NSA_HPG = NSA_HEADS // NSA_GROUPS
NSA_BRANCHES = 3
MLA_HEADS = 4
MLA_NOPE = 64
MLA_ROPE = 32
MLA_V = 64
MLA_QK = MLA_NOPE + MLA_ROPE
MLA_Q_LORA = 256
MLA_KV_LORA = 128
ROPE_THETA = 10000.0
CMP_LEN = 32
CMP_STRIDE = 16
CMP_HIDDEN = 256
SEL_BLOCK = 64
SEL_TOPK = 16
N_LOCAL_BLOCKS = 2
WINDOW = 512
NUM_BUCKETS = 32
MAX_DISTANCE = 1024
N_MIXERS = 3
N_EXPERTS = 8
EPS = 1e-6
NEG_INF = -1e30
FORCE = 1e30
LOG2E = math.log2(math.e)

SB_W = SB_HEADS * HEAD_DIM
NSA_QW = NSA_HEADS * HEAD_DIM
NSA_KVW = NSA_GROUPS * HEAD_DIM
GATE_W = N_MIXERS * D_MODEL
LANES = 128
MLA_PAD = LANES

_C_NQ = 3 * SB_W
_C_NKV = _C_NQ + NSA_QW
_C_NG = _C_NKV + 6 * NSA_KVW
_C_MCQ = _C_NG + NSA_HEADS * NSA_BRANCHES
_C_MCKV = _C_MCQ + MLA_Q_LORA
_C_MKR = _C_MCKV + MLA_KV_LORA
_C_GATE = _C_MKR + MLA_ROPE
_A_SB = 0
_A_NQ = 3 * SB_W
_A_NKV = _A_NQ + NSA_QW
_A_MCQ = _A_NKV + 6 * NSA_KVW
_A_MCKV = _A_MCQ + MLA_Q_LORA
_A_SMALL = _A_MCKV + MLA_KV_LORA
_A_COLS = _A_SMALL + LANES
_KR_LANE = MLA_NOPE

TM_PROJ = 256
TQ_SB = 512
TK_SB = 256
TQ_MLA = 256
TK_MLA = 1024
CHUNK_MLA = 256
CHUNK_SEL = 512
TQ_CMP = 128
CMP_PARTS = 4
TQ_SEL = 256
TK_SEL = 512
TM_MERGE = 512
TM_MERGE_ROUTER = 256
TM_FFN = 512
TM_MOE = 512
TF_MOE = 512
VMEM_LIMIT = 56 * 1024 * 1024
SB_EXIT = -104.0

_NT = (((1,), (1,)), ((), ()))


def _dot(a, b):
    return jnp.dot(a, b, preferred_element_type=F32)


def _dot_nt(a, b):
    return lax.dot_general(a, b, _NT, preferred_element_type=F32)


def _cparams(*sem):
    return pltpu.CompilerParams(dimension_semantics=sem, vmem_limit_bytes=VMEM_LIMIT)


def _const_spec(shape):
    nd = len(shape)
    return pl.BlockSpec(shape, lambda *_: (0,) * nd, pipeline_mode=pl.Buffered(1))


def _smem_spec():
    return pl.BlockSpec(memory_space=pltpu.SMEM)


def _split3(x):
    h1 = x.astype(BF16)
    r1 = x - h1.astype(F32)
    h2 = r1.astype(BF16)
    h3 = (r1 - h2.astype(F32)).astype(BF16)
    return h1, h2, h3


def _bias_chain(rel, thr_ref, tab_ref, head):
    out = jnp.full(rel.shape, tab_ref[0, head], F32)
    for b in range(1, NUM_BUCKETS):
        out = jnp.where(rel >= thr_ref[b], tab_ref[b, head], out)
    return out


def _rope(y, cos, sin_lo, sin_hi):
    half = MLA_ROPE // 2
    return y * cos + pltpu.roll(y, half, 1) * sin_hi + pltpu.roll(y, MLA_PAD - half, 1) * sin_lo


def _proj_kernel(x_ref, gain_ref, wa_ref, wg_ref, bg_ref, wuq_ref, wukv_ref, nqg_ref, nkg_ref,
                 qag_ref, kvag_ref, mqg_ref, mkg_ref, cos_ref, slo_ref, shi_ref,
                 sbq_ref, sbk_ref, sbv_ref, nq_ref, nkvc_ref, nks_ref, nvs_ref, nkw_ref, nvw_ref,
                 gsm_ref, mq_ref, mk_ref, mv_ref, g_ref):
    x = x_ref[...]
    ms = jnp.mean(x * x, axis=-1, keepdims=True)
    h = (x * lax.rsqrt(ms + EPS) * gain_ref[...]).astype(BF16)

    def rms(p, gain, width):
        m = jnp.sum(p * p, axis=-1, keepdims=True) * (1.0 / width)
        return p * lax.rsqrt(m + EPS) * gain

    p = _dot(h, wa_ref[:, _A_SB:_A_SB + 3 * SB_W])
    sbq_t = (p[:, :SB_W] * HEAD_DIM ** -0.5).T.astype(BF16)
    sbv_t = p[:, 2 * SB_W:3 * SB_W].T.astype(BF16)
    for j in range(SB_HEADS):
        sbq_ref[j] = sbq_t[j * HEAD_DIM:(j + 1) * HEAD_DIM]
        sbv_ref[j] = sbv_t[j * HEAD_DIM:(j + 1) * HEAD_DIM]
    for j in range(SB_HEADS // 2):
        sbk_ref[j] = p[:, SB_W + j * LANES:SB_W + (j + 1) * LANES].astype(BF16)

    p = _dot(h, wa_ref[:, _A_NQ:_A_NQ + NSA_QW])
    qn = [rms(p[:, j * HEAD_DIM:(j + 1) * HEAD_DIM], nqg_ref[...], HEAD_DIM) * (HEAD_DIM ** -0.5 * LOG2E)
          for j in range(NSA_HEADS)]
    qn_t = jnp.concatenate(qn, axis=1).T.astype(BF16)
    for j in range(NSA_HEADS):
        nq_ref[j] = qn_t[j * HEAD_DIM:(j + 1) * HEAD_DIM]

    p = _dot(h, wa_ref[:, _A_NKV:_A_NKV + 6 * NSA_KVW])
    vs_t = p[:, 3 * NSA_KVW:4 * NSA_KVW].T.astype(BF16)
    tag_shape = (x.shape[0], LANES - HEAD_DIM)
    block_id = (pl.program_id(0) * x.shape[0] + lax.broadcasted_iota(jnp.int32, tag_shape, 0)) // SEL_BLOCK
    block_tag = jnp.where(lax.broadcasted_iota(jnp.int32, tag_shape, 1) == block_id % (TK_SEL // SEL_BLOCK),
                          1.0, 0.0)
    vw_t = p[:, 5 * NSA_KVW:6 * NSA_KVW].T.astype(BF16)
    for j in range(NSA_GROUPS):
        def seg(s):
            return p[:, s * NSA_KVW + j * HEAD_DIM:s * NSA_KVW + (j + 1) * HEAD_DIM]
        nkvc_ref[0, j] = seg(0)
        nkvc_ref[1, j] = seg(1)
        nks_ref[j] = jnp.concatenate([rms(seg(2), nkg_ref[1:2, :], HEAD_DIM), block_tag], axis=1).astype(BF16)
        nvs_ref[j] = vs_t[j * HEAD_DIM:(j + 1) * HEAD_DIM]
        nkw_ref[j] = rms(seg(4), nkg_ref[2:3, :], HEAD_DIM).astype(BF16)
        nvw_ref[j] = vw_t[j * HEAD_DIM:(j + 1) * HEAD_DIM]

    p = _dot(h, wa_ref[:, _A_MCQ:_A_COLS])
    cq = rms(p[:, :MLA_Q_LORA], qag_ref[...], MLA_Q_LORA).astype(BF16)
    ckv = rms(p[:, MLA_Q_LORA:MLA_Q_LORA + MLA_KV_LORA], kvag_ref[...], MLA_KV_LORA).astype(BF16)
    small = p[:, MLA_Q_LORA + MLA_KV_LORA:]
    gsm_ref[...] = jax.nn.sigmoid(small).T
    lane = lax.broadcasted_iota(jnp.int32, small.shape, 1)
    krope = jnp.where((lane >= _KR_LANE) & (lane < _KR_LANE + MLA_ROPE), small, 0.0)
    cos, slo, shi = cos_ref[...], slo_ref[...], shi_ref[...]
    qu = _dot(cq, wuq_ref[...])
    kvu = _dot(ckv, wukv_ref[...])
    for j in range(MLA_HEADS):
        qh = rms(qu[:, j * MLA_PAD:(j + 1) * MLA_PAD], mqg_ref[...], MLA_QK)
        mq_ref[j] = (_rope(qh, cos, slo, shi) * MLA_QK ** -0.5).T.astype(BF16)
        kh = rms(kvu[:, j * MLA_PAD:(j + 1) * MLA_PAD] + krope, mkg_ref[...], MLA_QK)
        mk_ref[j] = _rope(kh, cos, slo, shi).astype(BF16)
    v_t = kvu[:, MLA_HEADS * MLA_PAD:].T.astype(BF16)
    for j in range(MLA_HEADS):
        mv_ref[j] = v_t[j * MLA_V:(j + 1) * MLA_V]

    g_ref[...] = jax.nn.sigmoid(_dot(h, wg_ref[...]) + bg_ref[...])


def _input_projection(x, gain, wa, wg, bg, wuq, wukv, nqg, nkg, qag, kvag, mqg, mkg, cos, slo, shi):
    s = x.shape[0]
    tm = TM_PROJ
    row = lambda w: pl.BlockSpec((tm, w), lambda i: (i, 0))
    heads = lambda n, w: pl.BlockSpec((n, tm, w), lambda i: (0, i, 0))
    hs = lambda n, w, dt: jax.ShapeDtypeStruct((n, s, w), dt)
    in_specs = [row(D_MODEL)] + [_const_spec(a.shape) for a in
                                 (gain, wa, wg, bg, wuq, wukv, nqg, nkg, qag, kvag, mqg, mkg)]
    in_specs += [row(MLA_PAD)] * 3
    hst = lambda n, w, dt: jax.ShapeDtypeStruct((n, w, s), dt)
    heads_t = lambda n, w: pl.BlockSpec((n, w, tm), lambda i: (0, 0, i))
    ng, hd = NSA_GROUPS, HEAD_DIM
    out_shape = (hst(SB_HEADS, hd, BF16), hs(SB_HEADS // 2, LANES, BF16), hst(SB_HEADS, hd, BF16))
    out_shape += (hst(NSA_HEADS, hd, BF16),)
    out_shape += (jax.ShapeDtypeStruct((2, ng, s, hd), F32),) + (hs(ng, LANES, BF16), hst(ng, hd, BF16), hs(ng, hd, BF16), hst(ng, hd, BF16))
    out_shape += (jax.ShapeDtypeStruct((LANES, s), F32),)
    out_shape += (hst(MLA_HEADS, MLA_PAD, BF16), hs(MLA_HEADS, MLA_PAD, BF16), hst(MLA_HEADS, MLA_V, BF16))
    out_shape += (jax.ShapeDtypeStruct((s, GATE_W), F32),)
    out_specs = (heads_t(SB_HEADS, hd), heads(SB_HEADS // 2, LANES), heads_t(SB_HEADS, hd))
    out_specs += (heads_t(NSA_HEADS, hd),)
    out_specs += (pl.BlockSpec((2, ng, tm, hd), lambda i: (0, 0, i, 0)),) + (heads(ng, LANES), heads_t(ng, hd), heads(ng, hd), heads_t(ng, hd))
    out_specs += (pl.BlockSpec((LANES, tm), lambda i: (0, i)),)
    out_specs += (heads_t(MLA_HEADS, MLA_PAD), heads(MLA_HEADS, MLA_PAD), heads_t(MLA_HEADS, MLA_V), row(GATE_W))
    return pl.pallas_call(
        _proj_kernel, grid=(s // tm,), in_specs=in_specs, out_specs=out_specs, out_shape=out_shape,
        compiler_params=_cparams("parallel"), name="input_projection",
    )(x, gain, wa, wg, bg, wuq, wukv, nqg, nkg, qag, kvag, mqg, mkg, cos, slo, shi)


def _sb_kernel(qt_ref, k2_ref, vt_ref, o_ref):
    tq, tk = TQ_SB, TK_SB
    nh = qt_ref.shape[0]
    i = pl.program_id(0)
    later = (lax.broadcasted_iota(jnp.int32, (tk, tk), 1)
             > lax.broadcasted_iota(jnp.int32, (tk, tk), 0)).astype(BF16)
    key = lax.broadcasted_iota(jnp.int32, (tk, tq), 0)
    qry = i * tq + lax.broadcasted_iota(jnp.int32, (tk, tq), 1)
    zeros = jnp.zeros((HEAD_DIM, tq), BF16)
    q_aug = [jnp.concatenate([qt_ref[h], zeros] if h % 2 == 0 else [zeros, qt_ref[h]], axis=0) for h in range(nh)]

    def tile(kt, h, carry, acc, diagonal):
        off = pl.multiple_of(kt * tk, tk)
        z = _dot(k2_ref[h // 2, pl.ds(off, tk), :], q_aug[h])
        log_stay = -(jnp.maximum(z, 0.0) + jnp.log(1.0 + jnp.exp(-jnp.abs(z))))
        if diagonal:
            causal = off + key < qry
            log_stay = jnp.where(causal, log_stay, 0.0)
        hi = log_stay.astype(BF16)
        lo = (log_stay - hi.astype(F32)).astype(BF16)
        suffix = _dot(later, hi) + _dot(later, lo)
        logw = z + log_stay + (carry + suffix)
        if diagonal:
            logw = jnp.where(causal, logw, NEG_INF)
        w = jnp.exp(logw).astype(BF16)
        acc = acc + _dot(vt_ref[h, :, pl.ds(off, tk)], w)
        carry = carry + suffix[0:1, :] + log_stay[0:1, :]
        return carry, acc

    def all_heads(kt, st, diagonal):
        out = [tile(kt, h, st[2 * h], st[2 * h + 1], diagonal) for h in range(nh)]
        return tuple(x for pair in out for x in pair)

    ratio = tq // tk
    st = (jnp.zeros((1, tq), F32), jnp.zeros((HEAD_DIM, tq), F32)) * nh
    for d in range(ratio):
        st = all_heads(ratio * i + ratio - 1 - d, st, True)

    def cond(c):
        top = c[1]
        for h in range(1, nh):
            top = jnp.maximum(top, c[1 + 2 * h])
        return jnp.logical_and(c[0] >= 0, jnp.max(top) > SB_EXIT)

    def body(c):
        return (c[0] - 1,) + all_heads(c[0], c[1:], False)

    c = lax.while_loop(cond, body, (ratio * i - 1,) + st)
    for h in range(nh):
        o_ref[h] = c[2 + 2 * h]


def _stick_breaking(q_t, k2, v_t):
    nh, _, s = q_t.shape
    tq = TQ_SB
    whole = lambda a: pl.BlockSpec(a.shape, lambda i: (0, 0, 0), pipeline_mode=pl.Buffered(1))
    return pl.pallas_call(
        _sb_kernel, grid=(s // tq,),
        in_specs=[pl.BlockSpec((nh, HEAD_DIM, tq), lambda i: (0, 0, i)), whole(k2), whole(v_t)],
        out_specs=pl.BlockSpec((nh, HEAD_DIM, tq), lambda i: (0, 0, i)),
        out_shape=jax.ShapeDtypeStruct((nh, HEAD_DIM, s), F32),
        compiler_params=_cparams("parallel"), name="stick_breaking",
    )(q_t, k2, v_t)


ONES_ROWS = 16


def _with_ones_rows(v_t):
    return jnp.concatenate([v_t, jnp.ones((ONES_ROWS, v_t.shape[1]), v_t.dtype)], axis=0)


def _pipelined_blocks(n_last, score_chunk, value_chunk, n_chunks, chunk, exp_fn, bufs, st):
    rows = lambda c: pl.ds(c * chunk, chunk)
    bufs, stat_ref, acc_ref = bufs[:3], bufs[3], bufs[4]
    m_row = 3
    stat_ref[m_row:m_row + 1, :] = st[0]
    acc_ref[...] = st[1]

    def prepare(kb, slot, maybe_diagonal):
        mx = None
        for c in range(n_chunks):
            s = score_chunk(kb, c, maybe_diagonal)
            bufs[slot][rows(c), :] = s
            smax = jnp.max(s, axis=0, keepdims=True)
            mx = smax if mx is None else jnp.maximum(mx, smax)
        stat_ref[slot:slot + 1, :] = mx

    def consume(kb, slot):
        m = stat_ref[m_row:m_row + 1, :]
        m_new = jnp.maximum(m, stat_ref[slot:slot + 1, :])
        stat_ref[m_row:m_row + 1, :] = m_new
        part = None
        for c in range(n_chunks):
            p = exp_fn(bufs[slot][rows(c), :] - m_new).astype(BF16)
            d = _dot(value_chunk(kb, c), p)
            part = d if part is None else part + d
        acc_ref[...] = exp_fn(m - m_new) * acc_ref[...] + part

    def run(kb0, n_blocks, last_is_diagonal):
        n_prep = n_blocks if last_is_diagonal else n_blocks + 2
        for j in range(n_blocks):
            if j + 2 < n_prep:
                prepare(kb0 + j + 2, (j + 2) % 3, last_is_diagonal and j + 2 == n_blocks - 1)
            consume(kb0 + j, j % 3)

    trips = jnp.maximum(n_last - 2, 0) // 3
    prepare(0, 0, True)
    prepare(1, 1, True)

    def trip(t, carry):
        run(3 * t, 3, False)
        return carry

    lax.fori_loop(0, trips, trip, 0)
    kb0 = 3 * trips
    for r in range(5):
        pl.when(n_last - kb0 == r)(functools.partial(run, kb0, r + 1, True))
    return stat_ref[m_row:m_row + 1, :], acc_ref[...]


def _mla_kernel(qt_ref, k_ref, vt_ref, o_ref, *bufs):
    tq, tk = TQ_MLA, TK_MLA
    i = pl.program_id(1)
    q_t = qt_ref[0]

    ck = CHUNK_MLA

    def score_chunk(kb, c, maybe_diagonal):
        off = pl.multiple_of(kb * tk + c * ck, ck)
        s = _dot(k_ref[0, pl.ds(off, ck), :], q_t)
        if maybe_diagonal:
            key = off + lax.broadcasted_iota(jnp.int32, (ck, tq), 0)
            qry = i * tq + lax.broadcasted_iota(jnp.int32, (ck, tq), 1)
            s = jnp.where(key <= qry, s, NEG_INF)
        return s

    def value_chunk(kb, c):
        off = pl.multiple_of(kb * tk + c * ck, ck)
        return _with_ones_rows(vt_ref[0, :, pl.ds(off, ck)])

    st = (jnp.full((1, tq), NEG_INF, F32), jnp.zeros((MLA_V + ONES_ROWS, tq), F32))
    _, acc = _pipelined_blocks((i * tq) // tk, score_chunk, value_chunk, tk // ck, ck, jnp.exp,
                               bufs, st)
    o_ref[0] = acc[:MLA_V] / acc[MLA_V:MLA_V + 1]


def _mla_attention(q_t, k, v_t):
    nh, s, _ = k.shape
    tq, tk = TQ_MLA, TK_MLA
    return pl.pallas_call(
        _mla_kernel, grid=(nh, s // tq),
        in_specs=[pl.BlockSpec((1, MLA_PAD, tq), lambda h, i: (h, 0, i)),
                  pl.BlockSpec((1, s, MLA_PAD), lambda h, i: (h, 0, 0)),
                  pl.BlockSpec((1, MLA_V, s), lambda h, i: (h, 0, 0))],
        out_specs=pl.BlockSpec((1, MLA_V, tq), lambda h, i: (h, 0, i)),
        out_shape=jax.ShapeDtypeStruct((nh, MLA_V, s), F32),
        scratch_shapes=[pltpu.VMEM((tk, tq), F32)] * 3 + [pltpu.VMEM((8, tq), F32),
                                                          pltpu.VMEM((MLA_V + ONES_ROWS, tq), F32)],
        compiler_params=_cparams("parallel", "parallel"), name="mla_attention",
    )(q_t, k, v_t)


def _compress_kernel(x_ref, pos_ref, w1_ref, w2_ref, kg_ref, o_ref, ot_ref):
    kv = pl.program_id(0)
    half = CMP_STRIDE * HEAD_DIM
    a = x_ref[0, 0].astype(BF16)
    w1 = w1_ref[0]
    first = _dot(a, w1[:half])
    second = _dot(a, w1[half:])
    n = second.shape[0]
    second = pltpu.roll(second, n - 1, 0)
    posb = jnp.broadcast_to(pos_ref[0], (8, 2 * half)).astype(BF16)
    hid = first + second + _dot(posb, w1)[0:1]
    hid = hid * jax.nn.sigmoid(hid)
    out = _dot(hid.astype(BF16), w2_ref[0])
    ms = jnp.mean(out * out, axis=-1, keepdims=True)
    normed = out * lax.rsqrt(ms + EPS) * kg_ref[...]
    res = jnp.where(kv == 0, normed, out)
    o_ref[0, 0] = res
    ot_ref[0, 0] = res.T


def _compress(x2, pos, w1, w2, kgain):
    _, ng, nc, cw = x2.shape
    return pl.pallas_call(
        _compress_kernel, grid=(2, ng),
        in_specs=[pl.BlockSpec((1, 1, nc, cw), lambda a, g: (a, g, 0, 0)),
                  pl.BlockSpec((1, 1, 2 * cw), lambda a, g: (a, 0, 0)),
                  pl.BlockSpec((1, 2 * cw, CMP_HIDDEN), lambda a, g: (a, 0, 0)),
                  pl.BlockSpec((1, CMP_HIDDEN, HEAD_DIM), lambda a, g: (a, 0, 0)),
                  _const_spec(kgain.shape)],
        out_specs=(pl.BlockSpec((1, 1, nc, HEAD_DIM), lambda a, g: (a, g, 0, 0)),
                   pl.BlockSpec((1, 1, HEAD_DIM, nc), lambda a, g: (a, g, 0, 0))),
        out_shape=(jax.ShapeDtypeStruct((2, ng, nc, HEAD_DIM), F32),
                   jax.ShapeDtypeStruct((2, ng, HEAD_DIM, nc), F32)),
        compiler_params=_cparams("parallel", "parallel"), name="nsa_compress",
    )(x2, pos, w1, w2, kgain)


def _nsa_cmp_kernel(thr_ref, tab_ref, qt_ref, kc_ref, vct_ref, gt_ref, oc_ref, selt_ref, band_ref, pool_ref):
    tq = TQ_CMP
    r_n = NSA_HPG
    g = pl.program_id(0)
    i = pl.program_id(1)
    nc = kc_ref.shape[2]
    nb = selt_ref.shape[1]
    cpt = tq // CMP_STRIDE

    @pl.when(i == 0)
    def _init():
        m = lax.broadcasted_iota(jnp.int32, (2 * nc, tq), 0)
        a = lax.broadcasted_iota(jnp.int32, (2 * nc, tq), 1)
        rel = a - CMP_STRIDE * (m - nc) - (CMP_LEN - 1)
        for r in range(r_n):
            band_ref[r] = _bias_chain(rel, thr_ref, tab_ref, g * r_n + r) * LOG2E
        b = lax.broadcasted_iota(jnp.int32, (nb, nc), 0)
        n = lax.broadcasted_iota(jnp.int32, (nb, nc), 1)
        rs = SEL_BLOCK // CMP_STRIDE
        rc = CMP_LEN // CMP_STRIDE
        member = (n >= rs * b - (rc - 1)) & (n <= rs * b + rs - 1) & (n < nc - (rc - 1))
        pool_ref[...] = jnp.where(member, 1.0, 0.0).astype(BF16)

    q_t = jnp.concatenate([qt_ref[r] for r in range(r_n)], axis=1)
    boff = pl.multiple_of(nc - cpt * i, cpt)
    gates = gt_ref[0]

    def tile_body(nc_e, nb_e):
        s = _dot(kc_ref[0, 0, :nc_e, :].astype(BF16), q_t)
        qpos = i * tq + lax.broadcasted_iota(jnp.int32, (nc_e, tq), 1)
        cmp_end = lax.broadcasted_iota(jnp.int32, (nc_e, tq), 0) * CMP_STRIDE + (CMP_LEN - 1)
        valid = cmp_end <= qpos
        imp = jnp.zeros((nc_e, tq), F32)
        probs = []
        for r in range(r_n):
            sr = s[:, r * tq:(r + 1) * tq] + band_ref[r, pl.ds(boff, nc_e), :]
            sr = jnp.where(valid, sr, NEG_INF)
            mx = jnp.max(sr, axis=0, keepdims=True)
            e = jnp.where(valid, jnp.exp2(sr - mx), 0.0)
            den = jnp.sum(e, axis=0, keepdims=True)
            p = e / jnp.where(den > 0.0, den, 1.0)
            imp = imp + p
            probs.append(p.astype(BF16))
        oc = _dot(vct_ref[0, 0, :, :nc_e].astype(BF16), jnp.concatenate(probs, axis=1))
        for r in range(r_n):
            c = NSA_BRANCHES * r
            oc_ref[r] = oc[:, r * tq:(r + 1) * tq] * gates[c:c + 1, :]

        pool = pool_ref[:nb_e, :nc_e]
        h1, h2, h3 = _split3(imp)
        imp_blk = _dot(pool, h1) + _dot(pool, h2) + _dot(pool, h3)
        blk = lax.broadcasted_iota(jnp.int32, (nb_e, tq), 0)
        qp = i * tq + lax.broadcasted_iota(jnp.int32, (nb_e, tq), 1)
        causal_b = blk * SEL_BLOCK <= qp
        dist = qp // SEL_BLOCK - blk
        forced = (blk == 0) | ((dist >= 0) & (dist < N_LOCAL_BLOCKS))
        score = jnp.where(causal_b, jnp.where(forced, FORCE, imp_blk), NEG_INF)
        for _ in range(min(SEL_TOPK, nb_e)):
            mx = jnp.max(score, axis=0, keepdims=True)
            first = jnp.min(jnp.where(score == mx, blk, nb_e), axis=0, keepdims=True)
            score = jnp.where(blk == first, -jnp.inf, score)
        selt_ref[0, :nb_e, :] = jnp.where((score == -jnp.inf) & causal_b, 1.0, 0.0)
        if nb_e < nb:
            selt_ref[0, nb_e:, :] = jnp.zeros((nb - nb_e, tq), F32)

    tiles_per_part = pl.num_programs(1) // CMP_PARTS
    part = i // tiles_per_part
    for j in range(CMP_PARTS):
        pl.when(part == j)(functools.partial(tile_body, (j + 1) * nc // CMP_PARTS, (j + 1) * nb // CMP_PARTS))


def _nsa_compressed(thr, tab, nq_t, cmp, cmp_t, gates_t):
    nh, _, s = nq_t.shape
    ng = NSA_GROUPS
    tq = TQ_CMP
    nc = cmp.shape[2]
    nb = s // SEL_BLOCK
    qspec = pl.BlockSpec((NSA_HPG, HEAD_DIM, tq), lambda g, i: (g, 0, i))
    return pl.pallas_call(
        _nsa_cmp_kernel, grid=(ng, s // tq),
        in_specs=[_smem_spec(), _smem_spec(), qspec,
                  pl.BlockSpec((1, 1, nc, HEAD_DIM), lambda g, i: (0, g, 0, 0)),
                  pl.BlockSpec((1, 1, HEAD_DIM, nc), lambda g, i: (1, g, 0, 0)),
                  pl.BlockSpec((1, NSA_HPG * NSA_BRANCHES, tq), lambda g, i: (g, 0, i))],
        out_specs=(qspec, pl.BlockSpec((1, nb, tq), lambda g, i: (g, 0, i))),
        out_shape=(jax.ShapeDtypeStruct((nh, HEAD_DIM, s), F32),
                   jax.ShapeDtypeStruct((ng, nb, s), F32)),
        scratch_shapes=[pltpu.VMEM((NSA_HPG, 2 * nc, tq), F32), pltpu.VMEM((nb, nc), BF16)],
        compiler_params=_cparams("arbitrary", "arbitrary"), name="nsa_compressed",
    )(thr, tab, nq_t, cmp, cmp_t, gates_t)


_NEAR_SPAN = ((MAX_DISTANCE + TK_SEL - 2) // TQ_SEL) * TQ_SEL
assert _NEAR_SPAN >= MAX_DISTANCE and TK_SEL % TQ_SEL == 0
_STRIP = 2 * TK_SEL + _NEAR_SPAN
_WIN_KEYS = WINDOW + TQ_SEL
_BLOCKS_PER_TILE = TK_SEL // SEL_BLOCK


def _nsa_sel_kernel(thr_ref, tab_ref, qt_ref, ks_ref, vst_ref, kw_ref, vwt_ref, selt_ref, gt_ref, oc_ref,
                    o_ref, strip_ref, *bufs):
    tq, tk, r_n = TQ_SEL, TK_SEL, NSA_HPG
    g = pl.program_id(0)
    i = pl.program_id(1)
    top = tk + _NEAR_SPAN
    lanes = r_n * tq

    @pl.when(i == 0)
    def _init():
        u = lax.broadcasted_iota(jnp.int32, (_STRIP, tq), 0)
        a = lax.broadcasted_iota(jnp.int32, (_STRIP, tq), 1)
        for r in range(r_n):
            strip_ref[:, r * tq:(r + 1) * tq] = _bias_chain(a - u + top, thr_ref, tab_ref, g * r_n + r) * LOG2E

    q_t = jnp.concatenate([qt_ref[r] for r in range(r_n)], axis=1)
    pad_rows = jnp.zeros((LANES - HEAD_DIM - _BLOCKS_PER_TILE, lanes), F32)
    qpos = i * tq + (lax.broadcasted_iota(jnp.int32, (1, lanes), 1) & (tq - 1))

    ck = CHUNK_SEL

    def score_chunk(kb, c, maybe_diagonal):
        off = pl.multiple_of(kb * tk + c * ck, ck)
        boff = pl.multiple_of(kb * _BLOCKS_PER_TILE, _BLOCKS_PER_TILE)
        picked = selt_ref[0, pl.ds(boff, _BLOCKS_PER_TILE), :]
        penalty = jnp.where(picked > 0.5, 0.0, NEG_INF)
        penalty = jnp.concatenate([penalty] * r_n, axis=1)
        q_aug = jnp.concatenate([q_t, jnp.concatenate([penalty, pad_rows], axis=0).astype(BF16)], axis=0)
        soff = pl.multiple_of(jnp.clip(top - (i * tq - kb * tk), 0, top) + c * ck, ck)
        s = _dot(ks_ref[0, pl.ds(off, ck), :], q_aug) + strip_ref[pl.ds(soff, ck), :]
        if maybe_diagonal:
            key = off + lax.broadcasted_iota(jnp.int32, (ck, lanes), 0)
            s = jnp.where(key <= qpos, s, NEG_INF)
        return s

    def value_chunk(kb, c):
        off = pl.multiple_of(kb * tk + c * ck, ck)
        return _with_ones_rows(vst_ref[0, :, pl.ds(off, ck)])

    woff = pl.multiple_of(i * tq, tq)
    w0 = top - WINDOW
    sw = _dot(kw_ref[0, pl.ds(woff, _WIN_KEYS), :], q_t) + strip_ref[w0:w0 + _WIN_KEYS, :]
    kpos = i * tq - WINDOW + lax.broadcasted_iota(jnp.int32, (_WIN_KEYS, lanes), 0)
    rel = qpos - kpos
    sw = jnp.where((rel >= 0) & (rel < WINDOW) & (kpos >= 0), sw, NEG_INF)
    pw = jnp.exp2(sw - jnp.max(sw, axis=0, keepdims=True)).astype(BF16)
    o_win = _dot(_with_ones_rows(vwt_ref[0, :, pl.ds(woff, _WIN_KEYS)]), pw)
    o_win = o_win[:HEAD_DIM] / o_win[HEAD_DIM:HEAD_DIM + 1]
    gates = gt_ref[0]
    for r in range(r_n):
        c0 = NSA_BRANCHES * r
        o_ref[r] = oc_ref[r] + gates[c0 + 2:c0 + 3, :] * o_win[:, r * tq:(r + 1) * tq]

    st = (jnp.full((1, lanes), NEG_INF, F32), jnp.zeros((HEAD_DIM + ONES_ROWS, lanes), F32))
    _, acc = _pipelined_blocks((i * tq) // tk, score_chunk, value_chunk, tk // ck, ck, jnp.exp2,
                               bufs, st)
    o_sel = acc[:HEAD_DIM] / acc[HEAD_DIM:HEAD_DIM + 1]
    for r in range(r_n):
        c0 = NSA_BRANCHES * r
        o_ref[r] += gates[c0 + 1:c0 + 2, :] * o_sel[:, r * tq:(r + 1) * tq]


def _nsa_selected(thr, tab, nq_t, ks, vs_t, kw_pad, vw_t_pad, sel_t, gates_t, oc):
    nh, _, s = nq_t.shape
    ng, tq = NSA_GROUPS, TQ_SEL
    nb = s // SEL_BLOCK
    qspec = pl.BlockSpec((NSA_HPG, HEAD_DIM, tq), lambda g, i: (g, 0, i))
    rows = lambda n: pl.BlockSpec((1, n, HEAD_DIM), lambda g, i: (g, 0, 0), pipeline_mode=pl.Buffered(1))
    cols = lambda n: pl.BlockSpec((1, HEAD_DIM, n), lambda g, i: (g, 0, 0), pipeline_mode=pl.Buffered(1))
    return pl.pallas_call(
        _nsa_sel_kernel, grid=(ng, s // tq),
        in_specs=[_smem_spec(), _smem_spec(), qspec,
                  pl.BlockSpec((1, s, LANES), lambda g, i: (g, 0, 0), pipeline_mode=pl.Buffered(1)),
                  cols(s), rows(s + WINDOW), cols(s + WINDOW),
                  pl.BlockSpec((1, nb, tq), lambda g, i: (g, 0, i)),
                  pl.BlockSpec((1, NSA_HPG * NSA_BRANCHES, tq), lambda g, i: (g, 0, i)), qspec],
        out_specs=qspec,
        out_shape=jax.ShapeDtypeStruct((nh, HEAD_DIM, s), F32),
        scratch_shapes=[pltpu.VMEM((_STRIP, NSA_HPG * tq), F32)] + [pltpu.VMEM((TK_SEL, NSA_HPG * tq), F32)] * 3
        + [pltpu.VMEM((8, NSA_HPG * tq), F32), pltpu.VMEM((HEAD_DIM + ONES_ROWS, NSA_HPG * tq), F32)],
        compiler_params=_cparams("arbitrary", "arbitrary"), name="nsa_selected",
    )(thr, tab, nq_t, ks, vs_t, kw_pad, vw_t_pad, sel_t, gates_t, oc)


def _merge_body(osb_ref, onsa_ref, omla_ref, g_ref, x_ref, wsb_ref, wnsa_ref, wmla_ref, wout_ref, fg_ref):
    def heads_proj(o_ref, w_ref):
        nh, hd = w_ref.shape[0], w_ref.shape[1]
        if o_ref.shape[1] == hd:
            o = o_ref[...].reshape(nh * hd, o_ref.shape[2]).T.astype(BF16)
            return _dot(o, w_ref[...].reshape(nh * hd, w_ref.shape[2]))
        y = _dot(o_ref[0], w_ref[0])
        for j in range(1, nh):
            y = y + _dot(o_ref[j], w_ref[j])
        return y

    merged = g_ref[:, 0:D_MODEL] * heads_proj(osb_ref, wsb_ref)
    merged = merged + g_ref[:, D_MODEL:2 * D_MODEL] * heads_proj(onsa_ref, wnsa_ref)
    merged = merged + g_ref[:, 2 * D_MODEL:3 * D_MODEL] * heads_proj(omla_ref, wmla_ref)
    x1 = x_ref[...] + _dot(merged.astype(BF16), wout_ref[...])
    ms = jnp.mean(x1 * x1, axis=-1, keepdims=True)
    return x1, x1 * lax.rsqrt(ms + EPS) * fg_ref[...]


def _merge_kernel(osb_ref, onsa_ref, omla_ref, g_ref, x_ref, wsb_ref, wnsa_ref, wmla_ref, wout_ref, fg_ref,
                  x1_ref, h_ref):
    x1, h = _merge_body(osb_ref, onsa_ref, omla_ref, g_ref, x_ref, wsb_ref, wnsa_ref, wmla_ref, wout_ref,
                        fg_ref)
    x1_ref[...] = x1
    h_ref[...] = h.astype(BF16)


def _merge_router_kernel(osb_ref, onsa_ref, omla_ref, g_ref, x_ref, wsb_ref, wnsa_ref, wmla_ref, wout_ref,
                         fg_ref, wr_ref, x1_ref, h_ref, comb_ref, sel_ref):
    x1, h = _merge_body(osb_ref, onsa_ref, omla_ref, g_ref, x_ref, wsb_ref, wnsa_ref, wmla_ref, wout_ref,
                        fg_ref)
    x1_ref[...] = x1
    h_ref[...] = h.astype(BF16)
    wr = wr_ref[...]
    hp = _split3(h)
    wp = _split3(wr)
    logits = jnp.zeros((h.shape[0], wr.shape[1]), F32)
    for a_i, b_i in ((2, 0), (0, 2), (1, 1), (1, 0), (0, 1), (0, 0)):
        logits = logits + _dot(hp[a_i], wp[b_i])
    lane = lax.broadcasted_iota(jnp.int32, logits.shape, 1)
    nl = logits.shape[1]
    logits = jnp.where(lane < N_EXPERTS, logits, -jnp.inf)
    v1 = jnp.max(logits, axis=-1, keepdims=True)
    i1 = jnp.min(jnp.where(logits == v1, lane, nl), axis=-1, keepdims=True)
    rest = jnp.where(lane == i1, -jnp.inf, logits)
    v2 = jnp.max(rest, axis=-1, keepdims=True)
    i2 = jnp.min(jnp.where(rest == v2, lane, nl), axis=-1, keepdims=True)
    e2 = jnp.exp(v2 - v1)
    w1 = 1.0 / (1.0 + e2)
    w2 = e2 / (1.0 + e2)
    comb_ref[...] = jnp.where(lane == i1, w1, jnp.where(lane == i2, w2, 0.0))
    sel_ref[...] = jnp.where((lane == i1) | (lane == i2), 1.0, 0.0)


def _merge(osb, onsa, omla, g, x, wsb, wnsa, wmla, wout, fgain, wrouter=None):
    s = x.shape[0]
    tm = TM_MERGE if wrouter is None else TM_MERGE_ROUTER

    def heads(a):
        if a.shape[1] == s:
            return pl.BlockSpec((a.shape[0], tm, a.shape[2]), lambda i: (0, i, 0))
        return pl.BlockSpec((a.shape[0], a.shape[1], tm), lambda i: (0, 0, i))

    row = lambda w: pl.BlockSpec((tm, w), lambda i: (i, 0))
    in_specs = [heads(osb), heads(onsa), heads(omla), row(GATE_W), row(D_MODEL)]
    in_specs += [_const_spec(a.shape) for a in (wsb, wnsa, wmla, wout, fgain)]
    out_shape = [jax.ShapeDtypeStruct((s, D_MODEL), F32), jax.ShapeDtypeStruct((s, D_MODEL), BF16)]
    out_specs = [row(D_MODEL), row(D_MODEL)]
    args = [osb, onsa, omla, g, x, wsb, wnsa, wmla, wout, fgain]
    if wrouter is None:
        kern = _merge_kernel
    else:
        kern = _merge_router_kernel
        in_specs.append(_const_spec(wrouter.shape))
        args.append(wrouter)
        out_shape += [jax.ShapeDtypeStruct((s, LANES), F32)] * 2
        out_specs += [row(LANES)] * 2
    return pl.pallas_call(
        kern, grid=(s // tm,), in_specs=in_specs, out_specs=tuple(out_specs), out_shape=tuple(out_shape),
        compiler_params=_cparams("parallel"), name="merge",
    )(*args)


def _ffn_kernel(x_ref, h_ref, wg_ref, wu_ref, wd_ref, o_ref, acc_ref):
    f = pl.program_id(1)

    @pl.when(f == 0)
    def _():
        acc_ref[...] = x_ref[...]

    h = h_ref[...]
    gate = _dot(h, wg_ref[...])
    up = _dot(h, wu_ref[...])
    act = (gate * jax.nn.sigmoid(gate) * up).astype(BF16)
    acc_ref[...] += _dot(act, wd_ref[...])

    @pl.when(f == pl.num_programs(1) - 1)
    def _():
        o_ref[...] = acc_ref[...]


def _dense_ffn(x1, h, wg, wu, wd, tf):
    s = x1.shape[0]
    tm = TM_FFN
    nf = wg.shape[1] // tf
    return pl.pallas_call(
        _ffn_kernel, grid=(s // tm, nf),
        in_specs=[pl.BlockSpec((tm, D_MODEL), lambda i, f: (i, 0)),
                  pl.BlockSpec((tm, D_MODEL), lambda i, f: (i, 0)),
                  pl.BlockSpec((D_MODEL, tf), lambda i, f: (0, f)),
                  pl.BlockSpec((D_MODEL, tf), lambda i, f: (0, f)),
                  pl.BlockSpec((tf, D_MODEL), lambda i, f: (f, 0))],
        out_specs=pl.BlockSpec((tm, D_MODEL), lambda i, f: (i, 0)),
        out_shape=jax.ShapeDtypeStruct((s, D_MODEL), F32),
        scratch_shapes=[pltpu.VMEM((tm, D_MODEL), F32)],
        compiler_params=_cparams("parallel", "arbitrary"), name="dense_ffn",
    )(x1, h, wg, wu, wd)


def _permute_kernel(tile_ref, chunk_ref, flag_ref, exp_ref, dest_ref, cw_ref, h_ref, xg_ref, rw_ref, acc_ref,
                    wacc_ref):
    k = pl.program_id(0)
    tm = TM_MOE
    flags = flag_ref[k]

    @pl.when((flags & 1) != 0)
    def _():
        acc_ref[...] = jnp.zeros_like(acc_ref)
        wacc_ref[...] = jnp.zeros_like(wacc_ref)

    @pl.when((flags & 4) != 0)
    def _():
        row = tile_ref[k] * tm + lax.broadcasted_iota(jnp.int32, (tm, tm), 0)
        hit = dest_ref[0] == row
        acc_ref[...] += _dot(jnp.where(hit, 1.0, 0.0).astype(BF16), h_ref[...])
        wacc_ref[...] += jnp.sum(jnp.where(hit, cw_ref[0], 0.0), axis=-1, keepdims=True)

    @pl.when((flags & 2) != 0)
    def _():
        xg_ref[...] = acc_ref[...].astype(BF16)
        rw_ref[...] = wacc_ref[...]


def _moe_ffn_kernel(te_ref, tv_ref, xg_ref, rw_ref, wg_ref, wu_ref, wd_ref, y_ref, acc_ref):
    t = pl.program_id(0)
    f = pl.program_id(1)

    @pl.when(tv_ref[t] != 0)
    def _():
        @pl.when(f == 0)
        def _():
            acc_ref[...] = jnp.zeros_like(acc_ref)

        x = xg_ref[...]
        gate = _dot(x, wg_ref[0])
        up = _dot(x, wu_ref[0])
        act = (gate * jax.nn.sigmoid(gate) * up * rw_ref[...]).astype(BF16)
        acc_ref[...] += _dot(act, wd_ref[0])

        @pl.when(f == pl.num_programs(1) - 1)
        def _():
            y_ref[...] = acc_ref[...].astype(BF16)

    @pl.when(tv_ref[t] == 0)
    def _():
        y_ref[...] = jnp.zeros_like(y_ref)


def _unpermute_kernel(tile_ref, chunk_ref, flag_ref, grow_ref, x_ref, y_ref, o_ref):
    k = pl.program_id(0)
    tm = TM_MOE
    flags = flag_ref[k]

    @pl.when((flags & 1) != 0)
    def _():
        o_ref[...] = x_ref[...]

    @pl.when((flags & 4) != 0)
    def _():
        row = chunk_ref[k] * tm + lax.broadcasted_iota(jnp.int32, (tm, tm), 1)
        hit = (grow_ref[:, 0:1] == row) | (grow_ref[:, 1:2] == row)
        o_ref[...] += _dot(jnp.where(hit, 1.0, 0.0).astype(BF16), y_ref[...])


def _work_list(first, last, n_items):
    n_tiles = first.shape[0]
    cnt_real = jnp.maximum(last - first + 1, 0)
    cnt = jnp.maximum(cnt_real, 1)
    ends = jnp.cumsum(cnt)
    starts = ends - cnt
    total = ends[-1]
    k = jnp.arange(n_items, dtype=jnp.int32)
    kk = jnp.minimum(k, total - 1)
    tile = jnp.minimum(jnp.sum(kk[:, None] >= ends[None, :], axis=1), n_tiles - 1).astype(jnp.int32)
    pos = kk - starts[tile]
    chunk = jnp.where(cnt_real[tile] > 0, first[tile] + pos, 0).astype(jnp.int32)
    live = k < total
    flags = (jnp.where(live & (pos == 0), 1, 0) | jnp.where(live & (pos == cnt[tile] - 1), 2, 0)
             | jnp.where(live & (pos < cnt_real[tile]), 4, 0))
    return tile, chunk, flags.astype(jnp.int32)


def _moe_ffn(x1, h, comb, selm, wg, wu, wd):
    s = x1.shape[0]
    tm, tf = TM_MOE, TF_MOE
    ne = N_EXPERTS
    n_chunks = s // tm
    n_tiles = 2 * s // tm + ne
    rows = n_tiles * tm
    sel_t = selm[:, :ne].T.astype(jnp.int32)
    cum = jnp.cumsum(sel_t, axis=1)
    counts = cum[:, -1]
    tiles_e = (counts + tm - 1) // tm
    tile_end = jnp.cumsum(tiles_e)
    tile_start = tile_end - tiles_e
    used = tile_end[-1]
    dest = jnp.where(sel_t > 0, tile_start[:, None] * tm + cum - 1, -1).astype(jnp.int32)
    t_idx = jnp.arange(n_tiles, dtype=jnp.int32)
    te_raw = jnp.minimum(jnp.sum(t_idx[:, None] >= tile_end[None, :], axis=1), ne - 1).astype(jnp.int32)
    tvalid = (t_idx < used).astype(jnp.int32)
    last_e = te_raw[jnp.maximum(used - 1, 0)]
    te = jnp.where(tvalid > 0, te_raw, last_e).astype(jnp.int32)
    dest_tiles = dest.reshape(ne, n_chunks, tm)
    big = jnp.int32(2 ** 30)
    d_max = jnp.max(dest_tiles, axis=2)
    d_min = jnp.min(jnp.where(dest_tiles >= 0, dest_tiles, big), axis=2)
    overlap = (d_max[te] >= t_idx[:, None] * tm) & (d_min[te] < (t_idx[:, None] + 1) * tm) & (tvalid[:, None] > 0)
    c_idx = jnp.arange(n_chunks, dtype=jnp.int32)[None, :]
    c_lo = jnp.min(jnp.where(overlap, c_idx, n_chunks), axis=1)
    c_hi = jnp.max(jnp.where(overlap, c_idx, -1), axis=1)
    c_lo = jnp.where(c_hi >= 0, c_lo, 0).astype(jnp.int32)
    p_tile, p_chunk, p_flag = _work_list(c_lo, c_hi.astype(jnp.int32), n_tiles + ne * n_chunks)
    p_exp = te[p_tile]

    cw_t = comb[:, :ne].T.reshape(ne, 1, s)
    dest3 = dest.reshape(ne, 1, s)
    xg, roww = pl.pallas_call(
        _permute_kernel,
        grid_spec=pltpu.PrefetchScalarGridSpec(
            num_scalar_prefetch=4, grid=(p_tile.shape[0],),
            in_specs=[pl.BlockSpec((1, 1, tm), lambda k, pt, pc, pf, pe: (pe[k], 0, pc[k])),
                      pl.BlockSpec((1, 1, tm), lambda k, pt, pc, pf, pe: (pe[k], 0, pc[k])),
                      pl.BlockSpec((tm, D_MODEL), lambda k, pt, pc, pf, pe: (pc[k], 0))],
            out_specs=(pl.BlockSpec((tm, D_MODEL), lambda k, pt, pc, pf, pe: (pt[k], 0)),
                       pl.BlockSpec((tm, 1), lambda k, pt, pc, pf, pe: (pt[k], 0))),
            scratch_shapes=[pltpu.VMEM((tm, D_MODEL), F32), pltpu.VMEM((tm, 1), F32)]),
        out_shape=(jax.ShapeDtypeStruct((rows, D_MODEL), BF16), jax.ShapeDtypeStruct((rows, 1), F32)),
        compiler_params=_cparams("arbitrary"), name="moe_permute",
    )(p_tile, p_chunk, p_flag, p_exp, dest3, cw_t, h)

    nf = wg.shape[2] // tf
    last_f = nf - 1
    fsel = lambda t, f, tv: f * tv[t] + last_f * (1 - tv[t])
    yg = pl.pallas_call(
        _moe_ffn_kernel,
        grid_spec=pltpu.PrefetchScalarGridSpec(
            num_scalar_prefetch=2, grid=(n_tiles, nf),
            in_specs=[pl.BlockSpec((tm, D_MODEL), lambda t, f, te_, tv: (t, 0)),
                      pl.BlockSpec((tm, 1), lambda t, f, te_, tv: (t, 0)),
                      pl.BlockSpec((1, D_MODEL, tf), lambda t, f, te_, tv: (te_[t], 0, fsel(t, f, tv))),
                      pl.BlockSpec((1, D_MODEL, tf), lambda t, f, te_, tv: (te_[t], 0, fsel(t, f, tv))),
                      pl.BlockSpec((1, tf, D_MODEL), lambda t, f, te_, tv: (te_[t], fsel(t, f, tv), 0))],
            out_specs=pl.BlockSpec((tm, D_MODEL), lambda t, f, te_, tv: (t, 0)),
            scratch_shapes=[pltpu.VMEM((tm, D_MODEL), F32)]),
        out_shape=jax.ShapeDtypeStruct((rows, D_MODEL), BF16),
        compiler_params=_cparams("arbitrary", "arbitrary"), name="moe_ffn",
    )(te, tvalid, xg, roww, wg, wu, wd)

    grow = jnp.stack([jnp.min(jnp.where(dest >= 0, dest, big), axis=0), jnp.max(dest, axis=0)], axis=1)
    has = d_max >= 0
    first = jnp.where(has, d_min // tm, 0).T.reshape(-1).astype(jnp.int32)
    last = jnp.where(has, d_max // tm, -1).T.reshape(-1).astype(jnp.int32)
    u_sub, u_chunk, u_flag = _work_list(first, last, n_chunks * 2 * ne)
    u_tile = (u_sub // ne).astype(jnp.int32)
    prev_tile = jnp.concatenate([jnp.full((1,), -1, jnp.int32), u_tile[:-1]])
    u_flag = (u_flag & 4) | jnp.where((u_flag != 0) & (u_tile != prev_tile), 1, 0)
    out = pl.pallas_call(
        _unpermute_kernel,
        grid_spec=pltpu.PrefetchScalarGridSpec(
            num_scalar_prefetch=3, grid=(u_tile.shape[0],),
            in_specs=[pl.BlockSpec((tm, 2), lambda k, ut, uc, uf: (ut[k], 0)),
                      pl.BlockSpec((tm, D_MODEL), lambda k, ut, uc, uf: (ut[k], 0)),
                      pl.BlockSpec((tm, D_MODEL), lambda k, ut, uc, uf: (uc[k], 0))],
            out_specs=pl.BlockSpec((tm, D_MODEL), lambda k, ut, uc, uf: (ut[k], 0))),
        out_shape=jax.ShapeDtypeStruct((s, D_MODEL), F32),
        compiler_params=_cparams("arbitrary"), name="moe_unpermute",
    )(u_tile, u_chunk, u_flag, grow, x1, yg)
    return out


def _t5_bucket(rel):
    n = jnp.maximum(rel, 0)
    max_exact = NUM_BUCKETS // 2
    nf = jnp.maximum(n, 1).astype(jnp.float32)
    large = max_exact + (jnp.log(nf / max_exact) / math.log(MAX_DISTANCE / max_exact)
                         * (NUM_BUCKETS - max_exact)).astype(jnp.int32)
    large = jnp.minimum(large, NUM_BUCKETS - 1)
    return jnp.where(n < max_exact, n, large)


def _bucket_thresholds():
    buckets = _t5_bucket(jnp.arange(MAX_DISTANCE + 1, dtype=jnp.int32))
    b = jnp.arange(NUM_BUCKETS, dtype=jnp.int32)
    return jnp.sum(buckets[None, :] < b[:, None], axis=1).astype(jnp.int32)


def _rope_tables(s):
    half = MLA_ROPE // 2
    inv = ROPE_THETA ** (-jnp.arange(half, dtype=jnp.float32) / half)
    ang = jnp.arange(s).astype(jnp.float32)[:, None] * inv[None, :]
    cos, sin = jnp.cos(ang), jnp.sin(ang)
    z = lambda w: jnp.zeros((s, w), F32)
    pad = MLA_PAD - MLA_QK
    cos_t = jnp.concatenate([jnp.ones((s, MLA_NOPE), F32), cos, cos, z(pad)], axis=1)
    sin_lo = jnp.concatenate([z(MLA_NOPE), -sin, z(half), z(pad)], axis=1)
    sin_hi = jnp.concatenate([z(MLA_NOPE), z(half), sin, z(pad)], axis=1)
    return cos_t, sin_lo, sin_hi


def _pad_lanes(a, width):
    return jnp.pad(a, [(0, 0)] * (a.ndim - 1) + [(0, width - a.shape[-1])])


def kernel(x, rel_bias_table, mix_norm, w_in, b_gate, sb_w_o, nsa_q_norm, nsa_k_norm, nsa_cmp_pos, nsa_cmp_w1, nsa_cmp_w2, nsa_w_o, mla_q_a_norm, mla_kv_a_norm, mla_w_uq, mla_w_ukv, mla_q_norm, mla_k_norm, mla_w_o, w_out, ffn_norm, dense_w_gate, dense_w_up, dense_w_down, moe_router, moe_w_gate, moe_w_up, moe_w_down):
    b, s, d = x.shape
    assert b == 1 and d == D_MODEL and s % TM_MOE == 0 and s // SEL_BLOCK >= SEL_TOPK
    depth = w_in.shape[0]
    xs = x.reshape(s, d)
    thr = _bucket_thresholds()
    cos_t, sin_lo, sin_hi = _rope_tables(s)
    n_chunk = s // CMP_STRIDE

    for layer in range(depth):
        w = w_in[layer]
        small = jnp.concatenate([
            w[:, _C_NG:_C_MCQ], jnp.zeros((d, _KR_LANE - NSA_HEADS * NSA_BRANCHES), F32),
            w[:, _C_MKR:_C_GATE], jnp.zeros((d, LANES - _KR_LANE - MLA_ROPE), F32)], axis=1)
        wa = jnp.concatenate([w[:, :_C_NG], w[:, _C_MCQ:_C_MKR], small], axis=1).astype(BF16)
        wg = w[:, _C_GATE:].astype(BF16)
        wuq = _pad_lanes(mla_w_uq[layer].reshape(MLA_Q_LORA, MLA_HEADS, MLA_QK), MLA_PAD)
        wuq = wuq.reshape(MLA_Q_LORA, MLA_HEADS * MLA_PAD).astype(BF16)
        wukv = mla_w_ukv[layer].reshape(MLA_KV_LORA, MLA_HEADS, MLA_NOPE + MLA_V)
        wukv = jnp.concatenate([
            _pad_lanes(wukv[:, :, :MLA_NOPE], MLA_PAD).reshape(MLA_KV_LORA, MLA_HEADS * MLA_PAD),
            wukv[:, :, MLA_NOPE:].reshape(MLA_KV_LORA, MLA_HEADS * MLA_V)], axis=1).astype(BF16)
        (sbq, sbk, sbv, nq, nkvc, nks, nvs, nkw, nvw, gsm, mq, mk, mv, g) = _input_projection(
            xs, mix_norm[layer][None], wa, wg, b_gate[layer][None], wuq, wukv,
            nsa_q_norm[layer][None], nsa_k_norm[layer], mla_q_a_norm[layer][None],
            mla_kv_a_norm[layer][None], _pad_lanes(mla_q_norm[layer][None], MLA_PAD),
            _pad_lanes(mla_k_norm[layer][None], MLA_PAD), cos_t, sin_lo, sin_hi)

        o_sb = _stick_breaking(sbq, sbk, sbv)
        o_mla = _mla_attention(mq, mk, mv)

        x2 = nkvc.reshape(2, NSA_GROUPS, n_chunk, CMP_STRIDE * HEAD_DIM)
        cmp, cmp_t = _compress(x2, nsa_cmp_pos[layer].reshape(2, 1, CMP_LEN * HEAD_DIM),
                               nsa_cmp_w1[layer].astype(BF16), nsa_cmp_w2[layer].astype(BF16),
                               nsa_k_norm[layer][0:1])
        gates = gsm[:NSA_HEADS * NSA_BRANCHES].reshape(NSA_GROUPS, NSA_HPG * NSA_BRANCHES, s)
        oc, sel = _nsa_compressed(thr, rel_bias_table, nq, cmp, cmp_t, gates)
        o_nsa = _nsa_selected(thr, rel_bias_table, nq, nks, nvs,
                              jnp.pad(nkw, ((0, 0), (WINDOW, 0), (0, 0))),
                              jnp.pad(nvw, ((0, 0), (0, 0), (WINDOW, 0))), sel, gates, oc)

        wsb = sb_w_o[layer].reshape(SB_HEADS, HEAD_DIM, d).astype(BF16)
        wnsa = nsa_w_o[layer].reshape(NSA_HEADS, HEAD_DIM, d).astype(BF16)
        wmla = mla_w_o[layer].reshape(MLA_HEADS, MLA_V, d).astype(BF16)
        wout = w_out[layer].astype(BF16)
        j = layer // 2
        if layer % 2 == 0:
            x1, h = _merge(o_sb, o_nsa, o_mla, g, xs, wsb, wnsa, wmla, wout, ffn_norm[layer][None])
            d_ff = dense_w_gate.shape[2]
            xs = _dense_ffn(x1, h, dense_w_gate[j].astype(BF16), dense_w_up[j].astype(BF16),
                            dense_w_down[j].astype(BF16), d_ff // 2)
        else:
            x1, h, comb, selm = _merge(o_sb, o_nsa, o_mla, g, xs, wsb, wnsa, wmla, wout,
                                       ffn_norm[layer][None], _pad_lanes(moe_router[j], LANES))
            xs = _moe_ffn(x1, h, comb, selm, moe_w_gate[j].astype(BF16), moe_w_up[j].astype(BF16),
                          moe_w_down[j].astype(BF16))
    return xs.reshape(b, s, d)
```

```python
import functools
import math

import jax
import jax.numpy as jnp
from jax import lax
from jax.experimental import pallas as pl
from jax.experimental.pallas import tpu as pltpu

F32 = jnp.float32
BF16 = jnp.bfloat16

D_MODEL = 1024
HEAD_DIM = 64
SB_HEADS = 4
NSA_HEADS = 8
NSA_GROUPS = 2
NSA_HPG = NSA_HEADS // NSA_GROUPS
NSA_BRANCHES = 3
MLA_HEADS = 4
MLA_NOPE = 64
MLA_ROPE = 32
MLA_V = 64
MLA_QK = MLA_NOPE + MLA_ROPE
MLA_Q_LORA = 256
MLA_KV_LORA = 128
ROPE_THETA = 10000.0
CMP_LEN = 32
CMP_STRIDE = 16
CMP_HIDDEN = 256
SEL_BLOCK = 64
SEL_TOPK = 16
N_LOCAL_BLOCKS = 2
WINDOW = 512
NUM_BUCKETS = 32
MAX_DISTANCE = 1024
N_MIXERS = 3
N_EXPERTS = 8
EPS = 1e-6
NEG_INF = -1e30
FORCE = 1e30
LOG2E = math.log2(math.e)

SB_W = SB_HEADS * HEAD_DIM
NSA_QW = NSA_HEADS * HEAD_DIM
NSA_KVW = NSA_GROUPS * HEAD_DIM
GATE_W = N_MIXERS * D_MODEL
LANES = 128
MLA_PAD = LANES

_C_NQ = 3 * SB_W
_C_NKV = _C_NQ + NSA_QW
_C_NG = _C_NKV + 6 * NSA_KVW
_C_MCQ = _C_NG + NSA_HEADS * NSA_BRANCHES
_C_MCKV = _C_MCQ + MLA_Q_LORA
_C_MKR = _C_MCKV + MLA_KV_LORA
_C_GATE = _C_MKR + MLA_ROPE
_A_SB = 0
_A_NQ = 3 * SB_W
_A_NKV = _A_NQ + NSA_QW
_A_MCQ = _A_NKV + 6 * NSA_KVW
_A_MCKV = _A_MCQ + MLA_Q_LORA
_A_SMALL = _A_MCKV + MLA_KV_LORA
_A_COLS = _A_SMALL + LANES
_KR_LANE = MLA_NOPE

TM_PROJ = 256
TQ_SB = 512
TK_SB = 256
TQ_MLA = 256
TK_MLA = 1024
CHUNK_MLA = 256
CHUNK_SEL = 512
TQ_CMP = 128
CMP_PARTS = 4
TQ_SEL = 256
TK_SEL = 512
TM_MERGE = 512
TM_MERGE_ROUTER = 256
TM_FFN = 512
TM_MOE = 512
TF_MOE = 512
TU_MOE = 256
VMEM_LIMIT = 56 * 1024 * 1024
SB_EXIT = -104.0

_NT = (((1,), (1,)), ((), ()))


def _dot(a, b):
    return jnp.dot(a, b, preferred_element_type=F32)


def _dot_nt(a, b):
    return lax.dot_general(a, b, _NT, preferred_element_type=F32)


def _cparams(*sem):
    return pltpu.CompilerParams(dimension_semantics=sem, vmem_limit_bytes=VMEM_LIMIT)


def _const_spec(shape):
    nd = len(shape)
    return pl.BlockSpec(shape, lambda *_: (0,) * nd, pipeline_mode=pl.Buffered(1))


def _smem_spec():
    return pl.BlockSpec(memory_space=pltpu.SMEM)


def _split3(x):
    h1 = x.astype(BF16)
    r1 = x - h1.astype(F32)
    h2 = r1.astype(BF16)
    h3 = (r1 - h2.astype(F32)).astype(BF16)
    return h1, h2, h3


def _bias_chain(rel, thr_ref, tab_ref, head):
    out = jnp.full(rel.shape, tab_ref[0, head], F32)
    for b in range(1, NUM_BUCKETS):
        out = jnp.where(rel >= thr_ref[b], tab_ref[b, head], out)
    return out


def _rope(y, cos, sin_lo, sin_hi):
    half = MLA_ROPE // 2
    return y * cos + pltpu.roll(y, half, 1) * sin_hi + pltpu.roll(y, MLA_PAD - half, 1) * sin_lo


def _proj_kernel(x_ref, gain_ref, wa_ref, wg_ref, bg_ref, wuq_ref, wukv_ref, nqg_ref, nkg_ref,
                 qag_ref, kvag_ref, mqg_ref, mkg_ref, cos_ref, slo_ref, shi_ref,
                 sbq_ref, sbk_ref, sbv_ref, nq_ref, nkvc_ref, nks_ref, nvs_ref, nkw_ref, nvw_ref,
                 gsm_ref, mq_ref, mk_ref, mv_ref, g_ref):
    x = x_ref[...]
    ms = jnp.mean(x * x, axis=-1, keepdims=True)
    h = (x * lax.rsqrt(ms + EPS) * gain_ref[...]).astype(BF16)

    def rms(p, gain, width):
        m = jnp.sum(p * p, axis=-1, keepdims=True) * (1.0 / width)
        return p * lax.rsqrt(m + EPS) * gain

    p = _dot(h, wa_ref[:, _A_SB:_A_SB + 3 * SB_W])
    sbq_t = (p[:, :SB_W] * HEAD_DIM ** -0.5).T.astype(BF16)
    sbv_t = p[:, 2 * SB_W:3 * SB_W].T.astype(BF16)
    for j in range(SB_HEADS):
        sbq_ref[j] = sbq_t[j * HEAD_DIM:(j + 1) * HEAD_DIM]
        sbv_ref[j] = sbv_t[j * HEAD_DIM:(j + 1) * HEAD_DIM]
    for j in range(SB_HEADS // 2):
        sbk_ref[j] = p[:, SB_W + j * LANES:SB_W + (j + 1) * LANES].astype(BF16)

    p = _dot(h, wa_ref[:, _A_NQ:_A_NQ + NSA_QW])
    qn = [rms(p[:, j * HEAD_DIM:(j + 1) * HEAD_DIM], nqg_ref[...], HEAD_DIM) * (HEAD_DIM ** -0.5 * LOG2E)
          for j in range(NSA_HEADS)]
    qn_t = jnp.concatenate(qn, axis=1).T.astype(BF16)
    for j in range(NSA_HEADS):
        nq_ref[j] = qn_t[j * HEAD_DIM:(j + 1) * HEAD_DIM]

    p = _dot(h, wa_ref[:, _A_NKV:_A_NKV + 6 * NSA_KVW])
    vs_t = p[:, 3 * NSA_KVW:4 * NSA_KVW].T.astype(BF16)
    tag_shape = (x.shape[0], LANES - HEAD_DIM)
    block_id = (pl.program_id(0) * x.shape[0] + lax.broadcasted_iota(jnp.int32, tag_shape, 0)) // SEL_BLOCK
    block_tag = jnp.where(lax.broadcasted_iota(jnp.int32, tag_shape, 1) == block_id % (TK_SEL // SEL_BLOCK),
                          1.0, 0.0)
    vw_t = p[:, 5 * NSA_KVW:6 * NSA_KVW].T.astype(BF16)
    for j in range(NSA_GROUPS):
        def seg(s):
            return p[:, s * NSA_KVW + j * HEAD_DIM:s * NSA_KVW + (j + 1) * HEAD_DIM]
        nkvc_ref[0, j] = seg(0)
        nkvc_ref[1, j] = seg(1)
        nks_ref[j] = jnp.concatenate([rms(seg(2), nkg_ref[1:2, :], HEAD_DIM), block_tag], axis=1).astype(BF16)
        nvs_ref[j] = vs_t[j * HEAD_DIM:(j + 1) * HEAD_DIM]
        nkw_ref[j] = rms(seg(4), nkg_ref[2:3, :], HEAD_DIM).astype(BF16)
        nvw_ref[j] = vw_t[j * HEAD_DIM:(j + 1) * HEAD_DIM]

    p = _dot(h, wa_ref[:, _A_MCQ:_A_COLS])
    cq = rms(p[:, :MLA_Q_LORA], qag_ref[...], MLA_Q_LORA).astype(BF16)
    ckv = rms(p[:, MLA_Q_LORA:MLA_Q_LORA + MLA_KV_LORA], kvag_ref[...], MLA_KV_LORA).astype(BF16)
    small = p[:, MLA_Q_LORA + MLA_KV_LORA:]
    gsm_ref[...] = jax.nn.sigmoid(small).T
    lane = lax.broadcasted_iota(jnp.int32, small.shape, 1)
    krope = jnp.where((lane >= _KR_LANE) & (lane < _KR_LANE + MLA_ROPE), small, 0.0)
    cos, slo, shi = cos_ref[...], slo_ref[...], shi_ref[...]
    qu = _dot(cq, wuq_ref[...])
    kvu = _dot(ckv, wukv_ref[...])
    for j in range(MLA_HEADS):
        qh = rms(qu[:, j * MLA_PAD:(j + 1) * MLA_PAD], mqg_ref[...], MLA_QK)
        mq_ref[j] = (_rope(qh, cos, slo, shi) * MLA_QK ** -0.5).T.astype(BF16)
        kh = rms(kvu[:, j * MLA_PAD:(j + 1) * MLA_PAD] + krope, mkg_ref[...], MLA_QK)
        mk_ref[j] = _rope(kh, cos, slo, shi).astype(BF16)
    v_t = kvu[:, MLA_HEADS * MLA_PAD:].T.astype(BF16)
    for j in range(MLA_HEADS):
        mv_ref[j] = v_t[j * MLA_V:(j + 1) * MLA_V]

    g_ref[...] = jax.nn.sigmoid(_dot(h, wg_ref[...]) + bg_ref[...])


def _input_projection(x, gain, wa, wg, bg, wuq, wukv, nqg, nkg, qag, kvag, mqg, mkg, cos, slo, shi):
    s = x.shape[0]
    tm = TM_PROJ
    row = lambda w: pl.BlockSpec((tm, w), lambda i: (i, 0))
    heads = lambda n, w: pl.BlockSpec((n, tm, w), lambda i: (0, i, 0))
    hs = lambda n, w, dt: jax.ShapeDtypeStruct((n, s, w), dt)
    in_specs = [row(D_MODEL)] + [_const_spec(a.shape) for a in
                                 (gain, wa, wg, bg, wuq, wukv, nqg, nkg, qag, kvag, mqg, mkg)]
    in_specs += [row(MLA_PAD)] * 3
    hst = lambda n, w, dt: jax.ShapeDtypeStruct((n, w, s), dt)
    heads_t = lambda n, w: pl.BlockSpec((n, w, tm), lambda i: (0, 0, i))
    ng, hd = NSA_GROUPS, HEAD_DIM
    out_shape = (hst(SB_HEADS, hd, BF16), hs(SB_HEADS // 2, LANES, BF16), hst(SB_HEADS, hd, BF16))
    out_shape += (hst(NSA_HEADS, hd, BF16),)
    out_shape += (jax.ShapeDtypeStruct((2, ng, s, hd), F32),) + (hs(ng, LANES, BF16), hst(ng, hd, BF16), hs(ng, hd, BF16), hst(ng, hd, BF16))
    out_shape += (jax.ShapeDtypeStruct((LANES, s), F32),)
    out_shape += (hst(MLA_HEADS, MLA_PAD, BF16), hs(MLA_HEADS, MLA_PAD, BF16), hst(MLA_HEADS, MLA_V, BF16))
    out_shape += (jax.ShapeDtypeStruct((s, GATE_W), F32),)
    out_specs = (heads_t(SB_HEADS, hd), heads(SB_HEADS // 2, LANES), heads_t(SB_HEADS, hd))
    out_specs += (heads_t(NSA_HEADS, hd),)
    out_specs += (pl.BlockSpec((2, ng, tm, hd), lambda i: (0, 0, i, 0)),) + (heads(ng, LANES), heads_t(ng, hd), heads(ng, hd), heads_t(ng, hd))
    out_specs += (pl.BlockSpec((LANES, tm), lambda i: (0, i)),)
    out_specs += (heads_t(MLA_HEADS, MLA_PAD), heads(MLA_HEADS, MLA_PAD), heads_t(MLA_HEADS, MLA_V), row(GATE_W))
    return pl.pallas_call(
        _proj_kernel, grid=(s // tm,), in_specs=in_specs, out_specs=out_specs, out_shape=out_shape,
        compiler_params=_cparams("parallel"), name="input_projection",
    )(x, gain, wa, wg, bg, wuq, wukv, nqg, nkg, qag, kvag, mqg, mkg, cos, slo, shi)


def _sb_kernel(qt_ref, k2_ref, vt_ref, o_ref):
    tq, tk = TQ_SB, TK_SB
    nh = qt_ref.shape[0]
    i = pl.program_id(0)
    later = (lax.broadcasted_iota(jnp.int32, (tk, tk), 1)
             > lax.broadcasted_iota(jnp.int32, (tk, tk), 0)).astype(BF16)
    key = lax.broadcasted_iota(jnp.int32, (tk, tq), 0)
    qry = i * tq + lax.broadcasted_iota(jnp.int32, (tk, tq), 1)
    zeros = jnp.zeros((HEAD_DIM, tq), BF16)
    q_aug = [jnp.concatenate([qt_ref[h], zeros] if h % 2 == 0 else [zeros, qt_ref[h]], axis=0) for h in range(nh)]

    def tile(kt, h, carry, acc, diagonal):
        off = pl.multiple_of(kt * tk, tk)
        z = _dot(k2_ref[h // 2, pl.ds(off, tk), :], q_aug[h])
        log_stay = -(jnp.maximum(z, 0.0) + jnp.log(1.0 + jnp.exp(-jnp.abs(z))))
        if diagonal:
            causal = off + key < qry
            log_stay = jnp.where(causal, log_stay, 0.0)
        hi = log_stay.astype(BF16)
        lo = (log_stay - hi.astype(F32)).astype(BF16)
        suffix = _dot(later, hi) + _dot(later, lo)
        logw = z + log_stay + (carry + suffix)
        if diagonal:
            logw = jnp.where(causal, logw, NEG_INF)
        w = jnp.exp(logw).astype(BF16)
        acc = acc + _dot(vt_ref[h, :, pl.ds(off, tk)], w)
        carry = carry + suffix[0:1, :] + log_stay[0:1, :]
        return carry, acc

    def all_heads(kt, st, diagonal):
        out = [tile(kt, h, st[2 * h], st[2 * h + 1], diagonal) for h in range(nh)]
        return tuple(x for pair in out for x in pair)

    ratio = tq // tk
    st = (jnp.zeros((1, tq), F32), jnp.zeros((HEAD_DIM, tq), F32)) * nh
    for d in range(ratio):
        st = all_heads(ratio * i + ratio - 1 - d, st, True)

    def cond(c):
        top = c[1]
        for h in range(1, nh):
            top = jnp.maximum(top, c[1 + 2 * h])
        return jnp.logical_and(c[0] >= 0, jnp.max(top) > SB_EXIT)

    def body(c):
        return (c[0] - 1,) + all_heads(c[0], c[1:], False)

    c = lax.while_loop(cond, body, (ratio * i - 1,) + st)
    for h in range(nh):
        o_ref[h] = c[2 + 2 * h]


def _stick_breaking(q_t, k2, v_t):
    nh, _, s = q_t.shape
    tq = TQ_SB
    whole = lambda a: pl.BlockSpec(a.shape, lambda i: (0, 0, 0), pipeline_mode=pl.Buffered(1))
    return pl.pallas_call(
        _sb_kernel, grid=(s // tq,),
        in_specs=[pl.BlockSpec((nh, HEAD_DIM, tq), lambda i: (0, 0, i)), whole(k2), whole(v_t)],
        out_specs=pl.BlockSpec((nh, HEAD_DIM, tq), lambda i: (0, 0, i)),
        out_shape=jax.ShapeDtypeStruct((nh, HEAD_DIM, s), F32),
        compiler_params=_cparams("parallel"), name="stick_breaking",
    )(q_t, k2, v_t)


ONES_ROWS = 16


def _with_ones_rows(v_t):
    return jnp.concatenate([v_t, jnp.ones((ONES_ROWS, v_t.shape[1]), v_t.dtype)], axis=0)


def _pipelined_blocks(n_last, score_chunk, value_chunk, n_chunks, chunk, exp_fn, bufs, st):
    rows = lambda c: pl.ds(c * chunk, chunk)
    bufs, stat_ref, acc_ref = bufs[:3], bufs[3], bufs[4]
    m_row = 3
    stat_ref[m_row:m_row + 1, :] = st[0]
    acc_ref[...] = st[1]

    def prepare(kb, slot, maybe_diagonal):
        mx = None
        for c in range(n_chunks):
            s = score_chunk(kb, c, maybe_diagonal)
            bufs[slot][rows(c), :] = s
            smax = jnp.max(s, axis=0, keepdims=True)
            mx = smax if mx is None else jnp.maximum(mx, smax)
        stat_ref[slot:slot + 1, :] = mx

    def consume(kb, slot):
        m = stat_ref[m_row:m_row + 1, :]
        m_new = jnp.maximum(m, stat_ref[slot:slot + 1, :])
        stat_ref[m_row:m_row + 1, :] = m_new
        part = None
        for c in range(n_chunks):
            p = exp_fn(bufs[slot][rows(c), :] - m_new).astype(BF16)
            d = _dot(value_chunk(kb, c), p)
            part = d if part is None else part + d
        acc_ref[...] = exp_fn(m - m_new) * acc_ref[...] + part

    def run(kb0, n_blocks, last_is_diagonal):
        n_prep = n_blocks if last_is_diagonal else n_blocks + 2
        for j in range(n_blocks):
            if j + 2 < n_prep:
                prepare(kb0 + j + 2, (j + 2) % 3, last_is_diagonal and j + 2 == n_blocks - 1)
            consume(kb0 + j, j % 3)

    trips = jnp.maximum(n_last - 2, 0) // 3
    prepare(0, 0, True)
    prepare(1, 1, True)

    def trip(t, carry):
        run(3 * t, 3, False)
        return carry

    lax.fori_loop(0, trips, trip, 0)
    kb0 = 3 * trips
    for r in range(5):
        pl.when(n_last - kb0 == r)(functools.partial(run, kb0, r + 1, True))
    return stat_ref[m_row:m_row + 1, :], acc_ref[...]


def _mla_kernel(qt_ref, k_ref, vt_ref, o_ref, *bufs):
    tq, tk = TQ_MLA, TK_MLA
    i = pl.program_id(1)
    q_t = qt_ref[0]

    ck = CHUNK_MLA

    def score_chunk(kb, c, maybe_diagonal):
        off = pl.multiple_of(kb * tk + c * ck, ck)
        s = _dot(k_ref[0, pl.ds(off, ck), :], q_t)
        if maybe_diagonal:
            key = off + lax.broadcasted_iota(jnp.int32, (ck, tq), 0)
            qry = i * tq + lax.broadcasted_iota(jnp.int32, (ck, tq), 1)
            s = jnp.where(key <= qry, s, NEG_INF)
        return s

    def value_chunk(kb, c):
        off = pl.multiple_of(kb * tk + c * ck, ck)
        return _with_ones_rows(vt_ref[0, :, pl.ds(off, ck)])

    st = (jnp.full((1, tq), NEG_INF, F32), jnp.zeros((MLA_V + ONES_ROWS, tq), F32))
    _, acc = _pipelined_blocks((i * tq) // tk, score_chunk, value_chunk, tk // ck, ck, jnp.exp,
                               bufs, st)
    o_ref[0] = acc[:MLA_V] / acc[MLA_V:MLA_V + 1]


def _mla_attention(q_t, k, v_t):
    nh, s, _ = k.shape
    tq, tk = TQ_MLA, TK_MLA
    return pl.pallas_call(
        _mla_kernel, grid=(nh, s // tq),
        in_specs=[pl.BlockSpec((1, MLA_PAD, tq), lambda h, i: (h, 0, i)),
                  pl.BlockSpec((1, s, MLA_PAD), lambda h, i: (h, 0, 0)),
                  pl.BlockSpec((1, MLA_V, s), lambda h, i: (h, 0, 0))],
        out_specs=pl.BlockSpec((1, MLA_V, tq), lambda h, i: (h, 0, i)),
        out_shape=jax.ShapeDtypeStruct((nh, MLA_V, s), F32),
        scratch_shapes=[pltpu.VMEM((tk, tq), F32)] * 3 + [pltpu.VMEM((8, tq), F32),
                                                          pltpu.VMEM((MLA_V + ONES_ROWS, tq), F32)],
        compiler_params=_cparams("parallel", "parallel"), name="mla_attention",
    )(q_t, k, v_t)


def _compress_kernel(x_ref, pos_ref, w1_ref, w2_ref, kg_ref, o_ref, ot_ref):
    kv = pl.program_id(0)
    half = CMP_STRIDE * HEAD_DIM
    a = x_ref[0, 0].astype(BF16)
    w1 = w1_ref[0]
    first = _dot(a, w1[:half])
    second = _dot(a, w1[half:])
    n = second.shape[0]
    second = pltpu.roll(second, n - 1, 0)
    posb = jnp.broadcast_to(pos_ref[0], (8, 2 * half)).astype(BF16)
    hid = first + second + _dot(posb, w1)[0:1]
    hid = hid * jax.nn.sigmoid(hid)
    out = _dot(hid.astype(BF16), w2_ref[0])
    ms = jnp.mean(out * out, axis=-1, keepdims=True)
    normed = out * lax.rsqrt(ms + EPS) * kg_ref[...]
    res = jnp.where(kv == 0, normed, out)
    o_ref[0, 0] = res
    ot_ref[0, 0] = res.T


def _compress(x2, pos, w1, w2, kgain):
    _, ng, nc, cw = x2.shape
    return pl.pallas_call(
        _compress_kernel, grid=(2, ng),
        in_specs=[pl.BlockSpec((1, 1, nc, cw), lambda a, g: (a, g, 0, 0)),
                  pl.BlockSpec((1, 1, 2 * cw), lambda a, g: (a, 0, 0)),
                  pl.BlockSpec((1, 2 * cw, CMP_HIDDEN), lambda a, g: (a, 0, 0)),
                  pl.BlockSpec((1, CMP_HIDDEN, HEAD_DIM), lambda a, g: (a, 0, 0)),
                  _const_spec(kgain.shape)],
        out_specs=(pl.BlockSpec((1, 1, nc, HEAD_DIM), lambda a, g: (a, g, 0, 0)),
                   pl.BlockSpec((1, 1, HEAD_DIM, nc), lambda a, g: (a, g, 0, 0))),
        out_shape=(jax.ShapeDtypeStruct((2, ng, nc, HEAD_DIM), F32),
                   jax.ShapeDtypeStruct((2, ng, HEAD_DIM, nc), F32)),
        compiler_params=_cparams("parallel", "parallel"), name="nsa_compress",
    )(x2, pos, w1, w2, kgain)


def _nsa_cmp_kernel(thr_ref, tab_ref, qt_ref, kc_ref, vct_ref, gt_ref, oc_ref, selt_ref, band_ref, pool_ref):
    tq = TQ_CMP
    r_n = NSA_HPG
    g = pl.program_id(0)
    i = pl.program_id(1)
    nc = kc_ref.shape[2]
    nb = selt_ref.shape[1]
    cpt = tq // CMP_STRIDE

    @pl.when(i == 0)
    def _init():
        m = lax.broadcasted_iota(jnp.int32, (2 * nc, tq), 0)
        a = lax.broadcasted_iota(jnp.int32, (2 * nc, tq), 1)
        rel = a - CMP_STRIDE * (m - nc) - (CMP_LEN - 1)
        for r in range(r_n):
            band_ref[r] = _bias_chain(rel, thr_ref, tab_ref, g * r_n + r) * LOG2E
        b = lax.broadcasted_iota(jnp.int32, (nb, nc), 0)
        n = lax.broadcasted_iota(jnp.int32, (nb, nc), 1)
        rs = SEL_BLOCK // CMP_STRIDE
        rc = CMP_LEN // CMP_STRIDE
        member = (n >= rs * b - (rc - 1)) & (n <= rs * b + rs - 1) & (n < nc - (rc - 1))
        pool_ref[...] = jnp.where(member, 1.0, 0.0).astype(BF16)

    q_t = jnp.concatenate([qt_ref[r] for r in range(r_n)], axis=1)
    boff = pl.multiple_of(nc - cpt * i, cpt)
    gates = gt_ref[0]

    def tile_body(nc_e, nb_e):
        s = _dot(kc_ref[0, 0, :nc_e, :].astype(BF16), q_t)
        qpos = i * tq + lax.broadcasted_iota(jnp.int32, (nc_e, tq), 1)
        cmp_end = lax.broadcasted_iota(jnp.int32, (nc_e, tq), 0) * CMP_STRIDE + (CMP_LEN - 1)
        valid = cmp_end <= qpos
        imp = jnp.zeros((nc_e, tq), F32)
        probs = []
        for r in range(r_n):
            sr = s[:, r * tq:(r + 1) * tq] + band_ref[r, pl.ds(boff, nc_e), :]
            sr = jnp.where(valid, sr, NEG_INF)
            mx = jnp.max(sr, axis=0, keepdims=True)
            e = jnp.where(valid, jnp.exp2(sr - mx), 0.0)
            den = jnp.sum(e, axis=0, keepdims=True)
            p = e / jnp.where(den > 0.0, den, 1.0)
            imp = imp + p
            probs.append(p.astype(BF16))
        oc = _dot(vct_ref[0, 0, :, :nc_e].astype(BF16), jnp.concatenate(probs, axis=1))
        for r in range(r_n):
            c = NSA_BRANCHES * r
            oc_ref[r] = oc[:, r * tq:(r + 1) * tq] * gates[c:c + 1, :]

        pool = pool_ref[:nb_e, :nc_e]
        h1, h2, h3 = _split3(imp)
        imp_blk = _dot(pool, h1) + _dot(pool, h2) + _dot(pool, h3)
        blk = lax.broadcasted_iota(jnp.int32, (nb_e, tq), 0)
        qp = i * tq + lax.broadcasted_iota(jnp.int32, (nb_e, tq), 1)
        causal_b = blk * SEL_BLOCK <= qp
        dist = qp // SEL_BLOCK - blk
        forced = (blk == 0) | ((dist >= 0) & (dist < N_LOCAL_BLOCKS))
        score = jnp.where(causal_b, jnp.where(forced, FORCE, imp_blk), NEG_INF)
        for _ in range(min(SEL_TOPK, nb_e)):
            mx = jnp.max(score, axis=0, keepdims=True)
            first = jnp.min(jnp.where(score == mx, blk, nb_e), axis=0, keepdims=True)
            score = jnp.where(blk == first, -jnp.inf, score)
        selt_ref[0, :nb_e, :] = jnp.where((score == -jnp.inf) & causal_b, 1.0, 0.0)
        if nb_e < nb:
            selt_ref[0, nb_e:, :] = jnp.zeros((nb - nb_e, tq), F32)

    tiles_per_part = pl.num_programs(1) // CMP_PARTS
    part = i // tiles_per_part
    for j in range(CMP_PARTS):
        pl.when(part == j)(functools.partial(tile_body, (j + 1) * nc // CMP_PARTS, (j + 1) * nb // CMP_PARTS))


def _nsa_compressed(thr, tab, nq_t, cmp, cmp_t, gates_t):
    nh, _, s = nq_t.shape
    ng = NSA_GROUPS
    tq = TQ_CMP
    nc = cmp.shape[2]
    nb = s // SEL_BLOCK
    qspec = pl.BlockSpec((NSA_HPG, HEAD_DIM, tq), lambda g, i: (g, 0, i))
    return pl.pallas_call(
        _nsa_cmp_kernel, grid=(ng, s // tq),
        in_specs=[_smem_spec(), _smem_spec(), qspec,
                  pl.BlockSpec((1, 1, nc, HEAD_DIM), lambda g, i: (0, g, 0, 0)),
                  pl.BlockSpec((1, 1, HEAD_DIM, nc), lambda g, i: (1, g, 0, 0)),
                  pl.BlockSpec((1, NSA_HPG * NSA_BRANCHES, tq), lambda g, i: (g, 0, i))],
        out_specs=(qspec, pl.BlockSpec((1, nb, tq), lambda g, i: (g, 0, i))),
        out_shape=(jax.ShapeDtypeStruct((nh, HEAD_DIM, s), F32),
                   jax.ShapeDtypeStruct((ng, nb, s), F32)),
        scratch_shapes=[pltpu.VMEM((NSA_HPG, 2 * nc, tq), F32), pltpu.VMEM((nb, nc), BF16)],
        compiler_params=_cparams("arbitrary", "arbitrary"), name="nsa_compressed",
    )(thr, tab, nq_t, cmp, cmp_t, gates_t)


_NEAR_SPAN = ((MAX_DISTANCE + TK_SEL - 2) // TQ_SEL) * TQ_SEL
assert _NEAR_SPAN >= MAX_DISTANCE and TK_SEL % TQ_SEL == 0
_STRIP = 2 * TK_SEL + _NEAR_SPAN
_WIN_KEYS = WINDOW + TQ_SEL
_BLOCKS_PER_TILE = TK_SEL // SEL_BLOCK


def _nsa_sel_kernel(thr_ref, tab_ref, qt_ref, ks_ref, vst_ref, kw_ref, vwt_ref, selt_ref, gt_ref, oc_ref,
                    o_ref, strip_ref, *bufs):
    tq, tk, r_n = TQ_SEL, TK_SEL, NSA_HPG
    g = pl.program_id(0)
    i = pl.program_id(1)
    top = tk + _NEAR_SPAN
    lanes = r_n * tq

    @pl.when(i == 0)
    def _init():
        u = lax.broadcasted_iota(jnp.int32, (_STRIP, tq), 0)
        a = lax.broadcasted_iota(jnp.int32, (_STRIP, tq), 1)
        for r in range(r_n):
            strip_ref[:, r * tq:(r + 1) * tq] = _bias_chain(a - u + top, thr_ref, tab_ref, g * r_n + r) * LOG2E

    q_t = jnp.concatenate([qt_ref[r] for r in range(r_n)], axis=1)
    pad_rows = jnp.zeros((LANES - HEAD_DIM - _BLOCKS_PER_TILE, lanes), F32)
    qpos = i * tq + (lax.broadcasted_iota(jnp.int32, (1, lanes), 1) & (tq - 1))

    ck = CHUNK_SEL

    def score_chunk(kb, c, maybe_diagonal):
        off = pl.multiple_of(kb * tk + c * ck, ck)
        boff = pl.multiple_of(kb * _BLOCKS_PER_TILE, _BLOCKS_PER_TILE)
        picked = selt_ref[0, pl.ds(boff, _BLOCKS_PER_TILE), :]
        penalty = jnp.where(picked > 0.5, 0.0, NEG_INF)
        penalty = jnp.concatenate([penalty] * r_n, axis=1)
        q_aug = jnp.concatenate([q_t, jnp.concatenate([penalty, pad_rows], axis=0).astype(BF16)], axis=0)
        soff = pl.multiple_of(jnp.clip(top - (i * tq - kb * tk), 0, top) + c * ck, ck)
        s = _dot(ks_ref[0, pl.ds(off, ck), :], q_aug) + strip_ref[pl.ds(soff, ck), :]
        if maybe_diagonal:
            key = off + lax.broadcasted_iota(jnp.int32, (ck, lanes), 0)
            s = jnp.where(key <= qpos, s, NEG_INF)
        return s

    def value_chunk(kb, c):
        off = pl.multiple_of(kb * tk + c * ck, ck)
        return _with_ones_rows(vst_ref[0, :, pl.ds(off, ck)])

    woff = pl.multiple_of(i * tq, tq)
    w0 = top - WINDOW
    sw = _dot(kw_ref[0, pl.ds(woff, _WIN_KEYS), :], q_t) + strip_ref[w0:w0 + _WIN_KEYS, :]
    kpos = i * tq - WINDOW + lax.broadcasted_iota(jnp.int32, (_WIN_KEYS, lanes), 0)
    rel = qpos - kpos
    sw = jnp.where((rel >= 0) & (rel < WINDOW) & (kpos >= 0), sw, NEG_INF)
    pw = jnp.exp2(sw - jnp.max(sw, axis=0, keepdims=True)).astype(BF16)
    o_win = _dot(_with_ones_rows(vwt_ref[0, :, pl.ds(woff, _WIN_KEYS)]), pw)
    o_win = o_win[:HEAD_DIM] / o_win[HEAD_DIM:HEAD_DIM + 1]
    gates = gt_ref[0]
    for r in range(r_n):
        c0 = NSA_BRANCHES * r
        o_ref[r] = oc_ref[r] + gates[c0 + 2:c0 + 3, :] * o_win[:, r * tq:(r + 1) * tq]

    st = (jnp.full((1, lanes), NEG_INF, F32), jnp.zeros((HEAD_DIM + ONES_ROWS, lanes), F32))
    _, acc = _pipelined_blocks((i * tq) // tk, score_chunk, value_chunk, tk // ck, ck, jnp.exp2,
                               bufs, st)
    o_sel = acc[:HEAD_DIM] / acc[HEAD_DIM:HEAD_DIM + 1]
    for r in range(r_n):
        c0 = NSA_BRANCHES * r
        o_ref[r] += gates[c0 + 1:c0 + 2, :] * o_sel[:, r * tq:(r + 1) * tq]


def _nsa_selected(thr, tab, nq_t, ks, vs_t, kw_pad, vw_t_pad, sel_t, gates_t, oc):
    nh, _, s = nq_t.shape
    ng, tq = NSA_GROUPS, TQ_SEL
    nb = s // SEL_BLOCK
    qspec = pl.BlockSpec((NSA_HPG, HEAD_DIM, tq), lambda g, i: (g, 0, i))
    rows = lambda n: pl.BlockSpec((1, n, HEAD_DIM), lambda g, i: (g, 0, 0), pipeline_mode=pl.Buffered(1))
    cols = lambda n: pl.BlockSpec((1, HEAD_DIM, n), lambda g, i: (g, 0, 0), pipeline_mode=pl.Buffered(1))
    return pl.pallas_call(
        _nsa_sel_kernel, grid=(ng, s // tq),
        in_specs=[_smem_spec(), _smem_spec(), qspec,
                  pl.BlockSpec((1, s, LANES), lambda g, i: (g, 0, 0), pipeline_mode=pl.Buffered(1)),
                  cols(s), rows(s + WINDOW), cols(s + WINDOW),
                  pl.BlockSpec((1, nb, tq), lambda g, i: (g, 0, i)),
                  pl.BlockSpec((1, NSA_HPG * NSA_BRANCHES, tq), lambda g, i: (g, 0, i)), qspec],
        out_specs=qspec,
        out_shape=jax.ShapeDtypeStruct((nh, HEAD_DIM, s), F32),
        scratch_shapes=[pltpu.VMEM((_STRIP, NSA_HPG * tq), F32)] + [pltpu.VMEM((TK_SEL, NSA_HPG * tq), F32)] * 3
        + [pltpu.VMEM((8, NSA_HPG * tq), F32), pltpu.VMEM((HEAD_DIM + ONES_ROWS, NSA_HPG * tq), F32)],
        compiler_params=_cparams("arbitrary", "arbitrary"), name="nsa_selected",
    )(thr, tab, nq_t, ks, vs_t, kw_pad, vw_t_pad, sel_t, gates_t, oc)


def _merge_body(osb_ref, onsa_ref, omla_ref, g_ref, x_ref, wsb_ref, wnsa_ref, wmla_ref, wout_ref, fg_ref):
    def heads_proj(o_ref, w_ref):
        nh, hd = w_ref.shape[0], w_ref.shape[1]
        if o_ref.shape[1] == hd:
            o = o_ref[...].reshape(nh * hd, o_ref.shape[2]).T.astype(BF16)
            return _dot(o, w_ref[...].reshape(nh * hd, w_ref.shape[2]))
        y = _dot(o_ref[0], w_ref[0])
        for j in range(1, nh):
            y = y + _dot(o_ref[j], w_ref[j])
        return y

    merged = g_ref[:, 0:D_MODEL] * heads_proj(osb_ref, wsb_ref)
    merged = merged + g_ref[:, D_MODEL:2 * D_MODEL] * heads_proj(onsa_ref, wnsa_ref)
    merged = merged + g_ref[:, 2 * D_MODEL:3 * D_MODEL] * heads_proj(omla_ref, wmla_ref)
    x1 = x_ref[...] + _dot(merged.astype(BF16), wout_ref[...])
    ms = jnp.mean(x1 * x1, axis=-1, keepdims=True)
    return x1, x1 * lax.rsqrt(ms + EPS) * fg_ref[...]


def _merge_kernel(osb_ref, onsa_ref, omla_ref, g_ref, x_ref, wsb_ref, wnsa_ref, wmla_ref, wout_ref, fg_ref,
                  x1_ref, h_ref):
    x1, h = _merge_body(osb_ref, onsa_ref, omla_ref, g_ref, x_ref, wsb_ref, wnsa_ref, wmla_ref, wout_ref,
                        fg_ref)
    x1_ref[...] = x1
    h_ref[...] = h.astype(BF16)


def _merge_router_kernel(osb_ref, onsa_ref, omla_ref, g_ref, x_ref, wsb_ref, wnsa_ref, wmla_ref, wout_ref,
                         fg_ref, wr_ref, x1_ref, h_ref, comb_ref, sel_ref):
    x1, h = _merge_body(osb_ref, onsa_ref, omla_ref, g_ref, x_ref, wsb_ref, wnsa_ref, wmla_ref, wout_ref,
                        fg_ref)
    x1_ref[...] = x1
    h_ref[...] = h.astype(BF16)
    wr = wr_ref[...]
    hp = _split3(h)
    wp = _split3(wr)
    logits = jnp.zeros((h.shape[0], wr.shape[1]), F32)
    for a_i, b_i in ((2, 0), (0, 2), (1, 1), (1, 0), (0, 1), (0, 0)):
        logits = logits + _dot(hp[a_i], wp[b_i])
    lane = lax.broadcasted_iota(jnp.int32, logits.shape, 1)
    nl = logits.shape[1]
    logits = jnp.where(lane < N_EXPERTS, logits, -jnp.inf)
    v1 = jnp.max(logits, axis=-1, keepdims=True)
    i1 = jnp.min(jnp.where(logits == v1, lane, nl), axis=-1, keepdims=True)
    rest = jnp.where(lane == i1, -jnp.inf, logits)
    v2 = jnp.max(rest, axis=-1, keepdims=True)
    i2 = jnp.min(jnp.where(rest == v2, lane, nl), axis=-1, keepdims=True)
    e2 = jnp.exp(v2 - v1)
    w1 = 1.0 / (1.0 + e2)
    w2 = e2 / (1.0 + e2)
    comb_ref[...] = jnp.where(lane == i1, w1, jnp.where(lane == i2, w2, 0.0))
    sel_ref[...] = jnp.where((lane == i1) | (lane == i2), 1.0, 0.0)


def _merge(osb, onsa, omla, g, x, wsb, wnsa, wmla, wout, fgain, wrouter=None):
    s = x.shape[0]
    tm = TM_MERGE if wrouter is None else TM_MERGE_ROUTER

    def heads(a):
        if a.shape[1] == s:
            return pl.BlockSpec((a.shape[0], tm, a.shape[2]), lambda i: (0, i, 0))
        return pl.BlockSpec((a.shape[0], a.shape[1], tm), lambda i: (0, 0, i))

    row = lambda w: pl.BlockSpec((tm, w), lambda i: (i, 0))
    in_specs = [heads(osb), heads(onsa), heads(omla), row(GATE_W), row(D_MODEL)]
    in_specs += [_const_spec(a.shape) for a in (wsb, wnsa, wmla, wout, fgain)]
    out_shape = [jax.ShapeDtypeStruct((s, D_MODEL), F32), jax.ShapeDtypeStruct((s, D_MODEL), BF16)]
    out_specs = [row(D_MODEL), row(D_MODEL)]
    args = [osb, onsa, omla, g, x, wsb, wnsa, wmla, wout, fgain]
    if wrouter is None:
        kern = _merge_kernel
    else:
        kern = _merge_router_kernel
        in_specs.append(_const_spec(wrouter.shape))
        args.append(wrouter)
        out_shape += [jax.ShapeDtypeStruct((s, LANES), F32)] * 2
        out_specs += [row(LANES)] * 2
    return pl.pallas_call(
        kern, grid=(s // tm,), in_specs=in_specs, out_specs=tuple(out_specs), out_shape=tuple(out_shape),
        compiler_params=_cparams("parallel"), name="merge",
    )(*args)


def _ffn_kernel(x_ref, h_ref, wg_ref, wu_ref, wd_ref, o_ref, acc_ref):
    f = pl.program_id(1)

    @pl.when(f == 0)
    def _():
        acc_ref[...] = x_ref[...]

    h = h_ref[...]
    gate = _dot(h, wg_ref[...])
    up = _dot(h, wu_ref[...])
    act = (gate * jax.nn.sigmoid(gate) * up).astype(BF16)
    acc_ref[...] += _dot(act, wd_ref[...])

    @pl.when(f == pl.num_programs(1) - 1)
    def _():
        o_ref[...] = acc_ref[...]


def _dense_ffn(x1, h, wg, wu, wd, tf):
    s = x1.shape[0]
    tm = TM_FFN
    nf = wg.shape[1] // tf
    return pl.pallas_call(
        _ffn_kernel, grid=(s // tm, nf),
        in_specs=[pl.BlockSpec((tm, D_MODEL), lambda i, f: (i, 0)),
                  pl.BlockSpec((tm, D_MODEL), lambda i, f: (i, 0)),
                  pl.BlockSpec((D_MODEL, tf), lambda i, f: (0, f)),
                  pl.BlockSpec((D_MODEL, tf), lambda i, f: (0, f)),
                  pl.BlockSpec((tf, D_MODEL), lambda i, f: (f, 0))],
        out_specs=pl.BlockSpec((tm, D_MODEL), lambda i, f: (i, 0)),
        out_shape=jax.ShapeDtypeStruct((s, D_MODEL), F32),
        scratch_shapes=[pltpu.VMEM((tm, D_MODEL), F32)],
        compiler_params=_cparams("parallel", "arbitrary"), name="dense_ffn",
    )(x1, h, wg, wu, wd)


def _permute_kernel(tile_ref, chunk_ref, flag_ref, exp_ref, dest_ref, cw_ref, h_ref, xg_ref, rw_ref, acc_ref,
                    wacc_ref):
    k = pl.program_id(0)
    tm = TM_MOE
    flags = flag_ref[k]

    @pl.when((flags & 1) != 0)
    def _():
        acc_ref[...] = jnp.zeros_like(acc_ref)
        wacc_ref[...] = jnp.zeros_like(wacc_ref)

    @pl.when((flags & 4) != 0)
    def _():
        row = tile_ref[k] * tm + lax.broadcasted_iota(jnp.int32, (tm, tm), 0)
        hit = dest_ref[0] == row
        acc_ref[...] += _dot(jnp.where(hit, 1.0, 0.0).astype(BF16), h_ref[...])
        wacc_ref[...] += jnp.sum(jnp.where(hit, cw_ref[0], 0.0), axis=-1, keepdims=True)

    @pl.when((flags & 2) != 0)
    def _():
        xg_ref[...] = acc_ref[...].astype(BF16)
        rw_ref[...] = wacc_ref[...]


def _moe_ffn_kernel(te_ref, tv_ref, xg_ref, rw_ref, wg_ref, wu_ref, wd_ref, y_ref, acc_ref):
    t = pl.program_id(0)
    f = pl.program_id(1)

    @pl.when(tv_ref[t] != 0)
    def _():
        @pl.when(f == 0)
        def _():
            acc_ref[...] = jnp.zeros_like(acc_ref)

        x = xg_ref[...]
        gate = _dot(x, wg_ref[0])
        up = _dot(x, wu_ref[0])
        act = (gate * jax.nn.sigmoid(gate) * up * rw_ref[...]).astype(BF16)
        acc_ref[...] += _dot(act, wd_ref[0])

        @pl.when(f == pl.num_programs(1) - 1)
        def _():
            y_ref[...] = acc_ref[...].astype(BF16)

    @pl.when(tv_ref[t] == 0)
    def _():
        y_ref[...] = jnp.zeros_like(y_ref)


def _unpermute_kernel(tile_ref, chunk_ref, flag_ref, grow_ref, x_ref, y_ref, o_ref):
    k = pl.program_id(0)
    tm = TM_MOE
    flags = flag_ref[k]

    @pl.when((flags & 1) != 0)
    def _():
        o_ref[...] = x_ref[...]

    @pl.when((flags & 4) != 0)
    def _():
        tu = y_ref.shape[0]
        row = chunk_ref[k] * tu + lax.broadcasted_iota(jnp.int32, (tm, tu), 1)
        hit = (grow_ref[:, 0:1] == row) | (grow_ref[:, 1:2] == row)
        o_ref[...] += _dot(jnp.where(hit, 1.0, 0.0).astype(BF16), y_ref[...])


def _work_list(first, last, n_items, min_one=True):
    n_tiles = first.shape[0]
    cnt_real = jnp.maximum(last - first + 1, 0)
    cnt = jnp.maximum(cnt_real, 1) if min_one else cnt_real
    ends = jnp.cumsum(cnt)
    starts = ends - cnt
    total = ends[-1]
    k = jnp.arange(n_items, dtype=jnp.int32)
    kk = jnp.minimum(k, total - 1)
    tile = jnp.minimum(jnp.sum(kk[:, None] >= ends[None, :], axis=1), n_tiles - 1).astype(jnp.int32)
    pos = kk - starts[tile]
    chunk = jnp.where(cnt_real[tile] > 0, first[tile] + pos, 0).astype(jnp.int32)
    live = k < total
    flags = (jnp.where(live & (pos == 0), 1, 0) | jnp.where(live & (pos == cnt[tile] - 1), 2, 0)
             | jnp.where(live & (pos < cnt_real[tile]), 4, 0))
    return tile, chunk, flags.astype(jnp.int32)


def _moe_ffn(x1, h, comb, selm, wg, wu, wd):
    s = x1.shape[0]
    tm, tf = TM_MOE, TF_MOE
    ne = N_EXPERTS
    n_chunks = s // tm
    n_tiles = 2 * s // tm + ne
    rows = n_tiles * tm
    sel_t = selm[:, :ne].T.astype(jnp.int32)
    cum = jnp.cumsum(sel_t, axis=1)
    counts = cum[:, -1]
    tiles_e = (counts + tm - 1) // tm
    tile_end = jnp.cumsum(tiles_e)
    tile_start = tile_end - tiles_e
    used = tile_end[-1]
    dest = jnp.where(sel_t > 0, tile_start[:, None] * tm + cum - 1, -1).astype(jnp.int32)
    t_idx = jnp.arange(n_tiles, dtype=jnp.int32)
    te_raw = jnp.minimum(jnp.sum(t_idx[:, None] >= tile_end[None, :], axis=1), ne - 1).astype(jnp.int32)
    tvalid = (t_idx < used).astype(jnp.int32)
    last_e = te_raw[jnp.maximum(used - 1, 0)]
    te = jnp.where(tvalid > 0, te_raw, last_e).astype(jnp.int32)
    dest_tiles = dest.reshape(ne, n_chunks, tm)
    big = jnp.int32(2 ** 30)
    d_max = jnp.max(dest_tiles, axis=2)
    d_min = jnp.min(jnp.where(dest_tiles >= 0, dest_tiles, big), axis=2)
    overlap = (d_max[te] >= t_idx[:, None] * tm) & (d_min[te] < (t_idx[:, None] + 1) * tm) & (tvalid[:, None] > 0)
    c_idx = jnp.arange(n_chunks, dtype=jnp.int32)[None, :]
    c_lo = jnp.min(jnp.where(overlap, c_idx, n_chunks), axis=1)
    c_hi = jnp.max(jnp.where(overlap, c_idx, -1), axis=1)
    c_lo = jnp.where(c_hi >= 0, c_lo, 0).astype(jnp.int32)
    p_tile, p_chunk, p_flag = _work_list(c_lo, c_hi.astype(jnp.int32), n_tiles + ne * n_chunks)
    p_exp = te[p_tile]

    cw_t = comb[:, :ne].T.reshape(ne, 1, s)
    dest3 = dest.reshape(ne, 1, s)
    xg, roww = pl.pallas_call(
        _permute_kernel,
        grid_spec=pltpu.PrefetchScalarGridSpec(
            num_scalar_prefetch=4, grid=(p_tile.shape[0],),
            in_specs=[pl.BlockSpec((1, 1, tm), lambda k, pt, pc, pf, pe: (pe[k], 0, pc[k])),
                      pl.BlockSpec((1, 1, tm), lambda k, pt, pc, pf, pe: (pe[k], 0, pc[k])),
                      pl.BlockSpec((tm, D_MODEL), lambda k, pt, pc, pf, pe: (pc[k], 0))],
            out_specs=(pl.BlockSpec((tm, D_MODEL), lambda k, pt, pc, pf, pe: (pt[k], 0)),
                       pl.BlockSpec((tm, 1), lambda k, pt, pc, pf, pe: (pt[k], 0))),
            scratch_shapes=[pltpu.VMEM((tm, D_MODEL), F32), pltpu.VMEM((tm, 1), F32)]),
        out_shape=(jax.ShapeDtypeStruct((rows, D_MODEL), BF16), jax.ShapeDtypeStruct((rows, 1), F32)),
        compiler_params=_cparams("arbitrary"), name="moe_permute",
    )(p_tile, p_chunk, p_flag, p_exp, dest3, cw_t, h)

    nf = wg.shape[2] // tf
    last_f = nf - 1
    fsel = lambda t, f, tv: f * tv[t] + last_f * (1 - tv[t])
    yg = pl.pallas_call(
        _moe_ffn_kernel,
        grid_spec=pltpu.PrefetchScalarGridSpec(
            num_scalar_prefetch=2, grid=(n_tiles, nf),
            in_specs=[pl.BlockSpec((tm, D_MODEL), lambda t, f, te_, tv: (t, 0)),
                      pl.BlockSpec((tm, 1), lambda t, f, te_, tv: (t, 0)),
                      pl.BlockSpec((1, D_MODEL, tf), lambda t, f, te_, tv: (te_[t], 0, fsel(t, f, tv))),
                      pl.BlockSpec((1, D_MODEL, tf), lambda t, f, te_, tv: (te_[t], 0, fsel(t, f, tv))),
                      pl.BlockSpec((1, tf, D_MODEL), lambda t, f, te_, tv: (te_[t], fsel(t, f, tv), 0))],
            out_specs=pl.BlockSpec((tm, D_MODEL), lambda t, f, te_, tv: (t, 0)),
            scratch_shapes=[pltpu.VMEM((tm, D_MODEL), F32)]),
        out_shape=jax.ShapeDtypeStruct((rows, D_MODEL), BF16),
        compiler_params=_cparams("arbitrary", "arbitrary"), name="moe_ffn",
    )(te, tvalid, xg, roww, wg, wu, wd)

    grow = jnp.stack([jnp.min(jnp.where(dest >= 0, dest, big), axis=0), jnp.max(dest, axis=0)], axis=1)
    has = d_max >= 0
    tu = TU_MOE
    first = jnp.where(has, d_min // tu, 0).T.reshape(-1).astype(jnp.int32)
    last = jnp.where(has, d_max // tu, -1).T.reshape(-1).astype(jnp.int32)
    u_sub, u_chunk, u_flag = _work_list(first, last, rows // tu + ne * n_chunks, min_one=False)
    u_tile = (u_sub // ne).astype(jnp.int32)
    prev_tile = jnp.concatenate([jnp.full((1,), -1, jnp.int32), u_tile[:-1]])
    u_flag = (u_flag & 4) | jnp.where((u_flag != 0) & (u_tile != prev_tile), 1, 0)
    out = pl.pallas_call(
        _unpermute_kernel,
        grid_spec=pltpu.PrefetchScalarGridSpec(
            num_scalar_prefetch=3, grid=(u_tile.shape[0],),
            in_specs=[pl.BlockSpec((tm, 2), lambda k, ut, uc, uf: (ut[k], 0)),
                      pl.BlockSpec((tm, D_MODEL), lambda k, ut, uc, uf: (ut[k], 0)),
                      pl.BlockSpec((tu, D_MODEL), lambda k, ut, uc, uf: (uc[k], 0))],
            out_specs=pl.BlockSpec((tm, D_MODEL), lambda k, ut, uc, uf: (ut[k], 0))),
        out_shape=jax.ShapeDtypeStruct((s, D_MODEL), F32),
        compiler_params=_cparams("arbitrary"), name="moe_unpermute",
    )(u_tile, u_chunk, u_flag, grow, x1, yg)
    return out


def _t5_bucket(rel):
    n = jnp.maximum(rel, 0)
    max_exact = NUM_BUCKETS // 2
    nf = jnp.maximum(n, 1).astype(jnp.float32)
    large = max_exact + (jnp.log(nf / max_exact) / math.log(MAX_DISTANCE / max_exact)
                         * (NUM_BUCKETS - max_exact)).astype(jnp.int32)
    large = jnp.minimum(large, NUM_BUCKETS - 1)
    return jnp.where(n < max_exact, n, large)


def _bucket_thresholds():
    buckets = _t5_bucket(jnp.arange(MAX_DISTANCE + 1, dtype=jnp.int32))
    b = jnp.arange(NUM_BUCKETS, dtype=jnp.int32)
    return jnp.sum(buckets[None, :] < b[:, None], axis=1).astype(jnp.int32)


def _rope_tables(s):
    half = MLA_ROPE // 2
    inv = ROPE_THETA ** (-jnp.arange(half, dtype=jnp.float32) / half)
    ang = jnp.arange(s).astype(jnp.float32)[:, None] * inv[None, :]
    cos, sin = jnp.cos(ang), jnp.sin(ang)
    z = lambda w: jnp.zeros((s, w), F32)
    pad = MLA_PAD - MLA_QK
    cos_t = jnp.concatenate([jnp.ones((s, MLA_NOPE), F32), cos, cos, z(pad)], axis=1)
    sin_lo = jnp.concatenate([z(MLA_NOPE), -sin, z(half), z(pad)], axis=1)
    sin_hi = jnp.concatenate([z(MLA_NOPE), z(half), sin, z(pad)], axis=1)
    return cos_t, sin_lo, sin_hi


def _pad_lanes(a, width):
    return jnp.pad(a, [(0, 0)] * (a.ndim - 1) + [(0, width - a.shape[-1])])


def kernel(x, rel_bias_table, mix_norm, w_in, b_gate, sb_w_o, nsa_q_norm, nsa_k_norm, nsa_cmp_pos, nsa_cmp_w1, nsa_cmp_w2, nsa_w_o, mla_q_a_norm, mla_kv_a_norm, mla_w_uq, mla_w_ukv, mla_q_norm, mla_k_norm, mla_w_o, w_out, ffn_norm, dense_w_gate, dense_w_up, dense_w_down, moe_router, moe_w_gate, moe_w_up, moe_w_down):
    b, s, d = x.shape
    assert b == 1 and d == D_MODEL and s % TM_MOE == 0 and s // SEL_BLOCK >= SEL_TOPK
    depth = w_in.shape[0]
    xs = x.reshape(s, d)
    thr = _bucket_thresholds()
    cos_t, sin_lo, sin_hi = _rope_tables(s)
    n_chunk = s // CMP_STRIDE

    for layer in range(depth):
        w = w_in[layer]
        small = jnp.concatenate([
            w[:, _C_NG:_C_MCQ], jnp.zeros((d, _KR_LANE - NSA_HEADS * NSA_BRANCHES), F32),
            w[:, _C_MKR:_C_GATE], jnp.zeros((d, LANES - _KR_LANE - MLA_ROPE), F32)], axis=1)
        wa = jnp.concatenate([w[:, :_C_NG], w[:, _C_MCQ:_C_MKR], small], axis=1).astype(BF16)
        wg = w[:, _C_GATE:].astype(BF16)
        wuq = _pad_lanes(mla_w_uq[layer].reshape(MLA_Q_LORA, MLA_HEADS, MLA_QK), MLA_PAD)
        wuq = wuq.reshape(MLA_Q_LORA, MLA_HEADS * MLA_PAD).astype(BF16)
        wukv = mla_w_ukv[layer].reshape(MLA_KV_LORA, MLA_HEADS, MLA_NOPE + MLA_V)
        wukv = jnp.concatenate([
            _pad_lanes(wukv[:, :, :MLA_NOPE], MLA_PAD).reshape(MLA_KV_LORA, MLA_HEADS * MLA_PAD),
            wukv[:, :, MLA_NOPE:].reshape(MLA_KV_LORA, MLA_HEADS * MLA_V)], axis=1).astype(BF16)
        (sbq, sbk, sbv, nq, nkvc, nks, nvs, nkw, nvw, gsm, mq, mk, mv, g) = _input_projection(
            xs, mix_norm[layer][None], wa, wg, b_gate[layer][None], wuq, wukv,
            nsa_q_norm[layer][None], nsa_k_norm[layer], mla_q_a_norm[layer][None],
            mla_kv_a_norm[layer][None], _pad_lanes(mla_q_norm[layer][None], MLA_PAD),
            _pad_lanes(mla_k_norm[layer][None], MLA_PAD), cos_t, sin_lo, sin_hi)

        o_sb = _stick_breaking(sbq, sbk, sbv)
        o_mla = _mla_attention(mq, mk, mv)

        x2 = nkvc.reshape(2, NSA_GROUPS, n_chunk, CMP_STRIDE * HEAD_DIM)
        cmp, cmp_t = _compress(x2, nsa_cmp_pos[layer].reshape(2, 1, CMP_LEN * HEAD_DIM),
                               nsa_cmp_w1[layer].astype(BF16), nsa_cmp_w2[layer].astype(BF16),
                               nsa_k_norm[layer][0:1])
        gates = gsm[:NSA_HEADS * NSA_BRANCHES].reshape(NSA_GROUPS, NSA_HPG * NSA_BRANCHES, s)
        oc, sel = _nsa_compressed(thr, rel_bias_table, nq, cmp, cmp_t, gates)
        o_nsa = _nsa_selected(thr, rel_bias_table, nq, nks, nvs,
                              jnp.pad(nkw, ((0, 0), (WINDOW, 0), (0, 0))),
                              jnp.pad(nvw, ((0, 0), (0, 0), (WINDOW, 0))), sel, gates, oc)

        wsb = sb_w_o[layer].reshape(SB_HEADS, HEAD_DIM, d).astype(BF16)
        wnsa = nsa_w_o[layer].reshape(NSA_HEADS, HEAD_DIM, d).astype(BF16)
        wmla = mla_w_o[layer].reshape(MLA_HEADS, MLA_V, d).astype(BF16)
        wout = w_out[layer].astype(BF16)
        j = layer // 2
        if layer % 2 == 0:
            x1, h = _merge(o_sb, o_nsa, o_mla, g, xs, wsb, wnsa, wmla, wout, ffn_norm[layer][None])
            d_ff = dense_w_gate.shape[2]
            xs = _dense_ffn(x1, h, dense_w_gate[j].astype(BF16), dense_w_up[j].astype(BF16),
                            dense_w_down[j].astype(BF16), d_ff // 2)
        else:
            x1, h, comb, selm = _merge(o_sb, o_nsa, o_mla, g, xs, wsb, wnsa, wmla, wout,
                                       ffn_norm[layer][None], _pad_lanes(moe_router[j], LANES))
            xs = _moe_ffn(x1, h, comb, selm, moe_w_gate[j].astype(BF16), moe_w_up[j].astype(BF16),
                          moe_w_down[j].astype(BF16))
    return xs.reshape(b, s, d)
```

```python
import functools
import math

import jax
import jax.numpy as jnp
from jax import lax
from jax.experimental import pallas as pl
from jax.experimental.pallas import tpu as pltpu

F32 = jnp.float32
BF16 = jnp.bfloat16

D_MODEL = 1024
HEAD_DIM = 64
SB_HEADS = 4
NSA_HEADS = 8
NSA_GROUPS = 2
NSA_HPG = NSA_HEADS // NSA_GROUPS
NSA_BRANCHES = 3
MLA_HEADS = 4
MLA_NOPE = 64
MLA_ROPE = 32
MLA_V = 64
MLA_QK = MLA_NOPE + MLA_ROPE
MLA_Q_LORA = 256
MLA_KV_LORA = 128
ROPE_THETA = 10000.0
CMP_LEN = 32
CMP_STRIDE = 16
CMP_HIDDEN = 256
SEL_BLOCK = 64
SEL_TOPK = 16
N_LOCAL_BLOCKS = 2
WINDOW = 512
NUM_BUCKETS = 32
MAX_DISTANCE = 1024
N_MIXERS = 3
N_EXPERTS = 8
EPS = 1e-6
NEG_INF = -1e30
FORCE = 1e30
LOG2E = math.log2(math.e)

SB_W = SB_HEADS * HEAD_DIM
NSA_QW = NSA_HEADS * HEAD_DIM
NSA_KVW = NSA_GROUPS * HEAD_DIM
GATE_W = N_MIXERS * D_MODEL
LANES = 128
MLA_PAD = LANES

_C_NQ = 3 * SB_W
_C_NKV = _C_NQ + NSA_QW
_C_NG = _C_NKV + 6 * NSA_KVW
_C_MCQ = _C_NG + NSA_HEADS * NSA_BRANCHES
_C_MCKV = _C_MCQ + MLA_Q_LORA
_C_MKR = _C_MCKV + MLA_KV_LORA
_C_GATE = _C_MKR + MLA_ROPE
_A_SB = 0
_A_NQ = 3 * SB_W
_A_NKV = _A_NQ + NSA_QW
_A_MCQ = _A_NKV + 6 * NSA_KVW
_A_MCKV = _A_MCQ + MLA_Q_LORA
_A_SMALL = _A_MCKV + MLA_KV_LORA
_A_COLS = _A_SMALL + LANES
_KR_LANE = MLA_NOPE

TM_PROJ = 256
TQ_SB = 512
TK_SB = 256
TQ_MLA = 256
TK_MLA = 1024
CHUNK_MLA = 256
CHUNK_SEL = 512
TQ_CMP = 128
CMP_PARTS = 4
TQ_SEL = 256
TK_SEL = 512
TM_MERGE = 512
TM_MERGE_ROUTER = 256
TM_FFN = 512
TM_MOE = 512
TF_MOE = 512
TU_MOE = 256
VMEM_LIMIT = 56 * 1024 * 1024
SB_EXIT = -104.0

_NT = (((1,), (1,)), ((), ()))


def _dot(a, b):
    return jnp.dot(a, b, preferred_element_type=F32)


def _dot_nt(a, b):
    return lax.dot_general(a, b, _NT, preferred_element_type=F32)


def _cparams(*sem):
    return pltpu.CompilerParams(dimension_semantics=sem, vmem_limit_bytes=VMEM_LIMIT)


def _const_spec(shape):
    nd = len(shape)
    return pl.BlockSpec(shape, lambda *_: (0,) * nd, pipeline_mode=pl.Buffered(1))


def _smem_spec():
    return pl.BlockSpec(memory_space=pltpu.SMEM)


def _split3(x):
    h1 = x.astype(BF16)
    r1 = x - h1.astype(F32)
    h2 = r1.astype(BF16)
    h3 = (r1 - h2.astype(F32)).astype(BF16)
    return h1, h2, h3


def _bias_chain(rel, thr_ref, tab_ref, head):
    out = jnp.full(rel.shape, tab_ref[0, head], F32)
    for b in range(1, NUM_BUCKETS):
        out = jnp.where(rel >= thr_ref[b], tab_ref[b, head], out)
    return out


def _rope(y, cos, sin_lo, sin_hi):
    half = MLA_ROPE // 2
    return y * cos + pltpu.roll(y, half, 1) * sin_hi + pltpu.roll(y, MLA_PAD - half, 1) * sin_lo


def _proj_kernel(x_ref, gain_ref, wa_ref, wg_ref, bg_ref, wuq_ref, wukv_ref, nqg_ref, nkg_ref,
                 qag_ref, kvag_ref, mqg_ref, mkg_ref, cos_ref, slo_ref, shi_ref,
                 sbq_ref, sbk_ref, sbv_ref, nq_ref, nkvc_ref, nks_ref, nvs_ref, nkw_ref, nvw_ref,
                 gsm_ref, mq_ref, mk_ref, mv_ref, g_ref):
    x = x_ref[...]
    ms = jnp.mean(x * x, axis=-1, keepdims=True)
    h = (x * lax.rsqrt(ms + EPS) * gain_ref[...]).astype(BF16)

    def rms(p, gain, width):
        m = jnp.sum(p * p, axis=-1, keepdims=True) * (1.0 / width)
        return p * lax.rsqrt(m + EPS) * gain

    p = _dot(h, wa_ref[:, _A_SB:_A_SB + 3 * SB_W])
    sbq_t = (p[:, :SB_W] * HEAD_DIM ** -0.5).T.astype(BF16)
    sbv_t = p[:, 2 * SB_W:3 * SB_W].T.astype(BF16)
    for j in range(SB_HEADS):
        sbq_ref[j] = sbq_t[j * HEAD_DIM:(j + 1) * HEAD_DIM]
        sbv_ref[j] = sbv_t[j * HEAD_DIM:(j + 1) * HEAD_DIM]
    for j in range(SB_HEADS // 2):
        sbk_ref[j] = p[:, SB_W + j * LANES:SB_W + (j + 1) * LANES].astype(BF16)

    p = _dot(h, wa_ref[:, _A_NQ:_A_NQ + NSA_QW])
    qn = [rms(p[:, j * HEAD_DIM:(j + 1) * HEAD_DIM], nqg_ref[...], HEAD_DIM) * (HEAD_DIM ** -0.5 * LOG2E)
          for j in range(NSA_HEADS)]
    qn_t = jnp.concatenate(qn, axis=1).T.astype(BF16)
    for j in range(NSA_HEADS):
        nq_ref[j] = qn_t[j * HEAD_DIM:(j + 1) * HEAD_DIM]

    p = _dot(h, wa_ref[:, _A_NKV:_A_NKV + 6 * NSA_KVW])
    vs_t = p[:, 3 * NSA_KVW:4 * NSA_KVW].T.astype(BF16)
    tag_shape = (x.shape[0], LANES - HEAD_DIM)
    block_id = (pl.program_id(0) * x.shape[0] + lax.broadcasted_iota(jnp.int32, tag_shape, 0)) // SEL_BLOCK
    block_tag = jnp.where(lax.broadcasted_iota(jnp.int32, tag_shape, 1) == block_id % (TK_SEL // SEL_BLOCK),
                          1.0, 0.0)
    vw_t = p[:, 5 * NSA_KVW:6 * NSA_KVW].T.astype(BF16)
    for j in range(NSA_GROUPS):
        def seg(s):
            return p[:, s * NSA_KVW + j * HEAD_DIM:s * NSA_KVW + (j + 1) * HEAD_DIM]
        nkvc_ref[0, j] = seg(0)
        nkvc_ref[1, j] = seg(1)
        nks_ref[j] = jnp.concatenate([rms(seg(2), nkg_ref[1:2, :], HEAD_DIM), block_tag], axis=1).astype(BF16)
        nvs_ref[j] = vs_t[j * HEAD_DIM:(j + 1) * HEAD_DIM]
        nkw_ref[j] = rms(seg(4), nkg_ref[2:3, :], HEAD_DIM).astype(BF16)
        nvw_ref[j] = vw_t[j * HEAD_DIM:(j + 1) * HEAD_DIM]

    p = _dot(h, wa_ref[:, _A_MCQ:_A_COLS])
    cq = rms(p[:, :MLA_Q_LORA], qag_ref[...], MLA_Q_LORA).astype(BF16)
    ckv = rms(p[:, MLA_Q_LORA:MLA_Q_LORA + MLA_KV_LORA], kvag_ref[...], MLA_KV_LORA).astype(BF16)
    small = p[:, MLA_Q_LORA + MLA_KV_LORA:]
    gsm_ref[...] = jax.nn.sigmoid(small).T
    lane = lax.broadcasted_iota(jnp.int32, small.shape, 1)
    krope = jnp.where((lane >= _KR_LANE) & (lane < _KR_LANE + MLA_ROPE), small, 0.0)
    cos, slo, shi = cos_ref[...], slo_ref[...], shi_ref[...]
    qu = _dot(cq, wuq_ref[...])
    kvu = _dot(ckv, wukv_ref[...])
    for j in range(MLA_HEADS):
        qh = rms(qu[:, j * MLA_PAD:(j + 1) * MLA_PAD], mqg_ref[...], MLA_QK)
        mq_ref[j] = (_rope(qh, cos, slo, shi) * MLA_QK ** -0.5).T.astype(BF16)
        kh = rms(kvu[:, j * MLA_PAD:(j + 1) * MLA_PAD] + krope, mkg_ref[...], MLA_QK)
        mk_ref[j] = _rope(kh, cos, slo, shi).astype(BF16)
    v_t = kvu[:, MLA_HEADS * MLA_PAD:].T.astype(BF16)
    for j in range(MLA_HEADS):
        mv_ref[j] = v_t[j * MLA_V:(j + 1) * MLA_V]

    g_ref[...] = jax.nn.sigmoid(_dot(h, wg_ref[...]) + bg_ref[...])


def _input_projection(x, gain, wa, wg, bg, wuq, wukv, nqg, nkg, qag, kvag, mqg, mkg, cos, slo, shi):
    s = x.shape[0]
    tm = TM_PROJ
    row = lambda w: pl.BlockSpec((tm, w), lambda i: (i, 0))
    heads = lambda n, w: pl.BlockSpec((n, tm, w), lambda i: (0, i, 0))
    hs = lambda n, w, dt: jax.ShapeDtypeStruct((n, s, w), dt)
    in_specs = [row(D_MODEL)] + [_const_spec(a.shape) for a in
                                 (gain, wa, wg, bg, wuq, wukv, nqg, nkg, qag, kvag, mqg, mkg)]
    in_specs += [row(MLA_PAD)] * 3
    hst = lambda n, w, dt: jax.ShapeDtypeStruct((n, w, s), dt)
    heads_t = lambda n, w: pl.BlockSpec((n, w, tm), lambda i: (0, 0, i))
    ng, hd = NSA_GROUPS, HEAD_DIM
    out_shape = (hst(SB_HEADS, hd, BF16), hs(SB_HEADS // 2, LANES, BF16), hst(SB_HEADS, hd, BF16))
    out_shape += (hst(NSA_HEADS, hd, BF16),)
    out_shape += (jax.ShapeDtypeStruct((2, ng, s, hd), F32),) + (hs(ng, LANES, BF16), hst(ng, hd, BF16), hs(ng, hd, BF16), hst(ng, hd, BF16))
    out_shape += (jax.ShapeDtypeStruct((LANES, s), F32),)
    out_shape += (hst(MLA_HEADS, MLA_PAD, BF16), hs(MLA_HEADS, MLA_PAD, BF16), hst(MLA_HEADS, MLA_V, BF16))
    out_shape += (jax.ShapeDtypeStruct((s, GATE_W), F32),)
    out_specs = (heads_t(SB_HEADS, hd), heads(SB_HEADS // 2, LANES), heads_t(SB_HEADS, hd))
    out_specs += (heads_t(NSA_HEADS, hd),)
    out_specs += (pl.BlockSpec((2, ng, tm, hd), lambda i: (0, 0, i, 0)),) + (heads(ng, LANES), heads_t(ng, hd), heads(ng, hd), heads_t(ng, hd))
    out_specs += (pl.BlockSpec((LANES, tm), lambda i: (0, i)),)
    out_specs += (heads_t(MLA_HEADS, MLA_PAD), heads(MLA_HEADS, MLA_PAD), heads_t(MLA_HEADS, MLA_V), row(GATE_W))
    return pl.pallas_call(
        _proj_kernel, grid=(s // tm,), in_specs=in_specs, out_specs=out_specs, out_shape=out_shape,
        compiler_params=_cparams("parallel"), name="input_projection",
    )(x, gain, wa, wg, bg, wuq, wukv, nqg, nkg, qag, kvag, mqg, mkg, cos, slo, shi)


def _sb_kernel(qt_ref, k2_ref, vt_ref, o_ref):
    tq, tk = TQ_SB, TK_SB
    nh = qt_ref.shape[0]
    i = pl.program_id(0)
    later = (lax.broadcasted_iota(jnp.int32, (tk, tk), 1)
             > lax.broadcasted_iota(jnp.int32, (tk, tk), 0)).astype(BF16)
    key = lax.broadcasted_iota(jnp.int32, (tk, tq), 0)
    qry = i * tq + lax.broadcasted_iota(jnp.int32, (tk, tq), 1)
    zeros = jnp.zeros((HEAD_DIM, tq), BF16)
    q_aug = [jnp.concatenate([qt_ref[h], zeros] if h % 2 == 0 else [zeros, qt_ref[h]], axis=0) for h in range(nh)]

    def tile(kt, h, carry, acc, diagonal):
        off = pl.multiple_of(kt * tk, tk)
        z = _dot(k2_ref[h // 2, pl.ds(off, tk), :], q_aug[h])
        log_stay = -(jnp.maximum(z, 0.0) + jnp.log(1.0 + jnp.exp(-jnp.abs(z))))
        if diagonal:
            causal = off + key < qry
            log_stay = jnp.where(causal, log_stay, 0.0)
        hi = log_stay.astype(BF16)
        lo = (log_stay - hi.astype(F32)).astype(BF16)
        suffix = _dot(later, hi) + _dot(later, lo)
        logw = z + log_stay + (carry + suffix)
        if diagonal:
            logw = jnp.where(causal, logw, NEG_INF)
        w = jnp.exp(logw).astype(BF16)
        acc = acc + _dot(vt_ref[h, :, pl.ds(off, tk)], w)
        carry = carry + suffix[0:1, :] + log_stay[0:1, :]
        return carry, acc

    def all_heads(kt, st, diagonal):
        out = [tile(kt, h, st[2 * h], st[2 * h + 1], diagonal) for h in range(nh)]
        return tuple(x for pair in out for x in pair)

    ratio = tq // tk
    st = (jnp.zeros((1, tq), F32), jnp.zeros((HEAD_DIM, tq), F32)) * nh
    for d in range(ratio):
        st = all_heads(ratio * i + ratio - 1 - d, st, True)

    def cond(c):
        top = c[1]
        for h in range(1, nh):
            top = jnp.maximum(top, c[1 + 2 * h])
        return jnp.logical_and(c[0] >= 0, jnp.max(top) > SB_EXIT)

    def body(c):
        return (c[0] - 1,) + all_heads(c[0], c[1:], False)

    c = lax.while_loop(cond, body, (ratio * i - 1,) + st)
    for h in range(nh):
        o_ref[h] = c[2 + 2 * h]


def _stick_breaking(q_t, k2, v_t):
    nh, _, s = q_t.shape
    tq = TQ_SB
    whole = lambda a: pl.BlockSpec(a.shape, lambda i: (0, 0, 0), pipeline_mode=pl.Buffered(1))
    return pl.pallas_call(
        _sb_kernel, grid=(s // tq,),
        in_specs=[pl.BlockSpec((nh, HEAD_DIM, tq), lambda i: (0, 0, i)), whole(k2), whole(v_t)],
        out_specs=pl.BlockSpec((nh, HEAD_DIM, tq), lambda i: (0, 0, i)),
        out_shape=jax.ShapeDtypeStruct((nh, HEAD_DIM, s), F32),
        compiler_params=_cparams("parallel"), name="stick_breaking",
    )(q_t, k2, v_t)


ONES_ROWS = 16


def _with_ones_rows(v_t):
    return jnp.concatenate([v_t, jnp.ones((ONES_ROWS, v_t.shape[1]), v_t.dtype)], axis=0)


def _pipelined_blocks(n_last, score_chunk, value_chunk, n_chunks, chunk, exp_fn, bufs, st):
    rows = lambda c: pl.ds(c * chunk, chunk)
    bufs, stat_ref, acc_ref = bufs[:3], bufs[3], bufs[4]
    m_row = 3
    stat_ref[m_row:m_row + 1, :] = st[0]
    acc_ref[...] = st[1]

    def prepare(kb, slot, maybe_diagonal):
        mx = None
        for c in range(n_chunks):
            s = score_chunk(kb, c, maybe_diagonal)
            bufs[slot][rows(c), :] = s
            smax = jnp.max(s, axis=0, keepdims=True)
            mx = smax if mx is None else jnp.maximum(mx, smax)
        stat_ref[slot:slot + 1, :] = mx

    def consume(kb, slot):
        m = stat_ref[m_row:m_row + 1, :]
        m_new = jnp.maximum(m, stat_ref[slot:slot + 1, :])
        stat_ref[m_row:m_row + 1, :] = m_new
        part = None
        for c in range(n_chunks):
            p = exp_fn(bufs[slot][rows(c), :] - m_new).astype(BF16)
            d = _dot(value_chunk(kb, c), p)
            part = d if part is None else part + d
        acc_ref[...] = exp_fn(m - m_new) * acc_ref[...] + part

    def run(kb0, n_blocks, last_is_diagonal):
        n_prep = n_blocks if last_is_diagonal else n_blocks + 2
        for j in range(n_blocks):
            if j + 2 < n_prep:
                prepare(kb0 + j + 2, (j + 2) % 3, last_is_diagonal and j + 2 == n_blocks - 1)
            consume(kb0 + j, j % 3)

    trips = jnp.maximum(n_last - 2, 0) // 3
    prepare(0, 0, True)
    prepare(1, 1, True)

    def trip(t, carry):
        run(3 * t, 3, False)
        return carry

    lax.fori_loop(0, trips, trip, 0)
    kb0 = 3 * trips
    for r in range(5):
        pl.when(n_last - kb0 == r)(functools.partial(run, kb0, r + 1, True))
    return stat_ref[m_row:m_row + 1, :], acc_ref[...]


def _mla_kernel(qt_ref, k_ref, vt_ref, o_ref, *bufs):
    tq, tk = TQ_MLA, TK_MLA
    i = pl.program_id(1)
    q_t = qt_ref[0]

    ck = CHUNK_MLA

    def score_chunk(kb, c, maybe_diagonal):
        off = pl.multiple_of(kb * tk + c * ck, ck)
        s = _dot(k_ref[0, pl.ds(off, ck), :], q_t)
        if maybe_diagonal:
            key = off + lax.broadcasted_iota(jnp.int32, (ck, tq), 0)
            qry = i * tq + lax.broadcasted_iota(jnp.int32, (ck, tq), 1)
            s = jnp.where(key <= qry, s, NEG_INF)
        return s

    def value_chunk(kb, c):
        off = pl.multiple_of(kb * tk + c * ck, ck)
        return _with_ones_rows(vt_ref[0, :, pl.ds(off, ck)])

    st = (jnp.full((1, tq), NEG_INF, F32), jnp.zeros((MLA_V + ONES_ROWS, tq), F32))
    _, acc = _pipelined_blocks((i * tq) // tk, score_chunk, value_chunk, tk // ck, ck, jnp.exp,
                               bufs, st)
    o_ref[0] = acc[:MLA_V] / acc[MLA_V:MLA_V + 1]


def _mla_attention(q_t, k, v_t):
    nh, s, _ = k.shape
    tq, tk = TQ_MLA, TK_MLA
    return pl.pallas_call(
        _mla_kernel, grid=(nh, s // tq),
        in_specs=[pl.BlockSpec((1, MLA_PAD, tq), lambda h, i: (h, 0, i)),
                  pl.BlockSpec((1, s, MLA_PAD), lambda h, i: (h, 0, 0)),
                  pl.BlockSpec((1, MLA_V, s), lambda h, i: (h, 0, 0))],
        out_specs=pl.BlockSpec((1, MLA_V, tq), lambda h, i: (h, 0, i)),
        out_shape=jax.ShapeDtypeStruct((nh, MLA_V, s), F32),
        scratch_shapes=[pltpu.VMEM((tk, tq), F32)] * 3 + [pltpu.VMEM((8, tq), F32),
                                                          pltpu.VMEM((MLA_V + ONES_ROWS, tq), F32)],
        compiler_params=_cparams("parallel", "parallel"), name="mla_attention",
    )(q_t, k, v_t)


def _compress_kernel(x_ref, pos_ref, w1_ref, w2_ref, kg_ref, o_ref, ot_ref):
    kv = pl.program_id(0)
    half = CMP_STRIDE * HEAD_DIM
    a = x_ref[0, 0].astype(BF16)
    w1 = w1_ref[0]
    first = _dot(a, w1[:half])
    second = _dot(a, w1[half:])
    n = second.shape[0]
    second = pltpu.roll(second, n - 1, 0)
    posb = jnp.broadcast_to(pos_ref[0], (8, 2 * half)).astype(BF16)
    hid = first + second + _dot(posb, w1)[0:1]
    hid = hid * jax.nn.sigmoid(hid)
    out = _dot(hid.astype(BF16), w2_ref[0])
    ms = jnp.mean(out * out, axis=-1, keepdims=True)
    normed = out * lax.rsqrt(ms + EPS) * kg_ref[...]
    res = jnp.where(kv == 0, normed, out)
    o_ref[0, 0] = res
    ot_ref[0, 0] = res.T


def _compress(x2, pos, w1, w2, kgain):
    _, ng, nc, cw = x2.shape
    return pl.pallas_call(
        _compress_kernel, grid=(2, ng),
        in_specs=[pl.BlockSpec((1, 1, nc, cw), lambda a, g: (a, g, 0, 0)),
                  pl.BlockSpec((1, 1, 2 * cw), lambda a, g: (a, 0, 0)),
                  pl.BlockSpec((1, 2 * cw, CMP_HIDDEN), lambda a, g: (a, 0, 0)),
                  pl.BlockSpec((1, CMP_HIDDEN, HEAD_DIM), lambda a, g: (a, 0, 0)),
                  _const_spec(kgain.shape)],
        out_specs=(pl.BlockSpec((1, 1, nc, HEAD_DIM), lambda a, g: (a, g, 0, 0)),
                   pl.BlockSpec((1, 1, HEAD_DIM, nc), lambda a, g: (a, g, 0, 0))),
        out_shape=(jax.ShapeDtypeStruct((2, ng, nc, HEAD_DIM), F32),
                   jax.ShapeDtypeStruct((2, ng, HEAD_DIM, nc), F32)),
        compiler_params=_cparams("parallel", "parallel"), name="nsa_compress",
    )(x2, pos, w1, w2, kgain)


_IMP_PAD = 8


def _nsa_cmp_kernel(thr_ref, tab_ref, qt_ref, kc_ref, vct_ref, gt_ref, oc_ref, selt_ref, band_ref, imp_ref):
    tq = TQ_CMP
    r_n = NSA_HPG
    g = pl.program_id(0)
    i = pl.program_id(1)
    nc = kc_ref.shape[2]
    nb = selt_ref.shape[1]
    cpt = tq // CMP_STRIDE

    @pl.when(i == 0)
    def _init():
        m = lax.broadcasted_iota(jnp.int32, (2 * nc, tq), 0)
        a = lax.broadcasted_iota(jnp.int32, (2 * nc, tq), 1)
        rel = a - CMP_STRIDE * (m - nc) - (CMP_LEN - 1)
        for r in range(r_n):
            band_ref[r] = _bias_chain(rel, thr_ref, tab_ref, g * r_n + r) * LOG2E
        imp_ref[0:_IMP_PAD, :] = jnp.zeros((_IMP_PAD, tq), F32)

    q_t = jnp.concatenate([qt_ref[r] for r in range(r_n)], axis=1)
    boff = pl.multiple_of(nc - cpt * i, cpt)
    gates = gt_ref[0]

    def tile_body(nc_e, nb_e):
        s = _dot(kc_ref[0, 0, :nc_e, :].astype(BF16), q_t)
        qpos = i * tq + lax.broadcasted_iota(jnp.int32, (nc_e, tq), 1)
        cmp_end = lax.broadcasted_iota(jnp.int32, (nc_e, tq), 0) * CMP_STRIDE + (CMP_LEN - 1)
        valid = cmp_end <= qpos
        imp = jnp.zeros((nc_e, tq), F32)
        probs = []
        for r in range(r_n):
            sr = s[:, r * tq:(r + 1) * tq] + band_ref[r, pl.ds(boff, nc_e), :]
            sr = jnp.where(valid, sr, NEG_INF)
            mx = jnp.max(sr, axis=0, keepdims=True)
            e = jnp.where(valid, jnp.exp2(sr - mx), 0.0)
            den = jnp.sum(e, axis=0, keepdims=True)
            p = e / jnp.where(den > 0.0, den, 1.0)
            imp = imp + p
            probs.append(p.astype(BF16))
        oc = _dot(vct_ref[0, 0, :, :nc_e].astype(BF16), jnp.concatenate(probs, axis=1))
        for r in range(r_n):
            c = NSA_BRANCHES * r
            oc_ref[r] = oc[:, r * tq:(r + 1) * tq] * gates[c:c + 1, :]

        rs = SEL_BLOCK // CMP_STRIDE
        rc = CMP_LEN // CMP_STRIDE
        imp_ref[_IMP_PAD:_IMP_PAD + nc_e, :] = imp
        imp_blk = None
        for m in range(rs + rc - 1):
            term = imp_ref[pl.ds(_IMP_PAD - (rc - 1) + m, nb_e, stride=rs), :]
            imp_blk = term if imp_blk is None else imp_blk + term
        blk = lax.broadcasted_iota(jnp.int32, (nb_e, tq), 0)
        qp = i * tq + lax.broadcasted_iota(jnp.int32, (nb_e, tq), 1)
        causal_b = blk * SEL_BLOCK <= qp
        dist = qp // SEL_BLOCK - blk
        forced = (blk == 0) | ((dist >= 0) & (dist < N_LOCAL_BLOCKS))
        score = jnp.where(causal_b, jnp.where(forced, FORCE, imp_blk), NEG_INF)
        for _ in range(min(SEL_TOPK, nb_e)):
            mx = jnp.max(score, axis=0, keepdims=True)
            first = jnp.min(jnp.where(score == mx, blk, nb_e), axis=0, keepdims=True)
            score = jnp.where(blk == first, -jnp.inf, score)
        selt_ref[0, :nb_e, :] = jnp.where((score == -jnp.inf) & causal_b, 1.0, 0.0)
        if nb_e < nb:
            selt_ref[0, nb_e:, :] = jnp.zeros((nb - nb_e, tq), F32)

    tiles_per_part = pl.num_programs(1) // CMP_PARTS
    part = i // tiles_per_part
    for j in range(CMP_PARTS):
        pl.when(part == j)(functools.partial(tile_body, (j + 1) * nc // CMP_PARTS, (j + 1) * nb // CMP_PARTS))


def _nsa_compressed(thr, tab, nq_t, cmp, cmp_t, gates_t):
    nh, _, s = nq_t.shape
    ng = NSA_GROUPS
    tq = TQ_CMP
    nc = cmp.shape[2]
    nb = s // SEL_BLOCK
    qspec = pl.BlockSpec((NSA_HPG, HEAD_DIM, tq), lambda g, i: (g, 0, i))
    return pl.pallas_call(
        _nsa_cmp_kernel, grid=(ng, s // tq),
        in_specs=[_smem_spec(), _smem_spec(), qspec,
                  pl.BlockSpec((1, 1, nc, HEAD_DIM), lambda g, i: (0, g, 0, 0)),
                  pl.BlockSpec((1, 1, HEAD_DIM, nc), lambda g, i: (1, g, 0, 0)),
                  pl.BlockSpec((1, NSA_HPG * NSA_BRANCHES, tq), lambda g, i: (g, 0, i))],
        out_specs=(qspec, pl.BlockSpec((1, nb, tq), lambda g, i: (g, 0, i))),
        out_shape=(jax.ShapeDtypeStruct((nh, HEAD_DIM, s), F32),
                   jax.ShapeDtypeStruct((ng, nb, s), F32)),
        scratch_shapes=[pltpu.VMEM((NSA_HPG, 2 * nc, tq), F32), pltpu.VMEM((_IMP_PAD + nc, tq), F32)],
        compiler_params=_cparams("arbitrary", "arbitrary"), name="nsa_compressed",
    )(thr, tab, nq_t, cmp, cmp_t, gates_t)


_NEAR_SPAN = ((MAX_DISTANCE + TK_SEL - 2) // TQ_SEL) * TQ_SEL
assert _NEAR_SPAN >= MAX_DISTANCE and TK_SEL % TQ_SEL == 0
_STRIP = 2 * TK_SEL + _NEAR_SPAN
_WIN_KEYS = WINDOW + TQ_SEL
_BLOCKS_PER_TILE = TK_SEL // SEL_BLOCK


def _nsa_sel_kernel(thr_ref, tab_ref, qt_ref, ks_ref, vst_ref, kw_ref, vwt_ref, selt_ref, gt_ref, oc_ref,
                    o_ref, strip_ref, *bufs):
    tq, tk, r_n = TQ_SEL, TK_SEL, NSA_HPG
    g = pl.program_id(0)
    i = pl.program_id(1)
    top = tk + _NEAR_SPAN
    lanes = r_n * tq

    @pl.when(i == 0)
    def _init():
        u = lax.broadcasted_iota(jnp.int32, (_STRIP, tq), 0)
        a = lax.broadcasted_iota(jnp.int32, (_STRIP, tq), 1)
        for r in range(r_n):
            strip_ref[:, r * tq:(r + 1) * tq] = _bias_chain(a - u + top, thr_ref, tab_ref, g * r_n + r) * LOG2E

    q_t = jnp.concatenate([qt_ref[r] for r in range(r_n)], axis=1)
    pad_rows = jnp.zeros((LANES - HEAD_DIM - _BLOCKS_PER_TILE, lanes), F32)
    qpos = i * tq + (lax.broadcasted_iota(jnp.int32, (1, lanes), 1) & (tq - 1))

    ck = CHUNK_SEL

    def score_chunk(kb, c, maybe_diagonal):
        off = pl.multiple_of(kb * tk + c * ck, ck)
        boff = pl.multiple_of(kb * _BLOCKS_PER_TILE, _BLOCKS_PER_TILE)
        picked = selt_ref[0, pl.ds(boff, _BLOCKS_PER_TILE), :]
        penalty = jnp.where(picked > 0.5, 0.0, NEG_INF)
        penalty = jnp.concatenate([penalty] * r_n, axis=1)
        q_aug = jnp.concatenate([q_t, jnp.concatenate([penalty, pad_rows], axis=0).astype(BF16)], axis=0)
        soff = pl.multiple_of(jnp.clip(top - (i * tq - kb * tk), 0, top) + c * ck, ck)
        s = _dot(ks_ref[0, pl.ds(off, ck), :], q_aug) + strip_ref[pl.ds(soff, ck), :]
        if maybe_diagonal:
            key = off + lax.broadcasted_iota(jnp.int32, (ck, lanes), 0)
            s = jnp.where(key <= qpos, s, NEG_INF)
        return s

    def value_chunk(kb, c):
        off = pl.multiple_of(kb * tk + c * ck, ck)
        return _with_ones_rows(vst_ref[0, :, pl.ds(off, ck)])

    woff = pl.multiple_of(i * tq, tq)
    w0 = top - WINDOW
    sw = _dot(kw_ref[0, pl.ds(woff, _WIN_KEYS), :], q_t) + strip_ref[w0:w0 + _WIN_KEYS, :]
    kpos = i * tq - WINDOW + lax.broadcasted_iota(jnp.int32, (_WIN_KEYS, lanes), 0)
    rel = qpos - kpos
    sw = jnp.where((rel >= 0) & (rel < WINDOW) & (kpos >= 0), sw, NEG_INF)
    pw = jnp.exp2(sw - jnp.max(sw, axis=0, keepdims=True)).astype(BF16)
    o_win = _dot(_with_ones_rows(vwt_ref[0, :, pl.ds(woff, _WIN_KEYS)]), pw)
    o_win = o_win[:HEAD_DIM] / o_win[HEAD_DIM:HEAD_DIM + 1]
    gates = gt_ref[0]
    for r in range(r_n):
        c0 = NSA_BRANCHES * r
        o_ref[r] = oc_ref[r] + gates[c0 + 2:c0 + 3, :] * o_win[:, r * tq:(r + 1) * tq]

    st = (jnp.full((1, lanes), NEG_INF, F32), jnp.zeros((HEAD_DIM + ONES_ROWS, lanes), F32))
    _, acc = _pipelined_blocks((i * tq) // tk, score_chunk, value_chunk, tk // ck, ck, jnp.exp2,
                               bufs, st)
    o_sel = acc[:HEAD_DIM] / acc[HEAD_DIM:HEAD_DIM + 1]
    for r in range(r_n):
        c0 = NSA_BRANCHES * r
        o_ref[r] += gates[c0 + 1:c0 + 2, :] * o_sel[:, r * tq:(r + 1) * tq]


def _nsa_selected(thr, tab, nq_t, ks, vs_t, kw_pad, vw_t_pad, sel_t, gates_t, oc):
    nh, _, s = nq_t.shape
    ng, tq = NSA_GROUPS, TQ_SEL
    nb = s // SEL_BLOCK
    qspec = pl.BlockSpec((NSA_HPG, HEAD_DIM, tq), lambda g, i: (g, 0, i))
    rows = lambda n: pl.BlockSpec((1, n, HEAD_DIM), lambda g, i: (g, 0, 0), pipeline_mode=pl.Buffered(1))
    cols = lambda n: pl.BlockSpec((1, HEAD_DIM, n), lambda g, i: (g, 0, 0), pipeline_mode=pl.Buffered(1))
    return pl.pallas_call(
        _nsa_sel_kernel, grid=(ng, s // tq),
        in_specs=[_smem_spec(), _smem_spec(), qspec,
                  pl.BlockSpec((1, s, LANES), lambda g, i: (g, 0, 0), pipeline_mode=pl.Buffered(1)),
                  cols(s), rows(s + WINDOW), cols(s + WINDOW),
                  pl.BlockSpec((1, nb, tq), lambda g, i: (g, 0, i)),
                  pl.BlockSpec((1, NSA_HPG * NSA_BRANCHES, tq), lambda g, i: (g, 0, i)), qspec],
        out_specs=qspec,
        out_shape=jax.ShapeDtypeStruct((nh, HEAD_DIM, s), F32),
        scratch_shapes=[pltpu.VMEM((_STRIP, NSA_HPG * tq), F32)] + [pltpu.VMEM((TK_SEL, NSA_HPG * tq), F32)] * 3
        + [pltpu.VMEM((8, NSA_HPG * tq), F32), pltpu.VMEM((HEAD_DIM + ONES_ROWS, NSA_HPG * tq), F32)],
        compiler_params=_cparams("arbitrary", "arbitrary"), name="nsa_selected",
    )(thr, tab, nq_t, ks, vs_t, kw_pad, vw_t_pad, sel_t, gates_t, oc)


def _merge_body(osb_ref, onsa_ref, omla_ref, g_ref, x_ref, wsb_ref, wnsa_ref, wmla_ref, wout_ref, fg_ref):
    def heads_proj(o_ref, w_ref):
        nh, hd = w_ref.shape[0], w_ref.shape[1]
        if o_ref.shape[1] == hd:
            o = o_ref[...].reshape(nh * hd, o_ref.shape[2]).T.astype(BF16)
            return _dot(o, w_ref[...].reshape(nh * hd, w_ref.shape[2]))
        y = _dot(o_ref[0], w_ref[0])
        for j in range(1, nh):
            y = y + _dot(o_ref[j], w_ref[j])
        return y

    merged = g_ref[:, 0:D_MODEL] * heads_proj(osb_ref, wsb_ref)
    merged = merged + g_ref[:, D_MODEL:2 * D_MODEL] * heads_proj(onsa_ref, wnsa_ref)
    merged = merged + g_ref[:, 2 * D_MODEL:3 * D_MODEL] * heads_proj(omla_ref, wmla_ref)
    x1 = x_ref[...] + _dot(merged.astype(BF16), wout_ref[...])
    ms = jnp.mean(x1 * x1, axis=-1, keepdims=True)
    return x1, x1 * lax.rsqrt(ms + EPS) * fg_ref[...]


def _merge_kernel(osb_ref, onsa_ref, omla_ref, g_ref, x_ref, wsb_ref, wnsa_ref, wmla_ref, wout_ref, fg_ref,
                  x1_ref, h_ref):
    x1, h = _merge_body(osb_ref, onsa_ref, omla_ref, g_ref, x_ref, wsb_ref, wnsa_ref, wmla_ref, wout_ref,
                        fg_ref)
    x1_ref[...] = x1
    h_ref[...] = h.astype(BF16)


def _merge_router_kernel(osb_ref, onsa_ref, omla_ref, g_ref, x_ref, wsb_ref, wnsa_ref, wmla_ref, wout_ref,
                         fg_ref, wr_ref, x1_ref, h_ref, comb_ref, sel_ref):
    x1, h = _merge_body(osb_ref, onsa_ref, omla_ref, g_ref, x_ref, wsb_ref, wnsa_ref, wmla_ref, wout_ref,
                        fg_ref)
    x1_ref[...] = x1
    h_ref[...] = h.astype(BF16)
    wr = wr_ref[...]
    hp = _split3(h)
    wp = _split3(wr)
    logits = jnp.zeros((h.shape[0], wr.shape[1]), F32)
    for a_i, b_i in ((2, 0), (0, 2), (1, 1), (1, 0), (0, 1), (0, 0)):
        logits = logits + _dot(hp[a_i], wp[b_i])
    lane = lax.broadcasted_iota(jnp.int32, logits.shape, 1)
    nl = logits.shape[1]
    logits = jnp.where(lane < N_EXPERTS, logits, -jnp.inf)
    v1 = jnp.max(logits, axis=-1, keepdims=True)
    i1 = jnp.min(jnp.where(logits == v1, lane, nl), axis=-1, keepdims=True)
    rest = jnp.where(lane == i1, -jnp.inf, logits)
    v2 = jnp.max(rest, axis=-1, keepdims=True)
    i2 = jnp.min(jnp.where(rest == v2, lane, nl), axis=-1, keepdims=True)
    e2 = jnp.exp(v2 - v1)
    w1 = 1.0 / (1.0 + e2)
    w2 = e2 / (1.0 + e2)
    comb_ref[...] = jnp.where(lane == i1, w1, jnp.where(lane == i2, w2, 0.0))
    sel_ref[...] = jnp.where((lane == i1) | (lane == i2), 1.0, 0.0)


def _merge(osb, onsa, omla, g, x, wsb, wnsa, wmla, wout, fgain, wrouter=None):
    s = x.shape[0]
    tm = TM_MERGE if wrouter is None else TM_MERGE_ROUTER

    def heads(a):
        if a.shape[1] == s:
            return pl.BlockSpec((a.shape[0], tm, a.shape[2]), lambda i: (0, i, 0))
        return pl.BlockSpec((a.shape[0], a.shape[1], tm), lambda i: (0, 0, i))

    row = lambda w: pl.BlockSpec((tm, w), lambda i: (i, 0))
    in_specs = [heads(osb), heads(onsa), heads(omla), row(GATE_W), row(D_MODEL)]
    in_specs += [_const_spec(a.shape) for a in (wsb, wnsa, wmla, wout, fgain)]
    out_shape = [jax.ShapeDtypeStruct((s, D_MODEL), F32), jax.ShapeDtypeStruct((s, D_MODEL), BF16)]
    out_specs = [row(D_MODEL), row(D_MODEL)]
    args = [osb, onsa, omla, g, x, wsb, wnsa, wmla, wout, fgain]
    if wrouter is None:
        kern = _merge_kernel
    else:
        kern = _merge_router_kernel
        in_specs.append(_const_spec(wrouter.shape))
        args.append(wrouter)
        out_shape += [jax.ShapeDtypeStruct((s, LANES), F32)] * 2
        out_specs += [row(LANES)] * 2
    return pl.pallas_call(
        kern, grid=(s // tm,), in_specs=in_specs, out_specs=tuple(out_specs), out_shape=tuple(out_shape),
        compiler_params=_cparams("parallel"), name="merge",
    )(*args)


def _ffn_kernel(x_ref, h_ref, wg_ref, wu_ref, wd_ref, o_ref, acc_ref):
    f = pl.program_id(1)

    @pl.when(f == 0)
    def _():
        acc_ref[...] = x_ref[...]

    h = h_ref[...]
    gate = _dot(h, wg_ref[...])
    up = _dot(h, wu_ref[...])
    act = (gate * jax.nn.sigmoid(gate) * up).astype(BF16)
    acc_ref[...] += _dot(act, wd_ref[...])

    @pl.when(f == pl.num_programs(1) - 1)
    def _():
        o_ref[...] = acc_ref[...]


def _dense_ffn(x1, h, wg, wu, wd, tf):
    s = x1.shape[0]
    tm = TM_FFN
    nf = wg.shape[1] // tf
    return pl.pallas_call(
        _ffn_kernel, grid=(s // tm, nf),
        in_specs=[pl.BlockSpec((tm, D_MODEL), lambda i, f: (i, 0)),
                  pl.BlockSpec((tm, D_MODEL), lambda i, f: (i, 0)),
                  pl.BlockSpec((D_MODEL, tf), lambda i, f: (0, f)),
                  pl.BlockSpec((D_MODEL, tf), lambda i, f: (0, f)),
                  pl.BlockSpec((tf, D_MODEL), lambda i, f: (f, 0))],
        out_specs=pl.BlockSpec((tm, D_MODEL), lambda i, f: (i, 0)),
        out_shape=jax.ShapeDtypeStruct((s, D_MODEL), F32),
        scratch_shapes=[pltpu.VMEM((tm, D_MODEL), F32)],
        compiler_params=_cparams("parallel", "arbitrary"), name="dense_ffn",
    )(x1, h, wg, wu, wd)


def _permute_kernel(tile_ref, chunk_ref, flag_ref, exp_ref, dest_ref, cw_ref, h_ref, xg_ref, rw_ref, acc_ref,
                    wacc_ref):
    k = pl.program_id(0)
    tm = TM_MOE
    flags = flag_ref[k]

    @pl.when((flags & 1) != 0)
    def _():
        acc_ref[...] = jnp.zeros_like(acc_ref)
        wacc_ref[...] = jnp.zeros_like(wacc_ref)

    @pl.when((flags & 4) != 0)
    def _():
        row = tile_ref[k] * tm + lax.broadcasted_iota(jnp.int32, (tm, tm), 0)
        hit = dest_ref[0] == row
        acc_ref[...] += _dot(jnp.where(hit, 1.0, 0.0).astype(BF16), h_ref[...])
        wacc_ref[...] += jnp.sum(jnp.where(hit, cw_ref[0], 0.0), axis=-1, keepdims=True)

    @pl.when((flags & 2) != 0)
    def _():
        xg_ref[...] = acc_ref[...].astype(BF16)
        rw_ref[...] = wacc_ref[...]


def _moe_ffn_kernel(te_ref, tv_ref, xg_ref, rw_ref, wg_ref, wu_ref, wd_ref, y_ref, acc_ref):
    t = pl.program_id(0)
    f = pl.program_id(1)

    @pl.when(tv_ref[t] != 0)
    def _():
        @pl.when(f == 0)
        def _():
            acc_ref[...] = jnp.zeros_like(acc_ref)

        x = xg_ref[...]
        gate = _dot(x, wg_ref[0])
        up = _dot(x, wu_ref[0])
        act = (gate * jax.nn.sigmoid(gate) * up * rw_ref[...]).astype(BF16)
        acc_ref[...] += _dot(act, wd_ref[0])

        @pl.when(f == pl.num_programs(1) - 1)
        def _():
            y_ref[...] = acc_ref[...].astype(BF16)

    @pl.when(tv_ref[t] == 0)
    def _():
        y_ref[...] = jnp.zeros_like(y_ref)


def _unpermute_kernel(tile_ref, chunk_ref, flag_ref, grow_ref, x_ref, y_ref, o_ref):
    k = pl.program_id(0)
    tm = TM_MOE
    flags = flag_ref[k]

    @pl.when((flags & 1) != 0)
    def _():
        o_ref[...] = x_ref[...]

    @pl.when((flags & 4) != 0)
    def _():
        tu = y_ref.shape[0]
        row = chunk_ref[k] * tu + lax.broadcasted_iota(jnp.int32, (tm, tu), 1)
        hit = (grow_ref[:, 0:1] == row) | (grow_ref[:, 1:2] == row)
        o_ref[...] += _dot(jnp.where(hit, 1.0, 0.0).astype(BF16), y_ref[...])


def _work_list(first, last, n_items, min_one=True):
    n_tiles = first.shape[0]
    cnt_real = jnp.maximum(last - first + 1, 0)
    cnt = jnp.maximum(cnt_real, 1) if min_one else cnt_real
    ends = jnp.cumsum(cnt)
    starts = ends - cnt
    total = ends[-1]
    k = jnp.arange(n_items, dtype=jnp.int32)
    kk = jnp.minimum(k, total - 1)
    tile = jnp.minimum(jnp.sum(kk[:, None] >= ends[None, :], axis=1), n_tiles - 1).astype(jnp.int32)
    pos = kk - starts[tile]
    chunk = jnp.where(cnt_real[tile] > 0, first[tile] + pos, 0).astype(jnp.int32)
    live = k < total
    flags = (jnp.where(live & (pos == 0), 1, 0) | jnp.where(live & (pos == cnt[tile] - 1), 2, 0)
             | jnp.where(live & (pos < cnt_real[tile]), 4, 0))
    return tile, chunk, flags.astype(jnp.int32)


def _moe_ffn(x1, h, comb, selm, wg, wu, wd):
    s = x1.shape[0]
    tm, tf = TM_MOE, TF_MOE
    ne = N_EXPERTS
    n_chunks = s // tm
    n_tiles = 2 * s // tm + ne
    rows = n_tiles * tm
    sel_t = selm[:, :ne].T.astype(jnp.int32)
    cum = jnp.cumsum(sel_t, axis=1)
    counts = cum[:, -1]
    tiles_e = (counts + tm - 1) // tm
    tile_end = jnp.cumsum(tiles_e)
    tile_start = tile_end - tiles_e
    used = tile_end[-1]
    dest = jnp.where(sel_t > 0, tile_start[:, None] * tm + cum - 1, -1).astype(jnp.int32)
    t_idx = jnp.arange(n_tiles, dtype=jnp.int32)
    te_raw = jnp.minimum(jnp.sum(t_idx[:, None] >= tile_end[None, :], axis=1), ne - 1).astype(jnp.int32)
    tvalid = (t_idx < used).astype(jnp.int32)
    last_e = te_raw[jnp.maximum(used - 1, 0)]
    te = jnp.where(tvalid > 0, te_raw, last_e).astype(jnp.int32)
    dest_tiles = dest.reshape(ne, n_chunks, tm)
    big = jnp.int32(2 ** 30)
    d_max = jnp.max(dest_tiles, axis=2)
    d_min = jnp.min(jnp.where(dest_tiles >= 0, dest_tiles, big), axis=2)
    overlap = (d_max[te] >= t_idx[:, None] * tm) & (d_min[te] < (t_idx[:, None] + 1) * tm) & (tvalid[:, None] > 0)
    c_idx = jnp.arange(n_chunks, dtype=jnp.int32)[None, :]
    c_lo = jnp.min(jnp.where(overlap, c_idx, n_chunks), axis=1)
    c_hi = jnp.max(jnp.where(overlap, c_idx, -1), axis=1)
    c_lo = jnp.where(c_hi >= 0, c_lo, 0).astype(jnp.int32)
    p_tile, p_chunk, p_flag = _work_list(c_lo, c_hi.astype(jnp.int32), n_tiles + ne * n_chunks)
    p_exp = te[p_tile]

    cw_t = comb[:, :ne].T.reshape(ne, 1, s)
    dest3 = dest.reshape(ne, 1, s)
    xg, roww = pl.pallas_call(
        _permute_kernel,
        grid_spec=pltpu.PrefetchScalarGridSpec(
            num_scalar_prefetch=4, grid=(p_tile.shape[0],),
            in_specs=[pl.BlockSpec((1, 1, tm), lambda k, pt, pc, pf, pe: (pe[k], 0, pc[k])),
                      pl.BlockSpec((1, 1, tm), lambda k, pt, pc, pf, pe: (pe[k], 0, pc[k])),
                      pl.BlockSpec((tm, D_MODEL), lambda k, pt, pc, pf, pe: (pc[k], 0))],
            out_specs=(pl.BlockSpec((tm, D_MODEL), lambda k, pt, pc, pf, pe: (pt[k], 0)),
                       pl.BlockSpec((tm, 1), lambda k, pt, pc, pf, pe: (pt[k], 0))),
            scratch_shapes=[pltpu.VMEM((tm, D_MODEL), F32), pltpu.VMEM((tm, 1), F32)]),
        out_shape=(jax.ShapeDtypeStruct((rows, D_MODEL), BF16), jax.ShapeDtypeStruct((rows, 1), F32)),
        compiler_params=_cparams("arbitrary"), name="moe_permute",
    )(p_tile, p_chunk, p_flag, p_exp, dest3, cw_t, h)

    nf = wg.shape[2] // tf
    last_f = nf - 1
    fsel = lambda t, f, tv: f * tv[t] + last_f * (1 - tv[t])
    yg = pl.pallas_call(
        _moe_ffn_kernel,
        grid_spec=pltpu.PrefetchScalarGridSpec(
            num_scalar_prefetch=2, grid=(n_tiles, nf),
            in_specs=[pl.BlockSpec((tm, D_MODEL), lambda t, f, te_, tv: (t, 0)),
                      pl.BlockSpec((tm, 1), lambda t, f, te_, tv: (t, 0)),
                      pl.BlockSpec((1, D_MODEL, tf), lambda t, f, te_, tv: (te_[t], 0, fsel(t, f, tv))),
                      pl.BlockSpec((1, D_MODEL, tf), lambda t, f, te_, tv: (te_[t], 0, fsel(t, f, tv))),
                      pl.BlockSpec((1, tf, D_MODEL), lambda t, f, te_, tv: (te_[t], fsel(t, f, tv), 0))],
            out_specs=pl.BlockSpec((tm, D_MODEL), lambda t, f, te_, tv: (t, 0)),
            scratch_shapes=[pltpu.VMEM((tm, D_MODEL), F32)]),
        out_shape=jax.ShapeDtypeStruct((rows, D_MODEL), BF16),
        compiler_params=_cparams("arbitrary", "arbitrary"), name="moe_ffn",
    )(te, tvalid, xg, roww, wg, wu, wd)

    grow = jnp.stack([jnp.min(jnp.where(dest >= 0, dest, big), axis=0), jnp.max(dest, axis=0)], axis=1)
    has = d_max >= 0
    tu = TU_MOE
    first = jnp.where(has, d_min // tu, 0).T.reshape(-1).astype(jnp.int32)
    last = jnp.where(has, d_max // tu, -1).T.reshape(-1).astype(jnp.int32)
    u_sub, u_chunk, u_flag = _work_list(first, last, rows // tu + ne * n_chunks, min_one=False)
    u_tile = (u_sub // ne).astype(jnp.int32)
    prev_tile = jnp.concatenate([jnp.full((1,), -1, jnp.int32), u_tile[:-1]])
    u_flag = (u_flag & 4) | jnp.where((u_flag != 0) & (u_tile != prev_tile), 1, 0)
    out = pl.pallas_call(
        _unpermute_kernel,
        grid_spec=pltpu.PrefetchScalarGridSpec(
            num_scalar_prefetch=3, grid=(u_tile.shape[0],),
            in_specs=[pl.BlockSpec((tm, 2), lambda k, ut, uc, uf: (ut[k], 0)),
                      pl.BlockSpec((tm, D_MODEL), lambda k, ut, uc, uf: (ut[k], 0)),
                      pl.BlockSpec((tu, D_MODEL), lambda k, ut, uc, uf: (uc[k], 0))],
            out_specs=pl.BlockSpec((tm, D_MODEL), lambda k, ut, uc, uf: (ut[k], 0))),
        out_shape=jax.ShapeDtypeStruct((s, D_MODEL), F32),
        compiler_params=_cparams("arbitrary"), name="moe_unpermute",
    )(u_tile, u_chunk, u_flag, grow, x1, yg)
    return out


def _t5_bucket(rel):
    n = jnp.maximum(rel, 0)
    max_exact = NUM_BUCKETS // 2
    nf = jnp.maximum(n, 1).astype(jnp.float32)
    large = max_exact + (jnp.log(nf / max_exact) / math.log(MAX_DISTANCE / max_exact)
                         * (NUM_BUCKETS - max_exact)).astype(jnp.int32)
    large = jnp.minimum(large, NUM_BUCKETS - 1)
    return jnp.where(n < max_exact, n, large)


def _bucket_thresholds():
    buckets = _t5_bucket(jnp.arange(MAX_DISTANCE + 1, dtype=jnp.int32))
    b = jnp.arange(NUM_BUCKETS, dtype=jnp.int32)
    return jnp.sum(buckets[None, :] < b[:, None], axis=1).astype(jnp.int32)


def _rope_tables(s):
    half = MLA_ROPE // 2
    inv = ROPE_THETA ** (-jnp.arange(half, dtype=jnp.float32) / half)
    ang = jnp.arange(s).astype(jnp.float32)[:, None] * inv[None, :]
    cos, sin = jnp.cos(ang), jnp.sin(ang)
    z = lambda w: jnp.zeros((s, w), F32)
    pad = MLA_PAD - MLA_QK
    cos_t = jnp.concatenate([jnp.ones((s, MLA_NOPE), F32), cos, cos, z(pad)], axis=1)
    sin_lo = jnp.concatenate([z(MLA_NOPE), -sin, z(half), z(pad)], axis=1)
    sin_hi = jnp.concatenate([z(MLA_NOPE), z(half), sin, z(pad)], axis=1)
    return cos_t, sin_lo, sin_hi


def _pad_lanes(a, width):
    return jnp.pad(a, [(0, 0)] * (a.ndim - 1) + [(0, width - a.shape[-1])])


def kernel(x, rel_bias_table, mix_norm, w_in, b_gate, sb_w_o, nsa_q_norm, nsa_k_norm, nsa_cmp_pos, nsa_cmp_w1, nsa_cmp_w2, nsa_w_o, mla_q_a_norm, mla_kv_a_norm, mla_w_uq, mla_w_ukv, mla_q_norm, mla_k_norm, mla_w_o, w_out, ffn_norm, dense_w_gate, dense_w_up, dense_w_down, moe_router, moe_w_gate, moe_w_up, moe_w_down):
    b, s, d = x.shape
    assert b == 1 and d == D_MODEL and s % TM_MOE == 0 and s // SEL_BLOCK >= SEL_TOPK
    depth = w_in.shape[0]
    xs = x.reshape(s, d)
    thr = _bucket_thresholds()
    cos_t, sin_lo, sin_hi = _rope_tables(s)
    n_chunk = s // CMP_STRIDE

    for layer in range(depth):
        w = w_in[layer]
        small = jnp.concatenate([
            w[:, _C_NG:_C_MCQ], jnp.zeros((d, _KR_LANE - NSA_HEADS * NSA_BRANCHES), F32),
            w[:, _C_MKR:_C_GATE], jnp.zeros((d, LANES - _KR_LANE - MLA_ROPE), F32)], axis=1)
        wa = jnp.concatenate([w[:, :_C_NG], w[:, _C_MCQ:_C_MKR], small], axis=1).astype(BF16)
        wg = w[:, _C_GATE:].astype(BF16)
        wuq = _pad_lanes(mla_w_uq[layer].reshape(MLA_Q_LORA, MLA_HEADS, MLA_QK), MLA_PAD)
        wuq = wuq.reshape(MLA_Q_LORA, MLA_HEADS * MLA_PAD).astype(BF16)
        wukv = mla_w_ukv[layer].reshape(MLA_KV_LORA, MLA_HEADS, MLA_NOPE + MLA_V)
        wukv = jnp.concatenate([
            _pad_lanes(wukv[:, :, :MLA_NOPE], MLA_PAD).reshape(MLA_KV_LORA, MLA_HEADS * MLA_PAD),
            wukv[:, :, MLA_NOPE:].reshape(MLA_KV_LORA, MLA_HEADS * MLA_V)], axis=1).astype(BF16)
        (sbq, sbk, sbv, nq, nkvc, nks, nvs, nkw, nvw, gsm, mq, mk, mv, g) = _input_projection(
            xs, mix_norm[layer][None], wa, wg, b_gate[layer][None], wuq, wukv,
            nsa_q_norm[layer][None], nsa_k_norm[layer], mla_q_a_norm[layer][None],
            mla_kv_a_norm[layer][None], _pad_lanes(mla_q_norm[layer][None], MLA_PAD),
            _pad_lanes(mla_k_norm[layer][None], MLA_PAD), cos_t, sin_lo, sin_hi)

        o_sb = _stick_breaking(sbq, sbk, sbv)
        o_mla = _mla_attention(mq, mk, mv)

        x2 = nkvc.reshape(2, NSA_GROUPS, n_chunk, CMP_STRIDE * HEAD_DIM)
        cmp, cmp_t = _compress(x2, nsa_cmp_pos[layer].reshape(2, 1, CMP_LEN * HEAD_DIM),
                               nsa_cmp_w1[layer].astype(BF16), nsa_cmp_w2[layer].astype(BF16),
                               nsa_k_norm[layer][0:1])
        gates = gsm[:NSA_HEADS * NSA_BRANCHES].reshape(NSA_GROUPS, NSA_HPG * NSA_BRANCHES, s)
        oc, sel = _nsa_compressed(thr, rel_bias_table, nq, cmp, cmp_t, gates)
        o_nsa = _nsa_selected(thr, rel_bias_table, nq, nks, nvs,
                              jnp.pad(nkw, ((0, 0), (WINDOW, 0), (0, 0))),
                              jnp.pad(nvw, ((0, 0), (0, 0), (WINDOW, 0))), sel, gates, oc)

        wsb = sb_w_o[layer].reshape(SB_HEADS, HEAD_DIM, d).astype(BF16)
        wnsa = nsa_w_o[layer].reshape(NSA_HEADS, HEAD_DIM, d).astype(BF16)
        wmla = mla_w_o[layer].reshape(MLA_HEADS, MLA_V, d).astype(BF16)
        wout = w_out[layer].astype(BF16)
        j = layer // 2
        if layer % 2 == 0:
            x1, h = _merge(o_sb, o_nsa, o_mla, g, xs, wsb, wnsa, wmla, wout, ffn_norm[layer][None])
            d_ff = dense_w_gate.shape[2]
            xs = _dense_ffn(x1, h, dense_w_gate[j].astype(BF16), dense_w_up[j].astype(BF16),
                            dense_w_down[j].astype(BF16), d_ff // 2)
        else:
            x1, h, comb, selm = _merge(o_sb, o_nsa, o_mla, g, xs, wsb, wnsa, wmla, wout,
                                       ffn_norm[layer][None], _pad_lanes(moe_router[j], LANES))
            xs = _moe_ffn(x1, h, comb, selm, moe_w_gate[j].astype(BF16), moe_w_up[j].astype(BF16),
                          moe_w_down[j].astype(BF16))
    return xs.reshape(b, s, d)
```

```python
import functools
import math

import jax
import jax.numpy as jnp
from jax import lax
from jax.experimental import pallas as pl
from jax.experimental.pallas import tpu as pltpu

F32 = jnp.float32
BF16 = jnp.bfloat16

D_MODEL = 1024
HEAD_DIM = 64
SB_HEADS = 4
NSA_HEADS = 8
NSA_GROUPS = 2
NSA_HPG = NSA_HEADS // NSA_GROUPS
NSA_BRANCHES = 3
MLA_HEADS = 4
MLA_NOPE = 64
MLA_ROPE = 32
MLA_V = 64
MLA_QK = MLA_NOPE + MLA_ROPE
MLA_Q_LORA = 256
MLA_KV_LORA = 128
ROPE_THETA = 10000.0
CMP_LEN = 32
CMP_STRIDE = 16
CMP_HIDDEN = 256
SEL_BLOCK = 64
SEL_TOPK = 16
N_LOCAL_BLOCKS = 2
WINDOW = 512
NUM_BUCKETS = 32
MAX_DISTANCE = 1024
N_MIXERS = 3
N_EXPERTS = 8
EPS = 1e-6
NEG_INF = -1e30
FORCE = 1e30
LOG2E = math.log2(math.e)

SB_W = SB_HEADS * HEAD_DIM
NSA_QW = NSA_HEADS * HEAD_DIM
NSA_KVW = NSA_GROUPS * HEAD_DIM
GATE_W = N_MIXERS * D_MODEL
LANES = 128
MLA_PAD = LANES

_C_NQ = 3 * SB_W
_C_NKV = _C_NQ + NSA_QW
_C_NG = _C_NKV + 6 * NSA_KVW
_C_MCQ = _C_NG + NSA_HEADS * NSA_BRANCHES
_C_MCKV = _C_MCQ + MLA_Q_LORA
_C_MKR = _C_MCKV + MLA_KV_LORA
_C_GATE = _C_MKR + MLA_ROPE
_A_SB = 0
_A_NQ = 3 * SB_W
_A_NKV = _A_NQ + NSA_QW
_A_MCQ = _A_NKV + 6 * NSA_KVW
_A_MCKV = _A_MCQ + MLA_Q_LORA
_A_SMALL = _A_MCKV + MLA_KV_LORA
_A_COLS = _A_SMALL + LANES
_KR_LANE = MLA_NOPE

TM_PROJ = 256
TQ_SB = 512
TK_SB = 256
TQ_MLA = 256
TK_MLA = 1024
CHUNK_MLA = 256
CHUNK_SEL = 512
TQ_CMP = 128
CMP_PARTS = 4
TQ_SEL = 256
TK_SEL = 512
TM_MERGE = 512
TM_MERGE_ROUTER = 256
TM_FFN = 512
TM_MOE = 512
TF_MOE = 512
TU_MOE = 256
VMEM_LIMIT = 56 * 1024 * 1024
SB_EXIT = -104.0

_NT = (((1,), (1,)), ((), ()))


def _dot(a, b):
    return jnp.dot(a, b, preferred_element_type=F32)


def _dot_nt(a, b):
    return lax.dot_general(a, b, _NT, preferred_element_type=F32)


def _cparams(*sem):
    return pltpu.CompilerParams(dimension_semantics=sem, vmem_limit_bytes=VMEM_LIMIT)


def _const_spec(shape):
    nd = len(shape)
    return pl.BlockSpec(shape, lambda *_: (0,) * nd, pipeline_mode=pl.Buffered(1))


def _smem_spec():
    return pl.BlockSpec(memory_space=pltpu.SMEM)


def _split3(x):
    h1 = x.astype(BF16)
    r1 = x - h1.astype(F32)
    h2 = r1.astype(BF16)
    h3 = (r1 - h2.astype(F32)).astype(BF16)
    return h1, h2, h3


def _bias_chain(rel, thr_ref, tab_ref, head):
    out = jnp.full(rel.shape, tab_ref[0, head], F32)
    for b in range(1, NUM_BUCKETS):
        out = jnp.where(rel >= thr_ref[b], tab_ref[b, head], out)
    return out


def _rope(y, cos, sin_lo, sin_hi):
    half = MLA_ROPE // 2
    return y * cos + pltpu.roll(y, half, 1) * sin_hi + pltpu.roll(y, MLA_PAD - half, 1) * sin_lo


def _proj_kernel(x_ref, gain_ref, wa_ref, wg_ref, bg_ref, wuq_ref, wukv_ref, nqg_ref, nkg_ref,
                 qag_ref, kvag_ref, mqg_ref, mkg_ref, cos_ref, slo_ref, shi_ref,
                 sbq_ref, sbk_ref, sbv_ref, nq_ref, nkvc_ref, nks_ref, nvs_ref, nkw_ref, nvw_ref,
                 gsm_ref, mq_ref, mk_ref, mv_ref, g_ref):
    x = x_ref[...]
    ms = jnp.mean(x * x, axis=-1, keepdims=True)
    h = (x * lax.rsqrt(ms + EPS) * gain_ref[...]).astype(BF16)

    def rms(p, gain, width):
        m = jnp.sum(p * p, axis=-1, keepdims=True) * (1.0 / width)
        return p * lax.rsqrt(m + EPS) * gain

    p = _dot(h, wa_ref[:, _A_SB:_A_SB + 3 * SB_W])
    sbq_t = (p[:, :SB_W] * HEAD_DIM ** -0.5).T.astype(BF16)
    sbv_t = p[:, 2 * SB_W:3 * SB_W].T.astype(BF16)
    for j in range(SB_HEADS):
        sbq_ref[j] = sbq_t[j * HEAD_DIM:(j + 1) * HEAD_DIM]
        sbv_ref[j] = sbv_t[j * HEAD_DIM:(j + 1) * HEAD_DIM]
    for j in range(SB_HEADS // 2):
        sbk_ref[j] = p[:, SB_W + j * LANES:SB_W + (j + 1) * LANES].astype(BF16)

    p = _dot(h, wa_ref[:, _A_NQ:_A_NQ + NSA_QW])
    qn = [rms(p[:, j * HEAD_DIM:(j + 1) * HEAD_DIM], nqg_ref[...], HEAD_DIM) * (HEAD_DIM ** -0.5 * LOG2E)
          for j in range(NSA_HEADS)]
    qn_t = jnp.concatenate(qn, axis=1).T.astype(BF16)
    for j in range(NSA_HEADS):
        nq_ref[j] = qn_t[j * HEAD_DIM:(j + 1) * HEAD_DIM]

    p = _dot(h, wa_ref[:, _A_NKV:_A_NKV + 6 * NSA_KVW])
    vs_t = p[:, 3 * NSA_KVW:4 * NSA_KVW].T.astype(BF16)
    tag_shape = (x.shape[0], LANES - HEAD_DIM)
    block_id = (pl.program_id(0) * x.shape[0] + lax.broadcasted_iota(jnp.int32, tag_shape, 0)) // SEL_BLOCK
    block_tag = jnp.where(lax.broadcasted_iota(jnp.int32, tag_shape, 1) == block_id % (TK_SEL // SEL_BLOCK),
                          1.0, 0.0)
    vw_t = p[:, 5 * NSA_KVW:6 * NSA_KVW].T.astype(BF16)
    for j in range(NSA_GROUPS):
        def seg(s):
            return p[:, s * NSA_KVW + j * HEAD_DIM:s * NSA_KVW + (j + 1) * HEAD_DIM]
        nkvc_ref[0, j] = seg(0)
        nkvc_ref[1, j] = seg(1)
        nks_ref[j] = jnp.concatenate([rms(seg(2), nkg_ref[1:2, :], HEAD_DIM), block_tag], axis=1).astype(BF16)
        nvs_ref[j] = vs_t[j * HEAD_DIM:(j + 1) * HEAD_DIM]
        nkw_ref[j] = rms(seg(4), nkg_ref[2:3, :], HEAD_DIM).astype(BF16)
        nvw_ref[j] = vw_t[j * HEAD_DIM:(j + 1) * HEAD_DIM]

    p = _dot(h, wa_ref[:, _A_MCQ:_A_COLS])
    cq = rms(p[:, :MLA_Q_LORA], qag_ref[...], MLA_Q_LORA).astype(BF16)
    ckv = rms(p[:, MLA_Q_LORA:MLA_Q_LORA + MLA_KV_LORA], kvag_ref[...], MLA_KV_LORA).astype(BF16)
    small = p[:, MLA_Q_LORA + MLA_KV_LORA:]
    gsm_ref[...] = jax.nn.sigmoid(small).T
    lane = lax.broadcasted_iota(jnp.int32, small.shape, 1)
    krope = jnp.where((lane >= _KR_LANE) & (lane < _KR_LANE + MLA_ROPE), small, 0.0)
    cos, slo, shi = cos_ref[...], slo_ref[...], shi_ref[...]
    qu = _dot(cq, wuq_ref[...])
    kvu = _dot(ckv, wukv_ref[...])
    for j in range(MLA_HEADS):
        qh = rms(qu[:, j * MLA_PAD:(j + 1) * MLA_PAD], mqg_ref[...], MLA_QK)
        mq_ref[j] = (_rope(qh, cos, slo, shi) * MLA_QK ** -0.5).T.astype(BF16)
        kh = rms(kvu[:, j * MLA_PAD:(j + 1) * MLA_PAD] + krope, mkg_ref[...], MLA_QK)
        mk_ref[j] = _rope(kh, cos, slo, shi).astype(BF16)
    v_t = kvu[:, MLA_HEADS * MLA_PAD:].T.astype(BF16)
    for j in range(MLA_HEADS):
        mv_ref[j] = v_t[j * MLA_V:(j + 1) * MLA_V]

    g_ref[...] = jax.nn.sigmoid(_dot(h, wg_ref[...]) + bg_ref[...])


def _input_projection(x, gain, wa, wg, bg, wuq, wukv, nqg, nkg, qag, kvag, mqg, mkg, cos, slo, shi):
    s = x.shape[0]
    tm = TM_PROJ
    row = lambda w: pl.BlockSpec((tm, w), lambda i: (i, 0))
    heads = lambda n, w: pl.BlockSpec((n, tm, w), lambda i: (0, i, 0))
    hs = lambda n, w, dt: jax.ShapeDtypeStruct((n, s, w), dt)
    in_specs = [row(D_MODEL)] + [_const_spec(a.shape) for a in
                                 (gain, wa, wg, bg, wuq, wukv, nqg, nkg, qag, kvag, mqg, mkg)]
    in_specs += [row(MLA_PAD)] * 3
    hst = lambda n, w, dt: jax.ShapeDtypeStruct((n, w, s), dt)
    heads_t = lambda n, w: pl.BlockSpec((n, w, tm), lambda i: (0, 0, i))
    ng, hd = NSA_GROUPS, HEAD_DIM
    out_shape = (hst(SB_HEADS, hd, BF16), hs(SB_HEADS // 2, LANES, BF16), hst(SB_HEADS, hd, BF16))
    out_shape += (hst(NSA_HEADS, hd, BF16),)
    out_shape += (jax.ShapeDtypeStruct((2, ng, s, hd), F32),) + (hs(ng, LANES, BF16), hst(ng, hd, BF16), hs(ng, hd, BF16), hst(ng, hd, BF16))
    out_shape += (jax.ShapeDtypeStruct((LANES, s), F32),)
    out_shape += (hst(MLA_HEADS, MLA_PAD, BF16), hs(MLA_HEADS, MLA_PAD, BF16), hst(MLA_HEADS, MLA_V, BF16))
    out_shape += (jax.ShapeDtypeStruct((s, GATE_W), F32),)
    out_specs = (heads_t(SB_HEADS, hd), heads(SB_HEADS // 2, LANES), heads_t(SB_HEADS, hd))
    out_specs += (heads_t(NSA_HEADS, hd),)
    out_specs += (pl.BlockSpec((2, ng, tm, hd), lambda i: (0, 0, i, 0)),) + (heads(ng, LANES), heads_t(ng, hd), heads(ng, hd), heads_t(ng, hd))
    out_specs += (pl.BlockSpec((LANES, tm), lambda i: (0, i)),)
    out_specs += (heads_t(MLA_HEADS, MLA_PAD), heads(MLA_HEADS, MLA_PAD), heads_t(MLA_HEADS, MLA_V), row(GATE_W))
    return pl.pallas_call(
        _proj_kernel, grid=(s // tm,), in_specs=in_specs, out_specs=out_specs, out_shape=out_shape,
        compiler_params=_cparams("parallel"), name="input_projection",
    )(x, gain, wa, wg, bg, wuq, wukv, nqg, nkg, qag, kvag, mqg, mkg, cos, slo, shi)


def _sb_kernel(qt_ref, k2_ref, vt_ref, o_ref):
    tq, tk = TQ_SB, TK_SB
    nh = qt_ref.shape[0]
    i = pl.program_id(0)
    later = (lax.broadcasted_iota(jnp.int32, (tk, tk), 1)
             > lax.broadcasted_iota(jnp.int32, (tk, tk), 0)).astype(BF16)
    key = lax.broadcasted_iota(jnp.int32, (tk, tq), 0)
    qry = i * tq + lax.broadcasted_iota(jnp.int32, (tk, tq), 1)
    zeros = jnp.zeros((HEAD_DIM, tq), BF16)
    q_aug = [jnp.concatenate([qt_ref[h], zeros] if h % 2 == 0 else [zeros, qt_ref[h]], axis=0) for h in range(nh)]

    def tile(kt, h, carry, acc, diagonal):
        off = pl.multiple_of(kt * tk, tk)
        z = _dot(k2_ref[h // 2, pl.ds(off, tk), :], q_aug[h])
        log_stay = -(jnp.maximum(z, 0.0) + jnp.log(1.0 + jnp.exp(-jnp.abs(z))))
        if diagonal:
            causal = off + key < qry
            log_stay = jnp.where(causal, log_stay, 0.0)
        hi = log_stay.astype(BF16)
        lo = (log_stay - hi.astype(F32)).astype(BF16)
        suffix = _dot(later, hi) + _dot(later, lo)
        logw = z + log_stay + (carry + suffix)
        if diagonal:
            logw = jnp.where(causal, logw, NEG_INF)
        w = jnp.exp(logw).astype(BF16)
        acc = acc + _dot(vt_ref[h, :, pl.ds(off, tk)], w)
        carry = carry + suffix[0:1, :] + log_stay[0:1, :]
        return carry, acc

    def all_heads(kt, st, diagonal):
        out = [tile(kt, h, st[2 * h], st[2 * h + 1], diagonal) for h in range(nh)]
        return tuple(x for pair in out for x in pair)

    ratio = tq // tk
    st = (jnp.zeros((1, tq), F32), jnp.zeros((HEAD_DIM, tq), F32)) * nh
    for d in range(ratio):
        st = all_heads(ratio * i + ratio - 1 - d, st, True)

    def cond(c):
        top = c[1]
        for h in range(1, nh):
            top = jnp.maximum(top, c[1 + 2 * h])
        return jnp.logical_and(c[0] >= 0, jnp.max(top) > SB_EXIT)

    def body(c):
        return (c[0] - 1,) + all_heads(c[0], c[1:], False)

    c = lax.while_loop(cond, body, (ratio * i - 1,) + st)
    for h in range(nh):
        o_ref[h] = c[2 + 2 * h]


def _stick_breaking(q_t, k2, v_t):
    nh, _, s = q_t.shape
    tq = TQ_SB
    whole = lambda a: pl.BlockSpec(a.shape, lambda i: (0, 0, 0), pipeline_mode=pl.Buffered(1))
    return pl.pallas_call(
        _sb_kernel, grid=(s // tq,),
        in_specs=[pl.BlockSpec((nh, HEAD_DIM, tq), lambda i: (0, 0, i)), whole(k2), whole(v_t)],
        out_specs=pl.BlockSpec((nh, HEAD_DIM, tq), lambda i: (0, 0, i)),
        out_shape=jax.ShapeDtypeStruct((nh, HEAD_DIM, s), F32),
        compiler_params=_cparams("parallel"), name="stick_breaking",
    )(q_t, k2, v_t)


ONES_ROWS = 16


def _with_ones_rows(v_t):
    return jnp.concatenate([v_t, jnp.ones((ONES_ROWS, v_t.shape[1]), v_t.dtype)], axis=0)


def _pipelined_blocks(n_last, score_chunk, value_chunk, n_chunks, chunk, exp_fn, bufs, st):
    rows = lambda c: pl.ds(c * chunk, chunk)
    bufs, stat_ref, acc_ref = bufs[:3], bufs[3], bufs[4]
    m_row = 3
    stat_ref[m_row:m_row + 1, :] = st[0]
    acc_ref[...] = st[1]

    def prepare(kb, slot, maybe_diagonal):
        mx = None
        for c in range(n_chunks):
            s = score_chunk(kb, c, maybe_diagonal)
            bufs[slot][rows(c), :] = s
            smax = jnp.max(s, axis=0, keepdims=True)
            mx = smax if mx is None else jnp.maximum(mx, smax)
        stat_ref[slot:slot + 1, :] = mx

    def consume(kb, slot):
        m = stat_ref[m_row:m_row + 1, :]
        m_new = jnp.maximum(m, stat_ref[slot:slot + 1, :])
        stat_ref[m_row:m_row + 1, :] = m_new
        part = None
        for c in range(n_chunks):
            p = exp_fn(bufs[slot][rows(c), :] - m_new).astype(BF16)
            d = _dot(value_chunk(kb, c), p)
            part = d if part is None else part + d
        acc_ref[...] = exp_fn(m - m_new) * acc_ref[...] + part

    def run(kb0, n_blocks, last_is_diagonal):
        n_prep = n_blocks if last_is_diagonal else n_blocks + 2
        for j in range(n_blocks):
            if j + 2 < n_prep:
                prepare(kb0 + j + 2, (j + 2) % 3, last_is_diagonal and j + 2 == n_blocks - 1)
            consume(kb0 + j, j % 3)

    trips = jnp.maximum(n_last - 2, 0) // 3
    prepare(0, 0, True)
    prepare(1, 1, True)

    def trip(t, carry):
        run(3 * t, 3, False)
        return carry

    lax.fori_loop(0, trips, trip, 0)
    kb0 = 3 * trips
    for r in range(5):
        pl.when(n_last - kb0 == r)(functools.partial(run, kb0, r + 1, True))
    return stat_ref[m_row:m_row + 1, :], acc_ref[...]


def _mla_kernel(qt_ref, k_ref, vt_ref, o_ref, *bufs):
    tq, tk = TQ_MLA, TK_MLA
    i = pl.program_id(1)
    q_t = qt_ref[0]

    ck = CHUNK_MLA

    def score_chunk(kb, c, maybe_diagonal):
        off = pl.multiple_of(kb * tk + c * ck, ck)
        s = _dot(k_ref[0, pl.ds(off, ck), :], q_t)
        if maybe_diagonal:
            key = off + lax.broadcasted_iota(jnp.int32, (ck, tq), 0)
            qry = i * tq + lax.broadcasted_iota(jnp.int32, (ck, tq), 1)
            s = jnp.where(key <= qry, s, NEG_INF)
        return s

    def value_chunk(kb, c):
        off = pl.multiple_of(kb * tk + c * ck, ck)
        return _with_ones_rows(vt_ref[0, :, pl.ds(off, ck)])

    st = (jnp.full((1, tq), NEG_INF, F32), jnp.zeros((MLA_V + ONES_ROWS, tq), F32))
    _, acc = _pipelined_blocks((i * tq) // tk, score_chunk, value_chunk, tk // ck, ck, jnp.exp,
                               bufs, st)
    o_ref[0] = acc[:MLA_V] / acc[MLA_V:MLA_V + 1]


def _mla_attention(q_t, k, v_t):
    nh, s, _ = k.shape
    tq, tk = TQ_MLA, TK_MLA
    return pl.pallas_call(
        _mla_kernel, grid=(nh, s // tq),
        in_specs=[pl.BlockSpec((1, MLA_PAD, tq), lambda h, i: (h, 0, i)),
                  pl.BlockSpec((1, s, MLA_PAD), lambda h, i: (h, 0, 0)),
                  pl.BlockSpec((1, MLA_V, s), lambda h, i: (h, 0, 0))],
        out_specs=pl.BlockSpec((1, MLA_V, tq), lambda h, i: (h, 0, i)),
        out_shape=jax.ShapeDtypeStruct((nh, MLA_V, s), F32),
        scratch_shapes=[pltpu.VMEM((tk, tq), F32)] * 3 + [pltpu.VMEM((8, tq), F32),
                                                          pltpu.VMEM((MLA_V + ONES_ROWS, tq), F32)],
        compiler_params=_cparams("parallel", "parallel"), name="mla_attention",
    )(q_t, k, v_t)


def _compress_kernel(x_ref, pos_ref, w1_ref, w2_ref, kg_ref, o_ref, ot_ref):
    kv = pl.program_id(0)
    half = CMP_STRIDE * HEAD_DIM
    a = x_ref[0, 0].astype(BF16)
    w1 = w1_ref[0]
    first = _dot(a, w1[:half])
    second = _dot(a, w1[half:])
    n = second.shape[0]
    second = pltpu.roll(second, n - 1, 0)
    posb = jnp.broadcast_to(pos_ref[0], (8, 2 * half)).astype(BF16)
    hid = first + second + _dot(posb, w1)[0:1]
    hid = hid * jax.nn.sigmoid(hid)
    out = _dot(hid.astype(BF16), w2_ref[0])
    ms = jnp.mean(out * out, axis=-1, keepdims=True)
    normed = out * lax.rsqrt(ms + EPS) * kg_ref[...]
    res = jnp.where(kv == 0, normed, out)
    o_ref[0, 0] = res
    ot_ref[0, 0] = res.T


def _compress(x2, pos, w1, w2, kgain):
    _, ng, nc, cw = x2.shape
    return pl.pallas_call(
        _compress_kernel, grid=(2, ng),
        in_specs=[pl.BlockSpec((1, 1, nc, cw), lambda a, g: (a, g, 0, 0)),
                  pl.BlockSpec((1, 1, 2 * cw), lambda a, g: (a, 0, 0)),
                  pl.BlockSpec((1, 2 * cw, CMP_HIDDEN), lambda a, g: (a, 0, 0)),
                  pl.BlockSpec((1, CMP_HIDDEN, HEAD_DIM), lambda a, g: (a, 0, 0)),
                  _const_spec(kgain.shape)],
        out_specs=(pl.BlockSpec((1, 1, nc, HEAD_DIM), lambda a, g: (a, g, 0, 0)),
                   pl.BlockSpec((1, 1, HEAD_DIM, nc), lambda a, g: (a, g, 0, 0))),
        out_shape=(jax.ShapeDtypeStruct((2, ng, nc, HEAD_DIM), F32),
                   jax.ShapeDtypeStruct((2, ng, HEAD_DIM, nc), F32)),
        compiler_params=_cparams("parallel", "parallel"), name="nsa_compress",
    )(x2, pos, w1, w2, kgain)


_IMP_PAD = 8


def _nsa_cmp_kernel(thr_ref, tab_ref, qt_ref, kc_ref, vct_ref, gt_ref, oc_ref, selt_ref, band_ref, imp_ref):
    tq = TQ_CMP
    r_n = NSA_HPG
    g = pl.program_id(0)
    i = pl.program_id(1)
    nc = kc_ref.shape[2]
    nb = selt_ref.shape[1]
    cpt = tq // CMP_STRIDE

    @pl.when(i == 0)
    def _init():
        m = lax.broadcasted_iota(jnp.int32, (2 * nc, tq), 0)
        a = lax.broadcasted_iota(jnp.int32, (2 * nc, tq), 1)
        rel = a - CMP_STRIDE * (m - nc) - (CMP_LEN - 1)
        for r in range(r_n):
            band_ref[r] = _bias_chain(rel, thr_ref, tab_ref, g * r_n + r) * LOG2E
        imp_ref[0:_IMP_PAD, :] = jnp.zeros((_IMP_PAD, tq), F32)

    q_t = jnp.concatenate([qt_ref[r] for r in range(r_n)], axis=1)
    boff = pl.multiple_of(nc - cpt * i, cpt)
    gates = gt_ref[0]

    def tile_body(nc_e, nb_e):
        s = _dot(kc_ref[0, 0, :nc_e, :].astype(BF16), q_t)
        qpos = i * tq + lax.broadcasted_iota(jnp.int32, (nc_e, tq), 1)
        cmp_end = lax.broadcasted_iota(jnp.int32, (nc_e, tq), 0) * CMP_STRIDE + (CMP_LEN - 1)
        valid = cmp_end <= qpos
        imp = jnp.zeros((nc_e, tq), F32)
        probs = []
        for r in range(r_n):
            sr = s[:, r * tq:(r + 1) * tq] + band_ref[r, pl.ds(boff, nc_e), :]
            sr = jnp.where(valid, sr, NEG_INF)
            mx = jnp.max(sr, axis=0, keepdims=True)
            e = jnp.where(valid, jnp.exp2(sr - mx), 0.0)
            den = jnp.sum(e, axis=0, keepdims=True)
            p = e / jnp.where(den > 0.0, den, 1.0)
            imp = imp + p
            probs.append(p.astype(BF16))
        oc = _dot(vct_ref[0, 0, :, :nc_e].astype(BF16), jnp.concatenate(probs, axis=1))
        for r in range(r_n):
            c = NSA_BRANCHES * r
            oc_ref[r] = oc[:, r * tq:(r + 1) * tq] * gates[c:c + 1, :]

        rs = SEL_BLOCK // CMP_STRIDE
        rc = CMP_LEN // CMP_STRIDE
        imp_ref[_IMP_PAD:_IMP_PAD + nc_e, :] = imp
        imp_blk = None
        for m in range(rs + rc - 1):
            term = imp_ref[pl.ds(_IMP_PAD - (rc - 1) + m, nb_e, stride=rs), :]
            imp_blk = term if imp_blk is None else imp_blk + term
        blk = lax.broadcasted_iota(jnp.int32, (nb_e, tq), 0)
        qp = i * tq + lax.broadcasted_iota(jnp.int32, (nb_e, tq), 1)
        causal_b = blk * SEL_BLOCK <= qp
        dist = qp // SEL_BLOCK - blk
        forced = (blk == 0) | ((dist >= 0) & (dist < N_LOCAL_BLOCKS))
        score = jnp.where(causal_b, jnp.where(forced, FORCE, imp_blk), NEG_INF)
        for _ in range(min(SEL_TOPK, nb_e)):
            mx = jnp.max(score, axis=0, keepdims=True)
            first = jnp.min(jnp.where(score == mx, blk, nb_e), axis=0, keepdims=True)
            score = jnp.where(blk == first, -jnp.inf, score)
        selt_ref[0, :nb_e, :] = jnp.where((score == -jnp.inf) & causal_b, 1.0, 0.0)
        if nb_e < nb:
            selt_ref[0, nb_e:, :] = jnp.zeros((nb - nb_e, tq), F32)

    tiles_per_part = pl.num_programs(1) // CMP_PARTS
    part = i // tiles_per_part
    for j in range(CMP_PARTS):
        pl.when(part == j)(functools.partial(tile_body, (j + 1) * nc // CMP_PARTS, (j + 1) * nb // CMP_PARTS))


def _nsa_compressed(thr, tab, nq_t, cmp, cmp_t, gates_t):
    nh, _, s = nq_t.shape
    ng = NSA_GROUPS
    tq = TQ_CMP
    nc = cmp.shape[2]
    nb = s // SEL_BLOCK
    qspec = pl.BlockSpec((NSA_HPG, HEAD_DIM, tq), lambda g, i: (g, 0, i))
    return pl.pallas_call(
        _nsa_cmp_kernel, grid=(ng, s // tq),
        in_specs=[_smem_spec(), _smem_spec(), qspec,
                  pl.BlockSpec((1, 1, nc, HEAD_DIM), lambda g, i: (0, g, 0, 0)),
                  pl.BlockSpec((1, 1, HEAD_DIM, nc), lambda g, i: (1, g, 0, 0)),
                  pl.BlockSpec((1, NSA_HPG * NSA_BRANCHES, tq), lambda g, i: (g, 0, i))],
        out_specs=(qspec, pl.BlockSpec((1, nb, tq), lambda g, i: (g, 0, i))),
        out_shape=(jax.ShapeDtypeStruct((nh, HEAD_DIM, s), F32),
                   jax.ShapeDtypeStruct((ng, nb, s), F32)),
        scratch_shapes=[pltpu.VMEM((NSA_HPG, 2 * nc, tq), F32), pltpu.VMEM((_IMP_PAD + nc, tq), F32)],
        compiler_params=_cparams("arbitrary", "arbitrary"), name="nsa_compressed",
    )(thr, tab, nq_t, cmp, cmp_t, gates_t)


_NEAR_SPAN = ((MAX_DISTANCE + TK_SEL - 2) // TQ_SEL) * TQ_SEL
assert _NEAR_SPAN >= MAX_DISTANCE and TK_SEL % TQ_SEL == 0
_STRIP = 2 * TK_SEL + _NEAR_SPAN
_WIN_KEYS = WINDOW + TQ_SEL
_BLOCKS_PER_TILE = TK_SEL // SEL_BLOCK


def _nsa_sel_kernel(thr_ref, tab_ref, qt_ref, ks_ref, vst_ref, kw_ref, vwt_ref, selt_ref, gt_ref, oc_ref,
                    o_ref, strip_ref, *bufs):
    tq, tk, r_n = TQ_SEL, TK_SEL, NSA_HPG
    g = pl.program_id(0)
    i = pl.program_id(1)
    top = tk + _NEAR_SPAN
    lanes = r_n * tq

    @pl.when(i == 0)
    def _init():
        u = lax.broadcasted_iota(jnp.int32, (_STRIP, tq), 0)
        a = lax.broadcasted_iota(jnp.int32, (_STRIP, tq), 1)
        for r in range(r_n):
            strip_ref[:, r * tq:(r + 1) * tq] = _bias_chain(a - u + top, thr_ref, tab_ref, g * r_n + r) * LOG2E

    q_t = jnp.concatenate([qt_ref[r] for r in range(r_n)], axis=1)
    pad_rows = jnp.zeros((LANES - HEAD_DIM - _BLOCKS_PER_TILE, lanes), F32)
    qpos = i * tq + (lax.broadcasted_iota(jnp.int32, (1, lanes), 1) & (tq - 1))

    ck = CHUNK_SEL

    def score_chunk(kb, c, maybe_diagonal):
        off = pl.multiple_of(kb * tk + c * ck, ck)
        boff = pl.multiple_of(kb * _BLOCKS_PER_TILE, _BLOCKS_PER_TILE)
        picked = selt_ref[0, pl.ds(boff, _BLOCKS_PER_TILE), :]
        penalty = jnp.where(picked > 0.5, 0.0, NEG_INF)
        penalty = jnp.concatenate([penalty] * r_n, axis=1)
        q_aug = jnp.concatenate([q_t, jnp.concatenate([penalty, pad_rows], axis=0).astype(BF16)], axis=0)
        soff = pl.multiple_of(jnp.clip(top - (i * tq - kb * tk), 0, top) + c * ck, math.gcd(tq, ck))
        s = _dot(ks_ref[0, pl.ds(off, ck), :], q_aug) + strip_ref[pl.ds(soff, ck), :]
        if maybe_diagonal:
            key = off + lax.broadcasted_iota(jnp.int32, (ck, lanes), 0)
            s = jnp.where(key <= qpos, s, NEG_INF)
        return s

    def value_chunk(kb, c):
        off = pl.multiple_of(kb * tk + c * ck, ck)
        return _with_ones_rows(vst_ref[0, :, pl.ds(off, ck)])

    woff = pl.multiple_of(i * tq, tq)
    w0 = top - WINDOW
    sw = _dot(kw_ref[0, pl.ds(woff, _WIN_KEYS), :], q_t) + strip_ref[w0:w0 + _WIN_KEYS, :]
    kpos = i * tq - WINDOW + lax.broadcasted_iota(jnp.int32, (_WIN_KEYS, lanes), 0)
    rel = qpos - kpos
    sw = jnp.where((rel >= 0) & (rel < WINDOW) & (kpos >= 0), sw, NEG_INF)
    pw = jnp.exp2(sw - jnp.max(sw, axis=0, keepdims=True)).astype(BF16)
    o_win = _dot(_with_ones_rows(vwt_ref[0, :, pl.ds(woff, _WIN_KEYS)]), pw)
    o_win = o_win[:HEAD_DIM] / o_win[HEAD_DIM:HEAD_DIM + 1]
    gates = gt_ref[0]
    for r in range(r_n):
        c0 = NSA_BRANCHES * r
        o_ref[r] = oc_ref[r] + gates[c0 + 2:c0 + 3, :] * o_win[:, r * tq:(r + 1) * tq]

    st = (jnp.full((1, lanes), NEG_INF, F32), jnp.zeros((HEAD_DIM + ONES_ROWS, lanes), F32))
    _, acc = _pipelined_blocks((i * tq) // tk, score_chunk, value_chunk, tk // ck, ck, jnp.exp2,
                               bufs, st)
    o_sel = acc[:HEAD_DIM] / acc[HEAD_DIM:HEAD_DIM + 1]
    for r in range(r_n):
        c0 = NSA_BRANCHES * r
        o_ref[r] += gates[c0 + 1:c0 + 2, :] * o_sel[:, r * tq:(r + 1) * tq]


def _nsa_selected(thr, tab, nq_t, ks, vs_t, kw_pad, vw_t_pad, sel_t, gates_t, oc):
    nh, _, s = nq_t.shape
    ng, tq = NSA_GROUPS, TQ_SEL
    nb = s // SEL_BLOCK
    qspec = pl.BlockSpec((NSA_HPG, HEAD_DIM, tq), lambda g, i: (g, 0, i))
    rows = lambda n: pl.BlockSpec((1, n, HEAD_DIM), lambda g, i: (g, 0, 0), pipeline_mode=pl.Buffered(1))
    cols = lambda n: pl.BlockSpec((1, HEAD_DIM, n), lambda g, i: (g, 0, 0), pipeline_mode=pl.Buffered(1))
    return pl.pallas_call(
        _nsa_sel_kernel, grid=(ng, s // tq),
        in_specs=[_smem_spec(), _smem_spec(), qspec,
                  pl.BlockSpec((1, s, LANES), lambda g, i: (g, 0, 0), pipeline_mode=pl.Buffered(1)),
                  cols(s), rows(s + WINDOW), cols(s + WINDOW),
                  pl.BlockSpec((1, nb, tq), lambda g, i: (g, 0, i)),
                  pl.BlockSpec((1, NSA_HPG * NSA_BRANCHES, tq), lambda g, i: (g, 0, i)), qspec],
        out_specs=qspec,
        out_shape=jax.ShapeDtypeStruct((nh, HEAD_DIM, s), F32),
        scratch_shapes=[pltpu.VMEM((_STRIP, NSA_HPG * tq), F32)] + [pltpu.VMEM((TK_SEL, NSA_HPG * tq), F32)] * 3
        + [pltpu.VMEM((8, NSA_HPG * tq), F32), pltpu.VMEM((HEAD_DIM + ONES_ROWS, NSA_HPG * tq), F32)],
        compiler_params=_cparams("arbitrary", "arbitrary"), name="nsa_selected",
    )(thr, tab, nq_t, ks, vs_t, kw_pad, vw_t_pad, sel_t, gates_t, oc)


def _merge_body(osb_ref, onsa_ref, omla_ref, g_ref, x_ref, wsb_ref, wnsa_ref, wmla_ref, wout_ref, fg_ref):
    def heads_proj(o_ref, w_ref):
        nh, hd = w_ref.shape[0], w_ref.shape[1]
        if o_ref.shape[1] == hd:
            o = o_ref[...].reshape(nh * hd, o_ref.shape[2]).T.astype(BF16)
            return _dot(o, w_ref[...].reshape(nh * hd, w_ref.shape[2]))
        y = _dot(o_ref[0], w_ref[0])
        for j in range(1, nh):
            y = y + _dot(o_ref[j], w_ref[j])
        return y

    merged = g_ref[:, 0:D_MODEL] * heads_proj(osb_ref, wsb_ref)
    merged = merged + g_ref[:, D_MODEL:2 * D_MODEL] * heads_proj(onsa_ref, wnsa_ref)
    merged = merged + g_ref[:, 2 * D_MODEL:3 * D_MODEL] * heads_proj(omla_ref, wmla_ref)
    x1 = x_ref[...] + _dot(merged.astype(BF16), wout_ref[...])
    ms = jnp.mean(x1 * x1, axis=-1, keepdims=True)
    return x1, x1 * lax.rsqrt(ms + EPS) * fg_ref[...]


def _merge_kernel(osb_ref, onsa_ref, omla_ref, g_ref, x_ref, wsb_ref, wnsa_ref, wmla_ref, wout_ref, fg_ref,
                  x1_ref, h_ref):
    x1, h = _merge_body(osb_ref, onsa_ref, omla_ref, g_ref, x_ref, wsb_ref, wnsa_ref, wmla_ref, wout_ref,
                        fg_ref)
    x1_ref[...] = x1
    h_ref[...] = h.astype(BF16)


def _merge_router_kernel(osb_ref, onsa_ref, omla_ref, g_ref, x_ref, wsb_ref, wnsa_ref, wmla_ref, wout_ref,
                         fg_ref, wr_ref, x1_ref, h_ref, comb_ref, sel_ref):
    x1, h = _merge_body(osb_ref, onsa_ref, omla_ref, g_ref, x_ref, wsb_ref, wnsa_ref, wmla_ref, wout_ref,
                        fg_ref)
    x1_ref[...] = x1
    h_ref[...] = h.astype(BF16)
    wr = wr_ref[...]
    hp = _split3(h)
    wp = _split3(wr)
    logits = jnp.zeros((h.shape[0], wr.shape[1]), F32)
    for a_i, b_i in ((2, 0), (0, 2), (1, 1), (1, 0), (0, 1), (0, 0)):
        logits = logits + _dot(hp[a_i], wp[b_i])
    lane = lax.broadcasted_iota(jnp.int32, logits.shape, 1)
    nl = logits.shape[1]
    logits = jnp.where(lane < N_EXPERTS, logits, -jnp.inf)
    v1 = jnp.max(logits, axis=-1, keepdims=True)
    i1 = jnp.min(jnp.where(logits == v1, lane, nl), axis=-1, keepdims=True)
    rest = jnp.where(lane == i1, -jnp.inf, logits)
    v2 = jnp.max(rest, axis=-1, keepdims=True)
    i2 = jnp.min(jnp.where(rest == v2, lane, nl), axis=-1, keepdims=True)
    e2 = jnp.exp(v2 - v1)
    w1 = 1.0 / (1.0 + e2)
    w2 = e2 / (1.0 + e2)
    comb_ref[...] = jnp.where(lane == i1, w1, jnp.where(lane == i2, w2, 0.0))
    sel_ref[...] = jnp.where((lane == i1) | (lane == i2), 1.0, 0.0)


def _merge(osb, onsa, omla, g, x, wsb, wnsa, wmla, wout, fgain, wrouter=None):
    s = x.shape[0]
    tm = TM_MERGE if wrouter is None else TM_MERGE_ROUTER

    def heads(a):
        if a.shape[1] == s:
            return pl.BlockSpec((a.shape[0], tm, a.shape[2]), lambda i: (0, i, 0))
        return pl.BlockSpec((a.shape[0], a.shape[1], tm), lambda i: (0, 0, i))

    row = lambda w: pl.BlockSpec((tm, w), lambda i: (i, 0))
    in_specs = [heads(osb), heads(onsa), heads(omla), row(GATE_W), row(D_MODEL)]
    in_specs += [_const_spec(a.shape) for a in (wsb, wnsa, wmla, wout, fgain)]
    out_shape = [jax.ShapeDtypeStruct((s, D_MODEL), F32), jax.ShapeDtypeStruct((s, D_MODEL), BF16)]
    out_specs = [row(D_MODEL), row(D_MODEL)]
    args = [osb, onsa, omla, g, x, wsb, wnsa, wmla, wout, fgain]
    if wrouter is None:
        kern = _merge_kernel
    else:
        kern = _merge_router_kernel
        in_specs.append(_const_spec(wrouter.shape))
        args.append(wrouter)
        out_shape += [jax.ShapeDtypeStruct((s, LANES), F32)] * 2
        out_specs += [row(LANES)] * 2
    return pl.pallas_call(
        kern, grid=(s // tm,), in_specs=in_specs, out_specs=tuple(out_specs), out_shape=tuple(out_shape),
        compiler_params=_cparams("parallel"), name="merge",
    )(*args)


def _ffn_kernel(x_ref, h_ref, wg_ref, wu_ref, wd_ref, o_ref, acc_ref):
    f = pl.program_id(1)

    @pl.when(f == 0)
    def _():
        acc_ref[...] = x_ref[...]

    h = h_ref[...]
    gate = _dot(h, wg_ref[...])
    up = _dot(h, wu_ref[...])
    act = (gate * jax.nn.sigmoid(gate) * up).astype(BF16)
    acc_ref[...] += _dot(act, wd_ref[...])

    @pl.when(f == pl.num_programs(1) - 1)
    def _():
        o_ref[...] = acc_ref[...]


def _dense_ffn(x1, h, wg, wu, wd, tf):
    s = x1.shape[0]
    tm = TM_FFN
    nf = wg.shape[1] // tf
    return pl.pallas_call(
        _ffn_kernel, grid=(s // tm, nf),
        in_specs=[pl.BlockSpec((tm, D_MODEL), lambda i, f: (i, 0)),
                  pl.BlockSpec((tm, D_MODEL), lambda i, f: (i, 0)),
                  pl.BlockSpec((D_MODEL, tf), lambda i, f: (0, f)),
                  pl.BlockSpec((D_MODEL, tf), lambda i, f: (0, f)),
                  pl.BlockSpec((tf, D_MODEL), lambda i, f: (f, 0))],
        out_specs=pl.BlockSpec((tm, D_MODEL), lambda i, f: (i, 0)),
        out_shape=jax.ShapeDtypeStruct((s, D_MODEL), F32),
        scratch_shapes=[pltpu.VMEM((tm, D_MODEL), F32)],
        compiler_params=_cparams("parallel", "arbitrary"), name="dense_ffn",
    )(x1, h, wg, wu, wd)


def _permute_kernel(tile_ref, chunk_ref, flag_ref, exp_ref, dest_ref, cw_ref, h_ref, xg_ref, rw_ref, acc_ref,
                    wacc_ref):
    k = pl.program_id(0)
    tm = TM_MOE
    flags = flag_ref[k]

    @pl.when((flags & 1) != 0)
    def _():
        acc_ref[...] = jnp.zeros_like(acc_ref)
        wacc_ref[...] = jnp.zeros_like(wacc_ref)

    @pl.when((flags & 4) != 0)
    def _():
        row = tile_ref[k] * tm + lax.broadcasted_iota(jnp.int32, (tm, tm), 0)
        hit = dest_ref[0] == row
        acc_ref[...] += _dot(jnp.where(hit, 1.0, 0.0).astype(BF16), h_ref[...])
        wacc_ref[...] += jnp.sum(jnp.where(hit, cw_ref[0], 0.0), axis=-1, keepdims=True)

    @pl.when((flags & 2) != 0)
    def _():
        xg_ref[...] = acc_ref[...].astype(BF16)
        rw_ref[...] = wacc_ref[...]


def _moe_ffn_kernel(te_ref, tv_ref, xg_ref, rw_ref, wg_ref, wu_ref, wd_ref, y_ref, acc_ref):
    t = pl.program_id(0)
    f = pl.program_id(1)

    @pl.when(tv_ref[t] != 0)
    def _():
        @pl.when(f == 0)
        def _():
            acc_ref[...] = jnp.zeros_like(acc_ref)

        x = xg_ref[...]
        gate = _dot(x, wg_ref[0])
        up = _dot(x, wu_ref[0])
        act = (gate * jax.nn.sigmoid(gate) * up * rw_ref[...]).astype(BF16)
        acc_ref[...] += _dot(act, wd_ref[0])

        @pl.when(f == pl.num_programs(1) - 1)
        def _():
            y_ref[...] = acc_ref[...].astype(BF16)

    @pl.when(tv_ref[t] == 0)
    def _():
        y_ref[...] = jnp.zeros_like(y_ref)


def _unpermute_kernel(tile_ref, chunk_ref, flag_ref, grow_ref, x_ref, y_ref, o_ref):
    k = pl.program_id(0)
    tm = TM_MOE
    flags = flag_ref[k]

    @pl.when((flags & 1) != 0)
    def _():
        o_ref[...] = x_ref[...]

    @pl.when((flags & 4) != 0)
    def _():
        tu = y_ref.shape[0]
        row = chunk_ref[k] * tu + lax.broadcasted_iota(jnp.int32, (tm, tu), 1)
        hit = (grow_ref[:, 0:1] == row) | (grow_ref[:, 1:2] == row)
        o_ref[...] += _dot(jnp.where(hit, 1.0, 0.0).astype(BF16), y_ref[...])


def _work_list(first, last, n_items, min_one=True):
    n_tiles = first.shape[0]
    cnt_real = jnp.maximum(last - first + 1, 0)
    cnt = jnp.maximum(cnt_real, 1) if min_one else cnt_real
    ends = jnp.cumsum(cnt)
    starts = ends - cnt
    total = ends[-1]
    k = jnp.arange(n_items, dtype=jnp.int32)
    kk = jnp.minimum(k, total - 1)
    tile = jnp.minimum(jnp.sum(kk[:, None] >= ends[None, :], axis=1), n_tiles - 1).astype(jnp.int32)
    pos = kk - starts[tile]
    chunk = jnp.where(cnt_real[tile] > 0, first[tile] + pos, 0).astype(jnp.int32)
    live = k < total
    flags = (jnp.where(live & (pos == 0), 1, 0) | jnp.where(live & (pos == cnt[tile] - 1), 2, 0)
             | jnp.where(live & (pos < cnt_real[tile]), 4, 0))
    return tile, chunk, flags.astype(jnp.int32)


def _moe_ffn(x1, h, comb, selm, wg, wu, wd):
    s = x1.shape[0]
    tm, tf = TM_MOE, TF_MOE
    ne = N_EXPERTS
    n_chunks = s // tm
    n_tiles = 2 * s // tm + ne
    rows = n_tiles * tm
    sel_t = selm[:, :ne].T.astype(jnp.int32)
    cum = jnp.cumsum(sel_t, axis=1)
    counts = cum[:, -1]
    tiles_e = (counts + tm - 1) // tm
    tile_end = jnp.cumsum(tiles_e)
    tile_start = tile_end - tiles_e
    used = tile_end[-1]
    dest = jnp.where(sel_t > 0, tile_start[:, None] * tm + cum - 1, -1).astype(jnp.int32)
    t_idx = jnp.arange(n_tiles, dtype=jnp.int32)
    te_raw = jnp.minimum(jnp.sum(t_idx[:, None] >= tile_end[None, :], axis=1), ne - 1).astype(jnp.int32)
    tvalid = (t_idx < used).astype(jnp.int32)
    last_e = te_raw[jnp.maximum(used - 1, 0)]
    te = jnp.where(tvalid > 0, te_raw, last_e).astype(jnp.int32)
    dest_tiles = dest.reshape(ne, n_chunks, tm)
    big = jnp.int32(2 ** 30)
    d_max = jnp.max(dest_tiles, axis=2)
    d_min = jnp.min(jnp.where(dest_tiles >= 0, dest_tiles, big), axis=2)
    overlap = (d_max[te] >= t_idx[:, None] * tm) & (d_min[te] < (t_idx[:, None] + 1) * tm) & (tvalid[:, None] > 0)
    c_idx = jnp.arange(n_chunks, dtype=jnp.int32)[None, :]
    c_lo = jnp.min(jnp.where(overlap, c_idx, n_chunks), axis=1)
    c_hi = jnp.max(jnp.where(overlap, c_idx, -1), axis=1)
    c_lo = jnp.where(c_hi >= 0, c_lo, 0).astype(jnp.int32)
    p_tile, p_chunk, p_flag = _work_list(c_lo, c_hi.astype(jnp.int32), n_tiles + ne * n_chunks)
    p_exp = te[p_tile]

    cw_t = comb[:, :ne].T.reshape(ne, 1, s)
    dest3 = dest.reshape(ne, 1, s)
    xg, roww = pl.pallas_call(
        _permute_kernel,
        grid_spec=pltpu.PrefetchScalarGridSpec(
            num_scalar_prefetch=4, grid=(p_tile.shape[0],),
            in_specs=[pl.BlockSpec((1, 1, tm), lambda k, pt, pc, pf, pe: (pe[k], 0, pc[k])),
                      pl.BlockSpec((1, 1, tm), lambda k, pt, pc, pf, pe: (pe[k], 0, pc[k])),
                      pl.BlockSpec((tm, D_MODEL), lambda k, pt, pc, pf, pe: (pc[k], 0))],
            out_specs=(pl.BlockSpec((tm, D_MODEL), lambda k, pt, pc, pf, pe: (pt[k], 0)),
                       pl.BlockSpec((tm, 1), lambda k, pt, pc, pf, pe: (pt[k], 0))),
            scratch_shapes=[pltpu.VMEM((tm, D_MODEL), F32), pltpu.VMEM((tm, 1), F32)]),
        out_shape=(jax.ShapeDtypeStruct((rows, D_MODEL), BF16), jax.ShapeDtypeStruct((rows, 1), F32)),
        compiler_params=_cparams("arbitrary"), name="moe_permute",
    )(p_tile, p_chunk, p_flag, p_exp, dest3, cw_t, h)

    nf = wg.shape[2] // tf
    last_f = nf - 1
    fsel = lambda t, f, tv: f * tv[t] + last_f * (1 - tv[t])
    yg = pl.pallas_call(
        _moe_ffn_kernel,
        grid_spec=pltpu.PrefetchScalarGridSpec(
            num_scalar_prefetch=2, grid=(n_tiles, nf),
            in_specs=[pl.BlockSpec((tm, D_MODEL), lambda t, f, te_, tv: (t, 0)),
                      pl.BlockSpec((tm, 1), lambda t, f, te_, tv: (t, 0)),
                      pl.BlockSpec((1, D_MODEL, tf), lambda t, f, te_, tv: (te_[t], 0, fsel(t, f, tv))),
                      pl.BlockSpec((1, D_MODEL, tf), lambda t, f, te_, tv: (te_[t], 0, fsel(t, f, tv))),
                      pl.BlockSpec((1, tf, D_MODEL), lambda t, f, te_, tv: (te_[t], fsel(t, f, tv), 0))],
            out_specs=pl.BlockSpec((tm, D_MODEL), lambda t, f, te_, tv: (t, 0)),
            scratch_shapes=[pltpu.VMEM((tm, D_MODEL), F32)]),
        out_shape=jax.ShapeDtypeStruct((rows, D_MODEL), BF16),
        compiler_params=_cparams("arbitrary", "arbitrary"), name="moe_ffn",
    )(te, tvalid, xg, roww, wg, wu, wd)

    grow = jnp.stack([jnp.min(jnp.where(dest >= 0, dest, big), axis=0), jnp.max(dest, axis=0)], axis=1)
    has = d_max >= 0
    tu = TU_MOE
    first = jnp.where(has, d_min // tu, 0).T.reshape(-1).astype(jnp.int32)
    last = jnp.where(has, d_max // tu, -1).T.reshape(-1).astype(jnp.int32)
    u_sub, u_chunk, u_flag = _work_list(first, last, rows // tu + ne * n_chunks, min_one=False)
    u_tile = (u_sub // ne).astype(jnp.int32)
    prev_tile = jnp.concatenate([jnp.full((1,), -1, jnp.int32), u_tile[:-1]])
    u_flag = (u_flag & 4) | jnp.where((u_flag != 0) & (u_tile != prev_tile), 1, 0)
    out = pl.pallas_call(
        _unpermute_kernel,
        grid_spec=pltpu.PrefetchScalarGridSpec(
            num_scalar_prefetch=3, grid=(u_tile.shape[0],),
            in_specs=[pl.BlockSpec((tm, 2), lambda k, ut, uc, uf: (ut[k], 0)),
                      pl.BlockSpec((tm, D_MODEL), lambda k, ut, uc, uf: (ut[k], 0)),
                      pl.BlockSpec((tu, D_MODEL), lambda k, ut, uc, uf: (uc[k], 0))],
            out_specs=pl.BlockSpec((tm, D_MODEL), lambda k, ut, uc, uf: (ut[k], 0))),
        out_shape=jax.ShapeDtypeStruct((s, D_MODEL), F32),
        compiler_params=_cparams("arbitrary"), name="moe_unpermute",
    )(u_tile, u_chunk, u_flag, grow, x1, yg)
    return out


def _t5_bucket(rel):
    n = jnp.maximum(rel, 0)
    max_exact = NUM_BUCKETS // 2
    nf = jnp.maximum(n, 1).astype(jnp.float32)
    large = max_exact + (jnp.log(nf / max_exact) / math.log(MAX_DISTANCE / max_exact)
                         * (NUM_BUCKETS - max_exact)).astype(jnp.int32)
    large = jnp.minimum(large, NUM_BUCKETS - 1)
    return jnp.where(n < max_exact, n, large)


def _bucket_thresholds():
    buckets = _t5_bucket(jnp.arange(MAX_DISTANCE + 1, dtype=jnp.int32))
    b = jnp.arange(NUM_BUCKETS, dtype=jnp.int32)
    return jnp.sum(buckets[None, :] < b[:, None], axis=1).astype(jnp.int32)


def _rope_tables(s):
    half = MLA_ROPE // 2
    inv = ROPE_THETA ** (-jnp.arange(half, dtype=jnp.float32) / half)
    ang = jnp.arange(s).astype(jnp.float32)[:, None] * inv[None, :]
    cos, sin = jnp.cos(ang), jnp.sin(ang)
    z = lambda w: jnp.zeros((s, w), F32)
    pad = MLA_PAD - MLA_QK
    cos_t = jnp.concatenate([jnp.ones((s, MLA_NOPE), F32), cos, cos, z(pad)], axis=1)
    sin_lo = jnp.concatenate([z(MLA_NOPE), -sin, z(half), z(pad)], axis=1)
    sin_hi = jnp.concatenate([z(MLA_NOPE), z(half), sin, z(pad)], axis=1)
    return cos_t, sin_lo, sin_hi


def _pad_lanes(a, width):
    return jnp.pad(a, [(0, 0)] * (a.ndim - 1) + [(0, width - a.shape[-1])])


def kernel(x, rel_bias_table, mix_norm, w_in, b_gate, sb_w_o, nsa_q_norm, nsa_k_norm, nsa_cmp_pos, nsa_cmp_w1, nsa_cmp_w2, nsa_w_o, mla_q_a_norm, mla_kv_a_norm, mla_w_uq, mla_w_ukv, mla_q_norm, mla_k_norm, mla_w_o, w_out, ffn_norm, dense_w_gate, dense_w_up, dense_w_down, moe_router, moe_w_gate, moe_w_up, moe_w_down):
    b, s, d = x.shape
    assert b == 1 and d == D_MODEL and s % TM_MOE == 0 and s // SEL_BLOCK >= SEL_TOPK
    depth = w_in.shape[0]
    xs = x.reshape(s, d)
    thr = _bucket_thresholds()
    cos_t, sin_lo, sin_hi = _rope_tables(s)
    n_chunk = s // CMP_STRIDE

    for layer in range(depth):
        w = w_in[layer]
        small = jnp.concatenate([
            w[:, _C_NG:_C_MCQ], jnp.zeros((d, _KR_LANE - NSA_HEADS * NSA_BRANCHES), F32),
            w[:, _C_MKR:_C_GATE], jnp.zeros((d, LANES - _KR_LANE - MLA_ROPE), F32)], axis=1)
        wa = jnp.concatenate([w[:, :_C_NG], w[:, _C_MCQ:_C_MKR], small], axis=1).astype(BF16)
        wg = w[:, _C_GATE:].astype(BF16)
        wuq = _pad_lanes(mla_w_uq[layer].reshape(MLA_Q_LORA, MLA_HEADS, MLA_QK), MLA_PAD)
        wuq = wuq.reshape(MLA_Q_LORA, MLA_HEADS * MLA_PAD).astype(BF16)
        wukv = mla_w_ukv[layer].reshape(MLA_KV_LORA, MLA_HEADS, MLA_NOPE + MLA_V)
        wukv = jnp.concatenate([
            _pad_lanes(wukv[:, :, :MLA_NOPE], MLA_PAD).reshape(MLA_KV_LORA, MLA_HEADS * MLA_PAD),
            wukv[:, :, MLA_NOPE:].reshape(MLA_KV_LORA, MLA_HEADS * MLA_V)], axis=1).astype(BF16)
        (sbq, sbk, sbv, nq, nkvc, nks, nvs, nkw, nvw, gsm, mq, mk, mv, g) = _input_projection(
            xs, mix_norm[layer][None], wa, wg, b_gate[layer][None], wuq, wukv,
            nsa_q_norm[layer][None], nsa_k_norm[layer], mla_q_a_norm[layer][None],
            mla_kv_a_norm[layer][None], _pad_lanes(mla_q_norm[layer][None], MLA_PAD),
            _pad_lanes(mla_k_norm[layer][None], MLA_PAD), cos_t, sin_lo, sin_hi)

        o_sb = _stick_breaking(sbq, sbk, sbv)
        o_mla = _mla_attention(mq, mk, mv)

        x2 = nkvc.reshape(2, NSA_GROUPS, n_chunk, CMP_STRIDE * HEAD_DIM)
        cmp, cmp_t = _compress(x2, nsa_cmp_pos[layer].reshape(2, 1, CMP_LEN * HEAD_DIM),
                               nsa_cmp_w1[layer].astype(BF16), nsa_cmp_w2[layer].astype(BF16),
                               nsa_k_norm[layer][0:1])
        gates = gsm[:NSA_HEADS * NSA_BRANCHES].reshape(NSA_GROUPS, NSA_HPG * NSA_BRANCHES, s)
        oc, sel = _nsa_compressed(thr, rel_bias_table, nq, cmp, cmp_t, gates)
        o_nsa = _nsa_selected(thr, rel_bias_table, nq, nks, nvs,
                              jnp.pad(nkw, ((0, 0), (WINDOW, 0), (0, 0))),
                              jnp.pad(nvw, ((0, 0), (0, 0), (WINDOW, 0))), sel, gates, oc)

        wsb = sb_w_o[layer].reshape(SB_HEADS, HEAD_DIM, d).astype(BF16)
        wnsa = nsa_w_o[layer].reshape(NSA_HEADS, HEAD_DIM, d).astype(BF16)
        wmla = mla_w_o[layer].reshape(MLA_HEADS, MLA_V, d).astype(BF16)
        wout = w_out[layer].astype(BF16)
        j = layer // 2
        if layer % 2 == 0:
            x1, h = _merge(o_sb, o_nsa, o_mla, g, xs, wsb, wnsa, wmla, wout, ffn_norm[layer][None])
            d_ff = dense_w_gate.shape[2]
            xs = _dense_ffn(x1, h, dense_w_gate[j].astype(BF16), dense_w_up[j].astype(BF16),
                            dense_w_down[j].astype(BF16), d_ff // 2)
        else:
            x1, h, comb, selm = _merge(o_sb, o_nsa, o_mla, g, xs, wsb, wnsa, wmla, wout,
                                       ffn_norm[layer][None], _pad_lanes(moe_router[j], LANES))
            xs = _moe_ffn(x1, h, comb, selm, moe_w_gate[j].astype(BF16), moe_w_up[j].astype(BF16),
                          moe_w_down[j].astype(BF16))
    return xs.reshape(b, s, d)
```

```python
import functools
import math

import jax
import jax.numpy as jnp
from jax import lax
from jax.experimental import pallas as pl
from jax.experimental.pallas import tpu as pltpu

F32 = jnp.float32
BF16 = jnp.bfloat16

D_MODEL = 1024
HEAD_DIM = 64
SB_HEADS = 4
NSA_HEADS = 8
NSA_GROUPS = 2
NSA_HPG = NSA_HEADS // NSA_GROUPS
NSA_BRANCHES = 3
MLA_HEADS = 4
MLA_NOPE = 64
MLA_ROPE = 32
MLA_V = 64
MLA_QK = MLA_NOPE + MLA_ROPE
MLA_Q_LORA = 256
MLA_KV_LORA = 128
ROPE_THETA = 10000.0
CMP_LEN = 32
CMP_STRIDE = 16
CMP_HIDDEN = 256
SEL_BLOCK = 64
SEL_TOPK = 16
N_LOCAL_BLOCKS = 2
WINDOW = 512
NUM_BUCKETS = 32
MAX_DISTANCE = 1024
N_MIXERS = 3
N_EXPERTS = 8
EPS = 1e-6
NEG_INF = -1e30
FORCE = 1e30
LOG2E = math.log2(math.e)

SB_W = SB_HEADS * HEAD_DIM
NSA_QW = NSA_HEADS * HEAD_DIM
NSA_KVW = NSA_GROUPS * HEAD_DIM
GATE_W = N_MIXERS * D_MODEL
LANES = 128
MLA_PAD = LANES

_C_NQ = 3 * SB_W
_C_NKV = _C_NQ + NSA_QW
_C_NG = _C_NKV + 6 * NSA_KVW
_C_MCQ = _C_NG + NSA_HEADS * NSA_BRANCHES
_C_MCKV = _C_MCQ + MLA_Q_LORA
_C_MKR = _C_MCKV + MLA_KV_LORA
_C_GATE = _C_MKR + MLA_ROPE
_A_SB = 0
_A_NQ = 3 * SB_W
_A_NKV = _A_NQ + NSA_QW
_A_MCQ = _A_NKV + 6 * NSA_KVW
_A_MCKV = _A_MCQ + MLA_Q_LORA
_A_SMALL = _A_MCKV + MLA_KV_LORA
_A_COLS = _A_SMALL + LANES
_KR_LANE = MLA_NOPE

TM_PROJ = 256
TQ_SB = 512
TK_SB = 256
TQ_MLA = 256
TK_MLA = 1024
CHUNK_MLA = 256
CHUNK_SEL = 512
TQ_CMP = 128
CMP_PARTS = 4
TQ_SEL = 256
TK_SEL = 512
TM_MERGE = 512
TM_MERGE_ROUTER = 256
TM_FFN = 512
TM_MOE = 512
TF_MOE = 512
TU_MOE = 256
VMEM_LIMIT = 56 * 1024 * 1024
SB_EXIT = -104.0

_NT = (((1,), (1,)), ((), ()))


def _dot(a, b):
    return jnp.dot(a, b, preferred_element_type=F32)


def _dot_nt(a, b):
    return lax.dot_general(a, b, _NT, preferred_element_type=F32)


def _cparams(*sem):
    return pltpu.CompilerParams(dimension_semantics=sem, vmem_limit_bytes=VMEM_LIMIT)


def _const_spec(shape):
    nd = len(shape)
    return pl.BlockSpec(shape, lambda *_: (0,) * nd, pipeline_mode=pl.Buffered(1))


def _smem_spec():
    return pl.BlockSpec(memory_space=pltpu.SMEM)


def _split3(x):
    h1 = x.astype(BF16)
    r1 = x - h1.astype(F32)
    h2 = r1.astype(BF16)
    h3 = (r1 - h2.astype(F32)).astype(BF16)
    return h1, h2, h3


def _bias_chain(rel, thr_ref, tab_ref, head):
    out = jnp.full(rel.shape, tab_ref[0, head], F32)
    for b in range(1, NUM_BUCKETS):
        out = jnp.where(rel >= thr_ref[b], tab_ref[b, head], out)
    return out


def _rope(y, cos, sin_lo, sin_hi):
    half = MLA_ROPE // 2
    return y * cos + pltpu.roll(y, half, 1) * sin_hi + pltpu.roll(y, MLA_PAD - half, 1) * sin_lo


def _proj_kernel(x_ref, gain_ref, wa_ref, wg_ref, bg_ref, wuq_ref, wukv_ref, nqg_ref, nkg_ref,
                 qag_ref, kvag_ref, mqg_ref, mkg_ref, cos_ref, slo_ref, shi_ref,
                 sbq_ref, sbk_ref, sbv_ref, nq_ref, nkvc_ref, nks_ref, nvs_ref, nkw_ref, nvw_ref,
                 gsm_ref, mq_ref, mk_ref, mv_ref, g_ref):
    x = x_ref[...]
    ms = jnp.mean(x * x, axis=-1, keepdims=True)
    h = (x * lax.rsqrt(ms + EPS) * gain_ref[...]).astype(BF16)

    def rms(p, gain, width):
        m = jnp.sum(p * p, axis=-1, keepdims=True) * (1.0 / width)
        return p * lax.rsqrt(m + EPS) * gain

    p = _dot(h, wa_ref[:, _A_SB:_A_SB + 3 * SB_W])
    sbq_t = (p[:, :SB_W] * HEAD_DIM ** -0.5).T.astype(BF16)
    sbv_t = p[:, 2 * SB_W:3 * SB_W].T.astype(BF16)
    for j in range(SB_HEADS):
        sbq_ref[j] = sbq_t[j * HEAD_DIM:(j + 1) * HEAD_DIM]
        sbv_ref[j] = sbv_t[j * HEAD_DIM:(j + 1) * HEAD_DIM]
    for j in range(SB_HEADS // 2):
        sbk_ref[j] = p[:, SB_W + j * LANES:SB_W + (j + 1) * LANES].astype(BF16)

    p = _dot(h, wa_ref[:, _A_NQ:_A_NQ + NSA_QW])
    qn = [rms(p[:, j * HEAD_DIM:(j + 1) * HEAD_DIM], nqg_ref[...], HEAD_DIM) * (HEAD_DIM ** -0.5 * LOG2E)
          for j in range(NSA_HEADS)]
    qn_t = jnp.concatenate(qn, axis=1).T.astype(BF16)
    for j in range(NSA_HEADS):
        nq_ref[j] = qn_t[j * HEAD_DIM:(j + 1) * HEAD_DIM]

    p = _dot(h, wa_ref[:, _A_NKV:_A_NKV + 6 * NSA_KVW])
    vs_t = p[:, 3 * NSA_KVW:4 * NSA_KVW].T.astype(BF16)
    tag_shape = (x.shape[0], LANES - HEAD_DIM)
    block_id = (pl.program_id(0) * x.shape[0] + lax.broadcasted_iota(jnp.int32, tag_shape, 0)) // SEL_BLOCK
    block_tag = jnp.where(lax.broadcasted_iota(jnp.int32, tag_shape, 1) == block_id % (TK_SEL // SEL_BLOCK),
                          1.0, 0.0)
    vw_t = p[:, 5 * NSA_KVW:6 * NSA_KVW].T.astype(BF16)
    for j in range(NSA_GROUPS):
        def seg(s):
            return p[:, s * NSA_KVW + j * HEAD_DIM:s * NSA_KVW + (j + 1) * HEAD_DIM]
        nkvc_ref[0, j] = seg(0)
        nkvc_ref[1, j] = seg(1)
        nks_ref[j] = jnp.concatenate([rms(seg(2), nkg_ref[1:2, :], HEAD_DIM), block_tag], axis=1).astype(BF16)
        nvs_ref[j] = vs_t[j * HEAD_DIM:(j + 1) * HEAD_DIM]
        nkw_ref[j] = rms(seg(4), nkg_ref[2:3, :], HEAD_DIM).astype(BF16)
        nvw_ref[j] = vw_t[j * HEAD_DIM:(j + 1) * HEAD_DIM]

    p = _dot(h, wa_ref[:, _A_MCQ:_A_COLS])
    cq = rms(p[:, :MLA_Q_LORA], qag_ref[...], MLA_Q_LORA).astype(BF16)
    ckv = rms(p[:, MLA_Q_LORA:MLA_Q_LORA + MLA_KV_LORA], kvag_ref[...], MLA_KV_LORA).astype(BF16)
    small = p[:, MLA_Q_LORA + MLA_KV_LORA:]
    gsm_ref[...] = jax.nn.sigmoid(small).T
    lane = lax.broadcasted_iota(jnp.int32, small.shape, 1)
    krope = jnp.where((lane >= _KR_LANE) & (lane < _KR_LANE + MLA_ROPE), small, 0.0)
    cos, slo, shi = cos_ref[...], slo_ref[...], shi_ref[...]
    qu = _dot(cq, wuq_ref[...])
    kvu = _dot(ckv, wukv_ref[...])
    for j in range(MLA_HEADS):
        qh = rms(qu[:, j * MLA_PAD:(j + 1) * MLA_PAD], mqg_ref[...], MLA_QK)
        mq_ref[j] = (_rope(qh, cos, slo, shi) * MLA_QK ** -0.5).T.astype(BF16)
        kh = rms(kvu[:, j * MLA_PAD:(j + 1) * MLA_PAD] + krope, mkg_ref[...], MLA_QK)
        mk_ref[j] = _rope(kh, cos, slo, shi).astype(BF16)
    v_t = kvu[:, MLA_HEADS * MLA_PAD:].T.astype(BF16)
    for j in range(MLA_HEADS):
        mv_ref[j] = v_t[j * MLA_V:(j + 1) * MLA_V]

    g_ref[...] = jax.nn.sigmoid(_dot(h, wg_ref[...]) + bg_ref[...])


def _input_projection(x, gain, wa, wg, bg, wuq, wukv, nqg, nkg, qag, kvag, mqg, mkg, cos, slo, shi):
    s = x.shape[0]
    tm = TM_PROJ
    row = lambda w: pl.BlockSpec((tm, w), lambda i: (i, 0))
    heads = lambda n, w: pl.BlockSpec((n, tm, w), lambda i: (0, i, 0))
    hs = lambda n, w, dt: jax.ShapeDtypeStruct((n, s, w), dt)
    in_specs = [row(D_MODEL)] + [_const_spec(a.shape) for a in
                                 (gain, wa, wg, bg, wuq, wukv, nqg, nkg, qag, kvag, mqg, mkg)]
    in_specs += [row(MLA_PAD)] * 3
    hst = lambda n, w, dt: jax.ShapeDtypeStruct((n, w, s), dt)
    heads_t = lambda n, w: pl.BlockSpec((n, w, tm), lambda i: (0, 0, i))
    ng, hd = NSA_GROUPS, HEAD_DIM
    out_shape = (hst(SB_HEADS, hd, BF16), hs(SB_HEADS // 2, LANES, BF16), hst(SB_HEADS, hd, BF16))
    out_shape += (hst(NSA_HEADS, hd, BF16),)
    out_shape += (jax.ShapeDtypeStruct((2, ng, s, hd), F32),) + (hs(ng, LANES, BF16), hst(ng, hd, BF16), hs(ng, hd, BF16), hst(ng, hd, BF16))
    out_shape += (jax.ShapeDtypeStruct((LANES, s), F32),)
    out_shape += (hst(MLA_HEADS, MLA_PAD, BF16), hs(MLA_HEADS, MLA_PAD, BF16), hst(MLA_HEADS, MLA_V, BF16))
    out_shape += (jax.ShapeDtypeStruct((s, GATE_W), F32),)
    out_specs = (heads_t(SB_HEADS, hd), heads(SB_HEADS // 2, LANES), heads_t(SB_HEADS, hd))
    out_specs += (heads_t(NSA_HEADS, hd),)
    out_specs += (pl.BlockSpec((2, ng, tm, hd), lambda i: (0, 0, i, 0)),) + (heads(ng, LANES), heads_t(ng, hd), heads(ng, hd), heads_t(ng, hd))
    out_specs += (pl.BlockSpec((LANES, tm), lambda i: (0, i)),)
    out_specs += (heads_t(MLA_HEADS, MLA_PAD), heads(MLA_HEADS, MLA_PAD), heads_t(MLA_HEADS, MLA_V), row(GATE_W))
    return pl.pallas_call(
        _proj_kernel, grid=(s // tm,), in_specs=in_specs, out_specs=out_specs, out_shape=out_shape,
        compiler_params=_cparams("parallel"), name="input_projection",
    )(x, gain, wa, wg, bg, wuq, wukv, nqg, nkg, qag, kvag, mqg, mkg, cos, slo, shi)


def _sb_kernel(qt_ref, k2_ref, vt_ref, o_ref):
    tq, tk = TQ_SB, TK_SB
    nh = qt_ref.shape[0]
    i = pl.program_id(0)
    later = (lax.broadcasted_iota(jnp.int32, (tk, tk), 1)
             > lax.broadcasted_iota(jnp.int32, (tk, tk), 0)).astype(BF16)
    key = lax.broadcasted_iota(jnp.int32, (tk, tq), 0)
    qry = i * tq + lax.broadcasted_iota(jnp.int32, (tk, tq), 1)
    zeros = jnp.zeros((HEAD_DIM, tq), BF16)
    q_aug = [jnp.concatenate([qt_ref[h], zeros] if h % 2 == 0 else [zeros, qt_ref[h]], axis=0) for h in range(nh)]

    def tile(kt, h, carry, acc, diagonal):
        off = pl.multiple_of(kt * tk, tk)
        z = _dot(k2_ref[h // 2, pl.ds(off, tk), :], q_aug[h])
        log_stay = -(jnp.maximum(z, 0.0) + jnp.log(1.0 + jnp.exp(-jnp.abs(z))))
        if diagonal:
            causal = off + key < qry
            log_stay = jnp.where(causal, log_stay, 0.0)
        hi = log_stay.astype(BF16)
        lo = (log_stay - hi.astype(F32)).astype(BF16)
        suffix = _dot(later, hi) + _dot(later, lo)
        logw = z + log_stay + (carry + suffix)
        if diagonal:
            logw = jnp.where(causal, logw, NEG_INF)
        w = jnp.exp(logw).astype(BF16)
        acc = acc + _dot(vt_ref[h, :, pl.ds(off, tk)], w)
        carry = carry + suffix[0:1, :] + log_stay[0:1, :]
        return carry, acc

    def all_heads(kt, st, diagonal):
        out = [tile(kt, h, st[2 * h], st[2 * h + 1], diagonal) for h in range(nh)]
        return tuple(x for pair in out for x in pair)

    ratio = tq // tk
    st = (jnp.zeros((1, tq), F32), jnp.zeros((HEAD_DIM, tq), F32)) * nh
    for d in range(ratio):
        st = all_heads(ratio * i + ratio - 1 - d, st, True)

    def cond(c):
        top = c[1]
        for h in range(1, nh):
            top = jnp.maximum(top, c[1 + 2 * h])
        return jnp.logical_and(c[0] >= 0, jnp.max(top) > SB_EXIT)

    def body(c):
        return (c[0] - 1,) + all_heads(c[0], c[1:], False)

    c = lax.while_loop(cond, body, (ratio * i - 1,) + st)
    for h in range(nh):
        o_ref[h] = c[2 + 2 * h]


def _stick_breaking(q_t, k2, v_t):
    nh, _, s = q_t.shape
    tq = TQ_SB
    whole = lambda a: pl.BlockSpec(a.shape, lambda i: (0, 0, 0), pipeline_mode=pl.Buffered(1))
    return pl.pallas_call(
        _sb_kernel, grid=(s // tq,),
        in_specs=[pl.BlockSpec((nh, HEAD_DIM, tq), lambda i: (0, 0, i)), whole(k2), whole(v_t)],
        out_specs=pl.BlockSpec((nh, HEAD_DIM, tq), lambda i: (0, 0, i)),
        out_shape=jax.ShapeDtypeStruct((nh, HEAD_DIM, s), F32),
        compiler_params=_cparams("parallel"), name="stick_breaking",
    )(q_t, k2, v_t)


ONES_ROWS = 16


def _with_ones_rows(v_t):
    return jnp.concatenate([v_t, jnp.ones((ONES_ROWS, v_t.shape[1]), v_t.dtype)], axis=0)


def _pipelined_blocks(n_last, score_chunk, value_chunk, n_chunks, chunk, exp_fn, bufs, st):
    rows = lambda c: pl.ds(c * chunk, chunk)
    bufs, stat_ref, acc_ref = bufs[:3], bufs[3], bufs[4]
    m_row = 3
    stat_ref[m_row:m_row + 1, :] = st[0]
    acc_ref[...] = st[1]

    def prepare(kb, slot, maybe_diagonal):
        mx = None
        for c in range(n_chunks):
            s = score_chunk(kb, c, maybe_diagonal)
            bufs[slot][rows(c), :] = s
            smax = jnp.max(s, axis=0, keepdims=True)
            mx = smax if mx is None else jnp.maximum(mx, smax)
        stat_ref[slot:slot + 1, :] = mx

    def consume(kb, slot):
        m = stat_ref[m_row:m_row + 1, :]
        m_new = jnp.maximum(m, stat_ref[slot:slot + 1, :])
        stat_ref[m_row:m_row + 1, :] = m_new
        part = None
        for c in range(n_chunks):
            p = exp_fn(bufs[slot][rows(c), :] - m_new).astype(BF16)
            d = _dot(value_chunk(kb, c), p)
            part = d if part is None else part + d
        acc_ref[...] = exp_fn(m - m_new) * acc_ref[...] + part

    def run(kb0, n_blocks, last_is_diagonal):
        n_prep = n_blocks if last_is_diagonal else n_blocks + 2
        for j in range(n_blocks):
            if j + 2 < n_prep:
                prepare(kb0 + j + 2, (j + 2) % 3, last_is_diagonal and j + 2 == n_blocks - 1)
            consume(kb0 + j, j % 3)

    trips = jnp.maximum(n_last - 2, 0) // 3
    prepare(0, 0, True)
    prepare(1, 1, True)

    def trip(t, carry):
        run(3 * t, 3, False)
        return carry

    lax.fori_loop(0, trips, trip, 0)
    kb0 = 3 * trips
    for r in range(5):
        pl.when(n_last - kb0 == r)(functools.partial(run, kb0, r + 1, True))
    return stat_ref[m_row:m_row + 1, :], acc_ref[...]


def _mla_kernel(qt_ref, k_ref, vt_ref, o_ref, *bufs):
    tq, tk = TQ_MLA, TK_MLA
    i = pl.program_id(1)
    q_t = qt_ref[0]

    ck = CHUNK_MLA

    def score_chunk(kb, c, maybe_diagonal):
        off = pl.multiple_of(kb * tk + c * ck, ck)
        s = _dot(k_ref[0, pl.ds(off, ck), :], q_t)
        if maybe_diagonal:
            key = off + lax.broadcasted_iota(jnp.int32, (ck, tq), 0)
            qry = i * tq + lax.broadcasted_iota(jnp.int32, (ck, tq), 1)
            s = jnp.where(key <= qry, s, NEG_INF)
        return s

    def value_chunk(kb, c):
        off = pl.multiple_of(kb * tk + c * ck, ck)
        return _with_ones_rows(vt_ref[0, :, pl.ds(off, ck)])

    st = (jnp.full((1, tq), NEG_INF, F32), jnp.zeros((MLA_V + ONES_ROWS, tq), F32))
    _, acc = _pipelined_blocks((i * tq) // tk, score_chunk, value_chunk, tk // ck, ck, jnp.exp,
                               bufs, st)
    o_ref[0] = acc[:MLA_V] / acc[MLA_V:MLA_V + 1]


def _mla_attention(q_t, k, v_t):
    nh, s, _ = k.shape
    tq, tk = TQ_MLA, TK_MLA
    return pl.pallas_call(
        _mla_kernel, grid=(nh, s // tq),
        in_specs=[pl.BlockSpec((1, MLA_PAD, tq), lambda h, i: (h, 0, i)),
                  pl.BlockSpec((1, s, MLA_PAD), lambda h, i: (h, 0, 0)),
                  pl.BlockSpec((1, MLA_V, s), lambda h, i: (h, 0, 0))],
        out_specs=pl.BlockSpec((1, MLA_V, tq), lambda h, i: (h, 0, i)),
        out_shape=jax.ShapeDtypeStruct((nh, MLA_V, s), F32),
        scratch_shapes=[pltpu.VMEM((tk, tq), F32)] * 3 + [pltpu.VMEM((8, tq), F32),
                                                          pltpu.VMEM((MLA_V + ONES_ROWS, tq), F32)],
        compiler_params=_cparams("parallel", "parallel"), name="mla_attention",
    )(q_t, k, v_t)


def _compress_kernel(x_ref, pos_ref, w1_ref, w2_ref, kg_ref, o_ref, ot_ref):
    kv = pl.program_id(0)
    half = CMP_STRIDE * HEAD_DIM
    a = x_ref[0, 0].astype(BF16)
    w1 = w1_ref[0]
    first = _dot(a, w1[:half])
    second = _dot(a, w1[half:])
    n = second.shape[0]
    second = pltpu.roll(second, n - 1, 0)
    posb = jnp.broadcast_to(pos_ref[0], (8, 2 * half)).astype(BF16)
    hid = first + second + _dot(posb, w1)[0:1]
    hid = hid * jax.nn.sigmoid(hid)
    out = _dot(hid.astype(BF16), w2_ref[0])
    ms = jnp.mean(out * out, axis=-1, keepdims=True)
    normed = out * lax.rsqrt(ms + EPS) * kg_ref[...]
    res = jnp.where(kv == 0, normed, out)
    o_ref[0, 0] = res
    ot_ref[0, 0] = res.T


def _compress(x2, pos, w1, w2, kgain):
    _, ng, nc, cw = x2.shape
    return pl.pallas_call(
        _compress_kernel, grid=(2, ng),
        in_specs=[pl.BlockSpec((1, 1, nc, cw), lambda a, g: (a, g, 0, 0)),
                  pl.BlockSpec((1, 1, 2 * cw), lambda a, g: (a, 0, 0)),
                  pl.BlockSpec((1, 2 * cw, CMP_HIDDEN), lambda a, g: (a, 0, 0)),
                  pl.BlockSpec((1, CMP_HIDDEN, HEAD_DIM), lambda a, g: (a, 0, 0)),
                  _const_spec(kgain.shape)],
        out_specs=(pl.BlockSpec((1, 1, nc, HEAD_DIM), lambda a, g: (a, g, 0, 0)),
                   pl.BlockSpec((1, 1, HEAD_DIM, nc), lambda a, g: (a, g, 0, 0))),
        out_shape=(jax.ShapeDtypeStruct((2, ng, nc, HEAD_DIM), F32),
                   jax.ShapeDtypeStruct((2, ng, HEAD_DIM, nc), F32)),
        compiler_params=_cparams("parallel", "parallel"), name="nsa_compress",
    )(x2, pos, w1, w2, kgain)


_IMP_PAD = 8


def _nsa_cmp_kernel(thr_ref, tab_ref, qt_ref, kc_ref, vct_ref, gt_ref, oc_ref, selt_ref, band_ref, imp_ref):
    tq = TQ_CMP
    r_n = NSA_HPG
    g = pl.program_id(0)
    i = pl.program_id(1)
    nc = kc_ref.shape[2]
    nb = selt_ref.shape[1]
    cpt = tq // CMP_STRIDE

    @pl.when(i == 0)
    def _init():
        m = lax.broadcasted_iota(jnp.int32, (2 * nc, tq), 0)
        a = lax.broadcasted_iota(jnp.int32, (2 * nc, tq), 1)
        rel = a - CMP_STRIDE * (m - nc) - (CMP_LEN - 1)
        for r in range(r_n):
            band_ref[r] = _bias_chain(rel, thr_ref, tab_ref, g * r_n + r) * LOG2E
        imp_ref[0:_IMP_PAD, :] = jnp.zeros((_IMP_PAD, tq), F32)

    q_t = jnp.concatenate([qt_ref[r] for r in range(r_n)], axis=1)
    boff = pl.multiple_of(nc - cpt * i, cpt)
    gates = gt_ref[0]

    def tile_body(nc_e, nb_e):
        s = _dot(kc_ref[0, 0, :nc_e, :].astype(BF16), q_t)
        qpos = i * tq + lax.broadcasted_iota(jnp.int32, (nc_e, tq), 1)
        cmp_end = lax.broadcasted_iota(jnp.int32, (nc_e, tq), 0) * CMP_STRIDE + (CMP_LEN - 1)
        valid = cmp_end <= qpos
        imp = jnp.zeros((nc_e, tq), F32)
        probs = []
        for r in range(r_n):
            sr = s[:, r * tq:(r + 1) * tq] + band_ref[r, pl.ds(boff, nc_e), :]
            sr = jnp.where(valid, sr, NEG_INF)
            mx = jnp.max(sr, axis=0, keepdims=True)
            e = jnp.exp2(sr - mx)
            den = jnp.sum(e, axis=0, keepdims=True)
            p = e * jnp.where(mx > 0.5 * NEG_INF, 1.0 / den, 0.0)
            imp = imp + p
            probs.append(p.astype(BF16))
        oc = _dot(vct_ref[0, 0, :, :nc_e].astype(BF16), jnp.concatenate(probs, axis=1))
        for r in range(r_n):
            c = NSA_BRANCHES * r
            oc_ref[r] = oc[:, r * tq:(r + 1) * tq] * gates[c:c + 1, :]

        rs = SEL_BLOCK // CMP_STRIDE
        rc = CMP_LEN // CMP_STRIDE
        imp_ref[_IMP_PAD:_IMP_PAD + nc_e, :] = imp
        imp_blk = None
        for m in range(rs + rc - 1):
            term = imp_ref[pl.ds(_IMP_PAD - (rc - 1) + m, nb_e, stride=rs), :]
            imp_blk = term if imp_blk is None else imp_blk + term
        blk = lax.broadcasted_iota(jnp.int32, (nb_e, tq), 0)
        qp = i * tq + lax.broadcasted_iota(jnp.int32, (nb_e, tq), 1)
        causal_b = blk * SEL_BLOCK <= qp
        dist = qp // SEL_BLOCK - blk
        forced = (blk == 0) | ((dist >= 0) & (dist < N_LOCAL_BLOCKS))
        score = jnp.where(causal_b, jnp.where(forced, FORCE, imp_blk), NEG_INF)
        for _ in range(min(SEL_TOPK, nb_e)):
            mx = jnp.max(score, axis=0, keepdims=True)
            first = jnp.min(jnp.where(score == mx, blk, nb_e), axis=0, keepdims=True)
            score = jnp.where(blk == first, -jnp.inf, score)
        selt_ref[0, :nb_e, :] = jnp.where((score == -jnp.inf) & causal_b, 1.0, 0.0)
        if nb_e < nb:
            selt_ref[0, nb_e:, :] = jnp.zeros((nb - nb_e, tq), F32)

    tiles_per_part = pl.num_programs(1) // CMP_PARTS
    part = i // tiles_per_part
    for j in range(CMP_PARTS):
        pl.when(part == j)(functools.partial(tile_body, (j + 1) * nc // CMP_PARTS, (j + 1) * nb // CMP_PARTS))


def _nsa_compressed(thr, tab, nq_t, cmp, cmp_t, gates_t):
    nh, _, s = nq_t.shape
    ng = NSA_GROUPS
    tq = TQ_CMP
    nc = cmp.shape[2]
    nb = s // SEL_BLOCK
    qspec = pl.BlockSpec((NSA_HPG, HEAD_DIM, tq), lambda g, i: (g, 0, i))
    return pl.pallas_call(
        _nsa_cmp_kernel, grid=(ng, s // tq),
        in_specs=[_smem_spec(), _smem_spec(), qspec,
                  pl.BlockSpec((1, 1, nc, HEAD_DIM), lambda g, i: (0, g, 0, 0)),
                  pl.BlockSpec((1, 1, HEAD_DIM, nc), lambda g, i: (1, g, 0, 0)),
                  pl.BlockSpec((1, NSA_HPG * NSA_BRANCHES, tq), lambda g, i: (g, 0, i))],
        out_specs=(qspec, pl.BlockSpec((1, nb, tq), lambda g, i: (g, 0, i))),
        out_shape=(jax.ShapeDtypeStruct((nh, HEAD_DIM, s), F32),
                   jax.ShapeDtypeStruct((ng, nb, s), F32)),
        scratch_shapes=[pltpu.VMEM((NSA_HPG, 2 * nc, tq), F32), pltpu.VMEM((_IMP_PAD + nc, tq), F32)],
        compiler_params=_cparams("arbitrary", "arbitrary"), name="nsa_compressed",
    )(thr, tab, nq_t, cmp, cmp_t, gates_t)


_NEAR_SPAN = ((MAX_DISTANCE + TK_SEL - 2) // TQ_SEL) * TQ_SEL
assert _NEAR_SPAN >= MAX_DISTANCE and TK_SEL % TQ_SEL == 0
_STRIP = 2 * TK_SEL + _NEAR_SPAN
_WIN_KEYS = WINDOW + TQ_SEL
_BLOCKS_PER_TILE = TK_SEL // SEL_BLOCK


def _nsa_sel_kernel(thr_ref, tab_ref, qt_ref, ks_ref, vst_ref, kw_ref, vwt_ref, selt_ref, gt_ref, oc_ref,
                    o_ref, strip_ref, *bufs):
    tq, tk, r_n = TQ_SEL, TK_SEL, NSA_HPG
    g = pl.program_id(0)
    i = pl.program_id(1)
    top = tk + _NEAR_SPAN
    lanes = r_n * tq

    @pl.when(i == 0)
    def _init():
        u = lax.broadcasted_iota(jnp.int32, (_STRIP, tq), 0)
        a = lax.broadcasted_iota(jnp.int32, (_STRIP, tq), 1)
        for r in range(r_n):
            strip_ref[:, r * tq:(r + 1) * tq] = _bias_chain(a - u + top, thr_ref, tab_ref, g * r_n + r) * LOG2E

    q_t = jnp.concatenate([qt_ref[r] for r in range(r_n)], axis=1)
    pad_rows = jnp.zeros((LANES - HEAD_DIM - _BLOCKS_PER_TILE, lanes), F32)
    qpos = i * tq + (lax.broadcasted_iota(jnp.int32, (1, lanes), 1) & (tq - 1))

    ck = CHUNK_SEL

    def score_chunk(kb, c, maybe_diagonal):
        off = pl.multiple_of(kb * tk + c * ck, ck)
        boff = pl.multiple_of(kb * _BLOCKS_PER_TILE, _BLOCKS_PER_TILE)
        picked = selt_ref[0, pl.ds(boff, _BLOCKS_PER_TILE), :]
        penalty = jnp.where(picked > 0.5, 0.0, NEG_INF)
        penalty = jnp.concatenate([penalty] * r_n, axis=1)
        q_aug = jnp.concatenate([q_t, jnp.concatenate([penalty, pad_rows], axis=0).astype(BF16)], axis=0)
        soff = pl.multiple_of(jnp.clip(top - (i * tq - kb * tk), 0, top) + c * ck, math.gcd(tq, ck))
        s = _dot(ks_ref[0, pl.ds(off, ck), :], q_aug) + strip_ref[pl.ds(soff, ck), :]
        if maybe_diagonal:
            key = off + lax.broadcasted_iota(jnp.int32, (ck, lanes), 0)
            s = jnp.where(key <= qpos, s, NEG_INF)
        return s

    def value_chunk(kb, c):
        off = pl.multiple_of(kb * tk + c * ck, ck)
        return _with_ones_rows(vst_ref[0, :, pl.ds(off, ck)])

    woff = pl.multiple_of(i * tq, tq)
    w0 = top - WINDOW
    sw = _dot(kw_ref[0, pl.ds(woff, _WIN_KEYS), :], q_t) + strip_ref[w0:w0 + _WIN_KEYS, :]
    kpos = i * tq - WINDOW + lax.broadcasted_iota(jnp.int32, (_WIN_KEYS, lanes), 0)
    rel = qpos - kpos
    sw = jnp.where((rel >= 0) & (rel < WINDOW) & (kpos >= 0), sw, NEG_INF)
    pw = jnp.exp2(sw - jnp.max(sw, axis=0, keepdims=True)).astype(BF16)
    o_win = _dot(_with_ones_rows(vwt_ref[0, :, pl.ds(woff, _WIN_KEYS)]), pw)
    o_win = o_win[:HEAD_DIM] / o_win[HEAD_DIM:HEAD_DIM + 1]
    gates = gt_ref[0]
    for r in range(r_n):
        c0 = NSA_BRANCHES * r
        o_ref[r] = oc_ref[r] + gates[c0 + 2:c0 + 3, :] * o_win[:, r * tq:(r + 1) * tq]

    st = (jnp.full((1, lanes), NEG_INF, F32), jnp.zeros((HEAD_DIM + ONES_ROWS, lanes), F32))
    _, acc = _pipelined_blocks((i * tq) // tk, score_chunk, value_chunk, tk // ck, ck, jnp.exp2,
                               bufs, st)
    o_sel = acc[:HEAD_DIM] / acc[HEAD_DIM:HEAD_DIM + 1]
    for r in range(r_n):
        c0 = NSA_BRANCHES * r
        o_ref[r] += gates[c0 + 1:c0 + 2, :] * o_sel[:, r * tq:(r + 1) * tq]


def _nsa_selected(thr, tab, nq_t, ks, vs_t, kw_pad, vw_t_pad, sel_t, gates_t, oc):
    nh, _, s = nq_t.shape
    ng, tq = NSA_GROUPS, TQ_SEL
    nb = s // SEL_BLOCK
    qspec = pl.BlockSpec((NSA_HPG, HEAD_DIM, tq), lambda g, i: (g, 0, i))
    rows = lambda n: pl.BlockSpec((1, n, HEAD_DIM), lambda g, i: (g, 0, 0), pipeline_mode=pl.Buffered(1))
    cols = lambda n: pl.BlockSpec((1, HEAD_DIM, n), lambda g, i: (g, 0, 0), pipeline_mode=pl.Buffered(1))
    return pl.pallas_call(
        _nsa_sel_kernel, grid=(ng, s // tq),
        in_specs=[_smem_spec(), _smem_spec(), qspec,
                  pl.BlockSpec((1, s, LANES), lambda g, i: (g, 0, 0), pipeline_mode=pl.Buffered(1)),
                  cols(s), rows(s + WINDOW), cols(s + WINDOW),
                  pl.BlockSpec((1, nb, tq), lambda g, i: (g, 0, i)),
                  pl.BlockSpec((1, NSA_HPG * NSA_BRANCHES, tq), lambda g, i: (g, 0, i)), qspec],
        out_specs=qspec,
        out_shape=jax.ShapeDtypeStruct((nh, HEAD_DIM, s), F32),
        scratch_shapes=[pltpu.VMEM((_STRIP, NSA_HPG * tq), F32)] + [pltpu.VMEM((TK_SEL, NSA_HPG * tq), F32)] * 3
        + [pltpu.VMEM((8, NSA_HPG * tq), F32), pltpu.VMEM((HEAD_DIM + ONES_ROWS, NSA_HPG * tq), F32)],
        compiler_params=_cparams("arbitrary", "arbitrary"), name="nsa_selected",
    )(thr, tab, nq_t, ks, vs_t, kw_pad, vw_t_pad, sel_t, gates_t, oc)


def _merge_body(osb_ref, onsa_ref, omla_ref, g_ref, x_ref, wsb_ref, wnsa_ref, wmla_ref, wout_ref, fg_ref):
    def heads_proj(o_ref, w_ref):
        nh, hd = w_ref.shape[0], w_ref.shape[1]
        if o_ref.shape[1] == hd:
            o = o_ref[...].reshape(nh * hd, o_ref.shape[2]).T.astype(BF16)
            return _dot(o, w_ref[...].reshape(nh * hd, w_ref.shape[2]))
        y = _dot(o_ref[0], w_ref[0])
        for j in range(1, nh):
            y = y + _dot(o_ref[j], w_ref[j])
        return y

    merged = g_ref[:, 0:D_MODEL] * heads_proj(osb_ref, wsb_ref)
    merged = merged + g_ref[:, D_MODEL:2 * D_MODEL] * heads_proj(onsa_ref, wnsa_ref)
    merged = merged + g_ref[:, 2 * D_MODEL:3 * D_MODEL] * heads_proj(omla_ref, wmla_ref)
    x1 = x_ref[...] + _dot(merged.astype(BF16), wout_ref[...])
    ms = jnp.mean(x1 * x1, axis=-1, keepdims=True)
    return x1, x1 * lax.rsqrt(ms + EPS) * fg_ref[...]


def _merge_kernel(osb_ref, onsa_ref, omla_ref, g_ref, x_ref, wsb_ref, wnsa_ref, wmla_ref, wout_ref, fg_ref,
                  x1_ref, h_ref):
    x1, h = _merge_body(osb_ref, onsa_ref, omla_ref, g_ref, x_ref, wsb_ref, wnsa_ref, wmla_ref, wout_ref,
                        fg_ref)
    x1_ref[...] = x1
    h_ref[...] = h.astype(BF16)


def _merge_router_kernel(osb_ref, onsa_ref, omla_ref, g_ref, x_ref, wsb_ref, wnsa_ref, wmla_ref, wout_ref,
                         fg_ref, wr_ref, x1_ref, h_ref, comb_ref, sel_ref):
    x1, h = _merge_body(osb_ref, onsa_ref, omla_ref, g_ref, x_ref, wsb_ref, wnsa_ref, wmla_ref, wout_ref,
                        fg_ref)
    x1_ref[...] = x1
    h_ref[...] = h.astype(BF16)
    wr = wr_ref[...]
    hp = _split3(h)
    wp = _split3(wr)
    logits = jnp.zeros((h.shape[0], wr.shape[1]), F32)
    for a_i, b_i in ((2, 0), (0, 2), (1, 1), (1, 0), (0, 1), (0, 0)):
        logits = logits + _dot(hp[a_i], wp[b_i])
    lane = lax.broadcasted_iota(jnp.int32, logits.shape, 1)
    nl = logits.shape[1]
    logits = jnp.where(lane < N_EXPERTS, logits, -jnp.inf)
    v1 = jnp.max(logits, axis=-1, keepdims=True)
    i1 = jnp.min(jnp.where(logits == v1, lane, nl), axis=-1, keepdims=True)
    rest = jnp.where(lane == i1, -jnp.inf, logits)
    v2 = jnp.max(rest, axis=-1, keepdims=True)
    i2 = jnp.min(jnp.where(rest == v2, lane, nl), axis=-1, keepdims=True)
    e2 = jnp.exp(v2 - v1)
    w1 = 1.0 / (1.0 + e2)
    w2 = e2 / (1.0 + e2)
    comb_ref[...] = jnp.where(lane == i1, w1, jnp.where(lane == i2, w2, 0.0))
    sel_ref[...] = jnp.where((lane == i1) | (lane == i2), 1.0, 0.0)


def _merge(osb, onsa, omla, g, x, wsb, wnsa, wmla, wout, fgain, wrouter=None):
    s = x.shape[0]
    tm = TM_MERGE if wrouter is None else TM_MERGE_ROUTER

    def heads(a):
        if a.shape[1] == s:
            return pl.BlockSpec((a.shape[0], tm, a.shape[2]), lambda i: (0, i, 0))
        return pl.BlockSpec((a.shape[0], a.shape[1], tm), lambda i: (0, 0, i))

    row = lambda w: pl.BlockSpec((tm, w), lambda i: (i, 0))
    in_specs = [heads(osb), heads(onsa), heads(omla), row(GATE_W), row(D_MODEL)]
    in_specs += [_const_spec(a.shape) for a in (wsb, wnsa, wmla, wout, fgain)]
    out_shape = [jax.ShapeDtypeStruct((s, D_MODEL), F32), jax.ShapeDtypeStruct((s, D_MODEL), BF16)]
    out_specs = [row(D_MODEL), row(D_MODEL)]
    args = [osb, onsa, omla, g, x, wsb, wnsa, wmla, wout, fgain]
    if wrouter is None:
        kern = _merge_kernel
    else:
        kern = _merge_router_kernel
        in_specs.append(_const_spec(wrouter.shape))
        args.append(wrouter)
        out_shape += [jax.ShapeDtypeStruct((s, LANES), F32)] * 2
        out_specs += [row(LANES)] * 2
    return pl.pallas_call(
        kern, grid=(s // tm,), in_specs=in_specs, out_specs=tuple(out_specs), out_shape=tuple(out_shape),
        compiler_params=_cparams("parallel"), name="merge",
    )(*args)


def _ffn_kernel(x_ref, h_ref, wg_ref, wu_ref, wd_ref, o_ref, acc_ref):
    f = pl.program_id(1)

    @pl.when(f == 0)
    def _():
        acc_ref[...] = x_ref[...]

    h = h_ref[...]
    gate = _dot(h, wg_ref[...])
    up = _dot(h, wu_ref[...])
    act = (gate * jax.nn.sigmoid(gate) * up).astype(BF16)
    acc_ref[...] += _dot(act, wd_ref[...])

    @pl.when(f == pl.num_programs(1) - 1)
    def _():
        o_ref[...] = acc_ref[...]


def _dense_ffn(x1, h, wg, wu, wd, tf):
    s = x1.shape[0]
    tm = TM_FFN
    nf = wg.shape[1] // tf
    return pl.pallas_call(
        _ffn_kernel, grid=(s // tm, nf),
        in_specs=[pl.BlockSpec((tm, D_MODEL), lambda i, f: (i, 0)),
                  pl.BlockSpec((tm, D_MODEL), lambda i, f: (i, 0)),
                  pl.BlockSpec((D_MODEL, tf), lambda i, f: (0, f)),
                  pl.BlockSpec((D_MODEL, tf), lambda i, f: (0, f)),
                  pl.BlockSpec((tf, D_MODEL), lambda i, f: (f, 0))],
        out_specs=pl.BlockSpec((tm, D_MODEL), lambda i, f: (i, 0)),
        out_shape=jax.ShapeDtypeStruct((s, D_MODEL), F32),
        scratch_shapes=[pltpu.VMEM((tm, D_MODEL), F32)],
        compiler_params=_cparams("parallel", "arbitrary"), name="dense_ffn",
    )(x1, h, wg, wu, wd)


def _permute_kernel(tile_ref, chunk_ref, flag_ref, exp_ref, dest_ref, cw_ref, h_ref, xg_ref, rw_ref, acc_ref,
                    wacc_ref):
    k = pl.program_id(0)
    tm = TM_MOE
    flags = flag_ref[k]

    @pl.when((flags & 1) != 0)
    def _():
        acc_ref[...] = jnp.zeros_like(acc_ref)
        wacc_ref[...] = jnp.zeros_like(wacc_ref)

    @pl.when((flags & 4) != 0)
    def _():
        row = tile_ref[k] * tm + lax.broadcasted_iota(jnp.int32, (tm, tm), 0)
        hit = dest_ref[0] == row
        acc_ref[...] += _dot(jnp.where(hit, 1.0, 0.0).astype(BF16), h_ref[...])
        wacc_ref[...] += jnp.sum(jnp.where(hit, cw_ref[0], 0.0), axis=-1, keepdims=True)

    @pl.when((flags & 2) != 0)
    def _():
        xg_ref[...] = acc_ref[...].astype(BF16)
        rw_ref[...] = wacc_ref[...]


def _moe_ffn_kernel(te_ref, tv_ref, xg_ref, rw_ref, wg_ref, wu_ref, wd_ref, y_ref, acc_ref):
    t = pl.program_id(0)
    f = pl.program_id(1)

    @pl.when(tv_ref[t] != 0)
    def _():
        @pl.when(f == 0)
        def _():
            acc_ref[...] = jnp.zeros_like(acc_ref)

        x = xg_ref[...]
        gate = _dot(x, wg_ref[0])
        up = _dot(x, wu_ref[0])
        act = (gate * jax.nn.sigmoid(gate) * up * rw_ref[...]).astype(BF16)
        acc_ref[...] += _dot(act, wd_ref[0])

        @pl.when(f == pl.num_programs(1) - 1)
        def _():
            y_ref[...] = acc_ref[...].astype(BF16)

    @pl.when(tv_ref[t] == 0)
    def _():
        y_ref[...] = jnp.zeros_like(y_ref)


def _unpermute_kernel(tile_ref, chunk_ref, flag_ref, grow_ref, x_ref, y_ref, o_ref):
    k = pl.program_id(0)
    tm = TM_MOE
    flags = flag_ref[k]

    @pl.when((flags & 1) != 0)
    def _():
        o_ref[...] = x_ref[...]

    @pl.when((flags & 4) != 0)
    def _():
        tu = y_ref.shape[0]
        row = chunk_ref[k] * tu + lax.broadcasted_iota(jnp.int32, (tm, tu), 1)
        hit = (grow_ref[:, 0:1] == row) | (grow_ref[:, 1:2] == row)
        o_ref[...] += _dot(jnp.where(hit, 1.0, 0.0).astype(BF16), y_ref[...])


def _work_list(first, last, n_items, min_one=True):
    n_tiles = first.shape[0]
    cnt_real = jnp.maximum(last - first + 1, 0)
    cnt = jnp.maximum(cnt_real, 1) if min_one else cnt_real
    ends = jnp.cumsum(cnt)
    starts = ends - cnt
    total = ends[-1]
    k = jnp.arange(n_items, dtype=jnp.int32)
    kk = jnp.minimum(k, total - 1)
    tile = jnp.minimum(jnp.sum(kk[:, None] >= ends[None, :], axis=1), n_tiles - 1).astype(jnp.int32)
    pos = kk - starts[tile]
    chunk = jnp.where(cnt_real[tile] > 0, first[tile] + pos, 0).astype(jnp.int32)
    live = k < total
    flags = (jnp.where(live & (pos == 0), 1, 0) | jnp.where(live & (pos == cnt[tile] - 1), 2, 0)
             | jnp.where(live & (pos < cnt_real[tile]), 4, 0))
    return tile, chunk, flags.astype(jnp.int32)


def _moe_ffn(x1, h, comb, selm, wg, wu, wd):
    s = x1.shape[0]
    tm, tf = TM_MOE, TF_MOE
    ne = N_EXPERTS
    n_chunks = s // tm
    n_tiles = 2 * s // tm + ne
    rows = n_tiles * tm
    sel_t = selm[:, :ne].T.astype(jnp.int32)
    cum = jnp.cumsum(sel_t, axis=1)
    counts = cum[:, -1]
    tiles_e = (counts + tm - 1) // tm
    tile_end = jnp.cumsum(tiles_e)
    tile_start = tile_end - tiles_e
    used = tile_end[-1]
    dest = jnp.where(sel_t > 0, tile_start[:, None] * tm + cum - 1, -1).astype(jnp.int32)
    t_idx = jnp.arange(n_tiles, dtype=jnp.int32)
    te_raw = jnp.minimum(jnp.sum(t_idx[:, None] >= tile_end[None, :], axis=1), ne - 1).astype(jnp.int32)
    tvalid = (t_idx < used).astype(jnp.int32)
    last_e = te_raw[jnp.maximum(used - 1, 0)]
    te = jnp.where(tvalid > 0, te_raw, last_e).astype(jnp.int32)
    dest_tiles = dest.reshape(ne, n_chunks, tm)
    big = jnp.int32(2 ** 30)
    d_max = jnp.max(dest_tiles, axis=2)
    d_min = jnp.min(jnp.where(dest_tiles >= 0, dest_tiles, big), axis=2)
    overlap = (d_max[te] >= t_idx[:, None] * tm) & (d_min[te] < (t_idx[:, None] + 1) * tm) & (tvalid[:, None] > 0)
    c_idx = jnp.arange(n_chunks, dtype=jnp.int32)[None, :]
    c_lo = jnp.min(jnp.where(overlap, c_idx, n_chunks), axis=1)
    c_hi = jnp.max(jnp.where(overlap, c_idx, -1), axis=1)
    c_lo = jnp.where(c_hi >= 0, c_lo, 0).astype(jnp.int32)
    p_tile, p_chunk, p_flag = _work_list(c_lo, c_hi.astype(jnp.int32), n_tiles + ne * n_chunks)
    p_exp = te[p_tile]

    cw_t = comb[:, :ne].T.reshape(ne, 1, s)
    dest3 = dest.reshape(ne, 1, s)
    xg, roww = pl.pallas_call(
        _permute_kernel,
        grid_spec=pltpu.PrefetchScalarGridSpec(
            num_scalar_prefetch=4, grid=(p_tile.shape[0],),
            in_specs=[pl.BlockSpec((1, 1, tm), lambda k, pt, pc, pf, pe: (pe[k], 0, pc[k])),
                      pl.BlockSpec((1, 1, tm), lambda k, pt, pc, pf, pe: (pe[k], 0, pc[k])),
                      pl.BlockSpec((tm, D_MODEL), lambda k, pt, pc, pf, pe: (pc[k], 0))],
            out_specs=(pl.BlockSpec((tm, D_MODEL), lambda k, pt, pc, pf, pe: (pt[k], 0)),
                       pl.BlockSpec((tm, 1), lambda k, pt, pc, pf, pe: (pt[k], 0))),
            scratch_shapes=[pltpu.VMEM((tm, D_MODEL), F32), pltpu.VMEM((tm, 1), F32)]),
        out_shape=(jax.ShapeDtypeStruct((rows, D_MODEL), BF16), jax.ShapeDtypeStruct((rows, 1), F32)),
        compiler_params=_cparams("arbitrary"), name="moe_permute",
    )(p_tile, p_chunk, p_flag, p_exp, dest3, cw_t, h)

    nf = wg.shape[2] // tf
    last_f = nf - 1
    fsel = lambda t, f, tv: f * tv[t] + last_f * (1 - tv[t])
    yg = pl.pallas_call(
        _moe_ffn_kernel,
        grid_spec=pltpu.PrefetchScalarGridSpec(
            num_scalar_prefetch=2, grid=(n_tiles, nf),
            in_specs=[pl.BlockSpec((tm, D_MODEL), lambda t, f, te_, tv: (t, 0)),
                      pl.BlockSpec((tm, 1), lambda t, f, te_, tv: (t, 0)),
                      pl.BlockSpec((1, D_MODEL, tf), lambda t, f, te_, tv: (te_[t], 0, fsel(t, f, tv))),
                      pl.BlockSpec((1, D_MODEL, tf), lambda t, f, te_, tv: (te_[t], 0, fsel(t, f, tv))),
                      pl.BlockSpec((1, tf, D_MODEL), lambda t, f, te_, tv: (te_[t], fsel(t, f, tv), 0))],
            out_specs=pl.BlockSpec((tm, D_MODEL), lambda t, f, te_, tv: (t, 0)),
            scratch_shapes=[pltpu.VMEM((tm, D_MODEL), F32)]),
        out_shape=jax.ShapeDtypeStruct((rows, D_MODEL), BF16),
        compiler_params=_cparams("arbitrary", "arbitrary"), name="moe_ffn",
    )(te, tvalid, xg, roww, wg, wu, wd)

    grow = jnp.stack([jnp.min(jnp.where(dest >= 0, dest, big), axis=0), jnp.max(dest, axis=0)], axis=1)
    has = d_max >= 0
    tu = TU_MOE
    first = jnp.where(has, d_min // tu, 0).T.reshape(-1).astype(jnp.int32)
    last = jnp.where(has, d_max // tu, -1).T.reshape(-1).astype(jnp.int32)
    u_sub, u_chunk, u_flag = _work_list(first, last, rows // tu + ne * n_chunks, min_one=False)
    u_tile = (u_sub // ne).astype(jnp.int32)
    prev_tile = jnp.concatenate([jnp.full((1,), -1, jnp.int32), u_tile[:-1]])
    u_flag = (u_flag & 4) | jnp.where((u_flag != 0) & (u_tile != prev_tile), 1, 0)
    out = pl.pallas_call(
        _unpermute_kernel,
        grid_spec=pltpu.PrefetchScalarGridSpec(
            num_scalar_prefetch=3, grid=(u_tile.shape[0],),
            in_specs=[pl.BlockSpec((tm, 2), lambda k, ut, uc, uf: (ut[k], 0)),
                      pl.BlockSpec((tm, D_MODEL), lambda k, ut, uc, uf: (ut[k], 0)),
                      pl.BlockSpec((tu, D_MODEL), lambda k, ut, uc, uf: (uc[k], 0))],
            out_specs=pl.BlockSpec((tm, D_MODEL), lambda k, ut, uc, uf: (ut[k], 0))),
        out_shape=jax.ShapeDtypeStruct((s, D_MODEL), F32),
        compiler_params=_cparams("arbitrary"), name="moe_unpermute",
    )(u_tile, u_chunk, u_flag, grow, x1, yg)
    return out


def _t5_bucket(rel):
    n = jnp.maximum(rel, 0)
    max_exact = NUM_BUCKETS // 2
    nf = jnp.maximum(n, 1).astype(jnp.float32)
    large = max_exact + (jnp.log(nf / max_exact) / math.log(MAX_DISTANCE / max_exact)
                         * (NUM_BUCKETS - max_exact)).astype(jnp.int32)
    large = jnp.minimum(large, NUM_BUCKETS - 1)
    return jnp.where(n < max_exact, n, large)


def _bucket_thresholds():
    buckets = _t5_bucket(jnp.arange(MAX_DISTANCE + 1, dtype=jnp.int32))
    b = jnp.arange(NUM_BUCKETS, dtype=jnp.int32)
    return jnp.sum(buckets[None, :] < b[:, None], axis=1).astype(jnp.int32)


def _rope_tables(s):
    half = MLA_ROPE // 2
    inv = ROPE_THETA ** (-jnp.arange(half, dtype=jnp.float32) / half)
    ang = jnp.arange(s).astype(jnp.float32)[:, None] * inv[None, :]
    cos, sin = jnp.cos(ang), jnp.sin(ang)
    z = lambda w: jnp.zeros((s, w), F32)
    pad = MLA_PAD - MLA_QK
    cos_t = jnp.concatenate([jnp.ones((s, MLA_NOPE), F32), cos, cos, z(pad)], axis=1)
    sin_lo = jnp.concatenate([z(MLA_NOPE), -sin, z(half), z(pad)], axis=1)
    sin_hi = jnp.concatenate([z(MLA_NOPE), z(half), sin, z(pad)], axis=1)
    return cos_t, sin_lo, sin_hi


def _pad_lanes(a, width):
    return jnp.pad(a, [(0, 0)] * (a.ndim - 1) + [(0, width - a.shape[-1])])


def kernel(x, rel_bias_table, mix_norm, w_in, b_gate, sb_w_o, nsa_q_norm, nsa_k_norm, nsa_cmp_pos, nsa_cmp_w1, nsa_cmp_w2, nsa_w_o, mla_q_a_norm, mla_kv_a_norm, mla_w_uq, mla_w_ukv, mla_q_norm, mla_k_norm, mla_w_o, w_out, ffn_norm, dense_w_gate, dense_w_up, dense_w_down, moe_router, moe_w_gate, moe_w_up, moe_w_down):
    b, s, d = x.shape
    assert b == 1 and d == D_MODEL and s % TM_MOE == 0 and s // SEL_BLOCK >= SEL_TOPK
    depth = w_in.shape[0]
    xs = x.reshape(s, d)
    thr = _bucket_thresholds()
    cos_t, sin_lo, sin_hi = _rope_tables(s)
    n_chunk = s // CMP_STRIDE

    for layer in range(depth):
        w = w_in[layer]
        small = jnp.concatenate([
            w[:, _C_NG:_C_MCQ], jnp.zeros((d, _KR_LANE - NSA_HEADS * NSA_BRANCHES), F32),
            w[:, _C_MKR:_C_GATE], jnp.zeros((d, LANES - _KR_LANE - MLA_ROPE), F32)], axis=1)
        wa = jnp.concatenate([w[:, :_C_NG], w[:, _C_MCQ:_C_MKR], small], axis=1).astype(BF16)
        wg = w[:, _C_GATE:].astype(BF16)
        wuq = _pad_lanes(mla_w_uq[layer].reshape(MLA_Q_LORA, MLA_HEADS, MLA_QK), MLA_PAD)
        wuq = wuq.reshape(MLA_Q_LORA, MLA_HEADS * MLA_PAD).astype(BF16)
        wukv = mla_w_ukv[layer].reshape(MLA_KV_LORA, MLA_HEADS, MLA_NOPE + MLA_V)
        wukv = jnp.concatenate([
            _pad_lanes(wukv[:, :, :MLA_NOPE], MLA_PAD).reshape(MLA_KV_LORA, MLA_HEADS * MLA_PAD),
            wukv[:, :, MLA_NOPE:].reshape(MLA_KV_LORA, MLA_HEADS * MLA_V)], axis=1).astype(BF16)
        (sbq, sbk, sbv, nq, nkvc, nks, nvs, nkw, nvw, gsm, mq, mk, mv, g) = _input_projection(
            xs, mix_norm[layer][None], wa, wg, b_gate[layer][None], wuq, wukv,
            nsa_q_norm[layer][None], nsa_k_norm[layer], mla_q_a_norm[layer][None],
            mla_kv_a_norm[layer][None], _pad_lanes(mla_q_norm[layer][None], MLA_PAD),
            _pad_lanes(mla_k_norm[layer][None], MLA_PAD), cos_t, sin_lo, sin_hi)

        o_sb = _stick_breaking(sbq, sbk, sbv)
        o_mla = _mla_attention(mq, mk, mv)

        x2 = nkvc.reshape(2, NSA_GROUPS, n_chunk, CMP_STRIDE * HEAD_DIM)
        cmp, cmp_t = _compress(x2, nsa_cmp_pos[layer].reshape(2, 1, CMP_LEN * HEAD_DIM),
                               nsa_cmp_w1[layer].astype(BF16), nsa_cmp_w2[layer].astype(BF16),
                               nsa_k_norm[layer][0:1])
        gates = gsm[:NSA_HEADS * NSA_BRANCHES].reshape(NSA_GROUPS, NSA_HPG * NSA_BRANCHES, s)
        oc, sel = _nsa_compressed(thr, rel_bias_table, nq, cmp, cmp_t, gates)
        o_nsa = _nsa_selected(thr, rel_bias_table, nq, nks, nvs,
                              jnp.pad(nkw, ((0, 0), (WINDOW, 0), (0, 0))),
                              jnp.pad(nvw, ((0, 0), (0, 0), (WINDOW, 0))), sel, gates, oc)

        wsb = sb_w_o[layer].reshape(SB_HEADS, HEAD_DIM, d).astype(BF16)
        wnsa = nsa_w_o[layer].reshape(NSA_HEADS, HEAD_DIM, d).astype(BF16)
        wmla = mla_w_o[layer].reshape(MLA_HEADS, MLA_V, d).astype(BF16)
        wout = w_out[layer].astype(BF16)
        j = layer // 2
        if layer % 2 == 0:
            x1, h = _merge(o_sb, o_nsa, o_mla, g, xs, wsb, wnsa, wmla, wout, ffn_norm[layer][None])
            d_ff = dense_w_gate.shape[2]
            xs = _dense_ffn(x1, h, dense_w_gate[j].astype(BF16), dense_w_up[j].astype(BF16),
                            dense_w_down[j].astype(BF16), d_ff // 2)
        else:
            x1, h, comb, selm = _merge(o_sb, o_nsa, o_mla, g, xs, wsb, wnsa, wmla, wout,
                                       ffn_norm[layer][None], _pad_lanes(moe_router[j], LANES))
            xs = _moe_ffn(x1, h, comb, selm, moe_w_gate[j].astype(BF16), moe_w_up[j].astype(BF16),
                          moe_w_down[j].astype(BF16))
    return xs.reshape(b, s, d)
```
